```python
import jax, jax.numpy as jnp
from jax import lax
import numpy as np

D_MODEL = 1024
BATCH = 4
SEQ = 4096
DEPTH = 2

D_MIX = D_MODEL
CONV_W = D_MIX // 4
LRU_W = (D_MIX - CONV_W) // 2
GLA_V = D_MIX - CONV_W - LRU_W
CONV_K = 31
LRU_CONV_K = 4
LRU_BLOCKS = 6
LRU_BW = LRU_W // LRU_BLOCKS
LRU_C = 8.0
GLA_HEADS = 4
GLA_DV = GLA_V // GLA_HEADS
GLA_DK = GLA_DV // 2
GLA_RANK = 16
GLA_TAU = 16.0
GLA_CHUNK = 64
IN_WIDTH = 2 * CONV_W + 2 * LRU_W + 2 * GLA_HEADS * GLA_DK + GLA_V + GLA_RANK + GLA_V
N_GROUPS = 4
EXPERTS_PER_GROUP = 4
TOP_K = 2
D_EXPERT = D_MODEL // 2
EPS = 1e-6

kernel_name = "hybrid_conv_rglru_gla_hmoe_adaln"


def rms_norm(x, g):
    xf = x.astype(jnp.float32)
    y = xf * lax.rsqrt(jnp.mean(xf * xf, axis=-1, keepdims=True) + EPS)
    return (y * g.astype(jnp.float32)).astype(x.dtype)


def layer_norm(x, g, b):
    xf = x.astype(jnp.float32)
    mu = jnp.mean(xf, axis=-1, keepdims=True)
    var = jnp.mean(jnp.square(xf - mu), axis=-1, keepdims=True)
    y = (xf - mu) * lax.rsqrt(var + EPS)
    return (y * g.astype(jnp.float32) + b.astype(jnp.float32)).astype(x.dtype)


def causal_dw_conv(x, w, b):
    k = w.shape[0]
    y = lax.conv_general_dilated(
        x, w[:, None, :].astype(x.dtype), window_strides=(1,), padding=((k - 1, 0),),
        dimension_numbers=("NWC", "WIO", "NWC"), feature_group_count=x.shape[-1])
    return y + b


def rg_lru(xb, w_a, b_a, w_i, b_i, lam):
    bsz, s, w = xb.shape
    xh = xb.reshape(bsz, s, LRU_BLOCKS, LRU_BW)
    r = jax.nn.sigmoid(jnp.einsum("bshi,hij->bshj", xh, w_a).reshape(bsz, s, w) + b_a)
    i = jax.nn.sigmoid(jnp.einsum("bshi,hij->bshj", xh, w_i).reshape(bsz, s, w) + b_i)
    log_a = -LRU_C * r.astype(jnp.float32) * jax.nn.softplus(-lam.astype(jnp.float32))
    a = jnp.exp(log_a)
    mult = jnp.sqrt(-jnp.expm1(2.0 * log_a))
    u = mult * (i * xb).astype(jnp.float32)

    def combine(left, right):
        a1, b1 = left
        a2, b2 = right
        return a1 * a2, a2 * b1 + b2

    _, h = lax.associative_scan(combine, (a, u), axis=1)
    return h.astype(xb.dtype)


def gla_chunked(q, k, v, lg):
    bsz, s, h, dk = q.shape
    n = s // GLA_CHUNK

    def chunk(t):
        return t.reshape(bsz, n, GLA_CHUNK, h, t.shape[-1]).transpose(0, 3, 1, 2, 4)

    q, k, v, lg = chunk(q) * (dk ** -0.5), chunk(k), chunk(v), chunk(lg)
    b = lax.cumsum(lg, axis=3)
    b_last = b[:, :, :, -1:, :]
    q_in = q * jnp.exp(b)
    k_in = k * jnp.exp(-b)
    mask = jnp.tril(jnp.ones((GLA_CHUNK, GLA_CHUNK), dtype=bool))
    scores = jnp.where(mask, jnp.einsum("bhncd,bhnjd->bhncj", q_in, k_in), 0.0)
    o_intra = jnp.einsum("bhncj,bhnje->bhnce", scores, v)
    kv = jnp.einsum("bhncd,bhnce->bhnde", k * jnp.exp(b_last - b), v)
    decay = jnp.exp(b_last[:, :, :, 0, :])

    def step(state, inp):
        d, kvn = inp
        return d[..., None] * state + kvn, state

    s0 = jnp.zeros((bsz, h, dk, v.shape[-1]), jnp.float32)
    _, s_prev = lax.scan(step, s0, (decay.transpose(2, 0, 1, 3), kv.transpose(2, 0, 1, 3, 4)))
    s_prev = s_prev.transpose(1, 2, 0, 3, 4)
    o = o_intra + jnp.einsum("bhncd,bhnde->bhnce", q_in, s_prev)
    return o.transpose(0, 2, 3, 1, 4).reshape(bsz, s, h, v.shape[-1])


def token_mix(h, w_in, conv_dw_w, conv_dw_b, conv_ln_g, conv_ln_b, lru_conv_w, lru_conv_b,
              lru_w_a, lru_b_a, lru_w_i, lru_b_i, lru_lam, gla_w_gate, gla_b_gate, gla_norm_g, w_out):
    bsz, s, _ = h.shape
    sizes = [CONV_W, CONV_W, LRU_W, LRU_W, GLA_HEADS * GLA_DK, GLA_HEADS * GLA_DK, GLA_V, GLA_RANK, GLA_V]
    z = h @ w_in
    cv_v, cv_g, lr_x, lr_y, q, k, v, g_lr, og = jnp.split(z, np.cumsum(sizes)[:-1].tolist(), axis=-1)
    u = cv_v * jax.nn.sigmoid(cv_g)
    u = causal_dw_conv(u, conv_dw_w, conv_dw_b)
    u = jax.nn.silu(layer_norm(u, conv_ln_g, conv_ln_b))
    r = rg_lru(causal_dw_conv(lr_x, lru_conv_w, lru_conv_b), lru_w_a, lru_b_a, lru_w_i, lru_b_i, lru_lam)
    r = r * jax.nn.gelu(lr_y)
    lg = jax.nn.log_sigmoid((g_lr @ gla_w_gate + gla_b_gate).astype(jnp.float32)) / GLA_TAU
    o = gla_chunked(q.reshape(bsz, s, GLA_HEADS, GLA_DK).astype(jnp.float32),
                    k.reshape(bsz, s, GLA_HEADS, GLA_DK).astype(jnp.float32),
                    v.reshape(bsz, s, GLA_HEADS, GLA_DV).astype(jnp.float32),
                    lg.reshape(bsz, s, GLA_HEADS, GLA_DK))
    o = rms_norm(o, gla_norm_g.reshape(GLA_HEADS, GLA_DV)).reshape(bsz, s, GLA_V).astype(h.dtype)
    o = o * jax.nn.silu(og)
    mixed = jnp.concatenate([u, r, o], axis=-1)
    return mixed @ w_out


def hier_moe(h, w_rg, b_rg, w_re, b_re, w_gate, w_up, w_down):
    hf = h.astype(jnp.float32)
    g_logits = hf @ w_rg.astype(jnp.float32) + b_rg.astype(jnp.float32)
    p_g = jax.nn.softmax(g_logits, axis=-1)
    g_star = jnp.argmax(g_logits, axis=-1)
    p_sel = jnp.take_along_axis(p_g, g_star[..., None], axis=-1)
    e_all = jnp.einsum("bsd,gde->bsge", hf, w_re.astype(jnp.float32)) + b_re.astype(jnp.float32)
    e_logits = jnp.take_along_axis(e_all, g_star[..., None, None], axis=2)[:, :, 0]
    top_v, top_i = lax.top_k(e_logits, TOP_K)
    top_w = jax.nn.softmax(top_v, axis=-1) * p_sel
    w_in_group = jnp.sum(jax.nn.one_hot(top_i, EXPERTS_PER_GROUP) * top_w[..., None], axis=-2)
    comb = (jax.nn.one_hot(g_star, N_GROUPS)[..., None] * w_in_group[..., None, :]).astype(h.dtype)
    y = jnp.zeros_like(h)
    for g in range(N_GROUPS):
        a = jnp.einsum("bsd,edf->bsef", h, w_gate[g])
        up = jnp.einsum("bsd,edf->bsef", h, w_up[g])
        hid = jax.nn.silu(a) * up * comb[:, :, g, :, None]
        y = y + jnp.einsum("bsef,efd->bsd", hid, w_down[g])
    return y


def setup_inputs(seed: int = 0) -> dict:
    key = jax.random.key(seed)
    ks = jax.random.split(key, 32)
    f32 = jnp.float32

    def nrm(k, shape, scale):
        return jax.random.normal(k, shape, f32) * scale

    L = DEPTH
    a0 = jax.random.uniform(ks[14], (L, LRU_W), f32, 0.9, 0.999)
    return {
        "x": nrm(ks[0], (BATCH, SEQ, D_MODEL), 1.0),
        "c": nrm(ks[1], (BATCH, D_MODEL), 1.0),
        "w_ada": nrm(ks[2], (L, D_MODEL, 6 * D_MODEL), 0.5 * D_MODEL ** -0.5),
        "b_ada": nrm(ks[3], (L, 6 * D_MODEL), 0.02),
        "g_mix": 1.0 + nrm(ks[4], (L, D_MODEL), 0.02),
        "w_in": nrm(ks[5], (L, D_MODEL, IN_WIDTH), D_MODEL ** -0.5),
        "conv_dw_w": nrm(ks[6], (L, CONV_K, CONV_W), CONV_K ** -0.5),
        "conv_dw_b": nrm(ks[7], (L, CONV_W), 0.02),
        "conv_ln_g": 1.0 + nrm(ks[8], (L, CONV_W), 0.02),
        "conv_ln_b": nrm(ks[9], (L, CONV_W), 0.02),
        "lru_conv_w": nrm(ks[10], (L, LRU_CONV_K, LRU_W), LRU_CONV_K ** -0.5),
        "lru_conv_b": nrm(ks[11], (L, LRU_W), 0.02),
        "lru_w_a": nrm(ks[12], (L, LRU_BLOCKS, LRU_BW, LRU_BW), LRU_BW ** -0.5),
        "lru_b_a": nrm(ks[13], (L, LRU_W), 0.02),
        "lru_w_i": nrm(ks[15], (L, LRU_BLOCKS, LRU_BW, LRU_BW), LRU_BW ** -0.5),
        "lru_b_i": nrm(ks[16], (L, LRU_W), 0.02),
        "lru_lam": jnp.log(a0) - jnp.log1p(-a0),
        "gla_w_gate": nrm(ks[17], (L, GLA_RANK, GLA_HEADS * GLA_DK), GLA_RANK ** -0.5),
        "gla_b_gate": nrm(ks[18], (L, GLA_HEADS * GLA_DK), 0.02),
        "gla_norm_g": 1.0 + nrm(ks[19], (L, GLA_V), 0.02),
        "w_out": nrm(ks[20], (L, D_MIX, D_MODEL), D_MIX ** -0.5),
        "g_ffn": 1.0 + nrm(ks[21], (L, D_MODEL), 0.02),
        "w_route_group": nrm(ks[22], (L, D_MODEL, N_GROUPS), D_MODEL ** -0.5),
        "b_route_group": nrm(ks[23], (L, N_GROUPS), 0.01),
        "w_route_expert": nrm(ks[24], (L, N_GROUPS, D_MODEL, EXPERTS_PER_GROUP), D_MODEL ** -0.5),
        "b_route_expert": nrm(ks[25], (L, N_GROUPS, EXPERTS_PER_GROUP), 0.01),
        "w_gate": nrm(ks[26], (L, N_GROUPS, EXPERTS_PER_GROUP, D_MODEL, D_EXPERT), D_MODEL ** -0.5),
        "w_up": nrm(ks[27], (L, N_GROUPS, EXPERTS_PER_GROUP, D_MODEL, D_EXPERT), D_MODEL ** -0.5),
        "w_down": nrm(ks[28], (L, N_GROUPS, EXPERTS_PER_GROUP, D_EXPERT, D_MODEL), D_EXPERT ** -0.5),
        "g_final": 1.0 + nrm(ks[29], (D_MODEL,), 0.02),
    }


def reference(x, c, w_ada, b_ada, g_mix, w_in, conv_dw_w, conv_dw_b, conv_ln_g, conv_ln_b,
              lru_conv_w, lru_conv_b, lru_w_a, lru_b_a, lru_w_i, lru_b_i, lru_lam,
              gla_w_gate, gla_b_gate, gla_norm_g, w_out, g_ffn, w_route_group, b_route_group,
              w_route_expert, b_route_expert, w_gate, w_up, w_down, g_final):
    c_act = jax.nn.silu(c)
    for l in range(DEPTH):
        mod = (c_act @ w_ada[l] + b_ada[l])[:, None, :]
        sh1, sc1, gt1, sh2, sc2, gt2 = jnp.split(mod, 6, axis=-1)
        h = rms_norm(x, g_mix[l]) * (1.0 + sc1) + sh1
        x = x + gt1 * token_mix(h, w_in[l], conv_dw_w[l], conv_dw_b[l], conv_ln_g[l], conv_ln_b[l],
                                lru_conv_w[l], lru_conv_b[l], lru_w_a[l], lru_b_a[l], lru_w_i[l],
                                lru_b_i[l], lru_lam[l], gla_w_gate[l], gla_b_gate[l], gla_norm_g[l],
                                w_out[l])
        h = rms_norm(x, g_ffn[l]) * (1.0 + sc2) + sh2
        x = x + gt2 * hier_moe(h, w_route_group[l], b_route_group[l], w_route_expert[l],
                               b_route_expert[l], w_gate[l], w_up[l], w_down[l])
    return rms_norm(x, g_final)
```

```python
import functools

import jax
import jax.numpy as jnp
import numpy as np
from jax import lax
from jax.experimental import pallas as pl
from jax.experimental.pallas import tpu as pltpu

D_MODEL = 1024
CONV_W = 256
LRU_W = 384
GLA_V = 384
CONV_K = 31
LRU_CONV_K = 4
LRU_BLOCKS = 6
LRU_BW = 64
LRU_C = 8.0
GLA_HEADS = 4
GLA_DV = 96
GLA_DK = 48
GLA_RANK = 16
GLA_TAU = 16.0
GLA_CHUNK = 64
N_GROUPS = 4
EXPERTS_PER_GROUP = 4
D_EXPERT = 512
EPS = 1e-6

LANES = 128
SUBLANES = 8
VMEM_LIMIT_BYTES = 56 * 1024 * 1024

DK_PAD = 64
DV_PAD = 128
QK_W = GLA_HEADS * DK_PAD
V_W = GLA_HEADS * DV_PAD
OFF_CVV = 0
OFF_CVG = OFF_CVV + CONV_W
OFF_LRX = OFF_CVG + CONV_W
OFF_LRY = OFF_LRX + LRU_W
OFF_Q = OFF_LRY + LRU_W
OFF_K = OFF_Q + QK_W
OFF_V = OFF_K + QK_W
OFF_GLR = OFF_V + V_W
OFF_OG = OFF_GLR + LANES
N_IN = OFF_OG + V_W
MIX_W = CONV_W + LRU_W + V_W

CONV_HIST = 32
LRU_HIST = 8
GLA_BLOCK = 256

N_PAIRS = 6
N_BUCKETS = N_GROUPS * N_PAIRS
ROW_W = D_MODEL + LANES

MIX_TS = 256
ROUTE_TT = 512
FFN_TM = 256
PERM_TB = 1024
COMB_TC = 512

_F32 = jnp.float32
_BF16 = jnp.bfloat16


def _sigmoid(x):
    return 1.0 / (1.0 + jnp.exp(-x))


def _silu(x):
    return x * _sigmoid(x)


def _dot(a, b):
    return jnp.dot(a, b, preferred_element_type=_F32)


def _dot_nt(a, b):
    return lax.dot_general(a, b, (((1,), (1,)), ((), ())), preferred_element_type=_F32)


def _split3(x):
    hi = x.astype(_BF16)
    r1 = x - hi.astype(_F32)
    mid = r1.astype(_BF16)
    lo = (r1 - mid.astype(_F32)).astype(_BF16)
    return hi, mid, lo


def _mod_kernel(c_ref, w_ref, b_ref, o_ref):
    c_act = _silu(c_ref[...])
    o_ref[0] = _dot(c_act.astype(_BF16), w_ref[0].astype(_BF16)) + b_ref[0]


def _modulation(c, w_ada, b_ada):
    n_layers, d, n = w_ada.shape
    bsz = c.shape[0]
    tn = 1536
    return pl.pallas_call(
        _mod_kernel,
        out_shape=jax.ShapeDtypeStruct((n_layers, bsz, n), _F32),
        grid=(n_layers, n // tn),
        in_specs=[
            pl.BlockSpec((bsz, d), lambda l, j: (0, 0)),
            pl.BlockSpec((1, d, tn), lambda l, j: (l, 0, j)),
            pl.BlockSpec((1, 1, tn), lambda l, j: (l, 0, j)),
        ],
        out_specs=pl.BlockSpec((1, bsz, tn), lambda l, j: (l, 0, j)),
        compiler_params=pltpu.CompilerParams(
            dimension_semantics=("arbitrary", "arbitrary"), vmem_limit_bytes=VMEM_LIMIT_BYTES),
        name="adaln_mod",
    )(c, w_ada, b_ada.reshape(n_layers, 1, n))


def _shift_rows(x, d, fill):
    ts, c = x.shape
    if d % SUBLANES == 0:
        return jnp.concatenate([jnp.full((d, c), fill, x.dtype), x[: ts - d]], axis=0)
    rolled = pltpu.roll(x, d, axis=0)
    row = lax.broadcasted_iota(jnp.int32, x.shape, 0)
    return jnp.where(row >= d, rolled, fill)


def _linear_scan(a, u):
    ts = a.shape[0]
    d = 1
    while d < ts:
        a_sh = _shift_rows(a, d, 1.0)
        u_sh = _shift_rows(u, d, 0.0)
        u = u + a * u_sh
        a = a * a_sh
        d *= 2
    return a, u


def _mix_kernel(x_ref, mod_ref, gmix_ref, win_ref, cw_ref, cb_ref, lng_ref, lnb_ref,
                lcw_ref, lcb_ref, wai_ref, bai_ref, lam_ref, wgate_ref, bgate_ref, gnorm_ref,
                wout_ref, o_ref, ubuf, lbuf, hcar, state):
    ts = x_ref.shape[1]
    s_idx = pl.program_id(1)

    @pl.when(s_idx == 0)
    def _():
        ubuf[0:CONV_HIST, :] = jnp.zeros((CONV_HIST, CONV_W), _F32)
        lbuf[0:LRU_HIST, :] = jnp.zeros((LRU_HIST, LRU_W), _F32)
        hcar[...] = jnp.zeros_like(hcar)
        state[...] = jnp.zeros_like(state)

    x = x_ref[0]
    sh1 = mod_ref[0, 0:1, :]
    sc1 = mod_ref[0, 1:2, :]
    gt1 = mod_ref[0, 2:3, :]
    ms = jnp.mean(x * x, axis=-1, keepdims=True)
    h = (x * lax.rsqrt(ms + EPS) * gmix_ref[...]) * (1.0 + sc1) + sh1
    hb = h.astype(_BF16)

    def proj(off, width):
        return _dot(hb, win_ref[:, off:off + width])

    u = proj(OFF_CVV, CONV_W) * _sigmoid(proj(OFF_CVG, CONV_W))
    ubuf[CONV_HIST:CONV_HIST + ts, :] = u
    acc = jnp.broadcast_to(cb_ref[...], (ts, CONV_W))
    for k in range(CONV_K):
        off = CONV_HIST - (CONV_K - 1) + k
        acc = acc + cw_ref[k:k + 1, :] * ubuf[off:off + ts, :]
    ubuf[0:CONV_HIST, :] = ubuf[ts:ts + CONV_HIST, :]
    mu = jnp.mean(acc, axis=-1, keepdims=True)
    cen = acc - mu
    var = jnp.mean(cen * cen, axis=-1, keepdims=True)
    u_out = _silu(cen * lax.rsqrt(var + EPS) * lng_ref[...] + lnb_ref[...])
    mixed = _dot(u_out.astype(_BF16), wout_ref[0:CONV_W, :])

    lbuf[LRU_HIST:LRU_HIST + ts, :] = proj(OFF_LRX, LRU_W)
    xb = jnp.broadcast_to(lcb_ref[...], (ts, LRU_W))
    for k in range(LRU_CONV_K):
        off = LRU_HIST - (LRU_CONV_K - 1) + k
        xb = xb + lcw_ref[k:k + 1, :] * lbuf[off:off + ts, :]
    lbuf[0:LRU_HIST, :] = lbuf[ts:ts + LRU_HIST, :]
    gates = _dot(xb.astype(_BF16), wai_ref[...]) + bai_ref[...]
    r_gate = _sigmoid(gates[:, 0:LRU_W])
    i_gate = _sigmoid(gates[:, LRU_W:2 * LRU_W])
    lam = lam_ref[...]
    softplus_neg_lam = jnp.maximum(-lam, 0.0) + jnp.log1p(jnp.exp(-jnp.abs(lam)))
    log_a = (-LRU_C) * r_gate * softplus_neg_lam
    a = jnp.exp(log_a)
    mult = jnp.sqrt(jnp.tanh(-log_a) * (a * a + 1.0))
    a_cum, u_scan = _linear_scan(a, mult * (i_gate * xb))
    h_lru = u_scan + a_cum * hcar[0:1, :]
    hcar[...] = jnp.broadcast_to(h_lru[ts - 1:ts, :], hcar.shape)
    r_out = h_lru * jax.nn.gelu(proj(OFF_LRY, LRU_W), approximate=True)
    mixed = mixed + _dot(r_out.astype(_BF16), wout_ref[CONV_W:CONV_W + LRU_W, :])

    zg = proj(OFF_GLR, LANES)
    glog = _dot(zg.astype(_BF16), wgate_ref[...]) + bgate_ref[...]
    lg = (jnp.minimum(glog, 0.0) - jnp.log1p(jnp.exp(-jnp.abs(glog)))) * (1.0 / GLA_TAU)
    zq = proj(OFF_Q, QK_W) * (GLA_DK ** -0.5)
    zk = proj(OFF_K, QK_W)
    zv = proj(OFF_V, V_W)

    nblk = ts // GLA_BLOCK
    cpb = GLA_BLOCK // GLA_CHUNK
    ri = lax.broadcasted_iota(jnp.int32, (GLA_BLOCK, GLA_BLOCK), 0)
    ci = lax.broadcasted_iota(jnp.int32, (GLA_BLOCK, GLA_BLOCK), 1)
    same_chunk = ri // GLA_CHUNK == ci // GLA_CHUNK
    causal = same_chunk & (ci <= ri)
    tri = jnp.where(causal, 1.0, 0.0).astype(_BF16)
    tri_after = jnp.where(same_chunk & (ci > ri), 1.0, 0.0).astype(_BF16)
    sel_r = lax.broadcasted_iota(jnp.int32, (LANES, GLA_BLOCK), 0)
    sel_c = lax.broadcasted_iota(jnp.int32, (LANES, GLA_BLOCK), 1)
    chunk_sel = jnp.where(sel_r == sel_c // GLA_CHUNK, 1.0, 0.0).astype(_BF16)
    qk_lane_head = lax.broadcasted_iota(jnp.int32, (GLA_BLOCK, QK_W), 1) // DK_PAD
    t_lane_chunk = lax.broadcasted_iota(jnp.int32, (QK_W, GLA_BLOCK), 1) // GLA_CHUNK
    st_row_head = lax.broadcasted_iota(jnp.int32, (QK_W, V_W), 0) // DK_PAD
    st_col_head = lax.broadcasted_iota(jnp.int32, (QK_W, V_W), 1) // DV_PAD
    head_diag = st_row_head == st_col_head

    o_blocks = []
    for blk in range(nblk):
        r0 = blk * GLA_BLOCK
        lg_b = lg[r0:r0 + GLA_BLOCK]
        p0, p1, p2 = _split3(lg_b)
        b = _dot(tri, p0) + _dot(tri, p1) + _dot(tri, p2)
        b_rest = _dot(tri_after, p0) + _dot(tri_after, p1) + _dot(tri_after, p2)
        b_tot = _dot(chunk_sel, p0) + _dot(chunk_sel, p1) + _dot(chunk_sel, p2)
        q_in = zq[r0:r0 + GLA_BLOCK] * jnp.exp(b)
        k_blk = zk[r0:r0 + GLA_BLOCK]
        k_in = (k_blk * jnp.exp(-b)).astype(_BF16)
        k_out_t = (k_blk * jnp.exp(b_rest)).T.astype(_BF16)
        v_b = zv[r0:r0 + GLA_BLOCK].astype(_BF16)
        q_in_b = q_in.astype(_BF16)
        decay_cols = jnp.exp(b_tot).T

        o_heads = []
        for hd in range(GLA_HEADS):
            q_h = jnp.where(qk_lane_head == hd, q_in_b, jnp.zeros_like(q_in_b))
            sc = jnp.where(causal, _dot_nt(q_h, k_in), 0.0)
            o_heads.append(_dot(sc.astype(_BF16), v_b[:, hd * DV_PAD:(hd + 1) * DV_PAD]))
        o_intra = jnp.concatenate(o_heads, axis=1)

        o_inter = []
        st = state[...]
        for c in range(cpb):
            c0 = c * GLA_CHUNK
            o_inter.append(_dot(q_in_b[c0:c0 + GLA_CHUNK], st.astype(_BF16)))
            kv = _dot(jnp.where(t_lane_chunk == c, k_out_t, jnp.zeros_like(k_out_t)), v_b)
            st = jnp.where(head_diag, st * decay_cols[:, c:c + 1] + kv, 0.0)
        state[...] = st
        o_blocks.append(o_intra + jnp.concatenate(o_inter, axis=0))
    o = o_blocks[0] if nblk == 1 else jnp.concatenate(o_blocks, axis=0)

    og = proj(OFF_OG, V_W)
    o_parts = []
    for hd in range(GLA_HEADS):
        o_h = o[:, hd * DV_PAD:(hd + 1) * DV_PAD]
        ms_h = jnp.sum(o_h * o_h, axis=-1, keepdims=True) * (1.0 / GLA_DV)
        o_parts.append(o_h * lax.rsqrt(ms_h + EPS))
    o_n = jnp.concatenate(o_parts, axis=1) * gnorm_ref[...]
    o_g = o_n * _silu(og)
    mixed = mixed + _dot(o_g.astype(_BF16), wout_ref[CONV_W + LRU_W:MIX_W, :])

    o_ref[0] = x + gt1 * mixed


def _token_mix(x, mod, p):
    bsz, seq, d = x.shape
    ts = MIX_TS
    full = lambda shape: pl.BlockSpec(shape, lambda b, s: (0,) * len(shape))
    return pl.pallas_call(
        _mix_kernel,
        out_shape=jax.ShapeDtypeStruct(x.shape, _F32),
        grid=(bsz, seq // ts),
        in_specs=[
            pl.BlockSpec((1, ts, d), lambda b, s: (b, s, 0)),
            pl.BlockSpec((1, 6, d), lambda b, s: (b, 0, 0)),
            full((1, d)),
            full((d, N_IN)),
            full((CONV_HIST, CONV_W)), full((1, CONV_W)), full((1, CONV_W)), full((1, CONV_W)),
            full((LRU_CONV_K, LRU_W)), full((1, LRU_W)),
            full((LRU_W, 2 * LRU_W)), full((1, 2 * LRU_W)), full((1, LRU_W)),
            full((LANES, QK_W)), full((1, QK_W)), full((1, V_W)),
            full((MIX_W, d)),
        ],
        out_specs=pl.BlockSpec((1, ts, d), lambda b, s: (b, s, 0)),
        scratch_shapes=[
            pltpu.VMEM((CONV_HIST + ts, CONV_W), _F32),
            pltpu.VMEM((LRU_HIST + ts, LRU_W), _F32),
            pltpu.VMEM((SUBLANES, LRU_W), _F32),
            pltpu.VMEM((QK_W, V_W), _F32),
        ],
        compiler_params=pltpu.CompilerParams(
            dimension_semantics=("arbitrary", "arbitrary"), vmem_limit_bytes=VMEM_LIMIT_BYTES),
        name="token_mix",
    )(x, mod, p["g_mix"], p["w_in"], p["conv_w"], p["conv_b"], p["ln_g"], p["ln_b"],
      p["lru_conv_w"], p["lru_conv_b"], p["w_ai"], p["b_ai"], p["lam"], p["w_gate"], p["b_gate"],
      p["g_norm"], p["w_out"])


def _route_kernel(x_ref, mod_ref, g_ref, wr_ref, br_ref, xp_ref, meta_ref, cnt_ref, carry):
    tt = x_ref.shape[0]
    i = pl.program_id(0)

    @pl.when(i == 0)
    def _():
        carry[...] = jnp.zeros_like(carry)

    x = x_ref[...]
    sh2 = mod_ref[0, 3:4, :]
    sc2 = mod_ref[0, 4:5, :]
    ms = jnp.mean(x * x, axis=-1, keepdims=True)
    h = (x * lax.rsqrt(ms + EPS) * g_ref[...]) * (1.0 + sc2) + sh2

    h0, h1, h2 = _split3(h)
    w0, w1, w2 = _split3(wr_ref[...])
    logits = (_dot(h0, w0) + (_dot(h0, w1) + _dot(h1, w0))
              + (_dot(h0, w2) + _dot(h1, w1) + _dot(h2, w0))) + br_ref[...]

    lane = lax.broadcasted_iota(jnp.int32, (tt, LANES), 1)
    lane_f = lane.astype(_F32)
    neg = -jnp.inf
    big = float(LANES)

    def first_argmax(vals, vmax):
        return jnp.min(jnp.where(vals == vmax, lane_f, big), axis=-1, keepdims=True).astype(jnp.int32)

    gl = jnp.where(lane < N_GROUPS, logits, neg)
    gmax = jnp.max(gl, axis=-1, keepdims=True)
    g_star = first_argmax(gl, gmax)
    p_sel = 1.0 / jnp.sum(jnp.exp(gl - gmax), axis=-1, keepdims=True)
    base = N_GROUPS + EXPERTS_PER_GROUP * g_star
    el = jnp.where((lane >= base) & (lane < base + EXPERTS_PER_GROUP), logits, neg)
    v0 = jnp.max(el, axis=-1, keepdims=True)
    i0 = first_argmax(el, v0)
    el2 = jnp.where(lane == i0, neg, el)
    v1 = jnp.max(el2, axis=-1, keepdims=True)
    i1 = first_argmax(el2, v1)
    ex = jnp.exp(v1 - v0)
    wt0 = p_sel / (1.0 + ex)
    wt1 = p_sel * ex / (1.0 + ex)
    e0 = i0 - base
    e1 = i1 - base
    e_lo = jnp.minimum(e0, e1)
    e_hi = jnp.maximum(e0, e1)
    w_lo = jnp.where(e0 < e1, wt0, wt1)
    w_hi = jnp.where(e0 < e1, wt1, wt0)
    pair = (e_lo * (2 * EXPERTS_PER_GROUP - 1 - e_lo)) // 2 + (e_hi - e_lo - 1)
    bucket = g_star * N_PAIRS + pair

    onehot = lane == bucket
    onehot_f = jnp.where(onehot, 1.0, 0.0)
    ri = lax.broadcasted_iota(jnp.int32, (tt, tt), 0)
    ci = lax.broadcasted_iota(jnp.int32, (tt, tt), 1)
    strict = jnp.where(ci < ri, 1.0, 0.0).astype(_BF16)
    prefix = _dot(strict, onehot_f.astype(_BF16)) + carry[0:1, :]
    rank = jnp.sum(jnp.where(onehot, prefix, 0.0), axis=-1, keepdims=True)
    carry[...] = carry[...] + jnp.sum(onehot_f, axis=0, keepdims=True)
    cnt_ref[...] = carry[...]

    meta = jnp.where(lane == 0, bucket.astype(_F32), jnp.where(lane == 1, rank, 0.0))
    meta_ref[...] = meta

    xp_ref[:, 0:D_MODEL] = h
    xp_ref[:, D_MODEL:ROW_W] = jnp.where(lane == 0, w_lo, jnp.where(lane == 1, w_hi, 0.0))


def _route(x2d, mod, g_ffn, w_r, b_r, seq):
    n_tok, d = x2d.shape
    tt = ROUTE_TT
    tiles_per_seq = seq // tt
    return pl.pallas_call(
        _route_kernel,
        out_shape=(
            jax.ShapeDtypeStruct((n_tok, ROW_W), _F32),
            jax.ShapeDtypeStruct((n_tok, LANES), _F32),
            jax.ShapeDtypeStruct((SUBLANES, LANES), _F32),
        ),
        grid=(n_tok // tt,),
        in_specs=[
            pl.BlockSpec((tt, d), lambda i: (i, 0)),
            pl.BlockSpec((1, 6, d), lambda i: (i // tiles_per_seq, 0, 0)),
            pl.BlockSpec((1, d), lambda i: (0, 0)),
            pl.BlockSpec((d, LANES), lambda i: (0, 0)),
            pl.BlockSpec((1, LANES), lambda i: (0, 0)),
        ],
        out_specs=(
            pl.BlockSpec((tt, ROW_W), lambda i: (i, 0)),
            pl.BlockSpec((tt, LANES), lambda i: (i, 0)),
            pl.BlockSpec((SUBLANES, LANES), lambda i: (0, 0)),
        ),
        scratch_shapes=[pltpu.VMEM((SUBLANES, LANES), _F32)],
        compiler_params=pltpu.CompilerParams(
            dimension_semantics=("arbitrary",), vmem_limit_bytes=VMEM_LIMIT_BYTES),
        name="moe_route",
    )(x2d, mod, g_ffn, w_r, b_r)


def _permute_kernel(starts_ref, bucket_ref, rank_ref, xp_hbm, zeros_hbm, xs_hbm, sem):
    del zeros_hbm
    tb = bucket_ref.shape[0]
    t0 = pl.program_id(0) * tb

    def issue(j, carry):
        pos = starts_ref[bucket_ref[j]] + rank_ref[j]
        pltpu.make_async_copy(xp_hbm.at[pl.ds(t0 + j, 1)], xs_hbm.at[pl.ds(pos, 1)], sem).start()
        return carry

    lax.fori_loop(0, tb, issue, 0)
    pltpu.make_async_copy(xp_hbm.at[pl.ds(0, tb)], xs_hbm.at[pl.ds(0, tb)], sem).wait()


def _permute(starts, bucket, rank, xp, n_rows):
    n_tok = xp.shape[0]
    tb = min(PERM_TB, n_tok)
    zeros = jnp.zeros((n_rows, ROW_W), _F32)
    return pl.pallas_call(
        _permute_kernel,
        out_shape=jax.ShapeDtypeStruct((n_rows, ROW_W), _F32),
        grid_spec=pltpu.PrefetchScalarGridSpec(
            num_scalar_prefetch=1,
            grid=(n_tok // tb,),
            in_specs=[
                pl.BlockSpec((tb,), lambda i, st: (i,), memory_space=pltpu.SMEM),
                pl.BlockSpec((tb,), lambda i, st: (i,), memory_space=pltpu.SMEM),
                pl.BlockSpec(memory_space=pl.ANY),
                pl.BlockSpec(memory_space=pl.ANY),
            ],
            out_specs=pl.BlockSpec(memory_space=pl.ANY),
            scratch_shapes=[pltpu.SemaphoreType.DMA],
        ),
        input_output_aliases={4: 0},
        compiler_params=pltpu.CompilerParams(dimension_semantics=("arbitrary",)),
        name="moe_permute",
    )(starts, bucket, rank, xp, zeros)


def _ffn_kernel(tg_ref, tlo_ref, thi_ref, nused_ref, xs_ref, wg0, wu0, wd0, wg1, wu1, wd1, y_ref):
    del tg_ref, tlo_ref, thi_ref

    @pl.when(pl.program_id(0) < nused_ref[0])
    def _():
        xb = xs_ref[:, 0:D_MODEL].astype(_BF16)
        info = xs_ref[:, D_MODEL:ROW_W]

        def expert(wg, wu, wd, wt):
            hid = _silu(_dot(xb, wg[...])) * _dot(xb, wu[...]) * wt
            return _dot(hid.astype(_BF16), wd[...])

        y_ref[...] = (expert(wg0, wu0, wd0, info[:, 0:1]) + expert(wg1, wu1, wd1, info[:, 1:2]))

    @pl.when(pl.program_id(0) >= nused_ref[0])
    def _():
        y_ref[...] = jnp.zeros_like(y_ref)


def _expert_ffn(tile_g, tile_lo, tile_hi, n_used, xs, w_gate, w_up, w_down):
    n_rows = xs.shape[0]
    tm = FFN_TM
    n_tiles = n_rows // tm

    def row_map(i, tg, tlo, thi, nu):
        return (jnp.minimum(i, nu[0] - 1), 0)

    def w_map(which):
        def f(i, tg, tlo, thi, nu):
            return (tg[i], (tlo if which == 0 else thi)[i], 0, 0)
        return f

    wspec_in = lambda which: pl.BlockSpec((None, None, D_MODEL, D_EXPERT), w_map(which))
    wspec_out = lambda which: pl.BlockSpec((None, None, D_EXPERT, D_MODEL), w_map(which))
    return pl.pallas_call(
        _ffn_kernel,
        out_shape=jax.ShapeDtypeStruct((n_rows, D_MODEL), _F32),
        grid_spec=pltpu.PrefetchScalarGridSpec(
            num_scalar_prefetch=4,
            grid=(n_tiles,),
            in_specs=[
                pl.BlockSpec((tm, ROW_W), row_map),
                wspec_in(0), wspec_in(0), wspec_out(0),
                wspec_in(1), wspec_in(1), wspec_out(1),
            ],
            out_specs=pl.BlockSpec((tm, D_MODEL), lambda i, tg, tlo, thi, nu: (i, 0)),
        ),
        compiler_params=pltpu.CompilerParams(
            dimension_semantics=("arbitrary",), vmem_limit_bytes=VMEM_LIMIT_BYTES),
        name="moe_ffn",
    )(tile_g, tile_lo, tile_hi, n_used, xs, w_gate, w_up, w_down, w_gate, w_up, w_down)


def _combine_kernel(starts_ref, bucket_ref, rank_ref, x_ref, mod_ref, gfin_ref, y_hbm, o_ref, ybuf, sem,
                    *, final_norm):
    tc = x_ref.shape[0]

    def issue(j, carry):
        pos = starts_ref[bucket_ref[j]] + rank_ref[j]
        pltpu.make_async_copy(y_hbm.at[pl.ds(pos, 1)], ybuf.at[pl.ds(j, 1)], sem).start()
        return carry

    lax.fori_loop(0, tc, issue, 0)
    pltpu.make_async_copy(y_hbm.at[pl.ds(0, tc)], ybuf, sem).wait()

    gt2 = mod_ref[0, 5:6, :]
    out = x_ref[...] + gt2 * ybuf[...]
    if final_norm:
        ms = jnp.mean(out * out, axis=-1, keepdims=True)
        out = out * lax.rsqrt(ms + EPS) * gfin_ref[...]
    o_ref[...] = out


def _combine(starts, bucket, rank, x2d, mod, g_final, y, seq, final_norm):
    n_tok, d = x2d.shape
    tc = COMB_TC
    tiles_per_seq = seq // tc
    return pl.pallas_call(
        functools.partial(_combine_kernel, final_norm=final_norm),
        out_shape=jax.ShapeDtypeStruct((n_tok, d), _F32),
        grid_spec=pltpu.PrefetchScalarGridSpec(
            num_scalar_prefetch=1,
            grid=(n_tok // tc,),
            in_specs=[
                pl.BlockSpec((tc,), lambda i, st: (i,), memory_space=pltpu.SMEM),
                pl.BlockSpec((tc,), lambda i, st: (i,), memory_space=pltpu.SMEM),
                pl.BlockSpec((tc, d), lambda i, st: (i, 0)),
                pl.BlockSpec((1, 6, d), lambda i, st: (i // tiles_per_seq, 0, 0)),
                pl.BlockSpec((1, d), lambda i, st: (0, 0)),
                pl.BlockSpec(memory_space=pl.ANY),
            ],
            out_specs=pl.BlockSpec((tc, d), lambda i, st: (i, 0)),
            scratch_shapes=[pltpu.VMEM((tc, d), _F32), pltpu.SemaphoreType.DMA],
        ),
        compiler_params=pltpu.CompilerParams(
            dimension_semantics=("arbitrary",), vmem_limit_bytes=VMEM_LIMIT_BYTES),
        name="moe_combine",
    )(starts, bucket, rank, x2d, mod, g_final, y)


def _pad_heads(w, heads, width, padded):
    lead = w.shape[:-1]
    w = w.reshape(lead + (heads, width))
    w = jnp.pad(w, [(0, 0)] * len(lead) + [(0, 0), (0, padded - width)])
    return w.reshape(lead + (heads * padded,))


def _block_diag(w):
    n, bw, _ = w.shape
    eye = jnp.eye(n, dtype=w.dtype)
    return (eye[:, None, :, None] * w[:, :, None, :]).reshape(n * bw, n * bw)


def _prep_layer(l, w_in, conv_dw_w, conv_dw_b, conv_ln_g, conv_ln_b, lru_conv_w, lru_conv_b, lru_w_a,
                lru_b_a, lru_w_i, lru_b_i, lru_lam, gla_w_gate, gla_b_gate, gla_norm_g, w_out, g_mix):
    sizes = [CONV_W, CONV_W, LRU_W, LRU_W, GLA_HEADS * GLA_DK, GLA_HEADS * GLA_DK, GLA_V, GLA_RANK, GLA_V]
    cv_v, cv_g, lr_x, lr_y, q, k, v, g_lr, og = jnp.split(w_in[l], np.cumsum(sizes)[:-1].tolist(), axis=-1)
    w_in_p = jnp.concatenate([
        cv_v, cv_g, lr_x, lr_y,
        _pad_heads(q, GLA_HEADS, GLA_DK, DK_PAD), _pad_heads(k, GLA_HEADS, GLA_DK, DK_PAD),
        _pad_heads(v, GLA_HEADS, GLA_DV, DV_PAD),
        jnp.pad(g_lr, ((0, 0), (0, LANES - GLA_RANK))),
        _pad_heads(og, GLA_HEADS, GLA_DV, DV_PAD)], axis=-1).astype(_BF16)
    wo = w_out[l]
    wo_o = wo[CONV_W + LRU_W:].reshape(GLA_HEADS, GLA_DV, D_MODEL)
    wo_o = jnp.pad(wo_o, ((0, 0), (0, DV_PAD - GLA_DV), (0, 0))).reshape(V_W, D_MODEL)
    w_out_p = jnp.concatenate([wo[:CONV_W + LRU_W], wo_o], axis=0).astype(_BF16)
    w_gate_p = jnp.pad(_pad_heads(gla_w_gate[l], GLA_HEADS, GLA_DK, DK_PAD),
                       ((0, LANES - GLA_RANK), (0, 0))).astype(_BF16)
    return {
        "g_mix": g_mix[l][None, :],
        "w_in": w_in_p,
        "conv_w": jnp.pad(conv_dw_w[l], ((0, CONV_HIST - CONV_K), (0, 0))),
        "conv_b": conv_dw_b[l][None, :],
        "ln_g": conv_ln_g[l][None, :],
        "ln_b": conv_ln_b[l][None, :],
        "lru_conv_w": lru_conv_w[l],
        "lru_conv_b": lru_conv_b[l][None, :],
        "w_ai": jnp.concatenate([_block_diag(lru_w_a[l]), _block_diag(lru_w_i[l])], axis=1).astype(_BF16),
        "b_ai": jnp.concatenate([lru_b_a[l], lru_b_i[l]])[None, :],
        "lam": lru_lam[l][None, :],
        "w_gate": w_gate_p,
        "b_gate": _pad_heads(gla_b_gate[l], GLA_HEADS, GLA_DK, DK_PAD)[None, :],
        "g_norm": _pad_heads(gla_norm_g[l], GLA_HEADS, GLA_DV, DV_PAD)[None, :],
        "w_out": w_out_p,
    }


def _bucket_layout(counts, n_tiles):
    tm = FFN_TM
    counts = counts.astype(jnp.int32)
    tiles = (counts + tm - 1) // tm
    tile_end = jnp.cumsum(tiles)
    tile_start = tile_end - tiles
    starts = (tile_start * tm).astype(jnp.int32)
    n_used = tile_end[-1]
    tile_idx = jnp.minimum(jnp.arange(n_tiles, dtype=jnp.int32), n_used - 1)
    tile_bucket = jnp.sum((tile_idx[:, None] >= tile_end[None, :]).astype(jnp.int32), axis=1)
    pair_lo = jnp.array([0, 0, 0, 1, 1, 2], jnp.int32)
    pair_hi = jnp.array([1, 2, 3, 2, 3, 3], jnp.int32)
    tile_g = tile_bucket // N_PAIRS
    tile_lo = pair_lo[tile_bucket % N_PAIRS]
    tile_hi = pair_hi[tile_bucket % N_PAIRS]
    return starts, tile_g, tile_lo, tile_hi, n_used.reshape(1).astype(jnp.int32)


def kernel(x, c, w_ada, b_ada, g_mix, w_in, conv_dw_w, conv_dw_b, conv_ln_g, conv_ln_b, lru_conv_w,
           lru_conv_b, lru_w_a, lru_b_a, lru_w_i, lru_b_i, lru_lam, gla_w_gate, gla_b_gate, gla_norm_g,
           w_out, g_ffn, w_route_group, b_route_group, w_route_expert, b_route_expert, w_gate, w_up,
           w_down, g_final):
    bsz, seq, d = x.shape
    n_layers = w_ada.shape[0]
    n_tok = bsz * seq
    assert d == D_MODEL and seq % max(MIX_TS, ROUTE_TT, COMB_TC) == 0 and MIX_TS % GLA_BLOCK == 0
    assert n_tok % PERM_TB == 0 or n_tok < PERM_TB
    n_tiles = n_tok // FFN_TM + N_BUCKETS
    n_rows = n_tiles * FFN_TM

    mod_all = _modulation(c, w_ada, b_ada).reshape(n_layers, bsz, 6, d)
    w_gate_b = w_gate.astype(_BF16)
    w_up_b = w_up.astype(_BF16)
    w_down_b = w_down.astype(_BF16)

    for l in range(n_layers):
        p = _prep_layer(l, w_in, conv_dw_w, conv_dw_b, conv_ln_g, conv_ln_b, lru_conv_w, lru_conv_b,
                        lru_w_a, lru_b_a, lru_w_i, lru_b_i, lru_lam, gla_w_gate, gla_b_gate, gla_norm_g,
                        w_out, g_mix)
        mod = mod_all[l]
        x = _token_mix(x, mod, p)

        x2d = x.reshape(n_tok, d)
        w_r = jnp.concatenate(
            [w_route_group[l], w_route_expert[l].transpose(1, 0, 2).reshape(d, N_GROUPS * EXPERTS_PER_GROUP)],
            axis=1)
        n_logits = w_r.shape[1]
        w_r = jnp.pad(w_r, ((0, 0), (0, LANES - n_logits)))
        b_r = jnp.pad(jnp.concatenate([b_route_group[l], b_route_expert[l].reshape(-1)]),
                      (0, LANES - n_logits))[None, :]
        xp, meta, counts = _route(x2d, mod, g_ffn[l][None, :], w_r, b_r, seq)
        bucket = meta[:, 0].astype(jnp.int32)
        rank = meta[:, 1].astype(jnp.int32)
        starts, tile_g, tile_lo, tile_hi, n_used = _bucket_layout(counts[0, :N_BUCKETS], n_tiles)
        starts_pad = jnp.pad(starts, (0, 32 - N_BUCKETS))

        xs = _permute(starts_pad, bucket, rank, xp, n_rows)
        y = _expert_ffn(tile_g, tile_lo, tile_hi, n_used, xs, w_gate_b[l], w_up_b[l], w_down_b[l])
        x2d = _combine(starts_pad, bucket, rank, x2d, mod, g_final[None, :], y, seq,
                       final_norm=(l == n_layers - 1))
        x = x2d.reshape(bsz, seq, d)
    return x
```

```python
import functools

import jax
import jax.numpy as jnp
import numpy as np
from jax import lax
from jax.experimental import pallas as pl
from jax.experimental.pallas import tpu as pltpu

D_MODEL = 1024
CONV_W = 256
LRU_W = 384
GLA_V = 384
CONV_K = 31
LRU_CONV_K = 4
LRU_BLOCKS = 6
LRU_BW = 64
LRU_C = 8.0
GLA_HEADS = 4
GLA_DV = 96
GLA_DK = 48
GLA_RANK = 16
GLA_TAU = 16.0
GLA_CHUNK = 64
N_GROUPS = 4
EXPERTS_PER_GROUP = 4
D_EXPERT = 512
EPS = 1e-6

LANES = 128
SUBLANES = 8
VMEM_LIMIT_BYTES = 56 * 1024 * 1024

DK_PAD = 64
DV_PAD = 128
QK_W = GLA_HEADS * DK_PAD
V_W = GLA_HEADS * DV_PAD
OFF_CVV = 0
OFF_CVG = OFF_CVV + CONV_W
OFF_LRX = OFF_CVG + CONV_W
OFF_LRY = OFF_LRX + LRU_W
OFF_Q = OFF_LRY + LRU_W
OFF_K = OFF_Q + QK_W
OFF_V = OFF_K + QK_W
OFF_GLR = OFF_V + V_W
OFF_OG = OFF_GLR + LANES
N_IN = OFF_OG + V_W
MIX_W = CONV_W + LRU_W + V_W

CONV_HIST = 32
LRU_HIST = 8
GLA_BLOCK = 256

N_PAIRS = 6
N_BUCKETS = N_GROUPS * N_PAIRS
ROW_W = D_MODEL + LANES

MIX_TS = 256
ROUTE_TT = 512
FFN_TM = 256
PERM_TB = 512
COMB_TC = 512
DMA_UNROLL = 8

_F32 = jnp.float32
_BF16 = jnp.bfloat16


def _sigmoid(x):
    return 1.0 / (1.0 + jnp.exp(-x))


def _silu(x):
    return x * _sigmoid(x)


def _dot(a, b):
    return jnp.dot(a, b, preferred_element_type=_F32)


def _dot_nt(a, b):
    return lax.dot_general(a, b, (((1,), (1,)), ((), ())), preferred_element_type=_F32)


def _split3(x):
    hi = x.astype(_BF16)
    r1 = x - hi.astype(_F32)
    mid = r1.astype(_BF16)
    lo = (r1 - mid.astype(_F32)).astype(_BF16)
    return hi, mid, lo


def _mod_kernel(c_ref, w_ref, b_ref, o_ref):
    c_act = _silu(c_ref[...])
    o_ref[0] = _dot(c_act.astype(_BF16), w_ref[0].astype(_BF16)) + b_ref[0]


def _modulation(c, w_ada, b_ada):
    n_layers, d, n = w_ada.shape
    bsz = c.shape[0]
    tn = 1536
    return pl.pallas_call(
        _mod_kernel,
        out_shape=jax.ShapeDtypeStruct((n_layers, bsz, n), _F32),
        grid=(n_layers, n // tn),
        in_specs=[
            pl.BlockSpec((bsz, d), lambda l, j: (0, 0)),
            pl.BlockSpec((1, d, tn), lambda l, j: (l, 0, j)),
            pl.BlockSpec((1, 1, tn), lambda l, j: (l, 0, j)),
        ],
        out_specs=pl.BlockSpec((1, bsz, tn), lambda l, j: (l, 0, j)),
        compiler_params=pltpu.CompilerParams(
            dimension_semantics=("arbitrary", "arbitrary"), vmem_limit_bytes=VMEM_LIMIT_BYTES),
        name="adaln_mod",
    )(c, w_ada, b_ada.reshape(n_layers, 1, n))


def _shift_rows(x, d, fill):
    ts, c = x.shape
    if d % SUBLANES == 0:
        return jnp.concatenate([jnp.full((d, c), fill, x.dtype), x[: ts - d]], axis=0)
    rolled = pltpu.roll(x, d, axis=0)
    row = lax.broadcasted_iota(jnp.int32, x.shape, 0)
    return jnp.where(row >= d, rolled, fill)


def _linear_scan(a, u):
    ts = a.shape[0]
    d = 1
    while d < ts:
        a_sh = _shift_rows(a, d, 1.0)
        u_sh = _shift_rows(u, d, 0.0)
        u = u + a * u_sh
        a = a * a_sh
        d *= 2
    return a, u


def _mix_kernel(x_ref, mod_ref, gmix_ref, win_ref, cw_ref, cb_ref, lng_ref, lnb_ref,
                lcw_ref, lcb_ref, wai_ref, bai_ref, lam_ref, wgate_ref, bgate_ref, gnorm_ref,
                wout_ref, o_ref, ubuf, lbuf, hcar, state):
    ts = x_ref.shape[1]
    s_idx = pl.program_id(1)

    @pl.when(s_idx == 0)
    def _():
        ubuf[0:CONV_HIST, :] = jnp.zeros((CONV_HIST, CONV_W), _F32)
        lbuf[0:LRU_HIST, :] = jnp.zeros((LRU_HIST, LRU_W), _F32)
        hcar[...] = jnp.zeros_like(hcar)
        state[...] = jnp.zeros_like(state)

    x = x_ref[0]
    sh1 = mod_ref[0, 0:1, :]
    sc1 = mod_ref[0, 1:2, :]
    gt1 = mod_ref[0, 2:3, :]
    ms = jnp.mean(x * x, axis=-1, keepdims=True)
    h = (x * lax.rsqrt(ms + EPS) * gmix_ref[...]) * (1.0 + sc1) + sh1
    hb = h.astype(_BF16)

    def proj(off, width):
        return _dot(hb, win_ref[:, off:off + width])

    u = proj(OFF_CVV, CONV_W) * _sigmoid(proj(OFF_CVG, CONV_W))
    ubuf[CONV_HIST:CONV_HIST + ts, :] = u
    acc = jnp.broadcast_to(cb_ref[...], (ts, CONV_W))
    for k in range(CONV_K):
        off = CONV_HIST - (CONV_K - 1) + k
        acc = acc + cw_ref[k:k + 1, :] * ubuf[off:off + ts, :]
    ubuf[0:CONV_HIST, :] = ubuf[ts:ts + CONV_HIST, :]
    mu = jnp.mean(acc, axis=-1, keepdims=True)
    cen = acc - mu
    var = jnp.mean(cen * cen, axis=-1, keepdims=True)
    u_out = _silu(cen * lax.rsqrt(var + EPS) * lng_ref[...] + lnb_ref[...])
    mixed = _dot(u_out.astype(_BF16), wout_ref[0:CONV_W, :])

    lbuf[LRU_HIST:LRU_HIST + ts, :] = proj(OFF_LRX, LRU_W)
    xb = jnp.broadcast_to(lcb_ref[...], (ts, LRU_W))
    for k in range(LRU_CONV_K):
        off = LRU_HIST - (LRU_CONV_K - 1) + k
        xb = xb + lcw_ref[k:k + 1, :] * lbuf[off:off + ts, :]
    lbuf[0:LRU_HIST, :] = lbuf[ts:ts + LRU_HIST, :]
    gates = _dot(xb.astype(_BF16), wai_ref[...]) + bai_ref[...]
    r_gate = _sigmoid(gates[:, 0:LRU_W])
    i_gate = _sigmoid(gates[:, LRU_W:2 * LRU_W])
    lam = lam_ref[...]
    softplus_neg_lam = jnp.maximum(-lam, 0.0) + jnp.log1p(jnp.exp(-jnp.abs(lam)))
    log_a = (-LRU_C) * r_gate * softplus_neg_lam
    a = jnp.exp(log_a)
    mult = jnp.sqrt(jnp.tanh(-log_a) * (a * a + 1.0))
    a_cum, u_scan = _linear_scan(a, mult * (i_gate * xb))
    h_lru = u_scan + a_cum * hcar[0:1, :]
    hcar[...] = jnp.broadcast_to(h_lru[ts - 1:ts, :], hcar.shape)
    r_out = h_lru * jax.nn.gelu(proj(OFF_LRY, LRU_W), approximate=True)
    mixed = mixed + _dot(r_out.astype(_BF16), wout_ref[CONV_W:CONV_W + LRU_W, :])

    zg = proj(OFF_GLR, LANES)
    glog = _dot(zg.astype(_BF16), wgate_ref[...]) + bgate_ref[...]
    lg = (jnp.minimum(glog, 0.0) - jnp.log1p(jnp.exp(-jnp.abs(glog)))) * (1.0 / GLA_TAU)
    zq = proj(OFF_Q, QK_W) * (GLA_DK ** -0.5)
    zk = proj(OFF_K, QK_W)
    zv = proj(OFF_V, V_W)

    nblk = ts // GLA_BLOCK
    cpb = GLA_BLOCK // GLA_CHUNK
    ri = lax.broadcasted_iota(jnp.int32, (GLA_BLOCK, GLA_BLOCK), 0)
    ci = lax.broadcasted_iota(jnp.int32, (GLA_BLOCK, GLA_BLOCK), 1)
    same_chunk = ri // GLA_CHUNK == ci // GLA_CHUNK
    causal = same_chunk & (ci <= ri)
    tri = jnp.where(causal, 1.0, 0.0).astype(_BF16)
    tri_after = jnp.where(same_chunk & (ci > ri), 1.0, 0.0).astype(_BF16)
    sel_r = lax.broadcasted_iota(jnp.int32, (LANES, GLA_BLOCK), 0)
    sel_c = lax.broadcasted_iota(jnp.int32, (LANES, GLA_BLOCK), 1)
    chunk_sel = jnp.where(sel_r == sel_c // GLA_CHUNK, 1.0, 0.0).astype(_BF16)
    qk_lane_head = lax.broadcasted_iota(jnp.int32, (GLA_BLOCK, QK_W), 1) // DK_PAD
    t_lane_chunk = lax.broadcasted_iota(jnp.int32, (QK_W, GLA_BLOCK), 1) // GLA_CHUNK
    st_row_head = lax.broadcasted_iota(jnp.int32, (QK_W, V_W), 0) // DK_PAD
    st_col_head = lax.broadcasted_iota(jnp.int32, (QK_W, V_W), 1) // DV_PAD
    head_diag = st_row_head == st_col_head

    o_blocks = []
    for blk in range(nblk):
        r0 = blk * GLA_BLOCK
        lg_b = lg[r0:r0 + GLA_BLOCK]
        p0, p1, p2 = _split3(lg_b)
        b = _dot(tri, p0) + _dot(tri, p1) + _dot(tri, p2)
        b_rest = _dot(tri_after, p0) + _dot(tri_after, p1) + _dot(tri_after, p2)
        b_tot = _dot(chunk_sel, p0) + _dot(chunk_sel, p1) + _dot(chunk_sel, p2)
        q_in = zq[r0:r0 + GLA_BLOCK] * jnp.exp(b)
        k_blk = zk[r0:r0 + GLA_BLOCK]
        k_in = (k_blk * jnp.exp(-b)).astype(_BF16)
        k_out_t = (k_blk * jnp.exp(b_rest)).T.astype(_BF16)
        v_b = zv[r0:r0 + GLA_BLOCK].astype(_BF16)
        q_in_b = q_in.astype(_BF16)
        decay_cols = jnp.exp(b_tot).T

        o_heads = []
        for hd in range(GLA_HEADS):
            q_h = jnp.where(qk_lane_head == hd, q_in_b, jnp.zeros_like(q_in_b))
            sc = jnp.where(causal, _dot_nt(q_h, k_in), 0.0)
            o_heads.append(_dot(sc.astype(_BF16), v_b[:, hd * DV_PAD:(hd + 1) * DV_PAD]))
        o_intra = jnp.concatenate(o_heads, axis=1)

        o_inter = []
        st = state[...]
        for c in range(cpb):
            c0 = c * GLA_CHUNK
            o_inter.append(_dot(q_in_b[c0:c0 + GLA_CHUNK], st.astype(_BF16)))
            kv = _dot(jnp.where(t_lane_chunk == c, k_out_t, jnp.zeros_like(k_out_t)), v_b)
            st = jnp.where(head_diag, st * decay_cols[:, c:c + 1] + kv, 0.0)
        state[...] = st
        o_blocks.append(o_intra + jnp.concatenate(o_inter, axis=0))
    o = o_blocks[0] if nblk == 1 else jnp.concatenate(o_blocks, axis=0)

    og = proj(OFF_OG, V_W)
    o_parts = []
    for hd in range(GLA_HEADS):
        o_h = o[:, hd * DV_PAD:(hd + 1) * DV_PAD]
        ms_h = jnp.sum(o_h * o_h, axis=-1, keepdims=True) * (1.0 / GLA_DV)
        o_parts.append(o_h * lax.rsqrt(ms_h + EPS))
    o_n = jnp.concatenate(o_parts, axis=1) * gnorm_ref[...]
    o_g = o_n * _silu(og)
    mixed = mixed + _dot(o_g.astype(_BF16), wout_ref[CONV_W + LRU_W:MIX_W, :])

    o_ref[0] = x + gt1 * mixed


def _token_mix(x, mod, p):
    bsz, seq, d = x.shape
    ts = MIX_TS
    full = lambda shape: pl.BlockSpec(shape, lambda b, s: (0,) * len(shape))
    return pl.pallas_call(
        _mix_kernel,
        out_shape=jax.ShapeDtypeStruct(x.shape, _F32),
        grid=(bsz, seq // ts),
        in_specs=[
            pl.BlockSpec((1, ts, d), lambda b, s: (b, s, 0)),
            pl.BlockSpec((1, 6, d), lambda b, s: (b, 0, 0)),
            full((1, d)),
            full((d, N_IN)),
            full((CONV_HIST, CONV_W)), full((1, CONV_W)), full((1, CONV_W)), full((1, CONV_W)),
            full((LRU_CONV_K, LRU_W)), full((1, LRU_W)),
            full((LRU_W, 2 * LRU_W)), full((1, 2 * LRU_W)), full((1, LRU_W)),
            full((LANES, QK_W)), full((1, QK_W)), full((1, V_W)),
            full((MIX_W, d)),
        ],
        out_specs=pl.BlockSpec((1, ts, d), lambda b, s: (b, s, 0)),
        scratch_shapes=[
            pltpu.VMEM((CONV_HIST + ts, CONV_W), _F32),
            pltpu.VMEM((LRU_HIST + ts, LRU_W), _F32),
            pltpu.VMEM((SUBLANES, LRU_W), _F32),
            pltpu.VMEM((QK_W, V_W), _F32),
        ],
        compiler_params=pltpu.CompilerParams(
            dimension_semantics=("arbitrary", "arbitrary"), vmem_limit_bytes=VMEM_LIMIT_BYTES),
        name="token_mix",
    )(x, mod, p["g_mix"], p["w_in"], p["conv_w"], p["conv_b"], p["ln_g"], p["ln_b"],
      p["lru_conv_w"], p["lru_conv_b"], p["w_ai"], p["b_ai"], p["lam"], p["w_gate"], p["b_gate"],
      p["g_norm"], p["w_out"])


def _route_kernel(x_ref, mod_ref, g_ref, wr_ref, br_ref, xp_ref, meta_ref, cnt_ref, carry):
    tt = x_ref.shape[0]
    i = pl.program_id(0)

    @pl.when(i == 0)
    def _():
        carry[...] = jnp.zeros_like(carry)

    x = x_ref[...]
    sh2 = mod_ref[0, 3:4, :]
    sc2 = mod_ref[0, 4:5, :]
    ms = jnp.mean(x * x, axis=-1, keepdims=True)
    h = (x * lax.rsqrt(ms + EPS) * g_ref[...]) * (1.0 + sc2) + sh2

    h0, h1, h2 = _split3(h)
    w0, w1, w2 = _split3(wr_ref[...])
    logits = (_dot(h0, w0) + (_dot(h0, w1) + _dot(h1, w0))
              + (_dot(h0, w2) + _dot(h1, w1) + _dot(h2, w0))) + br_ref[...]

    lane = lax.broadcasted_iota(jnp.int32, (tt, LANES), 1)
    lane_f = lane.astype(_F32)
    neg = -jnp.inf
    big = float(LANES)

    def first_argmax(vals, vmax):
        return jnp.min(jnp.where(vals == vmax, lane_f, big), axis=-1, keepdims=True).astype(jnp.int32)

    gl = jnp.where(lane < N_GROUPS, logits, neg)
    gmax = jnp.max(gl, axis=-1, keepdims=True)
    g_star = first_argmax(gl, gmax)
    p_sel = 1.0 / jnp.sum(jnp.exp(gl - gmax), axis=-1, keepdims=True)
    base = N_GROUPS + EXPERTS_PER_GROUP * g_star
    el = jnp.where((lane >= base) & (lane < base + EXPERTS_PER_GROUP), logits, neg)
    v0 = jnp.max(el, axis=-1, keepdims=True)
    i0 = first_argmax(el, v0)
    el2 = jnp.where(lane == i0, neg, el)
    v1 = jnp.max(el2, axis=-1, keepdims=True)
    i1 = first_argmax(el2, v1)
    ex = jnp.exp(v1 - v0)
    wt0 = p_sel / (1.0 + ex)
    wt1 = p_sel * ex / (1.0 + ex)
    e0 = i0 - base
    e1 = i1 - base
    e_lo = jnp.minimum(e0, e1)
    e_hi = jnp.maximum(e0, e1)
    w_lo = jnp.where(e0 < e1, wt0, wt1)
    w_hi = jnp.where(e0 < e1, wt1, wt0)
    pair = (e_lo * (2 * EXPERTS_PER_GROUP - 1 - e_lo)) // 2 + (e_hi - e_lo - 1)
    bucket = g_star * N_PAIRS + pair

    onehot = lane == bucket
    onehot_f = jnp.where(onehot, 1.0, 0.0)
    ri = lax.broadcasted_iota(jnp.int32, (tt, tt), 0)
    ci = lax.broadcasted_iota(jnp.int32, (tt, tt), 1)
    strict = jnp.where(ci < ri, 1.0, 0.0).astype(_BF16)
    prefix = _dot(strict, onehot_f.astype(_BF16)) + carry[0:1, :]
    rank = jnp.sum(jnp.where(onehot, prefix, 0.0), axis=-1, keepdims=True)
    carry[...] = carry[...] + jnp.sum(onehot_f, axis=0, keepdims=True)
    cnt_ref[...] = carry[...]

    meta = jnp.where(lane == 0, bucket.astype(_F32), jnp.where(lane == 1, rank, 0.0))
    meta_ref[...] = meta

    xp_ref[:, 0:D_MODEL] = h
    xp_ref[:, D_MODEL:ROW_W] = jnp.where(lane == 0, w_lo, jnp.where(lane == 1, w_hi, 0.0))


def _route(x2d, mod, g_ffn, w_r, b_r, seq):
    n_tok, d = x2d.shape
    tt = ROUTE_TT
    tiles_per_seq = seq // tt
    return pl.pallas_call(
        _route_kernel,
        out_shape=(
            jax.ShapeDtypeStruct((n_tok, ROW_W), _F32),
            jax.ShapeDtypeStruct((n_tok, LANES), _F32),
            jax.ShapeDtypeStruct((SUBLANES, LANES), _F32),
        ),
        grid=(n_tok // tt,),
        in_specs=[
            pl.BlockSpec((tt, d), lambda i: (i, 0)),
            pl.BlockSpec((1, 6, d), lambda i: (i // tiles_per_seq, 0, 0)),
            pl.BlockSpec((1, d), lambda i: (0, 0)),
            pl.BlockSpec((d, LANES), lambda i: (0, 0)),
            pl.BlockSpec((1, LANES), lambda i: (0, 0)),
        ],
        out_specs=(
            pl.BlockSpec((tt, ROW_W), lambda i: (i, 0)),
            pl.BlockSpec((tt, LANES), lambda i: (i, 0)),
            pl.BlockSpec((SUBLANES, LANES), lambda i: (0, 0)),
        ),
        scratch_shapes=[pltpu.VMEM((SUBLANES, LANES), _F32)],
        compiler_params=pltpu.CompilerParams(
            dimension_semantics=("arbitrary",), vmem_limit_bytes=VMEM_LIMIT_BYTES),
        name="moe_route",
    )(x2d, mod, g_ffn, w_r, b_r)


def _permute_kernel(pos_ref, xp_ref, zeros_hbm, xs_hbm, sem):
    del zeros_hbm
    tb = pos_ref.shape[0]

    def issue(j, carry):
        pltpu.make_async_copy(xp_ref.at[pl.ds(j, 1)], xs_hbm.at[pl.ds(pos_ref[j], 1)], sem).start()
        return carry

    lax.fori_loop(0, tb, issue, 0, unroll=DMA_UNROLL)
    pltpu.make_async_copy(xp_ref, xs_hbm.at[pl.ds(0, tb)], sem).wait()


def _permute(pos, xp, n_rows):
    n_tok = xp.shape[0]
    tb = PERM_TB
    zeros = jnp.zeros((n_rows, ROW_W), _F32)
    return pl.pallas_call(
        _permute_kernel,
        out_shape=jax.ShapeDtypeStruct((n_rows, ROW_W), _F32),
        grid=(n_tok // tb,),
        in_specs=[
            pl.BlockSpec((tb,), lambda i: (i,), memory_space=pltpu.SMEM),
            pl.BlockSpec((tb, ROW_W), lambda i: (i, 0)),
            pl.BlockSpec(memory_space=pl.ANY),
        ],
        out_specs=pl.BlockSpec(memory_space=pl.ANY),
        scratch_shapes=[pltpu.SemaphoreType.DMA],
        input_output_aliases={2: 0},
        compiler_params=pltpu.CompilerParams(
            dimension_semantics=("arbitrary",), vmem_limit_bytes=VMEM_LIMIT_BYTES),
        name="moe_permute",
    )(pos, xp, zeros)


def _ffn_kernel(tg_ref, tlo_ref, thi_ref, nused_ref, xs_ref, wg0, wu0, wd0, wg1, wu1, wd1, y_ref):
    del tg_ref, tlo_ref, thi_ref

    @pl.when(pl.program_id(0) < nused_ref[0])
    def _():
        xb = xs_ref[:, 0:D_MODEL].astype(_BF16)
        info = xs_ref[:, D_MODEL:ROW_W]

        def expert(wg, wu, wd, wt):
            hid = _silu(_dot(xb, wg[...])) * _dot(xb, wu[...]) * wt
            return _dot(hid.astype(_BF16), wd[...])

        y_ref[...] = (expert(wg0, wu0, wd0, info[:, 0:1]) + expert(wg1, wu1, wd1, info[:, 1:2]))

    @pl.when(pl.program_id(0) >= nused_ref[0])
    def _():
        y_ref[...] = jnp.zeros_like(y_ref)


def _expert_ffn(tile_g, tile_lo, tile_hi, n_used, xs, w_gate, w_up, w_down):
    n_rows = xs.shape[0]
    tm = FFN_TM
    n_tiles = n_rows // tm

    def row_map(i, tg, tlo, thi, nu):
        return (jnp.minimum(i, nu[0] - 1), 0)

    def w_map(which):
        def f(i, tg, tlo, thi, nu):
            return (tg[i], (tlo if which == 0 else thi)[i], 0, 0)
        return f

    wspec_in = lambda which: pl.BlockSpec((None, None, D_MODEL, D_EXPERT), w_map(which))
    wspec_out = lambda which: pl.BlockSpec((None, None, D_EXPERT, D_MODEL), w_map(which))
    return pl.pallas_call(
        _ffn_kernel,
        out_shape=jax.ShapeDtypeStruct((n_rows, D_MODEL), _F32),
        grid_spec=pltpu.PrefetchScalarGridSpec(
            num_scalar_prefetch=4,
            grid=(n_tiles,),
            in_specs=[
                pl.BlockSpec((tm, ROW_W), row_map),
                wspec_in(0), wspec_in(0), wspec_out(0),
                wspec_in(1), wspec_in(1), wspec_out(1),
            ],
            out_specs=pl.BlockSpec((tm, D_MODEL), lambda i, tg, tlo, thi, nu: (i, 0)),
        ),
        compiler_params=pltpu.CompilerParams(
            dimension_semantics=("arbitrary",), vmem_limit_bytes=VMEM_LIMIT_BYTES),
        name="moe_ffn",
    )(tile_g, tile_lo, tile_hi, n_used, xs, w_gate, w_up, w_down, w_gate, w_up, w_down)


def _combine_kernel(pos_ref, x_ref, mod_ref, gfin_ref, y_hbm, o_ref, ybuf, sem, *, final_norm):
    tc = x_ref.shape[0]

    def issue(j, carry):
        pltpu.make_async_copy(y_hbm.at[pl.ds(pos_ref[j], 1)], ybuf.at[pl.ds(j, 1)], sem).start()
        return carry

    lax.fori_loop(0, tc, issue, 0, unroll=DMA_UNROLL)
    pltpu.make_async_copy(y_hbm.at[pl.ds(0, tc)], ybuf, sem).wait()

    gt2 = mod_ref[0, 5:6, :]
    out = x_ref[...] + gt2 * ybuf[...]
    if final_norm:
        ms = jnp.mean(out * out, axis=-1, keepdims=True)
        out = out * lax.rsqrt(ms + EPS) * gfin_ref[...]
    o_ref[...] = out


def _combine(pos, x2d, mod, g_final, y, seq, final_norm):
    n_tok, d = x2d.shape
    tc = COMB_TC
    tiles_per_seq = seq // tc
    return pl.pallas_call(
        functools.partial(_combine_kernel, final_norm=final_norm),
        out_shape=jax.ShapeDtypeStruct((n_tok, d), _F32),
        grid=(n_tok // tc,),
        in_specs=[
            pl.BlockSpec((tc,), lambda i: (i,), memory_space=pltpu.SMEM),
            pl.BlockSpec((tc, d), lambda i: (i, 0)),
            pl.BlockSpec((1, 6, d), lambda i: (i // tiles_per_seq, 0, 0)),
            pl.BlockSpec((1, d), lambda i: (0, 0)),
            pl.BlockSpec(memory_space=pl.ANY),
        ],
        out_specs=pl.BlockSpec((tc, d), lambda i: (i, 0)),
        scratch_shapes=[pltpu.VMEM((tc, d), _F32), pltpu.SemaphoreType.DMA],
        compiler_params=pltpu.CompilerParams(
            dimension_semantics=("arbitrary",), vmem_limit_bytes=VMEM_LIMIT_BYTES),
        name="moe_combine",
    )(pos, x2d, mod, g_final, y)


def _pad_heads(w, heads, width, padded):
    lead = w.shape[:-1]
    w = w.reshape(lead + (heads, width))
    w = jnp.pad(w, [(0, 0)] * len(lead) + [(0, 0), (0, padded - width)])
    return w.reshape(lead + (heads * padded,))


def _block_diag(w):
    n, bw, _ = w.shape
    eye = jnp.eye(n, dtype=w.dtype)
    return (eye[:, None, :, None] * w[:, :, None, :]).reshape(n * bw, n * bw)


def _prep_layer(l, w_in, conv_dw_w, conv_dw_b, conv_ln_g, conv_ln_b, lru_conv_w, lru_conv_b, lru_w_a,
                lru_b_a, lru_w_i, lru_b_i, lru_lam, gla_w_gate, gla_b_gate, gla_norm_g, w_out, g_mix):
    sizes = [CONV_W, CONV_W, LRU_W, LRU_W, GLA_HEADS * GLA_DK, GLA_HEADS * GLA_DK, GLA_V, GLA_RANK, GLA_V]
    cv_v, cv_g, lr_x, lr_y, q, k, v, g_lr, og = jnp.split(w_in[l], np.cumsum(sizes)[:-1].tolist(), axis=-1)
    w_in_p = jnp.concatenate([
        cv_v, cv_g, lr_x, lr_y,
        _pad_heads(q, GLA_HEADS, GLA_DK, DK_PAD), _pad_heads(k, GLA_HEADS, GLA_DK, DK_PAD),
        _pad_heads(v, GLA_HEADS, GLA_DV, DV_PAD),
        jnp.pad(g_lr, ((0, 0), (0, LANES - GLA_RANK))),
        _pad_heads(og, GLA_HEADS, GLA_DV, DV_PAD)], axis=-1).astype(_BF16)
    wo = w_out[l]
    wo_o = wo[CONV_W + LRU_W:].reshape(GLA_HEADS, GLA_DV, D_MODEL)
    wo_o = jnp.pad(wo_o, ((0, 0), (0, DV_PAD - GLA_DV), (0, 0))).reshape(V_W, D_MODEL)
    w_out_p = jnp.concatenate([wo[:CONV_W + LRU_W], wo_o], axis=0).astype(_BF16)
    w_gate_p = jnp.pad(_pad_heads(gla_w_gate[l], GLA_HEADS, GLA_DK, DK_PAD),
                       ((0, LANES - GLA_RANK), (0, 0))).astype(_BF16)
    return {
        "g_mix": g_mix[l][None, :],
        "w_in": w_in_p,
        "conv_w": jnp.pad(conv_dw_w[l], ((0, CONV_HIST - CONV_K), (0, 0))),
        "conv_b": conv_dw_b[l][None, :],
        "ln_g": conv_ln_g[l][None, :],
        "ln_b": conv_ln_b[l][None, :],
        "lru_conv_w": lru_conv_w[l],
        "lru_conv_b": lru_conv_b[l][None, :],
        "w_ai": jnp.concatenate([_block_diag(lru_w_a[l]), _block_diag(lru_w_i[l])], axis=1).astype(_BF16),
        "b_ai": jnp.concatenate([lru_b_a[l], lru_b_i[l]])[None, :],
        "lam": lru_lam[l][None, :],
        "w_gate": w_gate_p,
        "b_gate": _pad_heads(gla_b_gate[l], GLA_HEADS, GLA_DK, DK_PAD)[None, :],
        "g_norm": _pad_heads(gla_norm_g[l], GLA_HEADS, GLA_DV, DV_PAD)[None, :],
        "w_out": w_out_p,
    }


def _bucket_layout(counts, n_tiles):
    tm = FFN_TM
    counts = counts.astype(jnp.int32)
    tiles = (counts + tm - 1) // tm
    tile_end = jnp.cumsum(tiles)
    tile_start = tile_end - tiles
    starts = (tile_start * tm).astype(jnp.int32)
    n_used = tile_end[-1]
    tile_idx = jnp.minimum(jnp.arange(n_tiles, dtype=jnp.int32), n_used - 1)
    tile_bucket = jnp.sum((tile_idx[:, None] >= tile_end[None, :]).astype(jnp.int32), axis=1)
    pair_lo = jnp.array([0, 0, 0, 1, 1, 2], jnp.int32)
    pair_hi = jnp.array([1, 2, 3, 2, 3, 3], jnp.int32)
    tile_g = tile_bucket // N_PAIRS
    tile_lo = pair_lo[tile_bucket % N_PAIRS]
    tile_hi = pair_hi[tile_bucket % N_PAIRS]
    return starts, tile_g, tile_lo, tile_hi, n_used.reshape(1).astype(jnp.int32)


def kernel(x, c, w_ada, b_ada, g_mix, w_in, conv_dw_w, conv_dw_b, conv_ln_g, conv_ln_b, lru_conv_w,
           lru_conv_b, lru_w_a, lru_b_a, lru_w_i, lru_b_i, lru_lam, gla_w_gate, gla_b_gate, gla_norm_g,
           w_out, g_ffn, w_route_group, b_route_group, w_route_expert, b_route_expert, w_gate, w_up,
           w_down, g_final):
    bsz, seq, d = x.shape
    n_layers = w_ada.shape[0]
    n_tok = bsz * seq
    assert d == D_MODEL and seq % max(MIX_TS, ROUTE_TT, COMB_TC) == 0 and MIX_TS % GLA_BLOCK == 0
    assert n_tok % PERM_TB == 0
    n_tiles = n_tok // FFN_TM + N_BUCKETS
    n_rows = n_tiles * FFN_TM

    mod_all = _modulation(c, w_ada, b_ada).reshape(n_layers, bsz, 6, d)
    w_gate_b = w_gate.astype(_BF16)
    w_up_b = w_up.astype(_BF16)
    w_down_b = w_down.astype(_BF16)

    for l in range(n_layers):
        p = _prep_layer(l, w_in, conv_dw_w, conv_dw_b, conv_ln_g, conv_ln_b, lru_conv_w, lru_conv_b,
                        lru_w_a, lru_b_a, lru_w_i, lru_b_i, lru_lam, gla_w_gate, gla_b_gate, gla_norm_g,
                        w_out, g_mix)
        mod = mod_all[l]
        x = _token_mix(x, mod, p)

        x2d = x.reshape(n_tok, d)
        w_r = jnp.concatenate(
            [w_route_group[l], w_route_expert[l].transpose(1, 0, 2).reshape(d, N_GROUPS * EXPERTS_PER_GROUP)],
            axis=1)
        n_logits = w_r.shape[1]
        w_r = jnp.pad(w_r, ((0, 0), (0, LANES - n_logits)))
        b_r = jnp.pad(jnp.concatenate([b_route_group[l], b_route_expert[l].reshape(-1)]),
                      (0, LANES - n_logits))[None, :]
        xp, meta, counts = _route(x2d, mod, g_ffn[l][None, :], w_r, b_r, seq)
        bucket = meta[:, 0].astype(jnp.int32)
        rank = meta[:, 1].astype(jnp.int32)
        starts, tile_g, tile_lo, tile_hi, n_used = _bucket_layout(counts[0, :N_BUCKETS], n_tiles)
        in_bucket = bucket[:, None] == jnp.arange(N_BUCKETS, dtype=jnp.int32)[None, :]
        pos = rank + jnp.sum(jnp.where(in_bucket, starts[None, :], 0), axis=1)

        xs = _permute(pos, xp, n_rows)
        y = _expert_ffn(tile_g, tile_lo, tile_hi, n_used, xs, w_gate_b[l], w_up_b[l], w_down_b[l])
        x2d = _combine(pos, x2d, mod, g_final[None, :], y, seq,
                       final_norm=(l == n_layers - 1))
        x = x2d.reshape(bsz, seq, d)
    return x
```

```python
import functools

import jax
import jax.numpy as jnp
import numpy as np
from jax import lax
from jax.experimental import pallas as pl
from jax.experimental.pallas import tpu as pltpu

D_MODEL = 1024
CONV_W = 256
LRU_W = 384
GLA_V = 384
CONV_K = 31
LRU_CONV_K = 4
LRU_BLOCKS = 6
LRU_BW = 64
LRU_C = 8.0
GLA_HEADS = 4
GLA_DV = 96
GLA_DK = 48
GLA_RANK = 16
GLA_TAU = 16.0
GLA_CHUNK = 64
N_GROUPS = 4
EXPERTS_PER_GROUP = 4
D_EXPERT = 512
EPS = 1e-6

LANES = 128
SUBLANES = 8
VMEM_LIMIT_BYTES = 56 * 1024 * 1024

DK_PAD = 64
DV_PAD = 128
QK_W = GLA_HEADS * DK_PAD
V_W = GLA_HEADS * DV_PAD
OFF_CVV = 0
OFF_CVG = OFF_CVV + CONV_W
OFF_LRX = OFF_CVG + CONV_W
OFF_LRY = OFF_LRX + LRU_W
OFF_Q = OFF_LRY + LRU_W
OFF_K = OFF_Q + QK_W
OFF_V = OFF_K + QK_W
OFF_GLR = OFF_V + V_W
OFF_OG = OFF_GLR + LANES
N_IN = OFF_OG + V_W
MIX_W = CONV_W + LRU_W + V_W

CONV_HIST = 32
LRU_HIST = 8
GLA_BLOCK = 256

N_PAIRS = 6
N_BUCKETS = N_GROUPS * N_PAIRS
ROW_W = D_MODEL + LANES

MIX_TS = 256
ROUTE_TT = 512
FFN_TM = 256
PERM_TB = 512
COMB_TC = 512
DMA_UNROLL = 8

_F32 = jnp.float32
_BF16 = jnp.bfloat16


def _sigmoid(x):
    return 1.0 / (1.0 + jnp.exp(-x))


def _silu(x):
    return x * _sigmoid(x)


def _dot(a, b):
    return jnp.dot(a, b, preferred_element_type=_F32)


def _dot_nt(a, b):
    return lax.dot_general(a, b, (((1,), (1,)), ((), ())), preferred_element_type=_F32)


def _split3(x):
    hi = x.astype(_BF16)
    r1 = x - hi.astype(_F32)
    mid = r1.astype(_BF16)
    lo = (r1 - mid.astype(_F32)).astype(_BF16)
    return hi, mid, lo


def _mod_kernel(c_ref, w_ref, b_ref, o_ref):
    c_act = _silu(c_ref[...])
    o_ref[0] = _dot(c_act.astype(_BF16), w_ref[0].astype(_BF16)) + b_ref[0]


def _modulation(c, w_ada, b_ada):
    n_layers, d, n = w_ada.shape
    bsz = c.shape[0]
    tn = 1536
    return pl.pallas_call(
        _mod_kernel,
        out_shape=jax.ShapeDtypeStruct((n_layers, bsz, n), _F32),
        grid=(n_layers, n // tn),
        in_specs=[
            pl.BlockSpec((bsz, d), lambda l, j: (0, 0)),
            pl.BlockSpec((1, d, tn), lambda l, j: (l, 0, j)),
            pl.BlockSpec((1, 1, tn), lambda l, j: (l, 0, j)),
        ],
        out_specs=pl.BlockSpec((1, bsz, tn), lambda l, j: (l, 0, j)),
        compiler_params=pltpu.CompilerParams(
            dimension_semantics=("arbitrary", "arbitrary"), vmem_limit_bytes=VMEM_LIMIT_BYTES),
        name="adaln_mod",
    )(c, w_ada, b_ada.reshape(n_layers, 1, n))


def _shift_rows(x, d, fill):
    ts, c = x.shape
    if d % SUBLANES == 0:
        return jnp.concatenate([jnp.full((d, c), fill, x.dtype), x[: ts - d]], axis=0)
    rolled = pltpu.roll(x, d, axis=0)
    row = lax.broadcasted_iota(jnp.int32, x.shape, 0)
    return jnp.where(row >= d, rolled, fill)


def _linear_scan(a, u):
    ts = a.shape[0]
    d = 1
    while d < ts:
        a_sh = _shift_rows(a, d, 1.0)
        u_sh = _shift_rows(u, d, 0.0)
        u = u + a * u_sh
        a = a * a_sh
        d *= 2
    return a, u


def _mix_kernel(x_ref, mod_ref, gmix_ref, win_ref, cw_ref, cb_ref, lng_ref, lnb_ref,
                lcw_ref, lcb_ref, wai_ref, bai_ref, lam_ref, wgate_ref, bgate_ref, gnorm_ref,
                wout_ref, o_ref, ubuf, lbuf, hcar, state):
    ts = x_ref.shape[1]
    s_idx = pl.program_id(1)

    @pl.when(s_idx == 0)
    def _():
        ubuf[0:CONV_HIST, :] = jnp.zeros((CONV_HIST, CONV_W), _F32)
        lbuf[0:LRU_HIST, :] = jnp.zeros((LRU_HIST, LRU_W), _F32)
        hcar[...] = jnp.zeros_like(hcar)
        state[...] = jnp.zeros_like(state)

    x = x_ref[0]
    sh1 = mod_ref[0, 0:1, :]
    sc1 = mod_ref[0, 1:2, :]
    gt1 = mod_ref[0, 2:3, :]
    ms = jnp.mean(x * x, axis=-1, keepdims=True)
    h = (x * lax.rsqrt(ms + EPS) * gmix_ref[...]) * (1.0 + sc1) + sh1
    hb = h.astype(_BF16)

    def proj(off, width):
        return _dot(hb, win_ref[:, off:off + width])

    u = proj(OFF_CVV, CONV_W) * _sigmoid(proj(OFF_CVG, CONV_W))
    ubuf[CONV_HIST:CONV_HIST + ts, :] = u
    acc = jnp.broadcast_to(cb_ref[...], (ts, CONV_W))
    for k in range(CONV_K):
        off = CONV_HIST - (CONV_K - 1) + k
        acc = acc + cw_ref[k:k + 1, :] * ubuf[off:off + ts, :]
    ubuf[0:CONV_HIST, :] = ubuf[ts:ts + CONV_HIST, :]
    mu = jnp.mean(acc, axis=-1, keepdims=True)
    cen = acc - mu
    var = jnp.mean(cen * cen, axis=-1, keepdims=True)
    u_out = _silu(cen * lax.rsqrt(var + EPS) * lng_ref[...] + lnb_ref[...])
    mixed = _dot(u_out.astype(_BF16), wout_ref[0:CONV_W, :])

    lbuf[LRU_HIST:LRU_HIST + ts, :] = proj(OFF_LRX, LRU_W)
    xb = jnp.broadcast_to(lcb_ref[...], (ts, LRU_W))
    for k in range(LRU_CONV_K):
        off = LRU_HIST - (LRU_CONV_K - 1) + k
        xb = xb + lcw_ref[k:k + 1, :] * lbuf[off:off + ts, :]
    lbuf[0:LRU_HIST, :] = lbuf[ts:ts + LRU_HIST, :]
    gates = _dot(xb.astype(_BF16), wai_ref[...]) + bai_ref[...]
    r_gate = _sigmoid(gates[:, 0:LRU_W])
    i_gate = _sigmoid(gates[:, LRU_W:2 * LRU_W])
    lam = lam_ref[...]
    softplus_neg_lam = jnp.maximum(-lam, 0.0) + jnp.log1p(jnp.exp(-jnp.abs(lam)))
    log_a = (-LRU_C) * r_gate * softplus_neg_lam
    a = jnp.exp(log_a)
    mult = jnp.sqrt(jnp.tanh(-log_a) * (a * a + 1.0))
    a_cum, u_scan = _linear_scan(a, mult * (i_gate * xb))
    h_lru = u_scan + a_cum * hcar[0:1, :]
    hcar[...] = jnp.broadcast_to(h_lru[ts - 1:ts, :], hcar.shape)
    r_out = h_lru * jax.nn.gelu(proj(OFF_LRY, LRU_W), approximate=True)
    mixed = mixed + _dot(r_out.astype(_BF16), wout_ref[CONV_W:CONV_W + LRU_W, :])

    zg = proj(OFF_GLR, LANES)
    glog = _dot(zg.astype(_BF16), wgate_ref[...]) + bgate_ref[...]
    lg = (jnp.minimum(glog, 0.0) - jnp.log1p(jnp.exp(-jnp.abs(glog)))) * (1.0 / GLA_TAU)
    zq = proj(OFF_Q, QK_W) * (GLA_DK ** -0.5)
    zk = proj(OFF_K, QK_W)
    zv = proj(OFF_V, V_W)

    nblk = ts // GLA_BLOCK
    cpb = GLA_BLOCK // GLA_CHUNK
    ri = lax.broadcasted_iota(jnp.int32, (GLA_BLOCK, GLA_BLOCK), 0)
    ci = lax.broadcasted_iota(jnp.int32, (GLA_BLOCK, GLA_BLOCK), 1)
    same_chunk = ri // GLA_CHUNK == ci // GLA_CHUNK
    causal = same_chunk & (ci <= ri)
    tri = jnp.where(causal, 1.0, 0.0).astype(_BF16)
    tri_after = jnp.where(same_chunk & (ci > ri), 1.0, 0.0).astype(_BF16)
    sel_r = lax.broadcasted_iota(jnp.int32, (LANES, GLA_BLOCK), 0)
    sel_c = lax.broadcasted_iota(jnp.int32, (LANES, GLA_BLOCK), 1)
    chunk_sel = jnp.where(sel_r == sel_c // GLA_CHUNK, 1.0, 0.0).astype(_BF16)
    qk_lane_head = lax.broadcasted_iota(jnp.int32, (GLA_BLOCK, QK_W), 1) // DK_PAD
    t_lane_chunk = lax.broadcasted_iota(jnp.int32, (QK_W, GLA_BLOCK), 1) // GLA_CHUNK
    st_row_head = lax.broadcasted_iota(jnp.int32, (QK_W, V_W), 0) // DK_PAD
    st_col_head = lax.broadcasted_iota(jnp.int32, (QK_W, V_W), 1) // DV_PAD
    head_diag = st_row_head == st_col_head

    o_blocks = []
    for blk in range(nblk):
        r0 = blk * GLA_BLOCK
        lg_b = lg[r0:r0 + GLA_BLOCK]
        p0, p1, p2 = _split3(lg_b)
        b = _dot(tri, p0) + _dot(tri, p1) + _dot(tri, p2)
        b_rest = _dot(tri_after, p0) + _dot(tri_after, p1) + _dot(tri_after, p2)
        b_tot = _dot(chunk_sel, p0) + _dot(chunk_sel, p1) + _dot(chunk_sel, p2)
        q_in = zq[r0:r0 + GLA_BLOCK] * jnp.exp(b)
        k_blk = zk[r0:r0 + GLA_BLOCK]
        k_in = (k_blk * jnp.exp(-b)).astype(_BF16)
        k_out_t = (k_blk * jnp.exp(b_rest)).T.astype(_BF16)
        v_b = zv[r0:r0 + GLA_BLOCK].astype(_BF16)
        q_in_b = q_in.astype(_BF16)
        decay_cols = jnp.exp(b_tot).T

        o_heads = []
        for hd in range(GLA_HEADS):
            q_h = jnp.where(qk_lane_head == hd, q_in_b, jnp.zeros_like(q_in_b))
            sc = jnp.where(causal, _dot_nt(q_h, k_in), 0.0)
            o_heads.append(_dot(sc.astype(_BF16), v_b[:, hd * DV_PAD:(hd + 1) * DV_PAD]))
        o_intra = jnp.concatenate(o_heads, axis=1)

        o_inter = []
        st = state[...]
        for c in range(cpb):
            c0 = c * GLA_CHUNK
            o_inter.append(_dot(q_in_b[c0:c0 + GLA_CHUNK], st.astype(_BF16)))
            kv = _dot(jnp.where(t_lane_chunk == c, k_out_t, jnp.zeros_like(k_out_t)), v_b)
            st = jnp.where(head_diag, st * decay_cols[:, c:c + 1] + kv, 0.0)
        state[...] = st
        o_blocks.append(o_intra + jnp.concatenate(o_inter, axis=0))
    o = o_blocks[0] if nblk == 1 else jnp.concatenate(o_blocks, axis=0)

    og = proj(OFF_OG, V_W)
    o_parts = []
    for hd in range(GLA_HEADS):
        o_h = o[:, hd * DV_PAD:(hd + 1) * DV_PAD]
        ms_h = jnp.sum(o_h * o_h, axis=-1, keepdims=True) * (1.0 / GLA_DV)
        o_parts.append(o_h * lax.rsqrt(ms_h + EPS))
    o_n = jnp.concatenate(o_parts, axis=1) * gnorm_ref[...]
    o_g = o_n * _silu(og)
    mixed = mixed + _dot(o_g.astype(_BF16), wout_ref[CONV_W + LRU_W:MIX_W, :])

    o_ref[0] = x + gt1 * mixed


def _token_mix(x, mod, p):
    bsz, seq, d = x.shape
    ts = MIX_TS
    full = lambda shape: pl.BlockSpec(shape, lambda b, s: (0,) * len(shape))
    return pl.pallas_call(
        _mix_kernel,
        out_shape=jax.ShapeDtypeStruct(x.shape, _F32),
        grid=(bsz, seq // ts),
        in_specs=[
            pl.BlockSpec((1, ts, d), lambda b, s: (b, s, 0)),
            pl.BlockSpec((1, 6, d), lambda b, s: (b, 0, 0)),
            full((1, d)),
            full((d, N_IN)),
            full((CONV_HIST, CONV_W)), full((1, CONV_W)), full((1, CONV_W)), full((1, CONV_W)),
            full((LRU_CONV_K, LRU_W)), full((1, LRU_W)),
            full((LRU_W, 2 * LRU_W)), full((1, 2 * LRU_W)), full((1, LRU_W)),
            full((LANES, QK_W)), full((1, QK_W)), full((1, V_W)),
            full((MIX_W, d)),
        ],
        out_specs=pl.BlockSpec((1, ts, d), lambda b, s: (b, s, 0)),
        scratch_shapes=[
            pltpu.VMEM((CONV_HIST + ts, CONV_W), _F32),
            pltpu.VMEM((LRU_HIST + ts, LRU_W), _F32),
            pltpu.VMEM((SUBLANES, LRU_W), _F32),
            pltpu.VMEM((QK_W, V_W), _F32),
        ],
        compiler_params=pltpu.CompilerParams(
            dimension_semantics=("arbitrary", "arbitrary"), vmem_limit_bytes=VMEM_LIMIT_BYTES),
        name="token_mix",
    )(x, mod, p["g_mix"], p["w_in"], p["conv_w"], p["conv_b"], p["ln_g"], p["ln_b"],
      p["lru_conv_w"], p["lru_conv_b"], p["w_ai"], p["b_ai"], p["lam"], p["w_gate"], p["b_gate"],
      p["g_norm"], p["w_out"])


def _route_kernel(x_ref, mod_ref, g_ref, wr_ref, br_ref, xp_ref, meta_ref, cnt_ref, carry):
    tt = x_ref.shape[0]
    i = pl.program_id(0)

    @pl.when(i == 0)
    def _():
        carry[...] = jnp.zeros_like(carry)

    x = x_ref[...]
    sh2 = mod_ref[0, 3:4, :]
    sc2 = mod_ref[0, 4:5, :]
    ms = jnp.mean(x * x, axis=-1, keepdims=True)
    h = (x * lax.rsqrt(ms + EPS) * g_ref[...]) * (1.0 + sc2) + sh2

    h0, h1, h2 = _split3(h)
    w0, w1, w2 = _split3(wr_ref[...])
    logits = (_dot(h0, w0) + (_dot(h0, w1) + _dot(h1, w0))
              + (_dot(h0, w2) + _dot(h1, w1) + _dot(h2, w0))) + br_ref[...]

    lane = lax.broadcasted_iota(jnp.int32, (tt, LANES), 1)
    lane_f = lane.astype(_F32)
    neg = -jnp.inf
    big = float(LANES)

    def first_argmax(vals, vmax):
        return jnp.min(jnp.where(vals == vmax, lane_f, big), axis=-1, keepdims=True).astype(jnp.int32)

    gl = jnp.where(lane < N_GROUPS, logits, neg)
    gmax = jnp.max(gl, axis=-1, keepdims=True)
    g_star = first_argmax(gl, gmax)
    p_sel = 1.0 / jnp.sum(jnp.exp(gl - gmax), axis=-1, keepdims=True)
    base = N_GROUPS + EXPERTS_PER_GROUP * g_star
    el = jnp.where((lane >= base) & (lane < base + EXPERTS_PER_GROUP), logits, neg)
    v0 = jnp.max(el, axis=-1, keepdims=True)
    i0 = first_argmax(el, v0)
    el2 = jnp.where(lane == i0, neg, el)
    v1 = jnp.max(el2, axis=-1, keepdims=True)
    i1 = first_argmax(el2, v1)
    ex = jnp.exp(v1 - v0)
    wt0 = p_sel / (1.0 + ex)
    wt1 = p_sel * ex / (1.0 + ex)
    e0 = i0 - base
    e1 = i1 - base
    e_lo = jnp.minimum(e0, e1)
    e_hi = jnp.maximum(e0, e1)
    w_lo = jnp.where(e0 < e1, wt0, wt1)
    w_hi = jnp.where(e0 < e1, wt1, wt0)
    pair = (e_lo * (2 * EXPERTS_PER_GROUP - 1 - e_lo)) // 2 + (e_hi - e_lo - 1)
    bucket = g_star * N_PAIRS + pair

    onehot = lane == bucket
    onehot_f = jnp.where(onehot, 1.0, 0.0)
    ri = lax.broadcasted_iota(jnp.int32, (tt, tt), 0)
    ci = lax.broadcasted_iota(jnp.int32, (tt, tt), 1)
    strict = jnp.where(ci < ri, 1.0, 0.0).astype(_BF16)
    prefix = _dot(strict, onehot_f.astype(_BF16)) + carry[0:1, :]
    rank = jnp.sum(jnp.where(onehot, prefix, 0.0), axis=-1, keepdims=True)
    carry[...] = carry[...] + jnp.sum(onehot_f, axis=0, keepdims=True)
    cnt_ref[...] = carry[...]

    meta = jnp.where(lane == 0, bucket.astype(_F32), jnp.where(lane == 1, rank, 0.0))
    meta_ref[...] = meta

    xp_ref[:, 0:D_MODEL] = h
    xp_ref[:, D_MODEL:ROW_W] = jnp.where(lane == 0, w_lo, jnp.where(lane == 1, w_hi, 0.0))


def _route(x2d, mod, g_ffn, w_r, b_r, seq):
    n_tok, d = x2d.shape
    tt = ROUTE_TT
    tiles_per_seq = seq // tt
    return pl.pallas_call(
        _route_kernel,
        out_shape=(
            jax.ShapeDtypeStruct((n_tok, ROW_W), _F32),
            jax.ShapeDtypeStruct((n_tok, LANES), _F32),
            jax.ShapeDtypeStruct((SUBLANES, LANES), _F32),
        ),
        grid=(n_tok // tt,),
        in_specs=[
            pl.BlockSpec((tt, d), lambda i: (i, 0)),
            pl.BlockSpec((1, 6, d), lambda i: (i // tiles_per_seq, 0, 0)),
            pl.BlockSpec((1, d), lambda i: (0, 0)),
            pl.BlockSpec((d, LANES), lambda i: (0, 0)),
            pl.BlockSpec((1, LANES), lambda i: (0, 0)),
        ],
        out_specs=(
            pl.BlockSpec((tt, ROW_W), lambda i: (i, 0)),
            pl.BlockSpec((tt, LANES), lambda i: (i, 0)),
            pl.BlockSpec((SUBLANES, LANES), lambda i: (0, 0)),
        ),
        scratch_shapes=[pltpu.VMEM((SUBLANES, LANES), _F32)],
        compiler_params=pltpu.CompilerParams(
            dimension_semantics=("arbitrary",), vmem_limit_bytes=VMEM_LIMIT_BYTES),
        name="moe_route",
    )(x2d, mod, g_ffn, w_r, b_r)


def _permute_kernel(fill_ref, pos_ref, xp_ref, xs_hbm, zbuf, sem, zsem):
    tb = pos_ref.shape[0]

    @pl.when(pl.program_id(0) == 0)
    def _():
        zbuf[...] = jnp.zeros_like(zbuf)
        n_used = fill_ref[2 * N_BUCKETS]
        n_tiles = xs_hbm.shape[0] // FFN_TM
        for wait in (False, True):
            for b in range(N_BUCKETS):
                for row0, cond in ((fill_ref[b], fill_ref[N_BUCKETS + b] > 0),
                                   ((n_used + b) * FFN_TM, n_used + b < n_tiles)):
                    @pl.when(cond)
                    def _():
                        dst = xs_hbm.at[pl.ds(pl.multiple_of(row0, FFN_TM), FFN_TM)]
                        fill = pltpu.make_async_copy(zbuf, dst, zsem)
                        fill.wait() if wait else fill.start()

    def issue(j, carry):
        pltpu.make_async_copy(xp_ref.at[pl.ds(j, 1)], xs_hbm.at[pl.ds(pos_ref[j], 1)], sem).start()
        return carry

    lax.fori_loop(0, tb, issue, 0, unroll=DMA_UNROLL)
    pltpu.make_async_copy(xp_ref, xs_hbm.at[pl.ds(0, tb)], sem).wait()


def _permute(fill_starts, pos, xp, n_rows):
    n_tok = xp.shape[0]
    tb = PERM_TB
    return pl.pallas_call(
        _permute_kernel,
        out_shape=jax.ShapeDtypeStruct((n_rows, ROW_W), _F32),
        grid_spec=pltpu.PrefetchScalarGridSpec(
            num_scalar_prefetch=1,
            grid=(n_tok // tb,),
            in_specs=[
                pl.BlockSpec((tb,), lambda i, fs: (i,), memory_space=pltpu.SMEM),
                pl.BlockSpec((tb, ROW_W), lambda i, fs: (i, 0)),
            ],
            out_specs=pl.BlockSpec(memory_space=pl.ANY),
            scratch_shapes=[pltpu.VMEM((FFN_TM, ROW_W), _F32), pltpu.SemaphoreType.DMA,
                            pltpu.SemaphoreType.DMA],
        ),
        compiler_params=pltpu.CompilerParams(
            dimension_semantics=("arbitrary",), vmem_limit_bytes=VMEM_LIMIT_BYTES),
        name="moe_permute",
    )(fill_starts, pos, xp)


def _ffn_kernel(tg_ref, tlo_ref, thi_ref, tnew_ref, nused_ref, xs_ref, wg0, wu0, wd0, wg1, wu1, wd1, y_ref,
                *wb):
    del tg_ref, tlo_ref, thi_ref
    i = pl.program_id(0)

    @pl.when(tnew_ref[i] == 1)
    def _():
        for src, dst in zip((wg0, wu0, wd0, wg1, wu1, wd1), wb):
            dst[...] = src[...].astype(_BF16)

    @pl.when(i < nused_ref[0])
    def _():
        xb = xs_ref[:, 0:D_MODEL].astype(_BF16)
        info = xs_ref[:, D_MODEL:ROW_W]

        def expert(wg, wu, wd, wt):
            hid = _silu(_dot(xb, wg[...])) * _dot(xb, wu[...]) * wt
            return _dot(hid.astype(_BF16), wd[...])

        y_ref[...] = (expert(wb[0], wb[1], wb[2], info[:, 0:1]) + expert(wb[3], wb[4], wb[5], info[:, 1:2]))

    @pl.when(i >= nused_ref[0])
    def _():
        y_ref[...] = jnp.zeros_like(y_ref)


def _expert_ffn(layer, tile_g, tile_lo, tile_hi, tile_new, n_used, xs, w_gate, w_up, w_down):
    n_rows = xs.shape[0]
    tm = FFN_TM
    n_tiles = n_rows // tm

    def row_map(i, tg, tlo, thi, tnew, nu):
        return (jnp.minimum(i, nu[0] - 1), 0)

    def w_map(which):
        def f(i, tg, tlo, thi, tnew, nu):
            return (layer, tg[i], (tlo if which == 0 else thi)[i], 0, 0)
        return f

    wspec_in = lambda which: pl.BlockSpec((None, None, None, D_MODEL, D_EXPERT), w_map(which))
    wspec_out = lambda which: pl.BlockSpec((None, None, None, D_EXPERT, D_MODEL), w_map(which))
    w_in_scratch = pltpu.VMEM((D_MODEL, D_EXPERT), _BF16)
    w_out_scratch = pltpu.VMEM((D_EXPERT, D_MODEL), _BF16)
    return pl.pallas_call(
        _ffn_kernel,
        out_shape=jax.ShapeDtypeStruct((n_rows, D_MODEL), _F32),
        grid_spec=pltpu.PrefetchScalarGridSpec(
            num_scalar_prefetch=5,
            grid=(n_tiles,),
            in_specs=[
                pl.BlockSpec((tm, ROW_W), row_map),
                wspec_in(0), wspec_in(0), wspec_out(0),
                wspec_in(1), wspec_in(1), wspec_out(1),
            ],
            out_specs=pl.BlockSpec((tm, D_MODEL), lambda i, tg, tlo, thi, tnew, nu: (i, 0)),
            scratch_shapes=[w_in_scratch, w_in_scratch, w_out_scratch,
                            w_in_scratch, w_in_scratch, w_out_scratch],
        ),
        compiler_params=pltpu.CompilerParams(
            dimension_semantics=("arbitrary",), vmem_limit_bytes=VMEM_LIMIT_BYTES),
        name="moe_ffn",
    )(tile_g, tile_lo, tile_hi, tile_new, n_used, xs, w_gate, w_up, w_down, w_gate, w_up, w_down)


def _combine_kernel(pos_ref, x_ref, mod_ref, gfin_ref, y_hbm, o_ref, ybuf, sem, *, final_norm):
    tc = x_ref.shape[0]

    def issue(j, carry):
        pltpu.make_async_copy(y_hbm.at[pl.ds(pos_ref[j], 1)], ybuf.at[pl.ds(j, 1)], sem).start()
        return carry

    lax.fori_loop(0, tc, issue, 0, unroll=DMA_UNROLL)
    pltpu.make_async_copy(y_hbm.at[pl.ds(0, tc)], ybuf, sem).wait()

    gt2 = mod_ref[0, 5:6, :]
    out = x_ref[...] + gt2 * ybuf[...]
    if final_norm:
        ms = jnp.mean(out * out, axis=-1, keepdims=True)
        out = out * lax.rsqrt(ms + EPS) * gfin_ref[...]
    o_ref[...] = out


def _combine(pos, x2d, mod, g_final, y, seq, final_norm):
    n_tok, d = x2d.shape
    tc = COMB_TC
    tiles_per_seq = seq // tc
    return pl.pallas_call(
        functools.partial(_combine_kernel, final_norm=final_norm),
        out_shape=jax.ShapeDtypeStruct((n_tok, d), _F32),
        grid=(n_tok // tc,),
        in_specs=[
            pl.BlockSpec((tc,), lambda i: (i,), memory_space=pltpu.SMEM),
            pl.BlockSpec((tc, d), lambda i: (i, 0)),
            pl.BlockSpec((1, 6, d), lambda i: (i // tiles_per_seq, 0, 0)),
            pl.BlockSpec((1, d), lambda i: (0, 0)),
            pl.BlockSpec(memory_space=pl.ANY),
        ],
        out_specs=pl.BlockSpec((tc, d), lambda i: (i, 0)),
        scratch_shapes=[pltpu.VMEM((tc, d), _F32), pltpu.SemaphoreType.DMA],
        compiler_params=pltpu.CompilerParams(
            dimension_semantics=("arbitrary",), vmem_limit_bytes=VMEM_LIMIT_BYTES),
        name="moe_combine",
    )(pos, x2d, mod, g_final, y)


def _pad_heads(w, heads, width, padded):
    lead = w.shape[:-1]
    w = w.reshape(lead + (heads, width))
    w = jnp.pad(w, [(0, 0)] * len(lead) + [(0, 0), (0, padded - width)])
    return w.reshape(lead + (heads * padded,))


def _block_diag(w):
    n, bw, _ = w.shape
    eye = jnp.eye(n, dtype=w.dtype)
    return (eye[:, None, :, None] * w[:, :, None, :]).reshape(n * bw, n * bw)


def _prep_layer(l, w_in, conv_dw_w, conv_dw_b, conv_ln_g, conv_ln_b, lru_conv_w, lru_conv_b, lru_w_a,
                lru_b_a, lru_w_i, lru_b_i, lru_lam, gla_w_gate, gla_b_gate, gla_norm_g, w_out, g_mix):
    sizes = [CONV_W, CONV_W, LRU_W, LRU_W, GLA_HEADS * GLA_DK, GLA_HEADS * GLA_DK, GLA_V, GLA_RANK, GLA_V]
    cv_v, cv_g, lr_x, lr_y, q, k, v, g_lr, og = jnp.split(w_in[l], np.cumsum(sizes)[:-1].tolist(), axis=-1)
    w_in_p = jnp.concatenate([
        cv_v, cv_g, lr_x, lr_y,
        _pad_heads(q, GLA_HEADS, GLA_DK, DK_PAD), _pad_heads(k, GLA_HEADS, GLA_DK, DK_PAD),
        _pad_heads(v, GLA_HEADS, GLA_DV, DV_PAD),
        jnp.pad(g_lr, ((0, 0), (0, LANES - GLA_RANK))),
        _pad_heads(og, GLA_HEADS, GLA_DV, DV_PAD)], axis=-1).astype(_BF16)
    wo = w_out[l]
    wo_o = wo[CONV_W + LRU_W:].reshape(GLA_HEADS, GLA_DV, D_MODEL)
    wo_o = jnp.pad(wo_o, ((0, 0), (0, DV_PAD - GLA_DV), (0, 0))).reshape(V_W, D_MODEL)
    w_out_p = jnp.concatenate([wo[:CONV_W + LRU_W], wo_o], axis=0).astype(_BF16)
    w_gate_p = jnp.pad(_pad_heads(gla_w_gate[l], GLA_HEADS, GLA_DK, DK_PAD),
                       ((0, LANES - GLA_RANK), (0, 0))).astype(_BF16)
    return {
        "g_mix": g_mix[l][None, :],
        "w_in": w_in_p,
        "conv_w": jnp.pad(conv_dw_w[l], ((0, CONV_HIST - CONV_K), (0, 0))),
        "conv_b": conv_dw_b[l][None, :],
        "ln_g": conv_ln_g[l][None, :],
        "ln_b": conv_ln_b[l][None, :],
        "lru_conv_w": lru_conv_w[l],
        "lru_conv_b": lru_conv_b[l][None, :],
        "w_ai": jnp.concatenate([_block_diag(lru_w_a[l]), _block_diag(lru_w_i[l])], axis=1).astype(_BF16),
        "b_ai": jnp.concatenate([lru_b_a[l], lru_b_i[l]])[None, :],
        "lam": lru_lam[l][None, :],
        "w_gate": w_gate_p,
        "b_gate": _pad_heads(gla_b_gate[l], GLA_HEADS, GLA_DK, DK_PAD)[None, :],
        "g_norm": _pad_heads(gla_norm_g[l], GLA_HEADS, GLA_DV, DV_PAD)[None, :],
        "w_out": w_out_p,
    }


def _bucket_layout(counts, n_tiles):
    tm = FFN_TM
    counts = counts.astype(jnp.int32)
    tiles = (counts + tm - 1) // tm
    tile_end = jnp.cumsum(tiles)
    tile_start = tile_end - tiles
    starts = (tile_start * tm).astype(jnp.int32)
    n_used = tile_end[-1]
    tile_idx = jnp.minimum(jnp.arange(n_tiles, dtype=jnp.int32), n_used - 1)
    tile_bucket = jnp.sum((tile_idx[:, None] >= tile_end[None, :]).astype(jnp.int32), axis=1)
    pair_lo = jnp.array([0, 0, 0, 1, 1, 2], jnp.int32)
    pair_hi = jnp.array([1, 2, 3, 2, 3, 3], jnp.int32)
    tile_g = tile_bucket // N_PAIRS
    tile_lo = pair_lo[tile_bucket % N_PAIRS]
    tile_hi = pair_hi[tile_bucket % N_PAIRS]
    tile_new = jnp.concatenate(
        [jnp.ones((1,), jnp.int32), (tile_bucket[1:] != tile_bucket[:-1]).astype(jnp.int32)])
    fill = jnp.concatenate([(jnp.maximum(tile_end - 1, 0) * tm).astype(jnp.int32), tiles, n_used[None]])
    return starts, tile_g, tile_lo, tile_hi, tile_new, fill, n_used.reshape(1).astype(jnp.int32)


def kernel(x, c, w_ada, b_ada, g_mix, w_in, conv_dw_w, conv_dw_b, conv_ln_g, conv_ln_b, lru_conv_w,
           lru_conv_b, lru_w_a, lru_b_a, lru_w_i, lru_b_i, lru_lam, gla_w_gate, gla_b_gate, gla_norm_g,
           w_out, g_ffn, w_route_group, b_route_group, w_route_expert, b_route_expert, w_gate, w_up,
           w_down, g_final):
    bsz, seq, d = x.shape
    n_layers = w_ada.shape[0]
    n_tok = bsz * seq
    assert d == D_MODEL and seq % max(MIX_TS, ROUTE_TT, COMB_TC) == 0 and MIX_TS % GLA_BLOCK == 0
    assert n_tok % PERM_TB == 0
    n_tiles = n_tok // FFN_TM + N_BUCKETS
    n_rows = n_tiles * FFN_TM

    mod_all = _modulation(c, w_ada, b_ada).reshape(n_layers, bsz, 6, d)

    for l in range(n_layers):
        p = _prep_layer(l, w_in, conv_dw_w, conv_dw_b, conv_ln_g, conv_ln_b, lru_conv_w, lru_conv_b,
                        lru_w_a, lru_b_a, lru_w_i, lru_b_i, lru_lam, gla_w_gate, gla_b_gate, gla_norm_g,
                        w_out, g_mix)
        mod = mod_all[l]
        x = _token_mix(x, mod, p)

        x2d = x.reshape(n_tok, d)
        w_r = jnp.concatenate(
            [w_route_group[l], w_route_expert[l].transpose(1, 0, 2).reshape(d, N_GROUPS * EXPERTS_PER_GROUP)],
            axis=1)
        n_logits = w_r.shape[1]
        w_r = jnp.pad(w_r, ((0, 0), (0, LANES - n_logits)))
        b_r = jnp.pad(jnp.concatenate([b_route_group[l], b_route_expert[l].reshape(-1)]),
                      (0, LANES - n_logits))[None, :]
        xp, meta, counts = _route(x2d, mod, g_ffn[l][None, :], w_r, b_r, seq)
        bucket = meta[:, 0].astype(jnp.int32)
        rank = meta[:, 1].astype(jnp.int32)
        starts, tile_g, tile_lo, tile_hi, tile_new, fill, n_used = _bucket_layout(
            counts[0, :N_BUCKETS], n_tiles)
        in_bucket = bucket[:, None] == jnp.arange(N_BUCKETS, dtype=jnp.int32)[None, :]
        pos = rank + jnp.sum(jnp.where(in_bucket, starts[None, :], 0), axis=1)

        xs = _permute(fill, pos, xp, n_rows)
        y = _expert_ffn(l, tile_g, tile_lo, tile_hi, tile_new, n_used, xs, w_gate, w_up, w_down)
        x2d = _combine(pos, x2d, mod, g_final[None, :], y, seq,
                       final_norm=(l == n_layers - 1))
        x = x2d.reshape(bsz, seq, d)
    return x
```

```python
import functools

import jax
import jax.numpy as jnp
import numpy as np
from jax import lax
from jax.experimental import pallas as pl
from jax.experimental.pallas import tpu as pltpu

D_MODEL = 1024
CONV_W = 256
LRU_W = 384
GLA_V = 384
CONV_K = 31
LRU_CONV_K = 4
LRU_BLOCKS = 6
LRU_BW = 64
LRU_C = 8.0
GLA_HEADS = 4
GLA_DV = 96
GLA_DK = 48
GLA_RANK = 16
GLA_TAU = 16.0
GLA_CHUNK = 64
N_GROUPS = 4
EXPERTS_PER_GROUP = 4
D_EXPERT = 512
EPS = 1e-6

LANES = 128
SUBLANES = 8
VMEM_LIMIT_BYTES = 56 * 1024 * 1024

DK_PAD = 64
DV_PAD = 128
QK_W = GLA_HEADS * DK_PAD
V_W = GLA_HEADS * DV_PAD
OFF_CVV = 0
OFF_CVG = OFF_CVV + CONV_W
OFF_LRX = OFF_CVG + CONV_W
OFF_LRY = OFF_LRX + LRU_W
OFF_Q = OFF_LRY + LRU_W
OFF_K = OFF_Q + QK_W
OFF_V = OFF_K + QK_W
OFF_GLR = OFF_V + V_W
OFF_OG = OFF_GLR + LANES
N_IN = OFF_OG + V_W
MIX_W = CONV_W + LRU_W + V_W

CONV_HIST = 32
LRU_HIST = 8
GLA_BLOCK = 256

N_PAIRS = 6
N_BUCKETS = N_GROUPS * N_PAIRS
ROW_W = D_MODEL + LANES

MIX_TS = 256
ROUTE_TT = 512
FFN_TM = 256
PERM_TB = 512
COMB_TC = 512
DMA_UNROLL = 8

_F32 = jnp.float32
_BF16 = jnp.bfloat16


def _sigmoid(x):
    return 1.0 / (1.0 + jnp.exp(-x))


def _silu(x):
    return x * _sigmoid(x)


def _dot(a, b):
    return jnp.dot(a, b, preferred_element_type=_F32)


def _dot_nt(a, b):
    return lax.dot_general(a, b, (((1,), (1,)), ((), ())), preferred_element_type=_F32)


def _split3(x):
    hi = x.astype(_BF16)
    r1 = x - hi.astype(_F32)
    mid = r1.astype(_BF16)
    lo = (r1 - mid.astype(_F32)).astype(_BF16)
    return hi, mid, lo


def _mod_kernel(c_ref, w_ref, b_ref, o_ref):
    c_act = _silu(c_ref[...])
    o_ref[0] = _dot(c_act.astype(_BF16), w_ref[0].astype(_BF16)) + b_ref[0]


def _modulation(c, w_ada, b_ada):
    n_layers, d, n = w_ada.shape
    bsz = c.shape[0]
    tn = 1536
    return pl.pallas_call(
        _mod_kernel,
        out_shape=jax.ShapeDtypeStruct((n_layers, bsz, n), _F32),
        grid=(n_layers, n // tn),
        in_specs=[
            pl.BlockSpec((bsz, d), lambda l, j: (0, 0)),
            pl.BlockSpec((1, d, tn), lambda l, j: (l, 0, j)),
            pl.BlockSpec((1, 1, tn), lambda l, j: (l, 0, j)),
        ],
        out_specs=pl.BlockSpec((1, bsz, tn), lambda l, j: (l, 0, j)),
        compiler_params=pltpu.CompilerParams(
            dimension_semantics=("arbitrary", "arbitrary"), vmem_limit_bytes=VMEM_LIMIT_BYTES),
        name="adaln_mod",
    )(c, w_ada, b_ada.reshape(n_layers, 1, n))


def _mix_kernel(x_ref, mod_ref, gmix_ref, win_ref, cw_ref, cb_ref, lng_ref, lnb_ref,
                lcw_ref, lcb_ref, wai_ref, bai_ref, lam_ref, wgate_ref, bgate_ref, gnorm_ref,
                wout_ref, *rest, fused_moe_gather):
    if fused_moe_gather:
        pos_cur, pos_nxt, modp_ref, y_hbm, o_ref = rest[:5]
        ybuf, gsem = rest[-2:]
        rest = rest[5:-2]
    else:
        o_ref, rest = rest[0], rest[1:]
    ubuf, lbuf, hcar, state, tri_ref, tria_ref, csel_ref, causal_ref, hdiag_ref = rest
    ts = x_ref.shape[1]
    s_idx = pl.program_id(1)

    @pl.when(s_idx == 0)
    def _():
        ubuf[0:CONV_HIST, :] = jnp.zeros((CONV_HIST, CONV_W), _F32)
        lbuf[0:LRU_HIST, :] = jnp.zeros((LRU_HIST, LRU_W), _F32)
        hcar[...] = jnp.zeros_like(hcar)
        state[...] = jnp.zeros_like(state)
        ri = lax.broadcasted_iota(jnp.int32, (GLA_BLOCK, GLA_BLOCK), 0)
        ci = lax.broadcasted_iota(jnp.int32, (GLA_BLOCK, GLA_BLOCK), 1)
        same_chunk = ri // GLA_CHUNK == ci // GLA_CHUNK
        causal_f = jnp.where(same_chunk & (ci <= ri), 1.0, 0.0)
        causal_ref[...] = causal_f
        tri_ref[...] = causal_f.astype(_BF16)
        tria_ref[...] = jnp.where(same_chunk & (ci > ri), 1.0, 0.0).astype(_BF16)
        sel_r = lax.broadcasted_iota(jnp.int32, (LANES, GLA_BLOCK), 0)
        sel_c = lax.broadcasted_iota(jnp.int32, (LANES, GLA_BLOCK), 1)
        csel_ref[...] = jnp.where(sel_r == sel_c // GLA_CHUNK, 1.0, 0.0).astype(_BF16)
        st_row_head = lax.broadcasted_iota(jnp.int32, (QK_W, V_W), 0) // DK_PAD
        st_col_head = lax.broadcasted_iota(jnp.int32, (QK_W, V_W), 1) // DV_PAD
        hdiag_ref[...] = jnp.where(st_row_head == st_col_head, 1.0, 0.0)

    x = x_ref[0]
    if fused_moe_gather:
        step = pl.program_id(0) * pl.num_programs(1) + s_idx
        n_steps = pl.num_programs(0) * pl.num_programs(1)
        slot = step % 2

        def row_copy(pos_ref, j, to_slot):
            return pltpu.make_async_copy(y_hbm.at[pl.ds(pos_ref[j], 1)], ybuf.at[to_slot, pl.ds(j, 1)],
                                         gsem.at[to_slot])

        def wait_rows(of_slot):
            pltpu.make_async_copy(y_hbm.at[pl.ds(0, ts)], ybuf.at[of_slot], gsem.at[of_slot]).wait()

        @pl.when(step == 0)
        def _():
            def issue(j, carry):
                row_copy(pos_cur, j, 0).start()
                return carry
            lax.fori_loop(0, ts, issue, 0, unroll=DMA_UNROLL)

        wait_rows(slot)
        x = x + modp_ref[0, 5:6, :] * ybuf[slot]

        for j in range(ts):
            row_copy(pos_nxt, j, 1 - slot).start()

    sh1 = mod_ref[0, 0:1, :]
    sc1 = mod_ref[0, 1:2, :]
    gt1 = mod_ref[0, 2:3, :]
    ms = jnp.mean(x * x, axis=-1, keepdims=True)
    h = (x * lax.rsqrt(ms + EPS) * gmix_ref[...]) * (1.0 + sc1) + sh1
    hb = h.astype(_BF16)

    def proj(off, width):
        return _dot(hb, win_ref[:, off:off + width])

    out = {}
    _run_alternately(
        _conv_lru_steps(out, proj, cw_ref, cb_ref, lng_ref, lnb_ref, lcw_ref, lcb_ref, wai_ref, bai_ref,
                        lam_ref, ubuf, lbuf, hcar, ts),
        _gla_steps(out, proj, wgate_ref, bgate_ref, gnorm_ref, state, tri_ref, tria_ref, csel_ref,
                   causal_ref, hdiag_ref, ts))

    mixed = (_dot(out["o_g"].astype(_BF16), wout_ref[CONV_W + LRU_W:MIX_W, :])
             + _dot(out["r_out"].astype(_BF16), wout_ref[CONV_W:CONV_W + LRU_W, :])
             + _dot(out["u_out"].astype(_BF16), wout_ref[0:CONV_W, :]))
    o_ref[0] = x + gt1 * mixed

    if fused_moe_gather:
        @pl.when(step == n_steps - 1)
        def _():
            wait_rows(1 - slot)


def _run_alternately(*step_generators):
    clock = [0] * len(step_generators)
    live = list(range(len(step_generators)))
    while live:
        i = min(live, key=lambda j: clock[j])
        try:
            clock[i] += next(step_generators[i]) * (1 if i else 0.5)
        except StopIteration:
            live.remove(i)


def _conv_lru_steps(out, proj, cw_ref, cb_ref, lng_ref, lnb_ref, lcw_ref, lcb_ref, wai_ref, bai_ref,
                    lam_ref, ubuf, lbuf, hcar, ts):
    u = proj(OFF_CVV, CONV_W) * _sigmoid(proj(OFF_CVG, CONV_W))
    ubuf[CONV_HIST:CONV_HIST + ts, :] = u
    yield 500
    lbuf[LRU_HIST:LRU_HIST + ts, :] = proj(OFF_LRX, LRU_W)
    xb = jnp.broadcast_to(lcb_ref[...], (ts, LRU_W))
    for k in range(LRU_CONV_K):
        off = LRU_HIST - (LRU_CONV_K - 1) + k
        xb = xb + lcw_ref[k:k + 1, :] * lbuf[off:off + ts, :]
    lbuf[0:LRU_HIST, :] = lbuf[ts:ts + LRU_HIST, :]
    gates = _dot(xb.astype(_BF16), wai_ref[...]) + bai_ref[...]
    yield 800

    acc = jnp.broadcast_to(cb_ref[...], (ts, CONV_W))
    for r in range(SUBLANES):
        z = None
        for a_blk in range((CONV_K - 1 - r) // SUBLANES + 1):
            k = CONV_K - 1 - (SUBLANES * a_blk + r)
            off = CONV_HIST - SUBLANES * (a_blk + 1)
            term = cw_ref[k:k + 1, :] * ubuf[off:off + ts + SUBLANES, :]
            z = term if z is None else z + term
        acc = acc + z[SUBLANES - r:SUBLANES - r + ts]
        yield 200
    ubuf[0:CONV_HIST, :] = ubuf[ts:ts + CONV_HIST, :]
    mu = jnp.mean(acc, axis=-1, keepdims=True)
    cen = acc - mu
    var = jnp.mean(cen * cen, axis=-1, keepdims=True)
    out["u_out"] = _silu(cen * lax.rsqrt(var + EPS) * lng_ref[...] + lnb_ref[...])
    yield 250

    r_gate = _sigmoid(gates[:, 0:LRU_W])
    i_gate = _sigmoid(gates[:, LRU_W:2 * LRU_W])
    lam = lam_ref[...]
    softplus_neg_lam = jnp.maximum(-lam, 0.0) + jnp.log1p(jnp.exp(-jnp.abs(lam)))
    log_a = (-LRU_C) * r_gate * softplus_neg_lam
    a = jnp.exp(log_a)
    mult = jnp.sqrt(jnp.tanh(-log_a) * (a * a + 1.0))
    u_in = mult * (i_gate * xb)
    yield 600
    row = lax.broadcasted_iota(jnp.int32, a.shape, 0) % SUBLANES
    d = 1
    while d < SUBLANES:
        keep = row >= d
        a_sh = jnp.where(keep, pltpu.roll(a, d, axis=0), 1.0)
        u_sh = jnp.where(keep, pltpu.roll(u_in, d, axis=0), 0.0)
        u_in = u_in + a * u_sh
        a = a * a_sh
        d *= 2
        yield 200
    groups = []
    h_prev = hcar[0:1, :]
    for g in range(ts // SUBLANES):
        r0 = g * SUBLANES
        h_g = u_in[r0:r0 + SUBLANES] + a[r0:r0 + SUBLANES] * h_prev
        groups.append(h_g)
        h_prev = h_g[SUBLANES - 1:SUBLANES]
        if g % SUBLANES == SUBLANES - 1:
            yield 50
    h_lru = jnp.concatenate(groups, axis=0)
    hcar[...] = jnp.broadcast_to(h_prev, hcar.shape)
    out["r_out"] = h_lru * jax.nn.gelu(proj(OFF_LRY, LRU_W), approximate=True)


def _gla_steps(out, proj, wgate_ref, bgate_ref, gnorm_ref, state, tri_ref, tria_ref, csel_ref, causal_ref,
               hdiag_ref, ts):
    zg = proj(OFF_GLR, LANES)
    glog = _dot(zg.astype(_BF16), wgate_ref[...]) + bgate_ref[...]
    lg = (jnp.minimum(glog, 0.0) - jnp.log1p(jnp.exp(-jnp.abs(glog)))) * (1.0 / GLA_TAU)
    yield 300
    zq = proj(OFF_Q, QK_W) * (GLA_DK ** -0.5)
    zk = proj(OFF_K, QK_W)
    yield 500
    zv = proj(OFF_V, V_W)
    yield 500

    nblk = ts // GLA_BLOCK
    cpb = GLA_BLOCK // GLA_CHUNK
    tri = tri_ref[...]
    tri_after = tria_ref[...]
    chunk_sel = csel_ref[...]
    causal = causal_ref[...] > 0.5
    qk_lane_head = lax.broadcasted_iota(jnp.int32, (1, QK_W), 1) // DK_PAD
    t_lane_chunk = lax.broadcasted_iota(jnp.int32, (1, GLA_BLOCK), 1) // GLA_CHUNK

    o_blocks = []
    for blk in range(nblk):
        r0 = blk * GLA_BLOCK
        lg_b = lg[r0:r0 + GLA_BLOCK]
        p0, p1, p2 = _split3(lg_b)
        b = _dot(tri, p0) + _dot(tri, p1) + _dot(tri, p2)
        b_rest = _dot(tri_after, p0) + _dot(tri_after, p1) + _dot(tri_after, p2)
        b_tot = _dot(chunk_sel, p0) + _dot(chunk_sel, p1) + _dot(chunk_sel, p2)
        yield 700
        q_in =zq[r0:r0 + GLA_BLOCK] * jnp.exp(b)
        k_blk = zk[r0:r0 + GLA_BLOCK]
        k_in = (k_blk * jnp.exp(-b)).astype(_BF16)
        k_out_t = (k_blk * jnp.exp(b_rest)).T.astype(_BF16)
        v_b = zv[r0:r0 + GLA_BLOCK].astype(_BF16)
        q_in_b = q_in.astype(_BF16)
        decay_cols = jnp.exp(b_tot).T

        o_heads = []
        for hd in range(GLA_HEADS):
            q_h = jnp.where(qk_lane_head == hd, q_in_b, jnp.zeros_like(q_in_b))
            sc = jnp.where(causal, _dot_nt(q_h, k_in), 0.0)
            o_heads.append(_dot(sc.astype(_BF16), v_b[:, hd * DV_PAD:(hd + 1) * DV_PAD]))
            yield 300
        o_intra = jnp.concatenate(o_heads, axis=1)

        o_inter = []
        st = state[...]
        for c in range(cpb):
            c0 = c * GLA_CHUNK
            o_inter.append(_dot(q_in_b[c0:c0 + GLA_CHUNK], st.astype(_BF16)))
            kv = _dot(jnp.where(t_lane_chunk == c, k_out_t, jnp.zeros_like(k_out_t)), v_b)
            st = st * decay_cols[:, c:c + 1] + jnp.where(hdiag_ref[...] > 0.5, kv, 0.0)
            yield 400
        state[...] = st
        o_blocks.append(o_intra + jnp.concatenate(o_inter, axis=0))
    o = o_blocks[0] if nblk == 1 else jnp.concatenate(o_blocks, axis=0)

    og = proj(OFF_OG, V_W)
    o_parts = []
    for hd in range(GLA_HEADS):
        o_h = o[:, hd * DV_PAD:(hd + 1) * DV_PAD]
        ms_h = jnp.sum(o_h * o_h, axis=-1, keepdims=True) * (1.0 / GLA_DV)
        o_parts.append(o_h * lax.rsqrt(ms_h + EPS))
    o_n = jnp.concatenate(o_parts, axis=1) * gnorm_ref[...]
    out["o_g"] = o_n * _silu(og)


def _token_mix(x, mod, p, moe=None):
    bsz, seq, d = x.shape
    ts = MIX_TS
    n_s = seq // ts
    full = lambda shape: pl.BlockSpec(shape, lambda b, s: (0,) * len(shape))
    in_specs = [
        pl.BlockSpec((1, ts, d), lambda b, s: (b, s, 0)),
        pl.BlockSpec((1, 6, d), lambda b, s: (b, 0, 0)),
        full((1, d)),
        full((d, N_IN)),
        full((CONV_HIST, CONV_W)), full((1, CONV_W)), full((1, CONV_W)), full((1, CONV_W)),
        full((LRU_CONV_K, LRU_W)), full((1, LRU_W)),
        full((LRU_W, 2 * LRU_W)), full((1, 2 * LRU_W)), full((1, LRU_W)),
        full((LANES, QK_W)), full((1, QK_W)), full((1, V_W)),
        full((MIX_W, d)),
    ]
    scratch_shapes = [
        pltpu.VMEM((CONV_HIST + ts, CONV_W), _F32),
        pltpu.VMEM((LRU_HIST + ts, LRU_W), _F32),
        pltpu.VMEM((SUBLANES, LRU_W), _F32),
        pltpu.VMEM((QK_W, V_W), _F32),
        pltpu.VMEM((GLA_BLOCK, GLA_BLOCK), _BF16),
        pltpu.VMEM((GLA_BLOCK, GLA_BLOCK), _BF16),
        pltpu.VMEM((LANES, GLA_BLOCK), _BF16),
        pltpu.VMEM((GLA_BLOCK, GLA_BLOCK), _F32),
        pltpu.VMEM((QK_W, V_W), _F32),
    ]
    args = [x, mod, p["g_mix"], p["w_in"], p["conv_w"], p["conv_b"], p["ln_g"], p["ln_b"],
            p["lru_conv_w"], p["lru_conv_b"], p["w_ai"], p["b_ai"], p["lam"], p["w_gate"], p["b_gate"],
            p["g_norm"], p["w_out"]]
    if moe is not None:
        pos, mod_prev, y = moe
        last = bsz * n_s - 1
        in_specs += [
            pl.BlockSpec((ts,), lambda b, s: (b * n_s + s,), memory_space=pltpu.SMEM),
            pl.BlockSpec((ts,), lambda b, s: (jnp.minimum(b * n_s + s + 1, last),), memory_space=pltpu.SMEM),
            pl.BlockSpec((1, 6, d), lambda b, s: (b, 0, 0)),
            pl.BlockSpec(memory_space=pl.ANY),
        ]
        scratch_shapes += [pltpu.VMEM((2, ts, d), _F32), pltpu.SemaphoreType.DMA((2,))]
        args += [pos, pos, mod_prev, y]
    return pl.pallas_call(
        functools.partial(_mix_kernel, fused_moe_gather=moe is not None),
        out_shape=jax.ShapeDtypeStruct(x.shape, _F32),
        grid=(bsz, n_s),
        in_specs=in_specs,
        out_specs=pl.BlockSpec((1, ts, d), lambda b, s: (b, s, 0)),
        scratch_shapes=scratch_shapes,
        compiler_params=pltpu.CompilerParams(
            dimension_semantics=("arbitrary", "arbitrary"), vmem_limit_bytes=VMEM_LIMIT_BYTES),
        name="token_mix",
    )(*args)


def _route_kernel(x_ref, mod_ref, g_ref, wr_ref, br_ref, xp_ref, meta_ref, cnt_ref, carry):
    tt = x_ref.shape[0]
    i = pl.program_id(0)

    @pl.when(i == 0)
    def _():
        carry[...] = jnp.zeros_like(carry)

    x = x_ref[...]
    sh2 = mod_ref[0, 3:4, :]
    sc2 = mod_ref[0, 4:5, :]
    ms = jnp.mean(x * x, axis=-1, keepdims=True)
    h = (x * lax.rsqrt(ms + EPS) * g_ref[...]) * (1.0 + sc2) + sh2

    h0, h1, h2 = _split3(h)
    w0, w1, w2 = _split3(wr_ref[...])
    logits = (_dot(h0, w0) + (_dot(h0, w1) + _dot(h1, w0))
              + (_dot(h0, w2) + _dot(h1, w1) + _dot(h2, w0))) + br_ref[...]

    lane = lax.broadcasted_iota(jnp.int32, (tt, LANES), 1)
    lane_f = lane.astype(_F32)
    neg = -jnp.inf
    big = float(LANES)

    def first_argmax(vals, vmax):
        return jnp.min(jnp.where(vals == vmax, lane_f, big), axis=-1, keepdims=True).astype(jnp.int32)

    gl = jnp.where(lane < N_GROUPS, logits, neg)
    gmax = jnp.max(gl, axis=-1, keepdims=True)
    g_star = first_argmax(gl, gmax)
    p_sel = 1.0 / jnp.sum(jnp.exp(gl - gmax), axis=-1, keepdims=True)
    base = N_GROUPS + EXPERTS_PER_GROUP * g_star
    el = jnp.where((lane >= base) & (lane < base + EXPERTS_PER_GROUP), logits, neg)
    v0 = jnp.max(el, axis=-1, keepdims=True)
    i0 = first_argmax(el, v0)
    el2 = jnp.where(lane == i0, neg, el)
    v1 = jnp.max(el2, axis=-1, keepdims=True)
    i1 = first_argmax(el2, v1)
    ex = jnp.exp(v1 - v0)
    wt0 = p_sel / (1.0 + ex)
    wt1 = p_sel * ex / (1.0 + ex)
    e0 = i0 - base
    e1 = i1 - base
    e_lo = jnp.minimum(e0, e1)
    e_hi = jnp.maximum(e0, e1)
    w_lo = jnp.where(e0 < e1, wt0, wt1)
    w_hi = jnp.where(e0 < e1, wt1, wt0)
    pair = (e_lo * (2 * EXPERTS_PER_GROUP - 1 - e_lo)) // 2 + (e_hi - e_lo - 1)
    bucket = g_star * N_PAIRS + pair

    onehot = lane == bucket
    onehot_f = jnp.where(onehot, 1.0, 0.0)
    ri = lax.broadcasted_iota(jnp.int32, (tt, tt), 0)
    ci = lax.broadcasted_iota(jnp.int32, (tt, tt), 1)
    strict = jnp.where(ci < ri, 1.0, 0.0).astype(_BF16)
    prefix = _dot(strict, onehot_f.astype(_BF16)) + carry[0:1, :]
    rank = jnp.sum(jnp.where(onehot, prefix, 0.0), axis=-1, keepdims=True)
    carry[...] = carry[...] + jnp.sum(onehot_f, axis=0, keepdims=True)
    cnt_ref[...] = carry[...]

    meta = jnp.where(lane == 0, bucket.astype(_F32), jnp.where(lane == 1, rank, 0.0))
    meta_ref[...] = meta

    xp_ref[:, 0:D_MODEL] = h
    xp_ref[:, D_MODEL:ROW_W] = jnp.where(lane == 0, w_lo, jnp.where(lane == 1, w_hi, 0.0))


def _route(x2d, mod, g_ffn, w_r, b_r, seq):
    n_tok, d = x2d.shape
    tt = ROUTE_TT
    tiles_per_seq = seq // tt
    return pl.pallas_call(
        _route_kernel,
        out_shape=(
            jax.ShapeDtypeStruct((n_tok, ROW_W), _F32),
            jax.ShapeDtypeStruct((n_tok, LANES), _F32),
            jax.ShapeDtypeStruct((SUBLANES, LANES), _F32),
        ),
        grid=(n_tok // tt,),
        in_specs=[
            pl.BlockSpec((tt, d), lambda i: (i, 0)),
            pl.BlockSpec((1, 6, d), lambda i: (i // tiles_per_seq, 0, 0)),
            pl.BlockSpec((1, d), lambda i: (0, 0)),
            pl.BlockSpec((d, LANES), lambda i: (0, 0)),
            pl.BlockSpec((1, LANES), lambda i: (0, 0)),
        ],
        out_specs=(
            pl.BlockSpec((tt, ROW_W), lambda i: (i, 0)),
            pl.BlockSpec((tt, LANES), lambda i: (i, 0)),
            pl.BlockSpec((SUBLANES, LANES), lambda i: (0, 0)),
        ),
        scratch_shapes=[pltpu.VMEM((SUBLANES, LANES), _F32)],
        compiler_params=pltpu.CompilerParams(
            dimension_semantics=("arbitrary",), vmem_limit_bytes=VMEM_LIMIT_BYTES),
        name="moe_route",
    )(x2d, mod, g_ffn, w_r, b_r)


def _permute_kernel(fill_ref, pos_ref, xp_ref, xs_hbm, zbuf, sem, zsem):
    tb = pos_ref.shape[0]

    @pl.when(pl.program_id(0) == 0)
    def _():
        zbuf[...] = jnp.zeros_like(zbuf)
        n_used = fill_ref[2 * N_BUCKETS]
        n_tiles = xs_hbm.shape[0] // FFN_TM
        for wait in (False, True):
            for b in range(N_BUCKETS):
                for row0, cond in ((fill_ref[b], fill_ref[N_BUCKETS + b] > 0),
                                   ((n_used + b) * FFN_TM, n_used + b < n_tiles)):
                    @pl.when(cond)
                    def _():
                        dst = xs_hbm.at[pl.ds(pl.multiple_of(row0, FFN_TM), FFN_TM)]
                        fill = pltpu.make_async_copy(zbuf, dst, zsem)
                        fill.wait() if wait else fill.start()

    for j in range(tb):
        pltpu.make_async_copy(xp_ref.at[pl.ds(j, 1)], xs_hbm.at[pl.ds(pos_ref[j], 1)], sem).start()
    pltpu.make_async_copy(xp_ref, xs_hbm.at[pl.ds(0, tb)], sem).wait()


def _permute(fill_starts, pos, xp, n_rows):
    n_tok = xp.shape[0]
    tb = PERM_TB
    return pl.pallas_call(
        _permute_kernel,
        out_shape=jax.ShapeDtypeStruct((n_rows, ROW_W), _F32),
        grid_spec=pltpu.PrefetchScalarGridSpec(
            num_scalar_prefetch=1,
            grid=(n_tok // tb,),
            in_specs=[
                pl.BlockSpec((tb,), lambda i, fs: (i,), memory_space=pltpu.SMEM),
                pl.BlockSpec((tb, ROW_W), lambda i, fs: (i, 0)),
            ],
            out_specs=pl.BlockSpec(memory_space=pl.ANY),
            scratch_shapes=[pltpu.VMEM((FFN_TM, ROW_W), _F32), pltpu.SemaphoreType.DMA,
                            pltpu.SemaphoreType.DMA],
        ),
        compiler_params=pltpu.CompilerParams(
            dimension_semantics=("arbitrary",), vmem_limit_bytes=VMEM_LIMIT_BYTES),
        name="moe_permute",
    )(fill_starts, pos, xp)


def _ffn_kernel(tg_ref, tlo_ref, thi_ref, tnew_ref, nused_ref, xs_ref, wg0, wu0, wd0, wg1, wu1, wd1, y_ref,
                *wb):
    del tg_ref, tlo_ref, thi_ref
    i = pl.program_id(0)

    @pl.when(tnew_ref[i] == 1)
    def _():
        for src, dst in zip((wg0, wu0, wd0, wg1, wu1, wd1), wb):
            dst[...] = src[...].astype(_BF16)

    @pl.when(i < nused_ref[0])
    def _():
        xb = xs_ref[:, 0:D_MODEL].astype(_BF16)
        info = xs_ref[:, D_MODEL:ROW_W]

        def expert(wg, wu, wd, wt):
            hid = _silu(_dot(xb, wg[...])) * _dot(xb, wu[...]) * wt
            return _dot(hid.astype(_BF16), wd[...])

        y_ref[...] = (expert(wb[0], wb[1], wb[2], info[:, 0:1]) + expert(wb[3], wb[4], wb[5], info[:, 1:2]))

    @pl.when(i >= nused_ref[0])
    def _():
        y_ref[...] = jnp.zeros_like(y_ref)


def _expert_ffn(layer, tile_g, tile_lo, tile_hi, tile_new, n_used, xs, w_gate, w_up, w_down):
    n_rows = xs.shape[0]
    tm = FFN_TM
    n_tiles = n_rows // tm

    def row_map(i, tg, tlo, thi, tnew, nu):
        return (jnp.minimum(i, nu[0] - 1), 0)

    def w_map(which):
        def f(i, tg, tlo, thi, tnew, nu):
            return (layer, tg[i], (tlo if which == 0 else thi)[i], 0, 0)
        return f

    wspec_in = lambda which: pl.BlockSpec((None, None, None, D_MODEL, D_EXPERT), w_map(which))
    wspec_out = lambda which: pl.BlockSpec((None, None, None, D_EXPERT, D_MODEL), w_map(which))
    w_in_scratch = pltpu.VMEM((D_MODEL, D_EXPERT), _BF16)
    w_out_scratch = pltpu.VMEM((D_EXPERT, D_MODEL), _BF16)
    return pl.pallas_call(
        _ffn_kernel,
        out_shape=jax.ShapeDtypeStruct((n_rows, D_MODEL), _F32),
        grid_spec=pltpu.PrefetchScalarGridSpec(
            num_scalar_prefetch=5,
            grid=(n_tiles,),
            in_specs=[
                pl.BlockSpec((tm, ROW_W), row_map),
                wspec_in(0), wspec_in(0), wspec_out(0),
                wspec_in(1), wspec_in(1), wspec_out(1),
            ],
            out_specs=pl.BlockSpec((tm, D_MODEL), lambda i, tg, tlo, thi, tnew, nu: (i, 0)),
            scratch_shapes=[w_in_scratch, w_in_scratch, w_out_scratch,
                            w_in_scratch, w_in_scratch, w_out_scratch],
        ),
        compiler_params=pltpu.CompilerParams(
            dimension_semantics=("arbitrary",), vmem_limit_bytes=VMEM_LIMIT_BYTES),
        name="moe_ffn",
    )(tile_g, tile_lo, tile_hi, tile_new, n_used, xs, w_gate, w_up, w_down, w_gate, w_up, w_down)


def _combine_kernel(pos_ref, x_ref, mod_ref, gfin_ref, y_hbm, o_ref, ybuf, sem):
    tc = x_ref.shape[0]
    for j in range(tc):
        pltpu.make_async_copy(y_hbm.at[pl.ds(pos_ref[j], 1)], ybuf.at[pl.ds(j, 1)], sem).start()
    pltpu.make_async_copy(y_hbm.at[pl.ds(0, tc)], ybuf, sem).wait()

    gt2 = mod_ref[0, 5:6, :]
    out = x_ref[...] + gt2 * ybuf[...]
    ms = jnp.mean(out * out, axis=-1, keepdims=True)
    o_ref[...] = out * lax.rsqrt(ms + EPS) * gfin_ref[...]


def _combine(pos, x2d, mod, g_final, y, seq):
    n_tok, d = x2d.shape
    tc = COMB_TC
    tiles_per_seq = seq // tc
    return pl.pallas_call(
        _combine_kernel,
        out_shape=jax.ShapeDtypeStruct((n_tok, d), _F32),
        grid=(n_tok // tc,),
        in_specs=[
            pl.BlockSpec((tc,), lambda i: (i,), memory_space=pltpu.SMEM),
            pl.BlockSpec((tc, d), lambda i: (i, 0)),
            pl.BlockSpec((1, 6, d), lambda i: (i // tiles_per_seq, 0, 0)),
            pl.BlockSpec((1, d), lambda i: (0, 0)),
            pl.BlockSpec(memory_space=pl.ANY),
        ],
        out_specs=pl.BlockSpec((tc, d), lambda i: (i, 0)),
        scratch_shapes=[pltpu.VMEM((tc, d), _F32), pltpu.SemaphoreType.DMA],
        compiler_params=pltpu.CompilerParams(
            dimension_semantics=("arbitrary",), vmem_limit_bytes=VMEM_LIMIT_BYTES),
        name="moe_combine",
    )(pos, x2d, mod, g_final, y)


def _pad_heads(w, heads, width, padded):
    lead = w.shape[:-1]
    w = w.reshape(lead + (heads, width))
    w = jnp.pad(w, [(0, 0)] * len(lead) + [(0, 0), (0, padded - width)])
    return w.reshape(lead + (heads * padded,))


def _block_diag(w):
    n, bw, _ = w.shape
    eye = jnp.eye(n, dtype=w.dtype)
    return (eye[:, None, :, None] * w[:, :, None, :]).reshape(n * bw, n * bw)


def _prep_layer(l, w_in, conv_dw_w, conv_dw_b, conv_ln_g, conv_ln_b, lru_conv_w, lru_conv_b, lru_w_a,
                lru_b_a, lru_w_i, lru_b_i, lru_lam, gla_w_gate, gla_b_gate, gla_norm_g, w_out, g_mix):
    sizes = [CONV_W, CONV_W, LRU_W, LRU_W, GLA_HEADS * GLA_DK, GLA_HEADS * GLA_DK, GLA_V, GLA_RANK, GLA_V]
    cv_v, cv_g, lr_x, lr_y, q, k, v, g_lr, og = jnp.split(w_in[l], np.cumsum(sizes)[:-1].tolist(), axis=-1)
    w_in_p = jnp.concatenate([
        cv_v, cv_g, lr_x, lr_y,
        _pad_heads(q, GLA_HEADS, GLA_DK, DK_PAD), _pad_heads(k, GLA_HEADS, GLA_DK, DK_PAD),
        _pad_heads(v, GLA_HEADS, GLA_DV, DV_PAD),
        jnp.pad(g_lr, ((0, 0), (0, LANES - GLA_RANK))),
        _pad_heads(og, GLA_HEADS, GLA_DV, DV_PAD)], axis=-1).astype(_BF16)
    wo = w_out[l]
    wo_o = wo[CONV_W + LRU_W:].reshape(GLA_HEADS, GLA_DV, D_MODEL)
    wo_o = jnp.pad(wo_o, ((0, 0), (0, DV_PAD - GLA_DV), (0, 0))).reshape(V_W, D_MODEL)
    w_out_p = jnp.concatenate([wo[:CONV_W + LRU_W], wo_o], axis=0).astype(_BF16)
    w_gate_p = jnp.pad(_pad_heads(gla_w_gate[l], GLA_HEADS, GLA_DK, DK_PAD),
                       ((0, LANES - GLA_RANK), (0, 0))).astype(_BF16)
    return {
        "g_mix": g_mix[l][None, :],
        "w_in": w_in_p,
        "conv_w": jnp.pad(conv_dw_w[l], ((0, CONV_HIST - CONV_K), (0, 0))),
        "conv_b": conv_dw_b[l][None, :],
        "ln_g": conv_ln_g[l][None, :],
        "ln_b": conv_ln_b[l][None, :],
        "lru_conv_w": lru_conv_w[l],
        "lru_conv_b": lru_conv_b[l][None, :],
        "w_ai": jnp.concatenate([_block_diag(lru_w_a[l]), _block_diag(lru_w_i[l])], axis=1).astype(_BF16),
        "b_ai": jnp.concatenate([lru_b_a[l], lru_b_i[l]])[None, :],
        "lam": lru_lam[l][None, :],
        "w_gate": w_gate_p,
        "b_gate": _pad_heads(gla_b_gate[l], GLA_HEADS, GLA_DK, DK_PAD)[None, :],
        "g_norm": _pad_heads(gla_norm_g[l], GLA_HEADS, GLA_DV, DV_PAD)[None, :],
        "w_out": w_out_p,
    }


def _bucket_layout(counts, n_tiles):
    tm = FFN_TM
    counts = counts.astype(jnp.int32)
    tiles = (counts + tm - 1) // tm
    tile_end = jnp.cumsum(tiles)
    tile_start = tile_end - tiles
    starts = (tile_start * tm).astype(jnp.int32)
    n_used = tile_end[-1]
    tile_idx = jnp.minimum(jnp.arange(n_tiles, dtype=jnp.int32), n_used - 1)
    tile_bucket = jnp.sum((tile_idx[:, None] >= tile_end[None, :]).astype(jnp.int32), axis=1)
    pair_lo = jnp.array([0, 0, 0, 1, 1, 2], jnp.int32)
    pair_hi = jnp.array([1, 2, 3, 2, 3, 3], jnp.int32)
    tile_g = tile_bucket // N_PAIRS
    tile_lo = pair_lo[tile_bucket % N_PAIRS]
    tile_hi = pair_hi[tile_bucket % N_PAIRS]
    tile_new = jnp.concatenate(
        [jnp.ones((1,), jnp.int32), (tile_bucket[1:] != tile_bucket[:-1]).astype(jnp.int32)])
    fill = jnp.concatenate([(jnp.maximum(tile_end - 1, 0) * tm).astype(jnp.int32), tiles, n_used[None]])
    return starts, tile_g, tile_lo, tile_hi, tile_new, fill, n_used.reshape(1).astype(jnp.int32)


def kernel(x, c, w_ada, b_ada, g_mix, w_in, conv_dw_w, conv_dw_b, conv_ln_g, conv_ln_b, lru_conv_w,
           lru_conv_b, lru_w_a, lru_b_a, lru_w_i, lru_b_i, lru_lam, gla_w_gate, gla_b_gate, gla_norm_g,
           w_out, g_ffn, w_route_group, b_route_group, w_route_expert, b_route_expert, w_gate, w_up,
           w_down, g_final):
    bsz, seq, d = x.shape
    n_layers = w_ada.shape[0]
    n_tok = bsz * seq
    assert d == D_MODEL and seq % max(MIX_TS, ROUTE_TT, COMB_TC) == 0 and MIX_TS % GLA_BLOCK == 0
    assert n_tok % PERM_TB == 0
    n_tiles = n_tok // FFN_TM + N_BUCKETS
    n_rows = n_tiles * FFN_TM

    mod_all = _modulation(c, w_ada, b_ada).reshape(n_layers, bsz, 6, d)

    pending_moe = None
    for l in range(n_layers):
        p = _prep_layer(l, w_in, conv_dw_w, conv_dw_b, conv_ln_g, conv_ln_b, lru_conv_w, lru_conv_b,
                        lru_w_a, lru_b_a, lru_w_i, lru_b_i, lru_lam, gla_w_gate, gla_b_gate, gla_norm_g,
                        w_out, g_mix)
        mod = mod_all[l]
        x = _token_mix(x, mod, p, pending_moe)

        x2d = x.reshape(n_tok, d)
        w_r = jnp.concatenate(
            [w_route_group[l], w_route_expert[l].transpose(1, 0, 2).reshape(d, N_GROUPS * EXPERTS_PER_GROUP)],
            axis=1)
        n_logits = w_r.shape[1]
        w_r = jnp.pad(w_r, ((0, 0), (0, LANES - n_logits)))
        b_r = jnp.pad(jnp.concatenate([b_route_group[l], b_route_expert[l].reshape(-1)]),
                      (0, LANES - n_logits))[None, :]
        xp, meta, counts = _route(x2d, mod, g_ffn[l][None, :], w_r, b_r, seq)
        bucket = meta[:, 0].astype(jnp.int32)
        rank = meta[:, 1].astype(jnp.int32)
        starts, tile_g, tile_lo, tile_hi, tile_new, fill, n_used = _bucket_layout(
            counts[0, :N_BUCKETS], n_tiles)
        in_bucket = bucket[:, None] == jnp.arange(N_BUCKETS, dtype=jnp.int32)[None, :]
        pos = rank + jnp.sum(jnp.where(in_bucket, starts[None, :], 0), axis=1)

        xs = _permute(fill, pos, xp, n_rows)
        y = _expert_ffn(l, tile_g, tile_lo, tile_hi, tile_new, n_used, xs, w_gate, w_up, w_down)
        pending_moe = (pos, mod, y)
    return _combine(pos, x2d, mod, g_final[None, :], y, seq).reshape(bsz, seq, d)
```

```python
import functools

import jax
import jax.numpy as jnp
import numpy as np
from jax import lax
from jax.experimental import pallas as pl
from jax.experimental.pallas import tpu as pltpu

D_MODEL = 1024
CONV_W = 256
LRU_W = 384
GLA_V = 384
CONV_K = 31
LRU_CONV_K = 4
LRU_BLOCKS = 6
LRU_BW = 64
LRU_C = 8.0
GLA_HEADS = 4
GLA_DV = 96
GLA_DK = 48
GLA_RANK = 16
GLA_TAU = 16.0
GLA_CHUNK = 64
N_GROUPS = 4
EXPERTS_PER_GROUP = 4
D_EXPERT = 512
EPS = 1e-6

LANES = 128
SUBLANES = 8
VMEM_LIMIT_BYTES = 56 * 1024 * 1024

DK_PAD = 64
DV_PAD = 128
QK_W = GLA_HEADS * DK_PAD
V_W = GLA_HEADS * DV_PAD
OFF_CVV = 0
OFF_CVG = OFF_CVV + CONV_W
OFF_LRX = OFF_CVG + CONV_W
OFF_LRY = OFF_LRX + LRU_W
OFF_Q = OFF_LRY + LRU_W
OFF_K = OFF_Q + QK_W
OFF_V = OFF_K + QK_W
OFF_GLR = OFF_V + V_W
OFF_OG = OFF_GLR + LANES
N_IN = OFF_OG + V_W
MIX_W = CONV_W + LRU_W + V_W

CONV_HIST = 32
LRU_HIST = 8
GLA_BLOCK = 256

N_PAIRS = 6
N_BUCKETS = N_GROUPS * N_PAIRS
ROW_W = D_MODEL + LANES

MIX_TS = 256
ROUTE_TT = 512
FFN_TM = 256
PERM_TB = 512
COMB_TC = 512
DMA_UNROLL = 8
GATHER_SLOTS = 3

_F32 = jnp.float32
_BF16 = jnp.bfloat16


def _sigmoid(x):
    return 1.0 / (1.0 + jnp.exp(-x))


def _silu(x):
    return x * _sigmoid(x)


def _dot(a, b):
    return jnp.dot(a, b, preferred_element_type=_F32)


def _dot_nt(a, b):
    return lax.dot_general(a, b, (((1,), (1,)), ((), ())), preferred_element_type=_F32)


def _split2(x):
    hi = x.astype(_BF16)
    return hi, (x - hi.astype(_F32)).astype(_BF16)


def _mod_kernel(c_ref, w_ref, b_ref, o_ref):
    c_act = _silu(c_ref[...])
    o_ref[0] = _dot(c_act.astype(_BF16), w_ref[0].astype(_BF16)) + b_ref[0]


def _modulation(c, w_ada, b_ada):
    n_layers, d, n = w_ada.shape
    bsz = c.shape[0]
    tn = 1536
    return pl.pallas_call(
        _mod_kernel,
        out_shape=jax.ShapeDtypeStruct((n_layers, bsz, n), _F32),
        grid=(n_layers, n // tn),
        in_specs=[
            pl.BlockSpec((bsz, d), lambda l, j: (0, 0)),
            pl.BlockSpec((1, d, tn), lambda l, j: (l, 0, j)),
            pl.BlockSpec((1, 1, tn), lambda l, j: (l, 0, j)),
        ],
        out_specs=pl.BlockSpec((1, bsz, tn), lambda l, j: (l, 0, j)),
        compiler_params=pltpu.CompilerParams(
            dimension_semantics=("arbitrary", "arbitrary"), vmem_limit_bytes=VMEM_LIMIT_BYTES),
        name="adaln_mod",
    )(c, w_ada, b_ada.reshape(n_layers, 1, n))


def _mix_kernel(x_ref, mod_ref, gmix_ref, win_ref, cw_ref, cb_ref, lng_ref, lnb_ref,
                lcw_ref, lcb_ref, wai_ref, bai_ref, lam_ref, wgate_ref, bgate_ref, gnorm_ref,
                wout_ref, *rest, fused_moe_gather):
    if fused_moe_gather:
        pos_cur, pos_nxt, pos_ahd, modp_ref, y_hbm, o_ref = rest[:6]
        ybuf, gsem = rest[-2:]
        rest = rest[6:-2]
    else:
        o_ref, rest = rest[0], rest[1:]
    ubuf, lbuf, hcar, state, tri_ref, tria_ref, csel_ref, causal_ref, hdiag_ref = rest
    ts = x_ref.shape[1]
    s_idx = pl.program_id(1)

    @pl.when(s_idx == 0)
    def _():
        ubuf[0:CONV_HIST, :] = jnp.zeros((CONV_HIST, CONV_W), _F32)
        lbuf[0:LRU_HIST, :] = jnp.zeros((LRU_HIST, LRU_W), _F32)
        hcar[...] = jnp.zeros_like(hcar)
        state[...] = jnp.zeros_like(state)
        ri = lax.broadcasted_iota(jnp.int32, (GLA_BLOCK, GLA_BLOCK), 0)
        ci = lax.broadcasted_iota(jnp.int32, (GLA_BLOCK, GLA_BLOCK), 1)
        same_chunk = ri // GLA_CHUNK == ci // GLA_CHUNK
        causal_f = jnp.where(same_chunk & (ci <= ri), 1.0, 0.0)
        causal_ref[...] = causal_f
        tri_ref[...] = causal_f.astype(_BF16)
        tria_ref[...] = jnp.where(same_chunk & (ci > ri), 1.0, 0.0).astype(_BF16)
        sel_r = lax.broadcasted_iota(jnp.int32, (LANES, GLA_BLOCK), 0)
        sel_c = lax.broadcasted_iota(jnp.int32, (LANES, GLA_BLOCK), 1)
        csel_ref[...] = jnp.where(sel_r == sel_c // GLA_CHUNK, 1.0, 0.0).astype(_BF16)
        st_row_head = lax.broadcasted_iota(jnp.int32, (QK_W, V_W), 0) // DK_PAD
        st_col_head = lax.broadcasted_iota(jnp.int32, (QK_W, V_W), 1) // DV_PAD
        hdiag_ref[...] = jnp.where(st_row_head == st_col_head, 1.0, 0.0)

    x = x_ref[0]
    if fused_moe_gather:
        step = pl.program_id(0) * pl.num_programs(1) + s_idx
        n_steps = pl.num_programs(0) * pl.num_programs(1)
        slot = step % GATHER_SLOTS

        def row_copy(pos_ref, j, to_slot):
            return pltpu.make_async_copy(y_hbm.at[pl.ds(pos_ref[j], 1)], ybuf.at[to_slot, pl.ds(j, 1)],
                                         gsem.at[to_slot])

        def wait_rows(of_slot):
            pltpu.make_async_copy(y_hbm.at[pl.ds(0, ts)], ybuf.at[of_slot], gsem.at[of_slot]).wait()

        @pl.when(step == 0)
        def _():
            def issue(j, carry):
                row_copy(pos_cur, j, 0).start()
                row_copy(pos_nxt, j, 1).start()
                return carry
            lax.fori_loop(0, ts, issue, 0, unroll=DMA_UNROLL)

        wait_rows(slot)
        x = x + modp_ref[0, 5:6, :] * ybuf[slot]

        ahead_slot = (step + GATHER_SLOTS - 1) % GATHER_SLOTS
        for j in range(ts):
            row_copy(pos_ahd, j, ahead_slot).start()

    sh1 = mod_ref[0, 0:1, :]
    sc1 = mod_ref[0, 1:2, :]
    gt1 = mod_ref[0, 2:3, :]
    ms = jnp.mean(x * x, axis=-1, keepdims=True)
    h = (x * lax.rsqrt(ms + EPS) * gmix_ref[...]) * (1.0 + sc1) + sh1
    hb = h.astype(_BF16)

    def proj(off, width):
        return _dot(hb, win_ref[:, off:off + width])

    out = {}
    _run_alternately(
        _conv_lru_steps(out, proj, cw_ref, cb_ref, lng_ref, lnb_ref, lcw_ref, lcb_ref, wai_ref, bai_ref,
                        lam_ref, ubuf, lbuf, hcar, ts),
        _gla_steps(out, proj, wgate_ref, bgate_ref, gnorm_ref, state, tri_ref, tria_ref, csel_ref,
                   causal_ref, hdiag_ref, ts))

    mixed = (_dot(out["o_g"].astype(_BF16), wout_ref[CONV_W + LRU_W:MIX_W, :])
             + _dot(out["r_out"].astype(_BF16), wout_ref[CONV_W:CONV_W + LRU_W, :])
             + _dot(out["u_out"].astype(_BF16), wout_ref[0:CONV_W, :]))
    o_ref[0] = x + gt1 * mixed

    if fused_moe_gather:
        @pl.when(step == n_steps - 1)
        def _():
            wait_rows((step + 1) % GATHER_SLOTS)
            wait_rows((step + 2) % GATHER_SLOTS)


def _run_alternately(*step_generators):
    clock = [0] * len(step_generators)
    live = list(range(len(step_generators)))
    while live:
        i = min(live, key=lambda j: clock[j])
        try:
            clock[i] += next(step_generators[i]) * (1 if i else 0.5)
        except StopIteration:
            live.remove(i)


def _conv_lru_steps(out, proj, cw_ref, cb_ref, lng_ref, lnb_ref, lcw_ref, lcb_ref, wai_ref, bai_ref,
                    lam_ref, ubuf, lbuf, hcar, ts):
    u = proj(OFF_CVV, CONV_W) * _sigmoid(proj(OFF_CVG, CONV_W))
    ubuf[CONV_HIST:CONV_HIST + ts, :] = u
    yield 500
    lbuf[LRU_HIST:LRU_HIST + ts, :] = proj(OFF_LRX, LRU_W)
    xb = jnp.broadcast_to(lcb_ref[...], (ts, LRU_W))
    for k in range(LRU_CONV_K):
        off = LRU_HIST - (LRU_CONV_K - 1) + k
        xb = xb + lcw_ref[k:k + 1, :] * lbuf[off:off + ts, :]
    lbuf[0:LRU_HIST, :] = lbuf[ts:ts + LRU_HIST, :]
    gates = _dot(xb.astype(_BF16), wai_ref[...]) + bai_ref[...]
    yield 800

    acc = jnp.broadcast_to(cb_ref[...], (ts, CONV_W))
    for r in range(SUBLANES):
        z = None
        for a_blk in range((CONV_K - 1 - r) // SUBLANES + 1):
            k = CONV_K - 1 - (SUBLANES * a_blk + r)
            off = CONV_HIST - SUBLANES * (a_blk + 1)
            term = cw_ref[k:k + 1, :] * ubuf[off:off + ts + SUBLANES, :]
            z = term if z is None else z + term
        acc = acc + z[SUBLANES - r:SUBLANES - r + ts]
        yield 200
    ubuf[0:CONV_HIST, :] = ubuf[ts:ts + CONV_HIST, :]
    mu = jnp.mean(acc, axis=-1, keepdims=True)
    cen = acc - mu
    var = jnp.mean(cen * cen, axis=-1, keepdims=True)
    out["u_out"] = _silu(cen * lax.rsqrt(var + EPS) * lng_ref[...] + lnb_ref[...])
    yield 250

    r_gate = _sigmoid(gates[:, 0:LRU_W])
    i_gate = _sigmoid(gates[:, LRU_W:2 * LRU_W])
    lam = lam_ref[...]
    softplus_neg_lam = jnp.maximum(-lam, 0.0) + jnp.log1p(jnp.exp(-jnp.abs(lam)))
    log_a = (-LRU_C) * r_gate * softplus_neg_lam
    a = jnp.exp(log_a)
    mult = jnp.sqrt(jnp.tanh(-log_a) * (a * a + 1.0))
    u_in = mult * (i_gate * xb)
    yield 600
    row = lax.broadcasted_iota(jnp.int32, a.shape, 0) % SUBLANES
    d = 1
    while d < SUBLANES:
        keep = row >= d
        a_sh = jnp.where(keep, pltpu.roll(a, d, axis=0), 1.0)
        u_sh = jnp.where(keep, pltpu.roll(u_in, d, axis=0), 0.0)
        u_in = u_in + a * u_sh
        a = a * a_sh
        d *= 2
        yield 200
    groups = []
    h_prev = hcar[0:1, :]
    for g in range(ts // SUBLANES):
        r0 = g * SUBLANES
        h_g = u_in[r0:r0 + SUBLANES] + a[r0:r0 + SUBLANES] * h_prev
        groups.append(h_g)
        h_prev = h_g[SUBLANES - 1:SUBLANES]
        if g % SUBLANES == SUBLANES - 1:
            yield 50
    h_lru = jnp.concatenate(groups, axis=0)
    hcar[...] = jnp.broadcast_to(h_prev, hcar.shape)
    out["r_out"] = h_lru * jax.nn.gelu(proj(OFF_LRY, LRU_W), approximate=True)


def _gla_steps(out, proj, wgate_ref, bgate_ref, gnorm_ref, state, tri_ref, tria_ref, csel_ref, causal_ref,
               hdiag_ref, ts):
    zg = proj(OFF_GLR, LANES)
    glog = _dot(zg.astype(_BF16), wgate_ref[...]) + bgate_ref[...]
    lg = (jnp.minimum(glog, 0.0) - jnp.log1p(jnp.exp(-jnp.abs(glog)))) * (1.0 / GLA_TAU)
    yield 300
    zq = proj(OFF_Q, QK_W) * (GLA_DK ** -0.5)
    zk = proj(OFF_K, QK_W)
    yield 500
    zv = proj(OFF_V, V_W)
    yield 500

    nblk = ts // GLA_BLOCK
    cpb = GLA_BLOCK // GLA_CHUNK
    tri = tri_ref[...]
    tri_after = tria_ref[...]
    chunk_sel = csel_ref[...]
    causal = causal_ref[...] > 0.5
    qk_lane_head = lax.broadcasted_iota(jnp.int32, (1, QK_W), 1) // DK_PAD
    t_lane_chunk = lax.broadcasted_iota(jnp.int32, (1, GLA_BLOCK), 1) // GLA_CHUNK

    o_blocks = []
    for blk in range(nblk):
        r0 = blk * GLA_BLOCK
        lg_b = lg[r0:r0 + GLA_BLOCK]
        p_hi, p_lo = _split2(lg_b)
        b = _dot(tri, p_hi) + _dot(tri, p_lo)
        b_rest = _dot(tri_after, p_hi) + _dot(tri_after, p_lo)
        b_tot = _dot(chunk_sel, p_hi) + _dot(chunk_sel, p_lo)
        yield 500
        q_in =zq[r0:r0 + GLA_BLOCK] * jnp.exp(b)
        k_blk = zk[r0:r0 + GLA_BLOCK]
        k_in = (k_blk * jnp.exp(-b)).astype(_BF16)
        k_out_t = (k_blk * jnp.exp(b_rest)).T.astype(_BF16)
        v_b = zv[r0:r0 + GLA_BLOCK].astype(_BF16)
        q_in_b = q_in.astype(_BF16)
        decay_cols = jnp.exp(b_tot).T

        o_heads = []
        for hd in range(GLA_HEADS):
            q_h = jnp.where(qk_lane_head == hd, q_in_b, jnp.zeros_like(q_in_b))
            sc = jnp.where(causal, _dot_nt(q_h, k_in), 0.0)
            o_heads.append(_dot(sc.astype(_BF16), v_b[:, hd * DV_PAD:(hd + 1) * DV_PAD]))
            yield 300
        o_intra = jnp.concatenate(o_heads, axis=1)

        o_inter = []
        st = state[...]
        for c in range(cpb):
            c0 = c * GLA_CHUNK
            o_inter.append(_dot(q_in_b[c0:c0 + GLA_CHUNK], st.astype(_BF16)))
            kv = _dot(jnp.where(t_lane_chunk == c, k_out_t, jnp.zeros_like(k_out_t)), v_b)
            st = st * decay_cols[:, c:c + 1] + jnp.where(hdiag_ref[...] > 0.5, kv, 0.0)
            yield 400
        state[...] = st
        o_blocks.append(o_intra + jnp.concatenate(o_inter, axis=0))
    o = o_blocks[0] if nblk == 1 else jnp.concatenate(o_blocks, axis=0)

    og = proj(OFF_OG, V_W)
    o_parts = []
    for hd in range(GLA_HEADS):
        o_h = o[:, hd * DV_PAD:(hd + 1) * DV_PAD]
        ms_h = jnp.sum(o_h * o_h, axis=-1, keepdims=True) * (1.0 / GLA_DV)
        o_parts.append(o_h * lax.rsqrt(ms_h + EPS))
    o_n = jnp.concatenate(o_parts, axis=1) * gnorm_ref[...]
    out["o_g"] = o_n * _silu(og)


def _token_mix(x, mod, p, moe=None):
    bsz, seq, d = x.shape
    ts = MIX_TS
    n_s = seq // ts
    full = lambda shape: pl.BlockSpec(shape, lambda b, s: (0,) * len(shape))
    in_specs = [
        pl.BlockSpec((1, ts, d), lambda b, s: (b, s, 0)),
        pl.BlockSpec((1, 6, d), lambda b, s: (b, 0, 0)),
        full((1, d)),
        full((d, N_IN)),
        full((CONV_HIST, CONV_W)), full((1, CONV_W)), full((1, CONV_W)), full((1, CONV_W)),
        full((LRU_CONV_K, LRU_W)), full((1, LRU_W)),
        full((LRU_W, 2 * LRU_W)), full((1, 2 * LRU_W)), full((1, LRU_W)),
        full((LANES, QK_W)), full((1, QK_W)), full((1, V_W)),
        full((MIX_W, d)),
    ]
    scratch_shapes = [
        pltpu.VMEM((CONV_HIST + ts, CONV_W), _F32),
        pltpu.VMEM((LRU_HIST + ts, LRU_W), _F32),
        pltpu.VMEM((SUBLANES, LRU_W), _F32),
        pltpu.VMEM((QK_W, V_W), _F32),
        pltpu.VMEM((GLA_BLOCK, GLA_BLOCK), _BF16),
        pltpu.VMEM((GLA_BLOCK, GLA_BLOCK), _BF16),
        pltpu.VMEM((LANES, GLA_BLOCK), _BF16),
        pltpu.VMEM((GLA_BLOCK, GLA_BLOCK), _F32),
        pltpu.VMEM((QK_W, V_W), _F32),
    ]
    args = [x, mod, p["g_mix"], p["w_in"], p["conv_w"], p["conv_b"], p["ln_g"], p["ln_b"],
            p["lru_conv_w"], p["lru_conv_b"], p["w_ai"], p["b_ai"], p["lam"], p["w_gate"], p["b_gate"],
            p["g_norm"], p["w_out"]]
    if moe is not None:
        pos, mod_prev, y = moe
        last = bsz * n_s - 1
        in_specs += [
            pl.BlockSpec((ts,), lambda b, s: (b * n_s + s,), memory_space=pltpu.SMEM),
            pl.BlockSpec((ts,), lambda b, s: (jnp.minimum(b * n_s + s + 1, last),), memory_space=pltpu.SMEM),
            pl.BlockSpec((ts,), lambda b, s: (jnp.minimum(b * n_s + s + 2, last),), memory_space=pltpu.SMEM),
            pl.BlockSpec((1, 6, d), lambda b, s: (b, 0, 0)),
            pl.BlockSpec(memory_space=pl.ANY),
        ]
        scratch_shapes += [pltpu.VMEM((GATHER_SLOTS, ts, d), _F32), pltpu.SemaphoreType.DMA((GATHER_SLOTS,))]
        args += [pos, pos, pos, mod_prev, y]
    return pl.pallas_call(
        functools.partial(_mix_kernel, fused_moe_gather=moe is not None),
        out_shape=jax.ShapeDtypeStruct(x.shape, _F32),
        grid=(bsz, n_s),
        in_specs=in_specs,
        out_specs=pl.BlockSpec((1, ts, d), lambda b, s: (b, s, 0)),
        scratch_shapes=scratch_shapes,
        compiler_params=pltpu.CompilerParams(
            dimension_semantics=("arbitrary", "arbitrary"), vmem_limit_bytes=VMEM_LIMIT_BYTES),
        name="token_mix",
    )(*args)


def _route_kernel(x_ref, mod_ref, g_ref, wr_ref, br_ref, xp_ref, meta_ref, cnt_ref, carry):
    tt = x_ref.shape[0]
    i = pl.program_id(0)

    @pl.when(i == 0)
    def _():
        carry[...] = jnp.zeros_like(carry)

    x = x_ref[...]
    sh2 = mod_ref[0, 3:4, :]
    sc2 = mod_ref[0, 4:5, :]
    ms = jnp.mean(x * x, axis=-1, keepdims=True)
    h = (x * lax.rsqrt(ms + EPS) * g_ref[...]) * (1.0 + sc2) + sh2

    h_hi, h_lo = _split2(h)
    w_hi, w_lo = _split2(wr_ref[...])
    w_cat = jnp.concatenate([w_hi, w_lo], axis=1)
    prod = _dot(h_hi, w_cat) + _dot(h_lo, w_cat)
    logits = prod[:, 0:LANES] + prod[:, LANES:2 * LANES] + br_ref[...]

    lane = lax.broadcasted_iota(jnp.int32, (tt, LANES), 1)
    lane_f = lane.astype(_F32)
    neg = -jnp.inf
    big = float(LANES)

    def first_argmax(vals, vmax):
        return jnp.min(jnp.where(vals == vmax, lane_f, big), axis=-1, keepdims=True).astype(jnp.int32)

    gl = jnp.where(lane < N_GROUPS, logits, neg)
    gmax = jnp.max(gl, axis=-1, keepdims=True)
    g_star = first_argmax(gl, gmax)
    p_sel = 1.0 / jnp.sum(jnp.exp(gl - gmax), axis=-1, keepdims=True)
    base = N_GROUPS + EXPERTS_PER_GROUP * g_star
    el = jnp.where((lane >= base) & (lane < base + EXPERTS_PER_GROUP), logits, neg)
    v0 = jnp.max(el, axis=-1, keepdims=True)
    i0 = first_argmax(el, v0)
    el2 = jnp.where(lane == i0, neg, el)
    v1 = jnp.max(el2, axis=-1, keepdims=True)
    i1 = first_argmax(el2, v1)
    ex = jnp.exp(v1 - v0)
    wt0 = p_sel / (1.0 + ex)
    wt1 = p_sel * ex / (1.0 + ex)
    e0 = i0 - base
    e1 = i1 - base
    e_lo = jnp.minimum(e0, e1)
    e_hi = jnp.maximum(e0, e1)
    w_lo = jnp.where(e0 < e1, wt0, wt1)
    w_hi = jnp.where(e0 < e1, wt1, wt0)
    pair = (e_lo * (2 * EXPERTS_PER_GROUP - 1 - e_lo)) // 2 + (e_hi - e_lo - 1)
    bucket = g_star * N_PAIRS + pair

    onehot = lane == bucket
    onehot_f = jnp.where(onehot, 1.0, 0.0)
    ri = lax.broadcasted_iota(jnp.int32, (tt, tt), 0)
    ci = lax.broadcasted_iota(jnp.int32, (tt, tt), 1)
    strict = jnp.where(ci < ri, 1.0, 0.0).astype(_BF16)
    prefix = _dot(strict, onehot_f.astype(_BF16)) + carry[0:1, :]
    rank = jnp.sum(jnp.where(onehot, prefix, 0.0), axis=-1, keepdims=True)
    carry[...] = carry[...] + jnp.sum(onehot_f, axis=0, keepdims=True)
    cnt_ref[...] = carry[...]

    meta = jnp.where(lane == 0, bucket.astype(_F32), jnp.where(lane == 1, rank, 0.0))
    meta_ref[...] = meta

    xp_ref[:, 0:D_MODEL] = h
    xp_ref[:, D_MODEL:ROW_W] = jnp.where(lane == 0, w_lo, jnp.where(lane == 1, w_hi, 0.0))


def _route(x2d, mod, g_ffn, w_r, b_r, seq):
    n_tok, d = x2d.shape
    tt = ROUTE_TT
    tiles_per_seq = seq // tt
    return pl.pallas_call(
        _route_kernel,
        out_shape=(
            jax.ShapeDtypeStruct((n_tok, ROW_W), _F32),
            jax.ShapeDtypeStruct((n_tok, LANES), _F32),
            jax.ShapeDtypeStruct((SUBLANES, LANES), _F32),
        ),
        grid=(n_tok // tt,),
        in_specs=[
            pl.BlockSpec((tt, d), lambda i: (i, 0)),
            pl.BlockSpec((1, 6, d), lambda i: (i // tiles_per_seq, 0, 0)),
            pl.BlockSpec((1, d), lambda i: (0, 0)),
            pl.BlockSpec((d, LANES), lambda i: (0, 0)),
            pl.BlockSpec((1, LANES), lambda i: (0, 0)),
        ],
        out_specs=(
            pl.BlockSpec((tt, ROW_W), lambda i: (i, 0)),
            pl.BlockSpec((tt, LANES), lambda i: (i, 0)),
            pl.BlockSpec((SUBLANES, LANES), lambda i: (0, 0)),
        ),
        scratch_shapes=[pltpu.VMEM((SUBLANES, LANES), _F32)],
        compiler_params=pltpu.CompilerParams(
            dimension_semantics=("arbitrary",), vmem_limit_bytes=VMEM_LIMIT_BYTES),
        name="moe_route",
    )(x2d, mod, g_ffn, w_r, b_r)


def _permute_kernel(fill_ref, pos_ref, xp_ref, xs_hbm, zbuf, sem, zsem):
    tb = pos_ref.shape[0]

    @pl.when(pl.program_id(0) == 0)
    def _():
        zbuf[...] = jnp.zeros_like(zbuf)
        n_used = fill_ref[2 * N_BUCKETS]
        n_tiles = xs_hbm.shape[0] // FFN_TM
        for wait in (False, True):
            for b in range(N_BUCKETS):
                for row0, cond in ((fill_ref[b], fill_ref[N_BUCKETS + b] > 0),
                                   ((n_used + b) * FFN_TM, n_used + b < n_tiles)):
                    @pl.when(cond)
                    def _():
                        dst = xs_hbm.at[pl.ds(pl.multiple_of(row0, FFN_TM), FFN_TM)]
                        fill = pltpu.make_async_copy(zbuf, dst, zsem)
                        fill.wait() if wait else fill.start()

    for j in range(tb):
        pltpu.make_async_copy(xp_ref.at[pl.ds(j, 1)], xs_hbm.at[pl.ds(pos_ref[j], 1)], sem).start()
    pltpu.make_async_copy(xp_ref, xs_hbm.at[pl.ds(0, tb)], sem).wait()


def _permute(fill_starts, pos, xp, n_rows):
    n_tok = xp.shape[0]
    tb = PERM_TB
    return pl.pallas_call(
        _permute_kernel,
        out_shape=jax.ShapeDtypeStruct((n_rows, ROW_W), _F32),
        grid_spec=pltpu.PrefetchScalarGridSpec(
            num_scalar_prefetch=1,
            grid=(n_tok // tb,),
            in_specs=[
                pl.BlockSpec((tb,), lambda i, fs: (i,), memory_space=pltpu.SMEM),
                pl.BlockSpec((tb, ROW_W), lambda i, fs: (i, 0)),
            ],
            out_specs=pl.BlockSpec(memory_space=pl.ANY),
            scratch_shapes=[pltpu.VMEM((FFN_TM, ROW_W), _F32), pltpu.SemaphoreType.DMA,
                            pltpu.SemaphoreType.DMA],
        ),
        compiler_params=pltpu.CompilerParams(
            dimension_semantics=("arbitrary",), vmem_limit_bytes=VMEM_LIMIT_BYTES),
        name="moe_permute",
    )(fill_starts, pos, xp)


def _ffn_kernel(tg_ref, tlo_ref, thi_ref, tnew_ref, nused_ref, xs_ref, wg0, wu0, wd0, wg1, wu1, wd1, y_ref,
                *wb):
    del tg_ref, tlo_ref, thi_ref
    i = pl.program_id(0)

    @pl.when(tnew_ref[i] == 1)
    def _():
        for src, dst in zip((wg0, wu0, wd0, wg1, wu1, wd1), wb):
            dst[...] = src[...].astype(_BF16)

    @pl.when(i < nused_ref[0])
    def _():
        xb = xs_ref[:, 0:D_MODEL].astype(_BF16)
        info = xs_ref[:, D_MODEL:ROW_W]

        def expert(wg, wu, wd, wt):
            hid = _silu(_dot(xb, wg[...])) * _dot(xb, wu[...]) * wt
            return _dot(hid.astype(_BF16), wd[...])

        y_ref[...] = (expert(wb[0], wb[1], wb[2], info[:, 0:1]) + expert(wb[3], wb[4], wb[5], info[:, 1:2]))

    @pl.when(i >= nused_ref[0])
    def _():
        y_ref[...] = jnp.zeros_like(y_ref)


def _expert_ffn(layer, tile_g, tile_lo, tile_hi, tile_new, n_used, xs, w_gate, w_up, w_down):
    n_rows = xs.shape[0]
    tm = FFN_TM
    n_tiles = n_rows // tm

    def row_map(i, tg, tlo, thi, tnew, nu):
        return (jnp.minimum(i, nu[0] - 1), 0)

    def w_map(which):
        def f(i, tg, tlo, thi, tnew, nu):
            return (layer, tg[i], (tlo if which == 0 else thi)[i], 0, 0)
        return f

    wspec_in = lambda which: pl.BlockSpec((None, None, None, D_MODEL, D_EXPERT), w_map(which))
    wspec_out = lambda which: pl.BlockSpec((None, None, None, D_EXPERT, D_MODEL), w_map(which))
    w_in_scratch = pltpu.VMEM((D_MODEL, D_EXPERT), _BF16)
    w_out_scratch = pltpu.VMEM((D_EXPERT, D_MODEL), _BF16)
    return pl.pallas_call(
        _ffn_kernel,
        out_shape=jax.ShapeDtypeStruct((n_rows, D_MODEL), _F32),
        grid_spec=pltpu.PrefetchScalarGridSpec(
            num_scalar_prefetch=5,
            grid=(n_tiles,),
            in_specs=[
                pl.BlockSpec((tm, ROW_W), row_map),
                wspec_in(0), wspec_in(0), wspec_out(0),
                wspec_in(1), wspec_in(1), wspec_out(1),
            ],
            out_specs=pl.BlockSpec((tm, D_MODEL), lambda i, tg, tlo, thi, tnew, nu: (i, 0)),
            scratch_shapes=[w_in_scratch, w_in_scratch, w_out_scratch,
                            w_in_scratch, w_in_scratch, w_out_scratch],
        ),
        compiler_params=pltpu.CompilerParams(
            dimension_semantics=("arbitrary",), vmem_limit_bytes=VMEM_LIMIT_BYTES),
        name="moe_ffn",
    )(tile_g, tile_lo, tile_hi, tile_new, n_used, xs, w_gate, w_up, w_down, w_gate, w_up, w_down)


def _combine_kernel(pos_ref, x_ref, mod_ref, gfin_ref, y_hbm, o_ref, ybuf, sem):
    tc = x_ref.shape[0]
    for j in range(tc):
        pltpu.make_async_copy(y_hbm.at[pl.ds(pos_ref[j], 1)], ybuf.at[pl.ds(j, 1)], sem).start()
    pltpu.make_async_copy(y_hbm.at[pl.ds(0, tc)], ybuf, sem).wait()

    gt2 = mod_ref[0, 5:6, :]
    out = x_ref[...] + gt2 * ybuf[...]
    ms = jnp.mean(out * out, axis=-1, keepdims=True)
    o_ref[...] = out * lax.rsqrt(ms + EPS) * gfin_ref[...]


def _combine(pos, x2d, mod, g_final, y, seq):
    n_tok, d = x2d.shape
    tc = COMB_TC
    tiles_per_seq = seq // tc
    return pl.pallas_call(
        _combine_kernel,
        out_shape=jax.ShapeDtypeStruct((n_tok, d), _F32),
        grid=(n_tok // tc,),
        in_specs=[
            pl.BlockSpec((tc,), lambda i: (i,), memory_space=pltpu.SMEM),
            pl.BlockSpec((tc, d), lambda i: (i, 0)),
            pl.BlockSpec((1, 6, d), lambda i: (i // tiles_per_seq, 0, 0)),
            pl.BlockSpec((1, d), lambda i: (0, 0)),
            pl.BlockSpec(memory_space=pl.ANY),
        ],
        out_specs=pl.BlockSpec((tc, d), lambda i: (i, 0)),
        scratch_shapes=[pltpu.VMEM((tc, d), _F32), pltpu.SemaphoreType.DMA],
        compiler_params=pltpu.CompilerParams(
            dimension_semantics=("arbitrary",), vmem_limit_bytes=VMEM_LIMIT_BYTES),
        name="moe_combine",
    )(pos, x2d, mod, g_final, y)


def _pad_heads(w, heads, width, padded):
    lead = w.shape[:-1]
    w = w.reshape(lead + (heads, width))
    w = jnp.pad(w, [(0, 0)] * len(lead) + [(0, 0), (0, padded - width)])
    return w.reshape(lead + (heads * padded,))


def _block_diag(w):
    n, bw, _ = w.shape
    eye = jnp.eye(n, dtype=w.dtype)
    return (eye[:, None, :, None] * w[:, :, None, :]).reshape(n * bw, n * bw)


def _prep_layer(l, w_in, conv_dw_w, conv_dw_b, conv_ln_g, conv_ln_b, lru_conv_w, lru_conv_b, lru_w_a,
                lru_b_a, lru_w_i, lru_b_i, lru_lam, gla_w_gate, gla_b_gate, gla_norm_g, w_out, g_mix):
    sizes = [CONV_W, CONV_W, LRU_W, LRU_W, GLA_HEADS * GLA_DK, GLA_HEADS * GLA_DK, GLA_V, GLA_RANK, GLA_V]
    cv_v, cv_g, lr_x, lr_y, q, k, v, g_lr, og = jnp.split(w_in[l], np.cumsum(sizes)[:-1].tolist(), axis=-1)
    w_in_p = jnp.concatenate([
        cv_v, cv_g, lr_x, lr_y,
        _pad_heads(q, GLA_HEADS, GLA_DK, DK_PAD), _pad_heads(k, GLA_HEADS, GLA_DK, DK_PAD),
        _pad_heads(v, GLA_HEADS, GLA_DV, DV_PAD),
        jnp.pad(g_lr, ((0, 0), (0, LANES - GLA_RANK))),
        _pad_heads(og, GLA_HEADS, GLA_DV, DV_PAD)], axis=-1).astype(_BF16)
    wo = w_out[l]
    wo_o = wo[CONV_W + LRU_W:].reshape(GLA_HEADS, GLA_DV, D_MODEL)
    wo_o = jnp.pad(wo_o, ((0, 0), (0, DV_PAD - GLA_DV), (0, 0))).reshape(V_W, D_MODEL)
    w_out_p = jnp.concatenate([wo[:CONV_W + LRU_W], wo_o], axis=0).astype(_BF16)
    w_gate_p = jnp.pad(_pad_heads(gla_w_gate[l], GLA_HEADS, GLA_DK, DK_PAD),
                       ((0, LANES - GLA_RANK), (0, 0))).astype(_BF16)
    return {
        "g_mix": g_mix[l][None, :],
        "w_in": w_in_p,
        "conv_w": jnp.pad(conv_dw_w[l], ((0, CONV_HIST - CONV_K), (0, 0))),
        "conv_b": conv_dw_b[l][None, :],
        "ln_g": conv_ln_g[l][None, :],
        "ln_b": conv_ln_b[l][None, :],
        "lru_conv_w": lru_conv_w[l],
        "lru_conv_b": lru_conv_b[l][None, :],
        "w_ai": jnp.concatenate([_block_diag(lru_w_a[l]), _block_diag(lru_w_i[l])], axis=1).astype(_BF16),
        "b_ai": jnp.concatenate([lru_b_a[l], lru_b_i[l]])[None, :],
        "lam": lru_lam[l][None, :],
        "w_gate": w_gate_p,
        "b_gate": _pad_heads(gla_b_gate[l], GLA_HEADS, GLA_DK, DK_PAD)[None, :],
        "g_norm": _pad_heads(gla_norm_g[l], GLA_HEADS, GLA_DV, DV_PAD)[None, :],
        "w_out": w_out_p,
    }


def _bucket_layout(counts, n_tiles):
    tm = FFN_TM
    counts = counts.astype(jnp.int32)
    tiles = (counts + tm - 1) // tm
    tile_end = jnp.cumsum(tiles)
    tile_start = tile_end - tiles
    starts = (tile_start * tm).astype(jnp.int32)
    n_used = tile_end[-1]
    tile_idx = jnp.minimum(jnp.arange(n_tiles, dtype=jnp.int32), n_used - 1)
    tile_bucket = jnp.sum((tile_idx[:, None] >= tile_end[None, :]).astype(jnp.int32), axis=1)
    pair_lo = jnp.array([0, 0, 0, 1, 1, 2], jnp.int32)
    pair_hi = jnp.array([1, 2, 3, 2, 3, 3], jnp.int32)
    tile_g = tile_bucket // N_PAIRS
    tile_lo = pair_lo[tile_bucket % N_PAIRS]
    tile_hi = pair_hi[tile_bucket % N_PAIRS]
    tile_new = jnp.concatenate(
        [jnp.ones((1,), jnp.int32), (tile_bucket[1:] != tile_bucket[:-1]).astype(jnp.int32)])
    fill = jnp.concatenate([(jnp.maximum(tile_end - 1, 0) * tm).astype(jnp.int32), tiles, n_used[None]])
    return starts, tile_g, tile_lo, tile_hi, tile_new, fill, n_used.reshape(1).astype(jnp.int32)


def kernel(x, c, w_ada, b_ada, g_mix, w_in, conv_dw_w, conv_dw_b, conv_ln_g, conv_ln_b, lru_conv_w,
           lru_conv_b, lru_w_a, lru_b_a, lru_w_i, lru_b_i, lru_lam, gla_w_gate, gla_b_gate, gla_norm_g,
           w_out, g_ffn, w_route_group, b_route_group, w_route_expert, b_route_expert, w_gate, w_up,
           w_down, g_final):
    bsz, seq, d = x.shape
    n_layers = w_ada.shape[0]
    n_tok = bsz * seq
    assert d == D_MODEL and seq % max(MIX_TS, ROUTE_TT, COMB_TC) == 0 and MIX_TS % GLA_BLOCK == 0
    assert n_tok % PERM_TB == 0
    n_tiles = n_tok // FFN_TM + N_BUCKETS
    n_rows = n_tiles * FFN_TM

    mod_all = _modulation(c, w_ada, b_ada).reshape(n_layers, bsz, 6, d)

    pending_moe = None
    for l in range(n_layers):
        p = _prep_layer(l, w_in, conv_dw_w, conv_dw_b, conv_ln_g, conv_ln_b, lru_conv_w, lru_conv_b,
                        lru_w_a, lru_b_a, lru_w_i, lru_b_i, lru_lam, gla_w_gate, gla_b_gate, gla_norm_g,
                        w_out, g_mix)
        mod = mod_all[l]
        x = _token_mix(x, mod, p, pending_moe)

        x2d = x.reshape(n_tok, d)
        w_r = jnp.concatenate(
            [w_route_group[l], w_route_expert[l].transpose(1, 0, 2).reshape(d, N_GROUPS * EXPERTS_PER_GROUP)],
            axis=1)
        n_logits = w_r.shape[1]
        w_r = jnp.pad(w_r, ((0, 0), (0, LANES - n_logits)))
        b_r = jnp.pad(jnp.concatenate([b_route_group[l], b_route_expert[l].reshape(-1)]),
                      (0, LANES - n_logits))[None, :]
        xp, meta, counts = _route(x2d, mod, g_ffn[l][None, :], w_r, b_r, seq)
        bucket = meta[:, 0].astype(jnp.int32)
        rank = meta[:, 1].astype(jnp.int32)
        starts, tile_g, tile_lo, tile_hi, tile_new, fill, n_used = _bucket_layout(
            counts[0, :N_BUCKETS], n_tiles)
        in_bucket = bucket[:, None] == jnp.arange(N_BUCKETS, dtype=jnp.int32)[None, :]
        pos = rank + jnp.sum(jnp.where(in_bucket, starts[None, :], 0), axis=1)

        xs = _permute(fill, pos, xp, n_rows)
        y = _expert_ffn(l, tile_g, tile_lo, tile_hi, tile_new, n_used, xs, w_gate, w_up, w_down)
        pending_moe = (pos, mod, y)
    return _combine(pos, x2d, mod, g_final[None, :], y, seq).reshape(bsz, seq, d)
```

```python
import functools

import jax
import jax.numpy as jnp
import numpy as np
from jax import lax
from jax.experimental import pallas as pl
from jax.experimental.pallas import tpu as pltpu

D_MODEL = 1024
CONV_W = 256
LRU_W = 384
GLA_V = 384
CONV_K = 31
LRU_CONV_K = 4
LRU_BLOCKS = 6
LRU_BW = 64
LRU_C = 8.0
GLA_HEADS = 4
GLA_DV = 96
GLA_DK = 48
GLA_RANK = 16
GLA_TAU = 16.0
GLA_CHUNK = 64
N_GROUPS = 4
EXPERTS_PER_GROUP = 4
D_EXPERT = 512
EPS = 1e-6

LANES = 128
SUBLANES = 8
VMEM_LIMIT_BYTES = 56 * 1024 * 1024

DK_PAD = 64
DV_PAD = 128
QK_W = GLA_HEADS * DK_PAD
V_W = GLA_HEADS * DV_PAD
OFF_CVV = 0
OFF_CVG = OFF_CVV + CONV_W
OFF_LRX = OFF_CVG + CONV_W
OFF_LRY = OFF_LRX + LRU_W
OFF_Q = OFF_LRY + LRU_W
OFF_K = OFF_Q + QK_W
OFF_V = OFF_K + QK_W
OFF_GLR = OFF_V + V_W
OFF_OG = OFF_GLR + LANES
N_IN = OFF_OG + V_W
MIX_W = CONV_W + LRU_W + V_W

CONV_HIST = 32
LRU_HIST = 8
GLA_BLOCK = 256

N_PAIRS = 6
N_BUCKETS = N_GROUPS * N_PAIRS
ROW_W = D_MODEL + LANES

MIX_TS = 256
ROUTE_TT = 512
FFN_TM = 256
PERM_TB = 512
COMB_TC = 512
DMA_UNROLL = 8
GATHER_SLOTS = 3

_F32 = jnp.float32
_BF16 = jnp.bfloat16


def _sigmoid(x):
    return 1.0 / (1.0 + jnp.exp(-x))


def _silu(x):
    return x * _sigmoid(x)


def _dot(a, b):
    return jnp.dot(a, b, preferred_element_type=_F32)


def _dot_nt(a, b):
    return lax.dot_general(a, b, (((1,), (1,)), ((), ())), preferred_element_type=_F32)


def _split2(x):
    hi = x.astype(_BF16)
    return hi, (x - hi.astype(_F32)).astype(_BF16)


def _mod_kernel(c_ref, w_ref, b_ref, o_ref):
    c_act = _silu(c_ref[...])
    o_ref[0] = _dot(c_act.astype(_BF16), w_ref[0].astype(_BF16)) + b_ref[0]


def _modulation(c, w_ada, b_ada):
    n_layers, d, n = w_ada.shape
    bsz = c.shape[0]
    tn = 1536
    return pl.pallas_call(
        _mod_kernel,
        out_shape=jax.ShapeDtypeStruct((n_layers, bsz, n), _F32),
        grid=(n_layers, n // tn),
        in_specs=[
            pl.BlockSpec((bsz, d), lambda l, j: (0, 0)),
            pl.BlockSpec((1, d, tn), lambda l, j: (l, 0, j)),
            pl.BlockSpec((1, 1, tn), lambda l, j: (l, 0, j)),
        ],
        out_specs=pl.BlockSpec((1, bsz, tn), lambda l, j: (l, 0, j)),
        compiler_params=pltpu.CompilerParams(
            dimension_semantics=("arbitrary", "arbitrary"), vmem_limit_bytes=VMEM_LIMIT_BYTES),
        name="adaln_mod",
    )(c, w_ada, b_ada.reshape(n_layers, 1, n))


def _mix_kernel(x_ref, mod_ref, gmix_ref, win_ref, cw_ref, cb_ref, lng_ref, lnb_ref,
                lcw_ref, lcb_ref, wai_ref, bai_ref, lam_ref, wgate_ref, bgate_ref, gnorm_ref,
                wout_ref, *rest, fused_moe_gather):
    if fused_moe_gather:
        pos_cur, pos_nxt, pos_ahd, modp_ref, y_hbm, o_ref = rest[:6]
        ybuf, gsem = rest[-2:]
        rest = rest[6:-2]
    else:
        o_ref, rest = rest[0], rest[1:]
    ubuf, lbuf, hcar, state, tri_ref, tria_ref, csel_ref, causal_ref, hdiag_ref = rest
    ts = x_ref.shape[1]
    s_idx = pl.program_id(1)

    @pl.when(s_idx == 0)
    def _():
        ubuf[0:CONV_HIST, :] = jnp.zeros((CONV_HIST, CONV_W), _F32)
        lbuf[0:LRU_HIST, :] = jnp.zeros((LRU_HIST, LRU_W), _F32)
        hcar[...] = jnp.zeros_like(hcar)
        state[...] = jnp.zeros_like(state)
        ri = lax.broadcasted_iota(jnp.int32, (GLA_BLOCK, GLA_BLOCK), 0)
        ci = lax.broadcasted_iota(jnp.int32, (GLA_BLOCK, GLA_BLOCK), 1)
        same_chunk = ri // GLA_CHUNK == ci // GLA_CHUNK
        causal_f = jnp.where(same_chunk & (ci <= ri), 1.0, 0.0)
        causal_ref[...] = causal_f
        tri_ref[...] = causal_f.astype(_BF16)
        tria_ref[...] = jnp.where(same_chunk & (ci > ri), 1.0, 0.0).astype(_BF16)
        sel_r = lax.broadcasted_iota(jnp.int32, (LANES, GLA_BLOCK), 0)
        sel_c = lax.broadcasted_iota(jnp.int32, (LANES, GLA_BLOCK), 1)
        csel_ref[...] = jnp.where(sel_r == sel_c // GLA_CHUNK, 1.0, 0.0).astype(_BF16)
        st_row_head = lax.broadcasted_iota(jnp.int32, (QK_W, V_W), 0) // DK_PAD
        st_col_head = lax.broadcasted_iota(jnp.int32, (QK_W, V_W), 1) // DV_PAD
        hdiag_ref[...] = jnp.where(st_row_head == st_col_head, 1.0, 0.0)

    x = x_ref[0]
    if fused_moe_gather:
        step = pl.program_id(0) * pl.num_programs(1) + s_idx
        n_steps = pl.num_programs(0) * pl.num_programs(1)
        slot = step % GATHER_SLOTS

        def row_copy(pos_ref, j, to_slot):
            return pltpu.make_async_copy(y_hbm.at[pl.ds(pos_ref[j], 1)], ybuf.at[to_slot, pl.ds(j, 1)],
                                         gsem.at[to_slot])

        def wait_rows(of_slot):
            pltpu.make_async_copy(y_hbm.at[pl.ds(0, ts)], ybuf.at[of_slot], gsem.at[of_slot]).wait()

        @pl.when(step == 0)
        def _():
            def issue(j, carry):
                row_copy(pos_cur, j, 0).start()
                row_copy(pos_nxt, j, 1).start()
                return carry
            lax.fori_loop(0, ts, issue, 0, unroll=DMA_UNROLL)

        wait_rows(slot)
        x = x + modp_ref[0, 5:6, :] * ybuf[slot]

        ahead_slot = (step + GATHER_SLOTS - 1) % GATHER_SLOTS
        for j in range(ts):
            row_copy(pos_ahd, j, ahead_slot).start()

    sh1 = mod_ref[0, 0:1, :]
    sc1 = mod_ref[0, 1:2, :]
    gt1 = mod_ref[0, 2:3, :]
    ms = jnp.mean(x * x, axis=-1, keepdims=True)
    h = (x * lax.rsqrt(ms + EPS) * gmix_ref[...]) * (1.0 + sc1) + sh1
    hb = h.astype(_BF16)

    def proj(off, width):
        return _dot(hb, win_ref[:, off:off + width])

    out = {}
    _run_alternately(
        _conv_lru_steps(out, proj, cw_ref, cb_ref, lng_ref, lnb_ref, lcw_ref, lcb_ref, wai_ref, bai_ref,
                        lam_ref, ubuf, lbuf, hcar, ts),
        _gla_steps(out, proj, wgate_ref, bgate_ref, gnorm_ref, state, tri_ref, tria_ref, csel_ref,
                   causal_ref, hdiag_ref, ts))

    mixed = (_dot(out["o_g"].astype(_BF16), wout_ref[CONV_W + LRU_W:MIX_W, :])
             + _dot(out["r_out"].astype(_BF16), wout_ref[CONV_W:CONV_W + LRU_W, :])
             + _dot(out["u_out"].astype(_BF16), wout_ref[0:CONV_W, :]))
    o_ref[0] = x + gt1 * mixed

    if fused_moe_gather:
        @pl.when(step == n_steps - 1)
        def _():
            wait_rows((step + 1) % GATHER_SLOTS)
            wait_rows((step + 2) % GATHER_SLOTS)


def _run_alternately(*step_generators):
    clock = [0] * len(step_generators)
    live = list(range(len(step_generators)))
    while live:
        i = min(live, key=lambda j: clock[j])
        try:
            clock[i] += next(step_generators[i]) * (1 if i else 0.5)
        except StopIteration:
            live.remove(i)


def _conv_lru_steps(out, proj, cw_ref, cb_ref, lng_ref, lnb_ref, lcw_ref, lcb_ref, wai_ref, bai_ref,
                    lam_ref, ubuf, lbuf, hcar, ts):
    u = proj(OFF_CVV, CONV_W) * _sigmoid(proj(OFF_CVG, CONV_W))
    ubuf[CONV_HIST:CONV_HIST + ts, :] = u
    yield 500
    lbuf[LRU_HIST:LRU_HIST + ts, :] = proj(OFF_LRX, LRU_W)
    xb = jnp.broadcast_to(lcb_ref[...], (ts, LRU_W))
    for k in range(LRU_CONV_K):
        off = LRU_HIST - (LRU_CONV_K - 1) + k
        xb = xb + lcw_ref[k:k + 1, :] * lbuf[off:off + ts, :]
    lbuf[0:LRU_HIST, :] = lbuf[ts:ts + LRU_HIST, :]
    gates = _dot(xb.astype(_BF16), wai_ref[...]) + bai_ref[...]
    yield 800

    acc = jnp.broadcast_to(cb_ref[...], (ts, CONV_W))
    for r in range(SUBLANES):
        z = None
        for a_blk in range((CONV_K - 1 - r) // SUBLANES + 1):
            k = CONV_K - 1 - (SUBLANES * a_blk + r)
            off = CONV_HIST - SUBLANES * (a_blk + 1)
            term = cw_ref[k:k + 1, :] * ubuf[off:off + ts + SUBLANES, :]
            z = term if z is None else z + term
        acc = acc + z[SUBLANES - r:SUBLANES - r + ts]
        yield 200
    ubuf[0:CONV_HIST, :] = ubuf[ts:ts + CONV_HIST, :]
    mu = jnp.mean(acc, axis=-1, keepdims=True)
    cen = acc - mu
    var = jnp.mean(cen * cen, axis=-1, keepdims=True)
    out["u_out"] = _silu(cen * lax.rsqrt(var + EPS) * lng_ref[...] + lnb_ref[...])
    yield 250

    r_gate = _sigmoid(gates[:, 0:LRU_W])
    i_gate = _sigmoid(gates[:, LRU_W:2 * LRU_W])
    lam = lam_ref[...]
    softplus_neg_lam = jnp.maximum(-lam, 0.0) + jnp.log1p(jnp.exp(-jnp.abs(lam)))
    log_a = (-LRU_C) * r_gate * softplus_neg_lam
    a = jnp.exp(log_a)
    mult = jnp.sqrt(jnp.tanh(-log_a) * (a * a + 1.0))
    u_in = mult * (i_gate * xb)
    yield 600
    row = lax.broadcasted_iota(jnp.int32, a.shape, 0) % SUBLANES
    d = 1
    while d < SUBLANES:
        keep = row >= d
        a_sh = jnp.where(keep, pltpu.roll(a, d, axis=0), 1.0)
        u_sh = jnp.where(keep, pltpu.roll(u_in, d, axis=0), 0.0)
        u_in = u_in + a * u_sh
        a = a * a_sh
        d *= 2
        yield 200
    groups = []
    h_prev = hcar[0:1, :]
    for g in range(ts // SUBLANES):
        r0 = g * SUBLANES
        h_g = u_in[r0:r0 + SUBLANES] + a[r0:r0 + SUBLANES] * h_prev
        groups.append(h_g)
        h_prev = h_g[SUBLANES - 1:SUBLANES]
        if g % SUBLANES == SUBLANES - 1:
            yield 50
    h_lru = jnp.concatenate(groups, axis=0)
    hcar[...] = jnp.broadcast_to(h_prev, hcar.shape)
    out["r_out"] = h_lru * jax.nn.gelu(proj(OFF_LRY, LRU_W), approximate=True)


def _gla_steps(out, proj, wgate_ref, bgate_ref, gnorm_ref, state, tri_ref, tria_ref, csel_ref, causal_ref,
               hdiag_ref, ts):
    zg = proj(OFF_GLR, LANES)
    glog = _dot(zg.astype(_BF16), wgate_ref[...]) + bgate_ref[...]
    lg = (jnp.minimum(glog, 0.0) - jnp.log1p(jnp.exp(-jnp.abs(glog)))) * (1.0 / GLA_TAU)
    yield 300
    zq = proj(OFF_Q, QK_W) * (GLA_DK ** -0.5)
    zk = proj(OFF_K, QK_W)
    yield 500
    zv = proj(OFF_V, V_W)
    yield 500

    nblk = ts // GLA_BLOCK
    cpb = GLA_BLOCK // GLA_CHUNK
    tri = tri_ref[...]
    tri_after = tria_ref[...]
    chunk_sel = csel_ref[...]
    causal = causal_ref[...] > 0.5
    qk_lane_head = lax.broadcasted_iota(jnp.int32, (1, QK_W), 1) // DK_PAD
    t_lane_chunk = lax.broadcasted_iota(jnp.int32, (1, GLA_BLOCK), 1) // GLA_CHUNK

    o_blocks = []
    for blk in range(nblk):
        r0 = blk * GLA_BLOCK
        lg_b = lg[r0:r0 + GLA_BLOCK]
        p_hi, p_lo = _split2(lg_b)
        b = _dot(tri, p_hi) + _dot(tri, p_lo)
        b_rest = _dot(tri_after, p_hi) + _dot(tri_after, p_lo)
        b_tot = _dot(chunk_sel, p_hi) + _dot(chunk_sel, p_lo)
        yield 500
        q_in =zq[r0:r0 + GLA_BLOCK] * jnp.exp(b)
        k_blk = zk[r0:r0 + GLA_BLOCK]
        k_in = (k_blk * jnp.exp(-b)).astype(_BF16)
        k_out_t = (k_blk * jnp.exp(b_rest)).T.astype(_BF16)
        v_b = zv[r0:r0 + GLA_BLOCK].astype(_BF16)
        q_in_b = q_in.astype(_BF16)
        decay_cols = jnp.exp(b_tot).T

        o_heads = []
        for hd in range(GLA_HEADS):
            q_h = jnp.where(qk_lane_head == hd, q_in_b, jnp.zeros_like(q_in_b))
            sc = jnp.where(causal, _dot_nt(q_h, k_in), 0.0)
            o_heads.append(_dot(sc.astype(_BF16), v_b[:, hd * DV_PAD:(hd + 1) * DV_PAD]))
            yield 300
        o_intra = jnp.concatenate(o_heads, axis=1)

        o_inter = []
        st = state[...]
        for c in range(cpb):
            c0 = c * GLA_CHUNK
            o_inter.append(_dot(q_in_b[c0:c0 + GLA_CHUNK], st.astype(_BF16)))
            kv = _dot(jnp.where(t_lane_chunk == c, k_out_t, jnp.zeros_like(k_out_t)), v_b)
            st = st * decay_cols[:, c:c + 1] + jnp.where(hdiag_ref[...] > 0.5, kv, 0.0)
            yield 400
        state[...] = st
        o_blocks.append(o_intra + jnp.concatenate(o_inter, axis=0))
    o = o_blocks[0] if nblk == 1 else jnp.concatenate(o_blocks, axis=0)

    og = proj(OFF_OG, V_W)
    o_parts = []
    for hd in range(GLA_HEADS):
        o_h = o[:, hd * DV_PAD:(hd + 1) * DV_PAD]
        ms_h = jnp.sum(o_h * o_h, axis=-1, keepdims=True) * (1.0 / GLA_DV)
        o_parts.append(o_h * lax.rsqrt(ms_h + EPS))
    o_n = jnp.concatenate(o_parts, axis=1) * gnorm_ref[...]
    out["o_g"] = o_n * _silu(og)


def _token_mix(x, mod, p, moe=None):
    bsz, seq, d = x.shape
    ts = MIX_TS
    n_s = seq // ts
    full = lambda shape: pl.BlockSpec(shape, lambda b, s: (0,) * len(shape))
    in_specs = [
        pl.BlockSpec((1, ts, d), lambda b, s: (b, s, 0)),
        pl.BlockSpec((1, 6, d), lambda b, s: (b, 0, 0)),
        full((1, d)),
        full((d, N_IN)),
        full((CONV_HIST, CONV_W)), full((1, CONV_W)), full((1, CONV_W)), full((1, CONV_W)),
        full((LRU_CONV_K, LRU_W)), full((1, LRU_W)),
        full((LRU_W, 2 * LRU_W)), full((1, 2 * LRU_W)), full((1, LRU_W)),
        full((LANES, QK_W)), full((1, QK_W)), full((1, V_W)),
        full((MIX_W, d)),
    ]
    scratch_shapes = [
        pltpu.VMEM((CONV_HIST + ts, CONV_W), _F32),
        pltpu.VMEM((LRU_HIST + ts, LRU_W), _F32),
        pltpu.VMEM((SUBLANES, LRU_W), _F32),
        pltpu.VMEM((QK_W, V_W), _F32),
        pltpu.VMEM((GLA_BLOCK, GLA_BLOCK), _BF16),
        pltpu.VMEM((GLA_BLOCK, GLA_BLOCK), _BF16),
        pltpu.VMEM((LANES, GLA_BLOCK), _BF16),
        pltpu.VMEM((GLA_BLOCK, GLA_BLOCK), _F32),
        pltpu.VMEM((QK_W, V_W), _F32),
    ]
    args = [x, mod, p["g_mix"], p["w_in"], p["conv_w"], p["conv_b"], p["ln_g"], p["ln_b"],
            p["lru_conv_w"], p["lru_conv_b"], p["w_ai"], p["b_ai"], p["lam"], p["w_gate"], p["b_gate"],
            p["g_norm"], p["w_out"]]
    if moe is not None:
        pos, mod_prev, y = moe
        last = bsz * n_s - 1
        in_specs += [
            pl.BlockSpec((ts,), lambda b, s: (b * n_s + s,), memory_space=pltpu.SMEM),
            pl.BlockSpec((ts,), lambda b, s: (jnp.minimum(b * n_s + s + 1, last),), memory_space=pltpu.SMEM),
            pl.BlockSpec((ts,), lambda b, s: (jnp.minimum(b * n_s + s + 2, last),), memory_space=pltpu.SMEM),
            pl.BlockSpec((1, 6, d), lambda b, s: (b, 0, 0)),
            pl.BlockSpec(memory_space=pl.ANY),
        ]
        scratch_shapes += [pltpu.VMEM((GATHER_SLOTS, ts, d), _F32), pltpu.SemaphoreType.DMA((GATHER_SLOTS,))]
        args += [pos, pos, pos, mod_prev, y]
    return pl.pallas_call(
        functools.partial(_mix_kernel, fused_moe_gather=moe is not None),
        out_shape=jax.ShapeDtypeStruct(x.shape, _F32),
        grid=(bsz, n_s),
        in_specs=in_specs,
        out_specs=pl.BlockSpec((1, ts, d), lambda b, s: (b, s, 0)),
        scratch_shapes=scratch_shapes,
        compiler_params=pltpu.CompilerParams(
            dimension_semantics=("arbitrary", "arbitrary"), vmem_limit_bytes=VMEM_LIMIT_BYTES),
        name="token_mix",
    )(*args)


def _route_kernel(x_ref, mod_ref, g_ref, wr_ref, br_ref, xp_ref, meta_ref, cnt_ref, carry):
    tt = x_ref.shape[0]
    i = pl.program_id(0)

    @pl.when(i == 0)
    def _():
        carry[...] = jnp.zeros_like(carry)

    x = x_ref[...]
    sh2 = mod_ref[0, 3:4, :]
    sc2 = mod_ref[0, 4:5, :]
    ms = jnp.mean(x * x, axis=-1, keepdims=True)
    h = (x * lax.rsqrt(ms + EPS) * g_ref[...]) * (1.0 + sc2) + sh2

    h_hi, h_lo = _split2(h)
    w_hi, w_lo = _split2(wr_ref[...])
    w_cat = jnp.concatenate([w_hi, w_lo], axis=1)
    prod = _dot(h_hi, w_cat) + _dot(h_lo, w_cat)
    logits = prod[:, 0:LANES] + prod[:, LANES:2 * LANES] + br_ref[...]

    lane = lax.broadcasted_iota(jnp.int32, (tt, LANES), 1)
    lane_f = lane.astype(_F32)
    neg = -jnp.inf
    big = float(LANES)

    def first_argmax(vals, vmax):
        return jnp.min(jnp.where(vals == vmax, lane_f, big), axis=-1, keepdims=True).astype(jnp.int32)

    gl = jnp.where(lane < N_GROUPS, logits, neg)
    gmax = jnp.max(gl, axis=-1, keepdims=True)
    g_star = first_argmax(gl, gmax)
    p_sel = 1.0 / jnp.sum(jnp.exp(gl - gmax), axis=-1, keepdims=True)
    base = N_GROUPS + EXPERTS_PER_GROUP * g_star
    el = jnp.where((lane >= base) & (lane < base + EXPERTS_PER_GROUP), logits, neg)
    v0 = jnp.max(el, axis=-1, keepdims=True)
    i0 = first_argmax(el, v0)
    el2 = jnp.where(lane == i0, neg, el)
    v1 = jnp.max(el2, axis=-1, keepdims=True)
    i1 = first_argmax(el2, v1)
    ex = jnp.exp(v1 - v0)
    wt0 = p_sel / (1.0 + ex)
    wt1 = p_sel * ex / (1.0 + ex)
    e0 = i0 - base
    e1 = i1 - base
    e_lo = jnp.minimum(e0, e1)
    e_hi = jnp.maximum(e0, e1)
    w_lo = jnp.where(e0 < e1, wt0, wt1)
    w_hi = jnp.where(e0 < e1, wt1, wt0)
    pair = (e_lo * (2 * EXPERTS_PER_GROUP - 1 - e_lo)) // 2 + (e_hi - e_lo - 1)
    bucket = g_star * N_PAIRS + pair

    onehot = lane == bucket
    onehot_f = jnp.where(onehot, 1.0, 0.0)
    ri = lax.broadcasted_iota(jnp.int32, (tt, tt), 0)
    ci = lax.broadcasted_iota(jnp.int32, (tt, tt), 1)
    strict = jnp.where(ci < ri, 1.0, 0.0).astype(_BF16)
    prefix = _dot(strict, onehot_f.astype(_BF16)) + carry[0:1, :]
    rank = jnp.sum(jnp.where(onehot, prefix, 0.0), axis=-1, keepdims=True)
    carry[...] = carry[...] + jnp.sum(onehot_f, axis=0, keepdims=True)
    cnt_ref[...] = carry[...]

    meta = jnp.where(lane == 0, bucket.astype(_F32), jnp.where(lane == 1, rank, 0.0))
    meta_ref[...] = meta

    xp_ref[:, 0:D_MODEL] = h
    xp_ref[:, D_MODEL:ROW_W] = jnp.where(lane == 0, w_lo, jnp.where(lane == 1, w_hi, 0.0))


def _route(x2d, mod, g_ffn, w_r, b_r, seq):
    n_tok, d = x2d.shape
    tt = ROUTE_TT
    tiles_per_seq = seq // tt
    return pl.pallas_call(
        _route_kernel,
        out_shape=(
            jax.ShapeDtypeStruct((n_tok, ROW_W), _F32),
            jax.ShapeDtypeStruct((n_tok, LANES), _F32),
            jax.ShapeDtypeStruct((SUBLANES, LANES), _F32),
        ),
        grid=(n_tok // tt,),
        in_specs=[
            pl.BlockSpec((tt, d), lambda i: (i, 0)),
            pl.BlockSpec((1, 6, d), lambda i: (i // tiles_per_seq, 0, 0)),
            pl.BlockSpec((1, d), lambda i: (0, 0)),
            pl.BlockSpec((d, LANES), lambda i: (0, 0)),
            pl.BlockSpec((1, LANES), lambda i: (0, 0)),
        ],
        out_specs=(
            pl.BlockSpec((tt, ROW_W), lambda i: (i, 0)),
            pl.BlockSpec((tt, LANES), lambda i: (i, 0)),
            pl.BlockSpec((SUBLANES, LANES), lambda i: (0, 0)),
        ),
        scratch_shapes=[pltpu.VMEM((SUBLANES, LANES), _F32)],
        compiler_params=pltpu.CompilerParams(
            dimension_semantics=("arbitrary",), vmem_limit_bytes=VMEM_LIMIT_BYTES),
        name="moe_route",
    )(x2d, mod, g_ffn, w_r, b_r)


def _permute_kernel(fill_ref, pos_ref, xp_ref, xs_hbm, zbuf, sem, zsem):
    tb = pos_ref.shape[0]

    @pl.when(pl.program_id(0) == 0)
    def _():
        zbuf[...] = jnp.zeros_like(zbuf)
        n_used = fill_ref[2 * N_BUCKETS]
        n_tiles = xs_hbm.shape[0] // FFN_TM
        for wait in (False, True):
            for b in range(N_BUCKETS):
                for row0, cond in ((fill_ref[b], fill_ref[N_BUCKETS + b] > 0),
                                   ((n_used + b) * FFN_TM, n_used + b < n_tiles)):
                    @pl.when(cond)
                    def _():
                        dst = xs_hbm.at[pl.ds(pl.multiple_of(row0, FFN_TM), FFN_TM)]
                        fill = pltpu.make_async_copy(zbuf, dst, zsem)
                        fill.wait() if wait else fill.start()

    for j in range(tb):
        pltpu.make_async_copy(xp_ref.at[pl.ds(j, 1)], xs_hbm.at[pl.ds(pos_ref[j], 1)], sem).start()
    pltpu.make_async_copy(xp_ref, xs_hbm.at[pl.ds(0, tb)], sem).wait()


def _permute(fill_starts, pos, xp, n_rows):
    n_tok = xp.shape[0]
    tb = PERM_TB
    return pl.pallas_call(
        _permute_kernel,
        out_shape=jax.ShapeDtypeStruct((n_rows, ROW_W), _F32),
        grid_spec=pltpu.PrefetchScalarGridSpec(
            num_scalar_prefetch=1,
            grid=(n_tok // tb,),
            in_specs=[
                pl.BlockSpec((tb,), lambda i, fs: (i,), memory_space=pltpu.SMEM),
                pl.BlockSpec((tb, ROW_W), lambda i, fs: (i, 0)),
            ],
            out_specs=pl.BlockSpec(memory_space=pl.ANY),
            scratch_shapes=[pltpu.VMEM((FFN_TM, ROW_W), _F32), pltpu.SemaphoreType.DMA,
                            pltpu.SemaphoreType.DMA],
        ),
        compiler_params=pltpu.CompilerParams(
            dimension_semantics=("arbitrary",), vmem_limit_bytes=VMEM_LIMIT_BYTES),
        name="moe_permute",
    )(fill_starts, pos, xp)


def _ffn_kernel(tg_ref, tlo_ref, thi_ref, clo_ref, chi_ref, ng_ref, nlo_ref, nhi_ref, plo_ref, phi_ref, nused_ref,
                xs_ref, wg_hbm, wu_hbm, wd_hbm, y_ref, sg, su, sd, ag, au, ad, sem, *, layer):
    i = pl.program_id(0)

    def weight_copies(slot, g, e):
        return (pltpu.make_async_copy(wg_hbm.at[layer, g, e], sg.at[slot], sem.at[slot, 0]),
                pltpu.make_async_copy(wu_hbm.at[layer, g, e], su.at[slot], sem.at[slot, 1]),
                pltpu.make_async_copy(wd_hbm.at[layer, g, e], sd.at[slot], sem.at[slot, 2]))

    @pl.when(i == 0)
    def _():
        for cp in weight_copies(0, tg_ref[0], tlo_ref[0]) + weight_copies(1, tg_ref[0], thi_ref[0]):
            cp.start()

    for slot, changed_ref, e_ref in ((0, clo_ref, tlo_ref), (1, chi_ref, thi_ref)):
        @pl.when(changed_ref[i] == 1)
        def _():
            for cp in weight_copies(slot, tg_ref[i], e_ref[i]):
                cp.wait()
            ag[slot] = sg[slot].astype(_BF16)
            au[slot] = su[slot].astype(_BF16)
            ad[slot] = sd[slot].astype(_BF16)

    for slot, prefetch_ref, e_ref in ((0, plo_ref, nlo_ref), (1, phi_ref, nhi_ref)):
        @pl.when(prefetch_ref[i] == 1)
        def _():
            for cp in weight_copies(slot, ng_ref[i], e_ref[i]):
                cp.start()

    @pl.when(i < nused_ref[0])
    def _():
        xb = xs_ref[:, 0:D_MODEL].astype(_BF16)
        info = xs_ref[:, D_MODEL:ROW_W]

        def expert(slot):
            hid = _silu(_dot(xb, ag[slot])) * _dot(xb, au[slot]) * info[:, slot:slot + 1]
            return _dot(hid.astype(_BF16), ad[slot])

        y_ref[...] = expert(0) + expert(1)

    @pl.when(i >= nused_ref[0])
    def _():
        y_ref[...] = jnp.zeros_like(y_ref)


def _expert_ffn(layer, plan, xs, w_gate, w_up, w_down):
    n_rows = xs.shape[0]
    tm = FFN_TM
    n_tiles = n_rows // tm

    def row_map(i, *prefetch):
        n_used = prefetch[-1]
        return (jnp.minimum(i, n_used[0] - 1), 0)

    hbm = pl.BlockSpec(memory_space=pl.ANY)
    return pl.pallas_call(
        functools.partial(_ffn_kernel, layer=layer),
        out_shape=jax.ShapeDtypeStruct((n_rows, D_MODEL), _F32),
        grid_spec=pltpu.PrefetchScalarGridSpec(
            num_scalar_prefetch=len(plan),
            grid=(n_tiles,),
            in_specs=[pl.BlockSpec((tm, ROW_W), row_map), hbm, hbm, hbm],
            out_specs=pl.BlockSpec((tm, D_MODEL), lambda i, *prefetch: (i, 0)),
            scratch_shapes=[
                pltpu.VMEM((2, D_MODEL, D_EXPERT), _F32), pltpu.VMEM((2, D_MODEL, D_EXPERT), _F32),
                pltpu.VMEM((2, D_EXPERT, D_MODEL), _F32),
                pltpu.VMEM((2, D_MODEL, D_EXPERT), _BF16), pltpu.VMEM((2, D_MODEL, D_EXPERT), _BF16),
                pltpu.VMEM((2, D_EXPERT, D_MODEL), _BF16),
                pltpu.SemaphoreType.DMA((2, 3)),
            ],
        ),
        compiler_params=pltpu.CompilerParams(
            dimension_semantics=("arbitrary",), vmem_limit_bytes=VMEM_LIMIT_BYTES),
        name="moe_ffn",
    )(*plan, xs, w_gate, w_up, w_down)


def _combine_kernel(pos_ref, x_ref, mod_ref, gfin_ref, y_hbm, o_ref, ybuf, sem):
    tc = x_ref.shape[0]
    for j in range(tc):
        pltpu.make_async_copy(y_hbm.at[pl.ds(pos_ref[j], 1)], ybuf.at[pl.ds(j, 1)], sem).start()
    pltpu.make_async_copy(y_hbm.at[pl.ds(0, tc)], ybuf, sem).wait()

    gt2 = mod_ref[0, 5:6, :]
    out = x_ref[...] + gt2 * ybuf[...]
    ms = jnp.mean(out * out, axis=-1, keepdims=True)
    o_ref[...] = out * lax.rsqrt(ms + EPS) * gfin_ref[...]


def _combine(pos, x2d, mod, g_final, y, seq):
    n_tok, d = x2d.shape
    tc = COMB_TC
    tiles_per_seq = seq // tc
    return pl.pallas_call(
        _combine_kernel,
        out_shape=jax.ShapeDtypeStruct((n_tok, d), _F32),
        grid=(n_tok // tc,),
        in_specs=[
            pl.BlockSpec((tc,), lambda i: (i,), memory_space=pltpu.SMEM),
            pl.BlockSpec((tc, d), lambda i: (i, 0)),
            pl.BlockSpec((1, 6, d), lambda i: (i // tiles_per_seq, 0, 0)),
            pl.BlockSpec((1, d), lambda i: (0, 0)),
            pl.BlockSpec(memory_space=pl.ANY),
        ],
        out_specs=pl.BlockSpec((tc, d), lambda i: (i, 0)),
        scratch_shapes=[pltpu.VMEM((tc, d), _F32), pltpu.SemaphoreType.DMA],
        compiler_params=pltpu.CompilerParams(
            dimension_semantics=("arbitrary",), vmem_limit_bytes=VMEM_LIMIT_BYTES),
        name="moe_combine",
    )(pos, x2d, mod, g_final, y)


def _pad_heads(w, heads, width, padded):
    lead = w.shape[:-1]
    w = w.reshape(lead + (heads, width))
    w = jnp.pad(w, [(0, 0)] * len(lead) + [(0, 0), (0, padded - width)])
    return w.reshape(lead + (heads * padded,))


def _block_diag(w):
    n, bw, _ = w.shape
    eye = jnp.eye(n, dtype=w.dtype)
    return (eye[:, None, :, None] * w[:, :, None, :]).reshape(n * bw, n * bw)


def _prep_layer(l, w_in, conv_dw_w, conv_dw_b, conv_ln_g, conv_ln_b, lru_conv_w, lru_conv_b, lru_w_a,
                lru_b_a, lru_w_i, lru_b_i, lru_lam, gla_w_gate, gla_b_gate, gla_norm_g, w_out, g_mix):
    sizes = [CONV_W, CONV_W, LRU_W, LRU_W, GLA_HEADS * GLA_DK, GLA_HEADS * GLA_DK, GLA_V, GLA_RANK, GLA_V]
    cv_v, cv_g, lr_x, lr_y, q, k, v, g_lr, og = jnp.split(w_in[l], np.cumsum(sizes)[:-1].tolist(), axis=-1)
    w_in_p = jnp.concatenate([
        cv_v, cv_g, lr_x, lr_y,
        _pad_heads(q, GLA_HEADS, GLA_DK, DK_PAD), _pad_heads(k, GLA_HEADS, GLA_DK, DK_PAD),
        _pad_heads(v, GLA_HEADS, GLA_DV, DV_PAD),
        jnp.pad(g_lr, ((0, 0), (0, LANES - GLA_RANK))),
        _pad_heads(og, GLA_HEADS, GLA_DV, DV_PAD)], axis=-1).astype(_BF16)
    wo = w_out[l]
    wo_o = wo[CONV_W + LRU_W:].reshape(GLA_HEADS, GLA_DV, D_MODEL)
    wo_o = jnp.pad(wo_o, ((0, 0), (0, DV_PAD - GLA_DV), (0, 0))).reshape(V_W, D_MODEL)
    w_out_p = jnp.concatenate([wo[:CONV_W + LRU_W], wo_o], axis=0).astype(_BF16)
    w_gate_p = jnp.pad(_pad_heads(gla_w_gate[l], GLA_HEADS, GLA_DK, DK_PAD),
                       ((0, LANES - GLA_RANK), (0, 0))).astype(_BF16)
    return {
        "g_mix": g_mix[l][None, :],
        "w_in": w_in_p,
        "conv_w": jnp.pad(conv_dw_w[l], ((0, CONV_HIST - CONV_K), (0, 0))),
        "conv_b": conv_dw_b[l][None, :],
        "ln_g": conv_ln_g[l][None, :],
        "ln_b": conv_ln_b[l][None, :],
        "lru_conv_w": lru_conv_w[l],
        "lru_conv_b": lru_conv_b[l][None, :],
        "w_ai": jnp.concatenate([_block_diag(lru_w_a[l]), _block_diag(lru_w_i[l])], axis=1).astype(_BF16),
        "b_ai": jnp.concatenate([lru_b_a[l], lru_b_i[l]])[None, :],
        "lam": lru_lam[l][None, :],
        "w_gate": w_gate_p,
        "b_gate": _pad_heads(gla_b_gate[l], GLA_HEADS, GLA_DK, DK_PAD)[None, :],
        "g_norm": _pad_heads(gla_norm_g[l], GLA_HEADS, GLA_DV, DV_PAD)[None, :],
        "w_out": w_out_p,
    }


def _bucket_layout(counts, n_tiles):
    tm = FFN_TM
    counts = counts.astype(jnp.int32)
    tiles = (counts + tm - 1) // tm
    tile_end = jnp.cumsum(tiles)
    tile_start = tile_end - tiles
    starts = (tile_start * tm).astype(jnp.int32)
    n_used = tile_end[-1]
    tile_idx = jnp.minimum(jnp.arange(n_tiles, dtype=jnp.int32), n_used - 1)
    bucket_of = lambda t: jnp.sum((t[:, None] >= tile_end[None, :]).astype(jnp.int32), axis=1)
    pair_lo = jnp.array([0, 0, 0, 1, 1, 2], jnp.int32)
    pair_hi = jnp.array([1, 2, 3, 2, 3, 3], jnp.int32)
    experts = lambda b: (b // N_PAIRS, pair_lo[b % N_PAIRS], pair_hi[b % N_PAIRS])

    tile_bucket = bucket_of(tile_idx)
    g, lo, hi = experts(tile_bucket)
    first = jnp.concatenate([jnp.ones((1,), bool), tile_bucket[1:] != tile_bucket[:-1]])
    prev = lambda v: jnp.concatenate([jnp.full((1,), -1, jnp.int32), v[:-1]])
    changed_lo = first & ((g != prev(g)) | (lo != prev(lo)))
    changed_hi = first & ((g != prev(g)) | (hi != prev(hi)))
    next_first_tile = tile_end[tile_bucket]
    has_next = next_first_tile < n_used
    ng, nlo, nhi = experts(bucket_of(jnp.minimum(next_first_tile, n_used - 1)))
    prefetch_lo = first & has_next & ((ng != g) | (nlo != lo))
    prefetch_hi = first & has_next & ((ng != g) | (nhi != hi))
    as_i32 = lambda v: v.astype(jnp.int32)
    ffn_plan = (g, lo, hi, as_i32(changed_lo), as_i32(changed_hi), ng, nlo, nhi,
                as_i32(prefetch_lo), as_i32(prefetch_hi), n_used.reshape(1).astype(jnp.int32))
    fill = jnp.concatenate([(jnp.maximum(tile_end - 1, 0) * tm).astype(jnp.int32), tiles, n_used[None]])
    return starts, fill, ffn_plan


def kernel(x, c, w_ada, b_ada, g_mix, w_in, conv_dw_w, conv_dw_b, conv_ln_g, conv_ln_b, lru_conv_w,
           lru_conv_b, lru_w_a, lru_b_a, lru_w_i, lru_b_i, lru_lam, gla_w_gate, gla_b_gate, gla_norm_g,
           w_out, g_ffn, w_route_group, b_route_group, w_route_expert, b_route_expert, w_gate, w_up,
           w_down, g_final):
    bsz, seq, d = x.shape
    n_layers = w_ada.shape[0]
    n_tok = bsz * seq
    assert d == D_MODEL and seq % max(MIX_TS, ROUTE_TT, COMB_TC) == 0 and MIX_TS % GLA_BLOCK == 0
    assert n_tok % PERM_TB == 0
    n_tiles = n_tok // FFN_TM + N_BUCKETS
    n_rows = n_tiles * FFN_TM

    mod_all = _modulation(c, w_ada, b_ada).reshape(n_layers, bsz, 6, d)

    pending_moe = None
    for l in range(n_layers):
        p = _prep_layer(l, w_in, conv_dw_w, conv_dw_b, conv_ln_g, conv_ln_b, lru_conv_w, lru_conv_b,
                        lru_w_a, lru_b_a, lru_w_i, lru_b_i, lru_lam, gla_w_gate, gla_b_gate, gla_norm_g,
                        w_out, g_mix)
        mod = mod_all[l]
        x = _token_mix(x, mod, p, pending_moe)

        x2d = x.reshape(n_tok, d)
        w_r = jnp.concatenate(
            [w_route_group[l], w_route_expert[l].transpose(1, 0, 2).reshape(d, N_GROUPS * EXPERTS_PER_GROUP)],
            axis=1)
        n_logits = w_r.shape[1]
        w_r = jnp.pad(w_r, ((0, 0), (0, LANES - n_logits)))
        b_r = jnp.pad(jnp.concatenate([b_route_group[l], b_route_expert[l].reshape(-1)]),
                      (0, LANES - n_logits))[None, :]
        xp, meta, counts = _route(x2d, mod, g_ffn[l][None, :], w_r, b_r, seq)
        bucket = meta[:, 0].astype(jnp.int32)
        rank = meta[:, 1].astype(jnp.int32)
        starts, fill, ffn_plan = _bucket_layout(counts[0, :N_BUCKETS], n_tiles)
        in_bucket = bucket[:, None] == jnp.arange(N_BUCKETS, dtype=jnp.int32)[None, :]
        pos = rank + jnp.sum(jnp.where(in_bucket, starts[None, :], 0), axis=1)

        xs = _permute(fill, pos, xp, n_rows)
        y = _expert_ffn(l, ffn_plan, xs, w_gate, w_up, w_down)
        pending_moe = (pos, mod, y)
    return _combine(pos, x2d, mod, g_final[None, :], y, seq).reshape(bsz, seq, d)
```

```python
import functools

import jax
import jax.numpy as jnp
import numpy as np
from jax import lax
from jax.experimental import pallas as pl
from jax.experimental.pallas import tpu as pltpu

D_MODEL = 1024
CONV_W = 256
LRU_W = 384
GLA_V = 384
CONV_K = 31
LRU_CONV_K = 4
LRU_BLOCKS = 6
LRU_BW = 64
LRU_C = 8.0
GLA_HEADS = 4
GLA_DV = 96
GLA_DK = 48
GLA_RANK = 16
GLA_TAU = 16.0
GLA_CHUNK = 64
N_GROUPS = 4
EXPERTS_PER_GROUP = 4
D_EXPERT = 512
EPS = 1e-6

LANES = 128
SUBLANES = 8
VMEM_LIMIT_BYTES = 56 * 1024 * 1024

DK_PAD = 64
DV_PAD = 128
QK_W = GLA_HEADS * DK_PAD
V_W = GLA_HEADS * DV_PAD
OFF_CVV = 0
OFF_CVG = OFF_CVV + CONV_W
OFF_LRX = OFF_CVG + CONV_W
OFF_LRY = OFF_LRX + LRU_W
OFF_Q = OFF_LRY + LRU_W
OFF_K = OFF_Q + QK_W
OFF_V = OFF_K + QK_W
OFF_GLR = OFF_V + V_W
OFF_OG = OFF_GLR + LANES
N_IN = OFF_OG + V_W
MIX_W = CONV_W + LRU_W + V_W

CONV_HIST = 32
LRU_HIST = 8
GLA_BLOCK = 256

N_PAIRS = 6
N_BUCKETS = N_GROUPS * N_PAIRS
ROW_W = D_MODEL + LANES

MIX_TS = 512
ROUTE_TT = 512
FFN_TM = 256
PERM_TB = 512
COMB_TC = 512
DMA_UNROLL = 8
GATHER_SLOTS = 3

_F32 = jnp.float32
_BF16 = jnp.bfloat16


def _sigmoid(x):
    return 1.0 / (1.0 + jnp.exp(-x))


def _silu(x):
    return x * _sigmoid(x)


def _dot(a, b):
    return jnp.dot(a, b, preferred_element_type=_F32)


def _dot_nt(a, b):
    return lax.dot_general(a, b, (((1,), (1,)), ((), ())), preferred_element_type=_F32)


def _split2(x):
    hi = x.astype(_BF16)
    return hi, (x - hi.astype(_F32)).astype(_BF16)


def _mod_kernel(c_ref, w_ref, b_ref, o_ref):
    c_act = _silu(c_ref[...])
    o_ref[0] = _dot(c_act.astype(_BF16), w_ref[0].astype(_BF16)) + b_ref[0]


def _modulation(c, w_ada, b_ada):
    n_layers, d, n = w_ada.shape
    bsz = c.shape[0]
    tn = 1536
    return pl.pallas_call(
        _mod_kernel,
        out_shape=jax.ShapeDtypeStruct((n_layers, bsz, n), _F32),
        grid=(n_layers, n // tn),
        in_specs=[
            pl.BlockSpec((bsz, d), lambda l, j: (0, 0)),
            pl.BlockSpec((1, d, tn), lambda l, j: (l, 0, j)),
            pl.BlockSpec((1, 1, tn), lambda l, j: (l, 0, j)),
        ],
        out_specs=pl.BlockSpec((1, bsz, tn), lambda l, j: (l, 0, j)),
        compiler_params=pltpu.CompilerParams(
            dimension_semantics=("arbitrary", "arbitrary"), vmem_limit_bytes=VMEM_LIMIT_BYTES),
        name="adaln_mod",
    )(c, w_ada, b_ada.reshape(n_layers, 1, n))


def _mix_kernel(x_ref, mod_ref, gmix_ref, win_ref, cw_ref, cb_ref, lng_ref, lnb_ref,
                lcw_ref, lcb_ref, wai_ref, bai_ref, lam_ref, wgate_ref, bgate_ref, gnorm_ref,
                wout_ref, *rest, fused_moe_gather):
    if fused_moe_gather:
        pos_cur, pos_nxt, pos_ahd, modp_ref, y_hbm, o_ref = rest[:6]
        ybuf, gsem = rest[-2:]
        rest = rest[6:-2]
    else:
        o_ref, rest = rest[0], rest[1:]
    ubuf, lbuf, hcar, state, tri_ref, tria_ref, csel_ref, causal_ref, hdiag_ref = rest
    ts = x_ref.shape[1]
    s_idx = pl.program_id(1)

    @pl.when(s_idx == 0)
    def _():
        ubuf[0:CONV_HIST, :] = jnp.zeros((CONV_HIST, CONV_W), _F32)
        lbuf[0:LRU_HIST, :] = jnp.zeros((LRU_HIST, LRU_W), _F32)
        hcar[...] = jnp.zeros_like(hcar)
        state[...] = jnp.zeros_like(state)
        ri = lax.broadcasted_iota(jnp.int32, (GLA_BLOCK, GLA_BLOCK), 0)
        ci = lax.broadcasted_iota(jnp.int32, (GLA_BLOCK, GLA_BLOCK), 1)
        same_chunk = ri // GLA_CHUNK == ci // GLA_CHUNK
        causal_f = jnp.where(same_chunk & (ci <= ri), 1.0, 0.0)
        causal_ref[...] = causal_f
        tri_ref[...] = causal_f.astype(_BF16)
        tria_ref[...] = jnp.where(same_chunk & (ci > ri), 1.0, 0.0).astype(_BF16)
        sel_r = lax.broadcasted_iota(jnp.int32, (LANES, GLA_BLOCK), 0)
        sel_c = lax.broadcasted_iota(jnp.int32, (LANES, GLA_BLOCK), 1)
        csel_ref[...] = jnp.where(sel_r == sel_c // GLA_CHUNK, 1.0, 0.0).astype(_BF16)
        st_row_head = lax.broadcasted_iota(jnp.int32, (QK_W, V_W), 0) // DK_PAD
        st_col_head = lax.broadcasted_iota(jnp.int32, (QK_W, V_W), 1) // DV_PAD
        hdiag_ref[...] = jnp.where(st_row_head == st_col_head, 1.0, 0.0)

    x = x_ref[0]
    if fused_moe_gather:
        step = pl.program_id(0) * pl.num_programs(1) + s_idx
        n_steps = pl.num_programs(0) * pl.num_programs(1)
        slot = step % GATHER_SLOTS

        def row_copy(pos_ref, j, to_slot):
            return pltpu.make_async_copy(y_hbm.at[pl.ds(pos_ref[j], 1)], ybuf.at[to_slot, pl.ds(j, 1)],
                                         gsem.at[to_slot])

        def wait_rows(of_slot):
            pltpu.make_async_copy(y_hbm.at[pl.ds(0, ts)], ybuf.at[of_slot], gsem.at[of_slot]).wait()

        @pl.when(step == 0)
        def _():
            def issue(j, carry):
                row_copy(pos_cur, j, 0).start()
                row_copy(pos_nxt, j, 1).start()
                return carry
            lax.fori_loop(0, ts, issue, 0, unroll=DMA_UNROLL)

        wait_rows(slot)
        x = x + modp_ref[0, 5:6, :] * ybuf[slot]

        ahead_slot = (step + GATHER_SLOTS - 1) % GATHER_SLOTS
        for j in range(ts):
            row_copy(pos_ahd, j, ahead_slot).start()

    sh1 = mod_ref[0, 0:1, :]
    gt1 = mod_ref[0, 2:3, :]
    scale = gmix_ref[...] * (1.0 + mod_ref[0, 1:2, :])
    ms = jnp.mean(x * x, axis=-1, keepdims=True)
    hb = (x * lax.rsqrt(ms + EPS) * scale + sh1).astype(_BF16)

    def proj(off, width):
        return _dot(hb, win_ref[:, off:off + width])

    out = {}
    _run_alternately(
        _conv_lru_steps(out, proj, cw_ref, cb_ref, lng_ref, lnb_ref, lcw_ref, lcb_ref, wai_ref, bai_ref,
                        lam_ref, ubuf, lbuf, hcar, ts),
        _gla_steps(out, proj, wgate_ref, bgate_ref, gnorm_ref, state, tri_ref, tria_ref, csel_ref,
                   causal_ref, hdiag_ref, ts))

    mixed = (_dot(out["o_g"].astype(_BF16), wout_ref[CONV_W + LRU_W:MIX_W, :])
             + _dot(out["r_out"].astype(_BF16), wout_ref[CONV_W:CONV_W + LRU_W, :])
             + _dot(out["u_out"].astype(_BF16), wout_ref[0:CONV_W, :]))
    o_ref[0] = x + gt1 * mixed

    if fused_moe_gather:
        @pl.when(step == n_steps - 1)
        def _():
            wait_rows((step + 1) % GATHER_SLOTS)
            wait_rows((step + 2) % GATHER_SLOTS)


def _run_alternately(*step_generators):
    clock = [0] * len(step_generators)
    live = list(range(len(step_generators)))
    while live:
        i = min(live, key=lambda j: clock[j])
        try:
            clock[i] += next(step_generators[i]) * (1 if i else 0.5)
        except StopIteration:
            live.remove(i)


def _conv_lru_steps(out, proj, cw_ref, cb_ref, lng_ref, lnb_ref, lcw_ref, lcb_ref, wai_ref, bai_ref,
                    lam_ref, ubuf, lbuf, hcar, ts):
    u = proj(OFF_CVV, CONV_W) * _sigmoid(proj(OFF_CVG, CONV_W))
    ubuf[CONV_HIST:CONV_HIST + ts, :] = u
    yield 500
    lbuf[LRU_HIST:LRU_HIST + ts, :] = proj(OFF_LRX, LRU_W)
    xb = jnp.broadcast_to(lcb_ref[...], (ts, LRU_W))
    for k in range(LRU_CONV_K):
        off = LRU_HIST - (LRU_CONV_K - 1) + k
        xb = xb + lcw_ref[k:k + 1, :] * lbuf[off:off + ts, :]
    lbuf[0:LRU_HIST, :] = lbuf[ts:ts + LRU_HIST, :]
    gates = _dot(xb.astype(_BF16), wai_ref[...]) + bai_ref[...]
    yield 800

    acc = jnp.broadcast_to(cb_ref[...], (ts, CONV_W))
    for r in range(SUBLANES):
        z = None
        for a_blk in range((CONV_K - 1 - r) // SUBLANES + 1):
            k = CONV_K - 1 - (SUBLANES * a_blk + r)
            off = CONV_HIST - SUBLANES * (a_blk + 1)
            term = cw_ref[k:k + 1, :] * ubuf[off:off + ts + SUBLANES, :]
            z = term if z is None else z + term
        acc = acc + z[SUBLANES - r:SUBLANES - r + ts]
        yield 200
    ubuf[0:CONV_HIST, :] = ubuf[ts:ts + CONV_HIST, :]
    mu = jnp.mean(acc, axis=-1, keepdims=True)
    cen = acc - mu
    var = jnp.mean(cen * cen, axis=-1, keepdims=True)
    out["u_out"] = _silu(cen * lax.rsqrt(var + EPS) * lng_ref[...] + lnb_ref[...])
    yield 250

    r_gate = _sigmoid(gates[:, 0:LRU_W])
    i_gate = _sigmoid(gates[:, LRU_W:2 * LRU_W])
    lam = lam_ref[...]
    softplus_neg_lam = jnp.maximum(-lam, 0.0) + jnp.log1p(jnp.exp(-jnp.abs(lam)))
    log_a = (-LRU_C) * r_gate * softplus_neg_lam
    a = jnp.exp(log_a)
    mult = jnp.sqrt(jnp.tanh(-log_a) * (a * a + 1.0))
    u_in = mult * (i_gate * xb)
    yield 600
    row = lax.broadcasted_iota(jnp.int32, a.shape, 0) % SUBLANES
    d = 1
    while d < SUBLANES:
        keep = row >= d
        a_sh = jnp.where(keep, pltpu.roll(a, d, axis=0), 1.0)
        u_sh = jnp.where(keep, pltpu.roll(u_in, d, axis=0), 0.0)
        u_in = u_in + a * u_sh
        a = a * a_sh
        d *= 2
        yield 200
    groups = []
    h_prev = hcar[0:1, :]
    for g in range(ts // SUBLANES):
        r0 = g * SUBLANES
        h_g = u_in[r0:r0 + SUBLANES] + a[r0:r0 + SUBLANES] * h_prev
        groups.append(h_g)
        h_prev = h_g[SUBLANES - 1:SUBLANES]
        if g % SUBLANES == SUBLANES - 1:
            yield 50
    h_lru = jnp.concatenate(groups, axis=0)
    hcar[...] = jnp.broadcast_to(h_prev, hcar.shape)
    out["r_out"] = h_lru * jax.nn.gelu(proj(OFF_LRY, LRU_W), approximate=True)


def _gla_steps(out, proj, wgate_ref, bgate_ref, gnorm_ref, state, tri_ref, tria_ref, csel_ref, causal_ref,
               hdiag_ref, ts):
    zg = proj(OFF_GLR, LANES)
    glog = _dot(zg.astype(_BF16), wgate_ref[...]) + bgate_ref[...]
    lg = (jnp.minimum(glog, 0.0) - jnp.log1p(jnp.exp(-jnp.abs(glog)))) * (1.0 / GLA_TAU)
    yield 300
    zq = proj(OFF_Q, QK_W) * (GLA_DK ** -0.5)
    zk = proj(OFF_K, QK_W)
    yield 500
    zv = proj(OFF_V, V_W)
    yield 500

    nblk = ts // GLA_BLOCK
    cpb = GLA_BLOCK // GLA_CHUNK
    tri = tri_ref[...]
    tri_after = tria_ref[...]
    chunk_sel = csel_ref[...]
    causal = causal_ref[...] > 0.5
    qk_lane_head = lax.broadcasted_iota(jnp.int32, (1, QK_W), 1) // DK_PAD
    t_lane_chunk = lax.broadcasted_iota(jnp.int32, (1, GLA_BLOCK), 1) // GLA_CHUNK

    o_blocks = []
    for blk in range(nblk):
        r0 = blk * GLA_BLOCK
        lg_b = lg[r0:r0 + GLA_BLOCK]
        p_hi, p_lo = _split2(lg_b)
        b = _dot(tri, p_hi) + _dot(tri, p_lo)
        b_rest = _dot(tri_after, p_hi) + _dot(tri_after, p_lo)
        b_tot = _dot(chunk_sel, p_hi) + _dot(chunk_sel, p_lo)
        yield 500
        q_in =zq[r0:r0 + GLA_BLOCK] * jnp.exp(b)
        k_blk = zk[r0:r0 + GLA_BLOCK]
        k_in = (k_blk * jnp.exp(-b)).astype(_BF16)
        k_out_t = (k_blk * jnp.exp(b_rest)).T.astype(_BF16)
        v_b = zv[r0:r0 + GLA_BLOCK].astype(_BF16)
        q_in_b = q_in.astype(_BF16)
        decay_cols = jnp.exp(b_tot).T

        o_heads = []
        for hd in range(GLA_HEADS):
            q_h = jnp.where(qk_lane_head == hd, q_in_b, jnp.zeros_like(q_in_b))
            sc = jnp.where(causal, _dot_nt(q_h, k_in), 0.0)
            o_heads.append(_dot(sc.astype(_BF16), v_b[:, hd * DV_PAD:(hd + 1) * DV_PAD]))
            yield 300
        o_intra = jnp.concatenate(o_heads, axis=1)

        o_inter = []
        st = state[...]
        for c in range(cpb):
            c0 = c * GLA_CHUNK
            o_inter.append(_dot(q_in_b[c0:c0 + GLA_CHUNK], st.astype(_BF16)))
            kv = _dot(jnp.where(t_lane_chunk == c, k_out_t, jnp.zeros_like(k_out_t)), v_b)
            st = st * decay_cols[:, c:c + 1] + kv * hdiag_ref[...]
            yield 400
        state[...] = st
        o_blocks.append(o_intra + jnp.concatenate(o_inter, axis=0))
    o = o_blocks[0] if nblk == 1 else jnp.concatenate(o_blocks, axis=0)

    og = proj(OFF_OG, V_W)
    o_parts = []
    for hd in range(GLA_HEADS):
        o_h = o[:, hd * DV_PAD:(hd + 1) * DV_PAD]
        ms_h = jnp.sum(o_h * o_h, axis=-1, keepdims=True) * (1.0 / GLA_DV)
        o_parts.append(o_h * lax.rsqrt(ms_h + EPS))
    o_n = jnp.concatenate(o_parts, axis=1) * gnorm_ref[...]
    out["o_g"] = o_n * _silu(og)


def _token_mix(x, mod, p, moe=None):
    bsz, seq, d = x.shape
    ts = MIX_TS
    n_s = seq // ts
    full = lambda shape: pl.BlockSpec(shape, lambda b, s: (0,) * len(shape))
    in_specs = [
        pl.BlockSpec((1, ts, d), lambda b, s: (b, s, 0)),
        pl.BlockSpec((1, 6, d), lambda b, s: (b, 0, 0)),
        full((1, d)),
        full((d, N_IN)),
        full((CONV_HIST, CONV_W)), full((1, CONV_W)), full((1, CONV_W)), full((1, CONV_W)),
        full((LRU_CONV_K, LRU_W)), full((1, LRU_W)),
        full((LRU_W, 2 * LRU_W)), full((1, 2 * LRU_W)), full((1, LRU_W)),
        full((LANES, QK_W)), full((1, QK_W)), full((1, V_W)),
        full((MIX_W, d)),
    ]
    scratch_shapes = [
        pltpu.VMEM((CONV_HIST + ts, CONV_W), _F32),
        pltpu.VMEM((LRU_HIST + ts, LRU_W), _F32),
        pltpu.VMEM((SUBLANES, LRU_W), _F32),
        pltpu.VMEM((QK_W, V_W), _F32),
        pltpu.VMEM((GLA_BLOCK, GLA_BLOCK), _BF16),
        pltpu.VMEM((GLA_BLOCK, GLA_BLOCK), _BF16),
        pltpu.VMEM((LANES, GLA_BLOCK), _BF16),
        pltpu.VMEM((GLA_BLOCK, GLA_BLOCK), _F32),
        pltpu.VMEM((QK_W, V_W), _F32),
    ]
    args = [x, mod, p["g_mix"], p["w_in"], p["conv_w"], p["conv_b"], p["ln_g"], p["ln_b"],
            p["lru_conv_w"], p["lru_conv_b"], p["w_ai"], p["b_ai"], p["lam"], p["w_gate"], p["b_gate"],
            p["g_norm"], p["w_out"]]
    if moe is not None:
        pos, mod_prev, y = moe
        last = bsz * n_s - 1
        in_specs += [
            pl.BlockSpec((ts,), lambda b, s: (b * n_s + s,), memory_space=pltpu.SMEM),
            pl.BlockSpec((ts,), lambda b, s: (jnp.minimum(b * n_s + s + 1, last),), memory_space=pltpu.SMEM),
            pl.BlockSpec((ts,), lambda b, s: (jnp.minimum(b * n_s + s + 2, last),), memory_space=pltpu.SMEM),
            pl.BlockSpec((1, 6, d), lambda b, s: (b, 0, 0)),
            pl.BlockSpec(memory_space=pl.ANY),
        ]
        scratch_shapes += [pltpu.VMEM((GATHER_SLOTS, ts, d), _F32), pltpu.SemaphoreType.DMA((GATHER_SLOTS,))]
        args += [pos, pos, pos, mod_prev, y]
    return pl.pallas_call(
        functools.partial(_mix_kernel, fused_moe_gather=moe is not None),
        out_shape=jax.ShapeDtypeStruct(x.shape, _F32),
        grid=(bsz, n_s),
        in_specs=in_specs,
        out_specs=pl.BlockSpec((1, ts, d), lambda b, s: (b, s, 0)),
        scratch_shapes=scratch_shapes,
        compiler_params=pltpu.CompilerParams(
            dimension_semantics=("arbitrary", "arbitrary"), vmem_limit_bytes=VMEM_LIMIT_BYTES),
        name="token_mix",
    )(*args)


def _route_kernel(x_ref, mod_ref, g_ref, wr_ref, br_ref, xp_ref, meta_ref, cnt_ref, carry):
    tt = x_ref.shape[0]
    i = pl.program_id(0)

    @pl.when(i == 0)
    def _():
        carry[...] = jnp.zeros_like(carry)

    x = x_ref[...]
    sh2 = mod_ref[0, 3:4, :]
    sc2 = mod_ref[0, 4:5, :]
    ms = jnp.mean(x * x, axis=-1, keepdims=True)
    h = (x * lax.rsqrt(ms + EPS) * g_ref[...]) * (1.0 + sc2) + sh2

    h_hi, h_lo = _split2(h)
    w_hi, w_lo = _split2(wr_ref[...])
    w_cat = jnp.concatenate([w_hi, w_lo], axis=1)
    prod = _dot(h_hi, w_cat) + _dot(h_lo, w_cat)
    logits = prod[:, 0:LANES] + prod[:, LANES:2 * LANES] + br_ref[...]

    lane = lax.broadcasted_iota(jnp.int32, (tt, LANES), 1)
    lane_f = lane.astype(_F32)
    neg = -jnp.inf
    big = float(LANES)

    def first_argmax(vals, vmax):
        return jnp.min(jnp.where(vals == vmax, lane_f, big), axis=-1, keepdims=True).astype(jnp.int32)

    gl = jnp.where(lane < N_GROUPS, logits, neg)
    gmax = jnp.max(gl, axis=-1, keepdims=True)
    g_star = first_argmax(gl, gmax)
    p_sel = 1.0 / jnp.sum(jnp.exp(gl - gmax), axis=-1, keepdims=True)
    base = N_GROUPS + EXPERTS_PER_GROUP * g_star
    el = jnp.where((lane >= base) & (lane < base + EXPERTS_PER_GROUP), logits, neg)
    v0 = jnp.max(el, axis=-1, keepdims=True)
    i0 = first_argmax(el, v0)
    el2 = jnp.where(lane == i0, neg, el)
    v1 = jnp.max(el2, axis=-1, keepdims=True)
    i1 = first_argmax(el2, v1)
    ex = jnp.exp(v1 - v0)
    wt0 = p_sel / (1.0 + ex)
    wt1 = p_sel * ex / (1.0 + ex)
    e0 = i0 - base
    e1 = i1 - base
    e_lo = jnp.minimum(e0, e1)
    e_hi = jnp.maximum(e0, e1)
    w_lo = jnp.where(e0 < e1, wt0, wt1)
    w_hi = jnp.where(e0 < e1, wt1, wt0)
    pair = (e_lo * (2 * EXPERTS_PER_GROUP - 1 - e_lo)) // 2 + (e_hi - e_lo - 1)
    bucket = g_star * N_PAIRS + pair

    onehot = lane == bucket
    onehot_f = jnp.where(onehot, 1.0, 0.0)
    ri = lax.broadcasted_iota(jnp.int32, (tt, tt), 0)
    ci = lax.broadcasted_iota(jnp.int32, (tt, tt), 1)
    strict = jnp.where(ci < ri, 1.0, 0.0).astype(_BF16)
    prefix = _dot(strict, onehot_f.astype(_BF16)) + carry[0:1, :]
    rank = jnp.sum(jnp.where(onehot, prefix, 0.0), axis=-1, keepdims=True)
    carry[...] = carry[...] + jnp.sum(onehot_f, axis=0, keepdims=True)
    cnt_ref[...] = carry[...]

    meta = jnp.where(lane == 0, bucket.astype(_F32), jnp.where(lane == 1, rank, 0.0))
    meta_ref[...] = meta

    xp_ref[:, 0:D_MODEL] = h
    xp_ref[:, D_MODEL:ROW_W] = jnp.where(lane == 0, w_lo, jnp.where(lane == 1, w_hi, 0.0))


def _route(x2d, mod, g_ffn, w_r, b_r, seq):
    n_tok, d = x2d.shape
    tt = ROUTE_TT
    tiles_per_seq = seq // tt
    return pl.pallas_call(
        _route_kernel,
        out_shape=(
            jax.ShapeDtypeStruct((n_tok, ROW_W), _F32),
            jax.ShapeDtypeStruct((n_tok, LANES), _F32),
            jax.ShapeDtypeStruct((SUBLANES, LANES), _F32),
        ),
        grid=(n_tok // tt,),
        in_specs=[
            pl.BlockSpec((tt, d), lambda i: (i, 0)),
            pl.BlockSpec((1, 6, d), lambda i: (i // tiles_per_seq, 0, 0)),
            pl.BlockSpec((1, d), lambda i: (0, 0)),
            pl.BlockSpec((d, LANES), lambda i: (0, 0)),
            pl.BlockSpec((1, LANES), lambda i: (0, 0)),
        ],
        out_specs=(
            pl.BlockSpec((tt, ROW_W), lambda i: (i, 0)),
            pl.BlockSpec((tt, LANES), lambda i: (i, 0)),
            pl.BlockSpec((SUBLANES, LANES), lambda i: (0, 0)),
        ),
        scratch_shapes=[pltpu.VMEM((SUBLANES, LANES), _F32)],
        compiler_params=pltpu.CompilerParams(
            dimension_semantics=("arbitrary",), vmem_limit_bytes=VMEM_LIMIT_BYTES),
        name="moe_route",
    )(x2d, mod, g_ffn, w_r, b_r)


def _permute_kernel(fill_ref, pos_ref, xp_ref, xs_hbm, zbuf, sem, zsem):
    tb = pos_ref.shape[0]

    @pl.when(pl.program_id(0) == 0)
    def _():
        zbuf[...] = jnp.zeros_like(zbuf)
        n_used = fill_ref[2 * N_BUCKETS]
        n_tiles = xs_hbm.shape[0] // FFN_TM
        for wait in (False, True):
            for b in range(N_BUCKETS):
                for row0, cond in ((fill_ref[b], fill_ref[N_BUCKETS + b] > 0),
                                   ((n_used + b) * FFN_TM, n_used + b < n_tiles)):
                    @pl.when(cond)
                    def _():
                        dst = xs_hbm.at[pl.ds(pl.multiple_of(row0, FFN_TM), FFN_TM)]
                        fill = pltpu.make_async_copy(zbuf, dst, zsem)
                        fill.wait() if wait else fill.start()

    for j in range(tb):
        pltpu.make_async_copy(xp_ref.at[pl.ds(j, 1)], xs_hbm.at[pl.ds(pos_ref[j], 1)], sem).start()
    pltpu.make_async_copy(xp_ref, xs_hbm.at[pl.ds(0, tb)], sem).wait()


def _permute(fill_starts, pos, xp, n_rows):
    n_tok = xp.shape[0]
    tb = PERM_TB
    return pl.pallas_call(
        _permute_kernel,
        out_shape=jax.ShapeDtypeStruct((n_rows, ROW_W), _F32),
        grid_spec=pltpu.PrefetchScalarGridSpec(
            num_scalar_prefetch=1,
            grid=(n_tok // tb,),
            in_specs=[
                pl.BlockSpec((tb,), lambda i, fs: (i,), memory_space=pltpu.SMEM),
                pl.BlockSpec((tb, ROW_W), lambda i, fs: (i, 0)),
            ],
            out_specs=pl.BlockSpec(memory_space=pl.ANY),
            scratch_shapes=[pltpu.VMEM((FFN_TM, ROW_W), _F32), pltpu.SemaphoreType.DMA,
                            pltpu.SemaphoreType.DMA],
        ),
        compiler_params=pltpu.CompilerParams(
            dimension_semantics=("arbitrary",), vmem_limit_bytes=VMEM_LIMIT_BYTES),
        name="moe_permute",
    )(fill_starts, pos, xp)


def _ffn_kernel(tg_ref, tlo_ref, thi_ref, clo_ref, chi_ref, ng_ref, nlo_ref, nhi_ref, plo_ref, phi_ref, nused_ref,
                xs_ref, wg_hbm, wu_hbm, wd_hbm, y_ref, sg, su, sd, ag, au, ad, sem, *, layer):
    i = pl.program_id(0)

    def weight_copies(slot, g, e):
        return (pltpu.make_async_copy(wg_hbm.at[layer, g, e], sg.at[slot], sem.at[slot, 0]),
                pltpu.make_async_copy(wu_hbm.at[layer, g, e], su.at[slot], sem.at[slot, 1]),
                pltpu.make_async_copy(wd_hbm.at[layer, g, e], sd.at[slot], sem.at[slot, 2]))

    @pl.when(i == 0)
    def _():
        for cp in weight_copies(0, tg_ref[0], tlo_ref[0]) + weight_copies(1, tg_ref[0], thi_ref[0]):
            cp.start()

    for slot, changed_ref, e_ref in ((0, clo_ref, tlo_ref), (1, chi_ref, thi_ref)):
        @pl.when(changed_ref[i] == 1)
        def _():
            for cp in weight_copies(slot, tg_ref[i], e_ref[i]):
                cp.wait()
            ag[slot] = sg[slot].astype(_BF16)
            au[slot] = su[slot].astype(_BF16)
            ad[slot] = sd[slot].astype(_BF16)

    for slot, prefetch_ref, e_ref in ((0, plo_ref, nlo_ref), (1, phi_ref, nhi_ref)):
        @pl.when(prefetch_ref[i] == 1)
        def _():
            for cp in weight_copies(slot, ng_ref[i], e_ref[i]):
                cp.start()

    @pl.when(i < nused_ref[0])
    def _():
        xb = xs_ref[:, 0:D_MODEL].astype(_BF16)
        info = xs_ref[:, D_MODEL:ROW_W]

        def expert(slot):
            hid = _silu(_dot(xb, ag[slot])) * _dot(xb, au[slot]) * info[:, slot:slot + 1]
            return _dot(hid.astype(_BF16), ad[slot])

        y_ref[...] = expert(0) + expert(1)

    @pl.when(i >= nused_ref[0])
    def _():
        y_ref[...] = jnp.zeros_like(y_ref)


def _expert_ffn(layer, plan, xs, w_gate, w_up, w_down):
    n_rows = xs.shape[0]
    tm = FFN_TM
    n_tiles = n_rows // tm

    def row_map(i, *prefetch):
        n_used = prefetch[-1]
        return (jnp.minimum(i, n_used[0] - 1), 0)

    hbm = pl.BlockSpec(memory_space=pl.ANY)
    return pl.pallas_call(
        functools.partial(_ffn_kernel, layer=layer),
        out_shape=jax.ShapeDtypeStruct((n_rows, D_MODEL), _F32),
        grid_spec=pltpu.PrefetchScalarGridSpec(
            num_scalar_prefetch=len(plan),
            grid=(n_tiles,),
            in_specs=[pl.BlockSpec((tm, ROW_W), row_map), hbm, hbm, hbm],
            out_specs=pl.BlockSpec((tm, D_MODEL), lambda i, *prefetch: (i, 0)),
            scratch_shapes=[
                pltpu.VMEM((2, D_MODEL, D_EXPERT), _F32), pltpu.VMEM((2, D_MODEL, D_EXPERT), _F32),
                pltpu.VMEM((2, D_EXPERT, D_MODEL), _F32),
                pltpu.VMEM((2, D_MODEL, D_EXPERT), _BF16), pltpu.VMEM((2, D_MODEL, D_EXPERT), _BF16),
                pltpu.VMEM((2, D_EXPERT, D_MODEL), _BF16),
                pltpu.SemaphoreType.DMA((2, 3)),
            ],
        ),
        compiler_params=pltpu.CompilerParams(
            dimension_semantics=("arbitrary",), vmem_limit_bytes=VMEM_LIMIT_BYTES),
        name="moe_ffn",
    )(*plan, xs, w_gate, w_up, w_down)


def _combine_kernel(pos_ref, x_ref, mod_ref, gfin_ref, y_hbm, o_ref, ybuf, sem):
    tc = x_ref.shape[0]
    for j in range(tc):
        pltpu.make_async_copy(y_hbm.at[pl.ds(pos_ref[j], 1)], ybuf.at[pl.ds(j, 1)], sem).start()
    pltpu.make_async_copy(y_hbm.at[pl.ds(0, tc)], ybuf, sem).wait()

    gt2 = mod_ref[0, 5:6, :]
    out = x_ref[...] + gt2 * ybuf[...]
    ms = jnp.mean(out * out, axis=-1, keepdims=True)
    o_ref[...] = out * lax.rsqrt(ms + EPS) * gfin_ref[...]


def _combine(pos, x2d, mod, g_final, y, seq):
    n_tok, d = x2d.shape
    tc = COMB_TC
    tiles_per_seq = seq // tc
    return pl.pallas_call(
        _combine_kernel,
        out_shape=jax.ShapeDtypeStruct((n_tok, d), _F32),
        grid=(n_tok // tc,),
        in_specs=[
            pl.BlockSpec((tc,), lambda i: (i,), memory_space=pltpu.SMEM),
            pl.BlockSpec((tc, d), lambda i: (i, 0)),
            pl.BlockSpec((1, 6, d), lambda i: (i // tiles_per_seq, 0, 0)),
            pl.BlockSpec((1, d), lambda i: (0, 0)),
            pl.BlockSpec(memory_space=pl.ANY),
        ],
        out_specs=pl.BlockSpec((tc, d), lambda i: (i, 0)),
        scratch_shapes=[pltpu.VMEM((tc, d), _F32), pltpu.SemaphoreType.DMA],
        compiler_params=pltpu.CompilerParams(
            dimension_semantics=("arbitrary",), vmem_limit_bytes=VMEM_LIMIT_BYTES),
        name="moe_combine",
    )(pos, x2d, mod, g_final, y)


def _pad_heads(w, heads, width, padded):
    lead = w.shape[:-1]
    w = w.reshape(lead + (heads, width))
    w = jnp.pad(w, [(0, 0)] * len(lead) + [(0, 0), (0, padded - width)])
    return w.reshape(lead + (heads * padded,))


def _block_diag(w):
    n, bw, _ = w.shape
    eye = jnp.eye(n, dtype=w.dtype)
    return (eye[:, None, :, None] * w[:, :, None, :]).reshape(n * bw, n * bw)


def _prep_layer(l, w_in, conv_dw_w, conv_dw_b, conv_ln_g, conv_ln_b, lru_conv_w, lru_conv_b, lru_w_a,
                lru_b_a, lru_w_i, lru_b_i, lru_lam, gla_w_gate, gla_b_gate, gla_norm_g, w_out, g_mix):
    sizes = [CONV_W, CONV_W, LRU_W, LRU_W, GLA_HEADS * GLA_DK, GLA_HEADS * GLA_DK, GLA_V, GLA_RANK, GLA_V]
    cv_v, cv_g, lr_x, lr_y, q, k, v, g_lr, og = jnp.split(w_in[l], np.cumsum(sizes)[:-1].tolist(), axis=-1)
    w_in_p = jnp.concatenate([
        cv_v, cv_g, lr_x, lr_y,
        _pad_heads(q, GLA_HEADS, GLA_DK, DK_PAD), _pad_heads(k, GLA_HEADS, GLA_DK, DK_PAD),
        _pad_heads(v, GLA_HEADS, GLA_DV, DV_PAD),
        jnp.pad(g_lr, ((0, 0), (0, LANES - GLA_RANK))),
        _pad_heads(og, GLA_HEADS, GLA_DV, DV_PAD)], axis=-1).astype(_BF16)
    wo = w_out[l]
    wo_o = wo[CONV_W + LRU_W:].reshape(GLA_HEADS, GLA_DV, D_MODEL)
    wo_o = jnp.pad(wo_o, ((0, 0), (0, DV_PAD - GLA_DV), (0, 0))).reshape(V_W, D_MODEL)
    w_out_p = jnp.concatenate([wo[:CONV_W + LRU_W], wo_o], axis=0).astype(_BF16)
    w_gate_p = jnp.pad(_pad_heads(gla_w_gate[l], GLA_HEADS, GLA_DK, DK_PAD),
                       ((0, LANES - GLA_RANK), (0, 0))).astype(_BF16)
    return {
        "g_mix": g_mix[l][None, :],
        "w_in": w_in_p,
        "conv_w": jnp.pad(conv_dw_w[l], ((0, CONV_HIST - CONV_K), (0, 0))),
        "conv_b": conv_dw_b[l][None, :],
        "ln_g": conv_ln_g[l][None, :],
        "ln_b": conv_ln_b[l][None, :],
        "lru_conv_w": lru_conv_w[l],
        "lru_conv_b": lru_conv_b[l][None, :],
        "w_ai": jnp.concatenate([_block_diag(lru_w_a[l]), _block_diag(lru_w_i[l])], axis=1).astype(_BF16),
        "b_ai": jnp.concatenate([lru_b_a[l], lru_b_i[l]])[None, :],
        "lam": lru_lam[l][None, :],
        "w_gate": w_gate_p,
        "b_gate": _pad_heads(gla_b_gate[l], GLA_HEADS, GLA_DK, DK_PAD)[None, :],
        "g_norm": _pad_heads(gla_norm_g[l], GLA_HEADS, GLA_DV, DV_PAD)[None, :],
        "w_out": w_out_p,
    }


def _bucket_layout(counts, n_tiles):
    tm = FFN_TM
    counts = counts.astype(jnp.int32)
    tiles = (counts + tm - 1) // tm
    tile_end = jnp.cumsum(tiles)
    tile_start = tile_end - tiles
    starts = (tile_start * tm).astype(jnp.int32)
    n_used = tile_end[-1]
    tile_idx = jnp.minimum(jnp.arange(n_tiles, dtype=jnp.int32), n_used - 1)
    bucket_of = lambda t: jnp.sum((t[:, None] >= tile_end[None, :]).astype(jnp.int32), axis=1)
    pair_lo = jnp.array([0, 0, 0, 1, 1, 2], jnp.int32)
    pair_hi = jnp.array([1, 2, 3, 2, 3, 3], jnp.int32)
    experts = lambda b: (b // N_PAIRS, pair_lo[b % N_PAIRS], pair_hi[b % N_PAIRS])

    tile_bucket = bucket_of(tile_idx)
    g, lo, hi = experts(tile_bucket)
    first = jnp.concatenate([jnp.ones((1,), bool), tile_bucket[1:] != tile_bucket[:-1]])
    prev = lambda v: jnp.concatenate([jnp.full((1,), -1, jnp.int32), v[:-1]])
    changed_lo = first & ((g != prev(g)) | (lo != prev(lo)))
    changed_hi = first & ((g != prev(g)) | (hi != prev(hi)))
    next_first_tile = tile_end[tile_bucket]
    has_next = next_first_tile < n_used
    ng, nlo, nhi = experts(bucket_of(jnp.minimum(next_first_tile, n_used - 1)))
    prefetch_lo = first & has_next & ((ng != g) | (nlo != lo))
    prefetch_hi = first & has_next & ((ng != g) | (nhi != hi))
    as_i32 = lambda v: v.astype(jnp.int32)
    ffn_plan = (g, lo, hi, as_i32(changed_lo), as_i32(changed_hi), ng, nlo, nhi,
                as_i32(prefetch_lo), as_i32(prefetch_hi), n_used.reshape(1).astype(jnp.int32))
    fill = jnp.concatenate([(jnp.maximum(tile_end - 1, 0) * tm).astype(jnp.int32), tiles, n_used[None]])
    return starts, fill, ffn_plan


def kernel(x, c, w_ada, b_ada, g_mix, w_in, conv_dw_w, conv_dw_b, conv_ln_g, conv_ln_b, lru_conv_w,
           lru_conv_b, lru_w_a, lru_b_a, lru_w_i, lru_b_i, lru_lam, gla_w_gate, gla_b_gate, gla_norm_g,
           w_out, g_ffn, w_route_group, b_route_group, w_route_expert, b_route_expert, w_gate, w_up,
           w_down, g_final):
    bsz, seq, d = x.shape
    n_layers = w_ada.shape[0]
    n_tok = bsz * seq
    assert d == D_MODEL and seq % max(MIX_TS, ROUTE_TT, COMB_TC) == 0 and MIX_TS % GLA_BLOCK == 0
    assert n_tok % PERM_TB == 0
    n_tiles = n_tok // FFN_TM + N_BUCKETS
    n_rows = n_tiles * FFN_TM

    mod_all = _modulation(c, w_ada, b_ada).reshape(n_layers, bsz, 6, d)

    pending_moe = None
    for l in range(n_layers):
        p = _prep_layer(l, w_in, conv_dw_w, conv_dw_b, conv_ln_g, conv_ln_b, lru_conv_w, lru_conv_b,
                        lru_w_a, lru_b_a, lru_w_i, lru_b_i, lru_lam, gla_w_gate, gla_b_gate, gla_norm_g,
                        w_out, g_mix)
        mod = mod_all[l]
        x = _token_mix(x, mod, p, pending_moe)

        x2d = x.reshape(n_tok, d)
        w_r = jnp.concatenate(
            [w_route_group[l], w_route_expert[l].transpose(1, 0, 2).reshape(d, N_GROUPS * EXPERTS_PER_GROUP)],
            axis=1)
        n_logits = w_r.shape[1]
        w_r = jnp.pad(w_r, ((0, 0), (0, LANES - n_logits)))
        b_r = jnp.pad(jnp.concatenate([b_route_group[l], b_route_expert[l].reshape(-1)]),
                      (0, LANES - n_logits))[None, :]
        xp, meta, counts = _route(x2d, mod, g_ffn[l][None, :], w_r, b_r, seq)
        bucket = meta[:, 0].astype(jnp.int32)
        rank = meta[:, 1].astype(jnp.int32)
        starts, fill, ffn_plan = _bucket_layout(counts[0, :N_BUCKETS], n_tiles)
        in_bucket = bucket[:, None] == jnp.arange(N_BUCKETS, dtype=jnp.int32)[None, :]
        pos = rank + jnp.sum(jnp.where(in_bucket, starts[None, :], 0), axis=1)

        xs = _permute(fill, pos, xp, n_rows)
        y = _expert_ffn(l, ffn_plan, xs, w_gate, w_up, w_down)
        pending_moe = (pos, mod, y)
    return _combine(pos, x2d, mod, g_final[None, :], y, seq).reshape(bsz, seq, d)
```

```python
import functools

import jax
import jax.numpy as jnp
import numpy as np
from jax import lax
from jax.experimental import pallas as pl
from jax.experimental.pallas import tpu as pltpu

D_MODEL = 1024
CONV_W = 256
LRU_W = 384
GLA_V = 384
CONV_K = 31
LRU_CONV_K = 4
LRU_BLOCKS = 6
LRU_BW = 64
LRU_C = 8.0
GLA_HEADS = 4
GLA_DV = 96
GLA_DK = 48
GLA_RANK = 16
GLA_TAU = 16.0
GLA_CHUNK = 64
N_GROUPS = 4
EXPERTS_PER_GROUP = 4
D_EXPERT = 512
EPS = 1e-6

LANES = 128
SUBLANES = 8
VMEM_LIMIT_BYTES = 56 * 1024 * 1024

DK_PAD = 64
DV_PAD = 128
QK_W = GLA_HEADS * DK_PAD
V_W = GLA_HEADS * DV_PAD
OFF_CVV = 0
OFF_CVG = OFF_CVV + CONV_W
OFF_LRX = OFF_CVG + CONV_W
OFF_LRY = OFF_LRX + LRU_W
OFF_Q = OFF_LRY + LRU_W
OFF_K = OFF_Q + QK_W
OFF_V = OFF_K + QK_W
OFF_GLR = OFF_V + V_W
OFF_OG = OFF_GLR + LANES
N_IN = OFF_OG + V_W
MIX_W = CONV_W + LRU_W + V_W

CONV_HIST = 32
LRU_HIST = 8
GLA_BLOCK = 256

N_PAIRS = 6
N_BUCKETS = N_GROUPS * N_PAIRS
ROW_W = D_MODEL + LANES

MIX_TS = 512
ROUTE_TT = 512
FFN_TM = 256
PERM_TB = 512
COMB_TC = 512
DMA_UNROLL = 8
GATHER_SLOTS = 3

_F32 = jnp.float32
_BF16 = jnp.bfloat16


def _sigmoid(x):
    return 1.0 / (1.0 + jnp.exp(-x))


def _silu(x):
    return x * _sigmoid(x)


def _dot(a, b):
    return jnp.dot(a, b, preferred_element_type=_F32)


def _dot_nt(a, b):
    return lax.dot_general(a, b, (((1,), (1,)), ((), ())), preferred_element_type=_F32)


def _split2(x):
    hi = x.astype(_BF16)
    return hi, (x - hi.astype(_F32)).astype(_BF16)


def _mod_kernel(c_ref, w_ref, b_ref, o_ref):
    c_act = _silu(c_ref[...])
    o_ref[0] = _dot(c_act.astype(_BF16), w_ref[0].astype(_BF16)) + b_ref[0]


def _modulation(c, w_ada, b_ada):
    n_layers, d, n = w_ada.shape
    bsz = c.shape[0]
    tn = 1536
    return pl.pallas_call(
        _mod_kernel,
        out_shape=jax.ShapeDtypeStruct((n_layers, bsz, n), _F32),
        grid=(n_layers, n // tn),
        in_specs=[
            pl.BlockSpec((bsz, d), lambda l, j: (0, 0)),
            pl.BlockSpec((1, d, tn), lambda l, j: (l, 0, j)),
            pl.BlockSpec((1, 1, tn), lambda l, j: (l, 0, j)),
        ],
        out_specs=pl.BlockSpec((1, bsz, tn), lambda l, j: (l, 0, j)),
        compiler_params=pltpu.CompilerParams(
            dimension_semantics=("arbitrary", "arbitrary"), vmem_limit_bytes=VMEM_LIMIT_BYTES),
        name="adaln_mod",
    )(c, w_ada, b_ada.reshape(n_layers, 1, n))


def _mix_kernel(x_ref, mod_ref, gmix_ref, win_ref, cw_ref, cb_ref, lng_ref, lnb_ref,
                lcw_ref, lcb_ref, wai_ref, bai_ref, lam_ref, wgate_ref, bgate_ref, gnorm_ref,
                wout_ref, *rest, fused_moe_gather):
    if fused_moe_gather:
        pos_cur, pos_nxt, pos_ahd, modp_ref, y_hbm, o_ref = rest[:6]
        ybuf, gsem = rest[-2:]
        rest = rest[6:-2]
    else:
        o_ref, rest = rest[0], rest[1:]
    ubuf, lbuf, hcar, state, tri_ref, tria_ref, csel_ref, causal_ref, hdiag_ref = rest
    ts = x_ref.shape[1]
    s_idx = pl.program_id(1)

    @pl.when(s_idx == 0)
    def _():
        ubuf[0:CONV_HIST, :] = jnp.zeros((CONV_HIST, CONV_W), _F32)
        lbuf[0:LRU_HIST, :] = jnp.zeros((LRU_HIST, LRU_W), _F32)
        hcar[...] = jnp.zeros_like(hcar)
        state[...] = jnp.zeros_like(state)
        ri = lax.broadcasted_iota(jnp.int32, (GLA_BLOCK, GLA_BLOCK), 0)
        ci = lax.broadcasted_iota(jnp.int32, (GLA_BLOCK, GLA_BLOCK), 1)
        same_chunk = ri // GLA_CHUNK == ci // GLA_CHUNK
        causal_f = jnp.where(same_chunk & (ci <= ri), 1.0, 0.0)
        causal_ref[...] = causal_f
        tri_ref[...] = causal_f.astype(_BF16)
        tria_ref[...] = jnp.where(same_chunk & (ci > ri), 1.0, 0.0).astype(_BF16)
        sel_r = lax.broadcasted_iota(jnp.int32, (LANES, GLA_BLOCK), 0)
        sel_c = lax.broadcasted_iota(jnp.int32, (LANES, GLA_BLOCK), 1)
        csel_ref[...] = jnp.where(sel_r == sel_c // GLA_CHUNK, 1.0, 0.0).astype(_BF16)
        st_row_head = lax.broadcasted_iota(jnp.int32, (QK_W, V_W), 0) // DK_PAD
        st_col_head = lax.broadcasted_iota(jnp.int32, (QK_W, V_W), 1) // DV_PAD
        hdiag_ref[...] = jnp.where(st_row_head == st_col_head, 1.0, 0.0)

    x = x_ref[0]
    if fused_moe_gather:
        step = pl.program_id(0) * pl.num_programs(1) + s_idx
        n_steps = pl.num_programs(0) * pl.num_programs(1)
        slot = step % GATHER_SLOTS

        def row_copy(pos_ref, j, to_slot):
            return pltpu.make_async_copy(y_hbm.at[pl.ds(pos_ref[j], 1)], ybuf.at[to_slot, pl.ds(j, 1)],
                                         gsem.at[to_slot])

        def wait_rows(of_slot):
            pltpu.make_async_copy(y_hbm.at[pl.ds(0, ts)], ybuf.at[of_slot], gsem.at[of_slot]).wait()

        @pl.when(step == 0)
        def _():
            def issue(j, carry):
                row_copy(pos_cur, j, 0).start()
                row_copy(pos_nxt, j, 1).start()
                return carry
            lax.fori_loop(0, ts, issue, 0, unroll=DMA_UNROLL)

        wait_rows(slot)
        x = x + modp_ref[0, 5:6, :] * ybuf[slot]

        ahead_slot = (step + GATHER_SLOTS - 1) % GATHER_SLOTS
        for j in range(ts):
            row_copy(pos_ahd, j, ahead_slot).start()

    sh1 = mod_ref[0, 0:1, :]
    gt1 = mod_ref[0, 2:3, :]
    scale = gmix_ref[...] * (1.0 + mod_ref[0, 1:2, :])
    ms = jnp.mean(x * x, axis=-1, keepdims=True)
    hb = (x * lax.rsqrt(ms + EPS) * scale + sh1).astype(_BF16)

    def proj(off, width):
        return _dot(hb, win_ref[:, off:off + width])

    out = {}
    _run_alternately(
        _conv_lru_steps(out, proj, cw_ref, cb_ref, lng_ref, lnb_ref, lcw_ref, lcb_ref, wai_ref, bai_ref,
                        lam_ref, ubuf, lbuf, hcar, ts),
        _gla_steps(out, proj, wgate_ref, bgate_ref, gnorm_ref, state, tri_ref, tria_ref, csel_ref,
                   causal_ref, hdiag_ref, ts))

    mixed = (_dot(out["o_g"].astype(_BF16), wout_ref[CONV_W + LRU_W:MIX_W, :])
             + _dot(out["r_out"].astype(_BF16), wout_ref[CONV_W:CONV_W + LRU_W, :])
             + _dot(out["u_out"].astype(_BF16), wout_ref[0:CONV_W, :]))
    o_ref[0] = x + gt1 * mixed

    if fused_moe_gather:
        @pl.when(step == n_steps - 1)
        def _():
            wait_rows((step + 1) % GATHER_SLOTS)
            wait_rows((step + 2) % GATHER_SLOTS)


def _run_alternately(*step_generators):
    clock = [0] * len(step_generators)
    live = list(range(len(step_generators)))
    while live:
        i = min(live, key=lambda j: clock[j])
        try:
            clock[i] += next(step_generators[i]) * (1 if i else 0.5)
        except StopIteration:
            live.remove(i)


def _conv_lru_steps(out, proj, cw_ref, cb_ref, lng_ref, lnb_ref, lcw_ref, lcb_ref, wai_ref, bai_ref,
                    lam_ref, ubuf, lbuf, hcar, ts):
    u = proj(OFF_CVV, CONV_W) * _sigmoid(proj(OFF_CVG, CONV_W))
    ubuf[CONV_HIST:CONV_HIST + ts, :] = u
    yield 500
    lbuf[LRU_HIST:LRU_HIST + ts, :] = proj(OFF_LRX, LRU_W)
    xb = jnp.broadcast_to(lcb_ref[...], (ts, LRU_W))
    for k in range(LRU_CONV_K):
        off = LRU_HIST - (LRU_CONV_K - 1) + k
        xb = xb + lcw_ref[k:k + 1, :] * lbuf[off:off + ts, :]
    lbuf[0:LRU_HIST, :] = lbuf[ts:ts + LRU_HIST, :]
    gates = _dot(xb.astype(_BF16), wai_ref[...]) + bai_ref[...]
    yield 800

    acc = jnp.broadcast_to(cb_ref[...], (ts, CONV_W))
    for r in range(SUBLANES):
        z = None
        for a_blk in range((CONV_K - 1 - r) // SUBLANES + 1):
            k = CONV_K - 1 - (SUBLANES * a_blk + r)
            off = CONV_HIST - SUBLANES * (a_blk + 1)
            term = cw_ref[k:k + 1, :] * ubuf[off:off + ts + SUBLANES, :]
            z = term if z is None else z + term
        acc = acc + z[SUBLANES - r:SUBLANES - r + ts]
        yield 200
    ubuf[0:CONV_HIST, :] = ubuf[ts:ts + CONV_HIST, :]
    mu = jnp.mean(acc, axis=-1, keepdims=True)
    cen = acc - mu
    var = jnp.mean(cen * cen, axis=-1, keepdims=True)
    out["u_out"] = _silu(cen * lax.rsqrt(var + EPS) * lng_ref[...] + lnb_ref[...])
    yield 250

    r_gate = _sigmoid(gates[:, 0:LRU_W])
    i_gate = _sigmoid(gates[:, LRU_W:2 * LRU_W])
    lam = lam_ref[...]
    softplus_neg_lam = jnp.maximum(-lam, 0.0) + jnp.log1p(jnp.exp(-jnp.abs(lam)))
    log_a = (-LRU_C) * r_gate * softplus_neg_lam
    a = jnp.exp(log_a)
    mult = jnp.sqrt(jnp.tanh(-log_a) * (a * a + 1.0))
    u_in = mult * (i_gate * xb)
    yield 600
    row = lax.broadcasted_iota(jnp.int32, a.shape, 0) % SUBLANES
    d = 1
    while d < SUBLANES:
        keep = row >= d
        a_sh = jnp.where(keep, pltpu.roll(a, d, axis=0), 1.0)
        u_sh = jnp.where(keep, pltpu.roll(u_in, d, axis=0), 0.0)
        u_in = u_in + a * u_sh
        a = a * a_sh
        d *= 2
        yield 200
    groups = []
    h_prev = hcar[0:1, :]
    for g in range(ts // SUBLANES):
        r0 = g * SUBLANES
        h_g = u_in[r0:r0 + SUBLANES] + a[r0:r0 + SUBLANES] * h_prev
        groups.append(h_g)
        h_prev = h_g[SUBLANES - 1:SUBLANES]
        if g % SUBLANES == SUBLANES - 1:
            yield 50
    h_lru = jnp.concatenate(groups, axis=0)
    hcar[...] = jnp.broadcast_to(h_prev, hcar.shape)
    out["r_out"] = h_lru * jax.nn.gelu(proj(OFF_LRY, LRU_W), approximate=True)


def _gla_steps(out, proj, wgate_ref, bgate_ref, gnorm_ref, state, tri_ref, tria_ref, csel_ref, causal_ref,
               hdiag_ref, ts):
    zg = proj(OFF_GLR, LANES)
    glog = _dot(zg.astype(_BF16), wgate_ref[...]) + bgate_ref[...]
    lg = (jnp.minimum(glog, 0.0) - jnp.log1p(jnp.exp(-jnp.abs(glog)))) * (1.0 / GLA_TAU)
    yield 300
    zq = proj(OFF_Q, QK_W) * (GLA_DK ** -0.5)
    zk = proj(OFF_K, QK_W)
    yield 500
    zv = proj(OFF_V, V_W)
    yield 500

    nblk = ts // GLA_BLOCK
    cpb = GLA_BLOCK // GLA_CHUNK
    tri = tri_ref[...]
    tri_after = tria_ref[...]
    chunk_sel = csel_ref[...]
    causal = causal_ref[...] > 0.5
    qk_lane_head = lax.broadcasted_iota(jnp.int32, (1, QK_W), 1) // DK_PAD
    t_lane_chunk = lax.broadcasted_iota(jnp.int32, (1, GLA_BLOCK), 1) // GLA_CHUNK

    o_blocks = []
    for blk in range(nblk):
        r0 = blk * GLA_BLOCK
        lg_b = lg[r0:r0 + GLA_BLOCK]
        p_hi, p_lo = _split2(lg_b)
        b = _dot(tri, p_hi) + _dot(tri, p_lo)
        b_rest = _dot(tri_after, p_hi) + _dot(tri_after, p_lo)
        b_tot = _dot(chunk_sel, p_hi) + _dot(chunk_sel, p_lo)
        yield 500
        q_in =zq[r0:r0 + GLA_BLOCK] * jnp.exp(b)
        k_blk = zk[r0:r0 + GLA_BLOCK]
        k_in = (k_blk * jnp.exp(-b)).astype(_BF16)
        k_out_t = (k_blk * jnp.exp(b_rest)).T.astype(_BF16)
        v_b = zv[r0:r0 + GLA_BLOCK].astype(_BF16)
        q_in_b = q_in.astype(_BF16)
        decay_cols = jnp.exp(b_tot).T

        o_heads = []
        for hd in range(GLA_HEADS):
            q_h = jnp.where(qk_lane_head == hd, q_in_b, jnp.zeros_like(q_in_b))
            sc = jnp.where(causal, _dot_nt(q_h, k_in), 0.0)
            o_heads.append(_dot(sc.astype(_BF16), v_b[:, hd * DV_PAD:(hd + 1) * DV_PAD]))
            yield 300
        o_intra = jnp.concatenate(o_heads, axis=1)

        o_inter = []
        st = state[...]
        for c in range(cpb):
            c0 = c * GLA_CHUNK
            o_inter.append(_dot(q_in_b[c0:c0 + GLA_CHUNK], st.astype(_BF16)))
            kv = _dot(jnp.where(t_lane_chunk == c, k_out_t, jnp.zeros_like(k_out_t)), v_b)
            st = st * decay_cols[:, c:c + 1] + kv * hdiag_ref[...]
            yield 400
        state[...] = st
        o_blocks.append(o_intra + jnp.concatenate(o_inter, axis=0))
    o = o_blocks[0] if nblk == 1 else jnp.concatenate(o_blocks, axis=0)

    og = proj(OFF_OG, V_W)
    o_parts = []
    for hd in range(GLA_HEADS):
        o_h = o[:, hd * DV_PAD:(hd + 1) * DV_PAD]
        ms_h = jnp.sum(o_h * o_h, axis=-1, keepdims=True) * (1.0 / GLA_DV)
        o_parts.append(o_h * lax.rsqrt(ms_h + EPS))
    o_n = jnp.concatenate(o_parts, axis=1) * gnorm_ref[...]
    out["o_g"] = o_n * _silu(og)


def _token_mix(x, mod, params, layer, moe=None):
    bsz, seq, d = x.shape
    ts = MIX_TS
    n_s = seq // ts
    full = lambda shape: pl.BlockSpec((None,) + shape, lambda b, s: (layer,) + (0,) * len(shape))
    in_specs = [
        pl.BlockSpec((1, ts, d), lambda b, s: (b, s, 0)),
        pl.BlockSpec((1, 6, d), lambda b, s: (b, 0, 0)),
        full((1, d)),
        full((d, N_IN)),
        full((CONV_HIST, CONV_W)), full((1, CONV_W)), full((1, CONV_W)), full((1, CONV_W)),
        full((LRU_CONV_K, LRU_W)), full((1, LRU_W)),
        full((LRU_W, 2 * LRU_W)), full((1, 2 * LRU_W)), full((1, LRU_W)),
        full((LANES, QK_W)), full((1, QK_W)), full((1, V_W)),
        full((MIX_W, d)),
    ]
    scratch_shapes = [
        pltpu.VMEM((CONV_HIST + ts, CONV_W), _F32),
        pltpu.VMEM((LRU_HIST + ts, LRU_W), _F32),
        pltpu.VMEM((SUBLANES, LRU_W), _F32),
        pltpu.VMEM((QK_W, V_W), _F32),
        pltpu.VMEM((GLA_BLOCK, GLA_BLOCK), _BF16),
        pltpu.VMEM((GLA_BLOCK, GLA_BLOCK), _BF16),
        pltpu.VMEM((LANES, GLA_BLOCK), _BF16),
        pltpu.VMEM((GLA_BLOCK, GLA_BLOCK), _F32),
        pltpu.VMEM((QK_W, V_W), _F32),
    ]
    args = [x, mod] + list(params)
    if moe is not None:
        pos, mod_prev, y = moe
        last = bsz * n_s - 1
        in_specs += [
            pl.BlockSpec((ts,), lambda b, s: (b * n_s + s,), memory_space=pltpu.SMEM),
            pl.BlockSpec((ts,), lambda b, s: (jnp.minimum(b * n_s + s + 1, last),), memory_space=pltpu.SMEM),
            pl.BlockSpec((ts,), lambda b, s: (jnp.minimum(b * n_s + s + 2, last),), memory_space=pltpu.SMEM),
            pl.BlockSpec((1, 6, d), lambda b, s: (b, 0, 0)),
            pl.BlockSpec(memory_space=pl.ANY),
        ]
        scratch_shapes += [pltpu.VMEM((GATHER_SLOTS, ts, d), _F32), pltpu.SemaphoreType.DMA((GATHER_SLOTS,))]
        args += [pos, pos, pos, mod_prev, y]
    return pl.pallas_call(
        functools.partial(_mix_kernel, fused_moe_gather=moe is not None),
        out_shape=jax.ShapeDtypeStruct(x.shape, _F32),
        grid=(bsz, n_s),
        in_specs=in_specs,
        out_specs=pl.BlockSpec((1, ts, d), lambda b, s: (b, s, 0)),
        scratch_shapes=scratch_shapes,
        compiler_params=pltpu.CompilerParams(
            dimension_semantics=("arbitrary", "arbitrary"), vmem_limit_bytes=VMEM_LIMIT_BYTES),
        name="token_mix",
    )(*args)


def _route_kernel(x_ref, mod_ref, g_ref, wr_ref, br_ref, xp_ref, meta_ref, cnt_ref, carry):
    tt = x_ref.shape[0]
    i = pl.program_id(0)

    @pl.when(i == 0)
    def _():
        carry[...] = jnp.zeros_like(carry)

    x = x_ref[...]
    sh2 = mod_ref[0, 3:4, :]
    sc2 = mod_ref[0, 4:5, :]
    ms = jnp.mean(x * x, axis=-1, keepdims=True)
    h = (x * lax.rsqrt(ms + EPS) * g_ref[...]) * (1.0 + sc2) + sh2

    h_hi, h_lo = _split2(h)
    w_hi, w_lo = _split2(wr_ref[...])
    w_cat = jnp.concatenate([w_hi, w_lo], axis=1)
    prod = _dot(h_hi, w_cat) + _dot(h_lo, w_cat)
    logits = prod[:, 0:LANES] + prod[:, LANES:2 * LANES] + br_ref[...]

    lane = lax.broadcasted_iota(jnp.int32, (tt, LANES), 1)
    lane_f = lane.astype(_F32)
    neg = -jnp.inf
    big = float(LANES)

    def first_argmax(vals, vmax):
        return jnp.min(jnp.where(vals == vmax, lane_f, big), axis=-1, keepdims=True).astype(jnp.int32)

    gl = jnp.where(lane < N_GROUPS, logits, neg)
    gmax = jnp.max(gl, axis=-1, keepdims=True)
    g_star = first_argmax(gl, gmax)
    p_sel = 1.0 / jnp.sum(jnp.exp(gl - gmax), axis=-1, keepdims=True)
    base = N_GROUPS + EXPERTS_PER_GROUP * g_star
    el = jnp.where((lane >= base) & (lane < base + EXPERTS_PER_GROUP), logits, neg)
    v0 = jnp.max(el, axis=-1, keepdims=True)
    i0 = first_argmax(el, v0)
    el2 = jnp.where(lane == i0, neg, el)
    v1 = jnp.max(el2, axis=-1, keepdims=True)
    i1 = first_argmax(el2, v1)
    ex = jnp.exp(v1 - v0)
    wt0 = p_sel / (1.0 + ex)
    wt1 = p_sel * ex / (1.0 + ex)
    e0 = i0 - base
    e1 = i1 - base
    e_lo = jnp.minimum(e0, e1)
    e_hi = jnp.maximum(e0, e1)
    w_lo = jnp.where(e0 < e1, wt0, wt1)
    w_hi = jnp.where(e0 < e1, wt1, wt0)
    pair = (e_lo * (2 * EXPERTS_PER_GROUP - 1 - e_lo)) // 2 + (e_hi - e_lo - 1)
    bucket = g_star * N_PAIRS + pair

    onehot = lane == bucket
    onehot_f = jnp.where(onehot, 1.0, 0.0)
    ri = lax.broadcasted_iota(jnp.int32, (tt, tt), 0)
    ci = lax.broadcasted_iota(jnp.int32, (tt, tt), 1)
    strict = jnp.where(ci < ri, 1.0, 0.0).astype(_BF16)
    prefix = _dot(strict, onehot_f.astype(_BF16)) + carry[0:1, :]
    rank = jnp.sum(jnp.where(onehot, prefix, 0.0), axis=-1, keepdims=True)
    carry[...] = carry[...] + jnp.sum(onehot_f, axis=0, keepdims=True)
    cnt_ref[...] = carry[...]

    meta = jnp.where(lane == 0, bucket.astype(_F32), jnp.where(lane == 1, rank, 0.0))
    meta_ref[...] = meta.T[0:SUBLANES, :]

    xp_ref[:, 0:D_MODEL] = h
    xp_ref[:, D_MODEL:ROW_W] = jnp.where(lane == 0, w_lo, jnp.where(lane == 1, w_hi, 0.0))


def _route(x2d, mod, g_ffn, w_r, b_r, layer, seq):
    n_tok, d = x2d.shape
    tt = ROUTE_TT
    tiles_per_seq = seq // tt
    per_layer = lambda shape: pl.BlockSpec((None,) + shape, lambda i: (layer, 0, 0))
    return pl.pallas_call(
        _route_kernel,
        out_shape=(
            jax.ShapeDtypeStruct((n_tok, ROW_W), _F32),
            jax.ShapeDtypeStruct((SUBLANES, n_tok), _F32),
            jax.ShapeDtypeStruct((SUBLANES, LANES), _F32),
        ),
        grid=(n_tok // tt,),
        in_specs=[
            pl.BlockSpec((tt, d), lambda i: (i, 0)),
            pl.BlockSpec((1, 6, d), lambda i: (i // tiles_per_seq, 0, 0)),
            per_layer((1, d)),
            per_layer((d, LANES)),
            per_layer((1, LANES)),
        ],
        out_specs=(
            pl.BlockSpec((tt, ROW_W), lambda i: (i, 0)),
            pl.BlockSpec((SUBLANES, tt), lambda i: (0, i)),
            pl.BlockSpec((SUBLANES, LANES), lambda i: (0, 0)),
        ),
        scratch_shapes=[pltpu.VMEM((SUBLANES, LANES), _F32)],
        compiler_params=pltpu.CompilerParams(
            dimension_semantics=("arbitrary",), vmem_limit_bytes=VMEM_LIMIT_BYTES),
        name="moe_route",
    )(x2d, mod, g_ffn, w_r, b_r)


def _permute_kernel(fill_ref, pos_ref, xp_ref, xs_hbm, zbuf, sem, zsem):
    tb = pos_ref.shape[0]

    @pl.when(pl.program_id(0) == 0)
    def _():
        zbuf[...] = jnp.zeros_like(zbuf)
        n_used = fill_ref[2 * N_BUCKETS]
        n_tiles = xs_hbm.shape[0] // FFN_TM
        for wait in (False, True):
            for b in range(N_BUCKETS):
                for row0, cond in ((fill_ref[b], fill_ref[N_BUCKETS + b] > 0),
                                   ((n_used + b) * FFN_TM, n_used + b < n_tiles)):
                    @pl.when(cond)
                    def _():
                        dst = xs_hbm.at[pl.ds(pl.multiple_of(row0, FFN_TM), FFN_TM)]
                        fill = pltpu.make_async_copy(zbuf, dst, zsem)
                        fill.wait() if wait else fill.start()

    for j in range(tb):
        pltpu.make_async_copy(xp_ref.at[pl.ds(j, 1)], xs_hbm.at[pl.ds(pos_ref[j], 1)], sem).start()
    pltpu.make_async_copy(xp_ref, xs_hbm.at[pl.ds(0, tb)], sem).wait()


def _permute(fill_starts, pos, xp, n_rows):
    n_tok = xp.shape[0]
    tb = PERM_TB
    return pl.pallas_call(
        _permute_kernel,
        out_shape=jax.ShapeDtypeStruct((n_rows, ROW_W), _F32),
        grid_spec=pltpu.PrefetchScalarGridSpec(
            num_scalar_prefetch=1,
            grid=(n_tok // tb,),
            in_specs=[
                pl.BlockSpec((tb,), lambda i, fs: (i,), memory_space=pltpu.SMEM),
                pl.BlockSpec((tb, ROW_W), lambda i, fs: (i, 0)),
            ],
            out_specs=pl.BlockSpec(memory_space=pl.ANY),
            scratch_shapes=[pltpu.VMEM((FFN_TM, ROW_W), _F32), pltpu.SemaphoreType.DMA,
                            pltpu.SemaphoreType.DMA],
        ),
        compiler_params=pltpu.CompilerParams(
            dimension_semantics=("arbitrary",), vmem_limit_bytes=VMEM_LIMIT_BYTES),
        name="moe_permute",
    )(fill_starts, pos, xp)


def _ffn_kernel(tg_ref, tlo_ref, thi_ref, clo_ref, chi_ref, ng_ref, nlo_ref, nhi_ref, plo_ref, phi_ref, nused_ref,
                xs_ref, wg_hbm, wu_hbm, wd_hbm, y_ref, sg, su, sd, ag, au, ad, sem, *, layer):
    i = pl.program_id(0)

    def weight_copies(slot, g, e):
        return (pltpu.make_async_copy(wg_hbm.at[layer, g, e], sg.at[slot], sem.at[slot, 0]),
                pltpu.make_async_copy(wu_hbm.at[layer, g, e], su.at[slot], sem.at[slot, 1]),
                pltpu.make_async_copy(wd_hbm.at[layer, g, e], sd.at[slot], sem.at[slot, 2]))

    @pl.when(i == 0)
    def _():
        for cp in weight_copies(0, tg_ref[0], tlo_ref[0]) + weight_copies(1, tg_ref[0], thi_ref[0]):
            cp.start()

    for slot, changed_ref, e_ref in ((0, clo_ref, tlo_ref), (1, chi_ref, thi_ref)):
        @pl.when(changed_ref[i] == 1)
        def _():
            for cp in weight_copies(slot, tg_ref[i], e_ref[i]):
                cp.wait()
            ag[slot] = sg[slot].astype(_BF16)
            au[slot] = su[slot].astype(_BF16)
            ad[slot] = sd[slot].astype(_BF16)

    for slot, prefetch_ref, e_ref in ((0, plo_ref, nlo_ref), (1, phi_ref, nhi_ref)):
        @pl.when(prefetch_ref[i] == 1)
        def _():
            for cp in weight_copies(slot, ng_ref[i], e_ref[i]):
                cp.start()

    @pl.when(i < nused_ref[0])
    def _():
        xb = xs_ref[:, 0:D_MODEL].astype(_BF16)
        info = xs_ref[:, D_MODEL:ROW_W]

        def expert(slot):
            hid = _silu(_dot(xb, ag[slot])) * _dot(xb, au[slot]) * info[:, slot:slot + 1]
            return _dot(hid.astype(_BF16), ad[slot])

        y_ref[...] = expert(0) + expert(1)

    @pl.when(i >= nused_ref[0])
    def _():
        y_ref[...] = jnp.zeros_like(y_ref)


def _expert_ffn(layer, plan, xs, w_gate, w_up, w_down):
    n_rows = xs.shape[0]
    tm = FFN_TM
    n_tiles = n_rows // tm

    def row_map(i, *prefetch):
        n_used = prefetch[-1]
        return (jnp.minimum(i, n_used[0] - 1), 0)

    hbm = pl.BlockSpec(memory_space=pl.ANY)
    return pl.pallas_call(
        functools.partial(_ffn_kernel, layer=layer),
        out_shape=jax.ShapeDtypeStruct((n_rows, D_MODEL), _F32),
        grid_spec=pltpu.PrefetchScalarGridSpec(
            num_scalar_prefetch=len(plan),
            grid=(n_tiles,),
            in_specs=[pl.BlockSpec((tm, ROW_W), row_map), hbm, hbm, hbm],
            out_specs=pl.BlockSpec((tm, D_MODEL), lambda i, *prefetch: (i, 0)),
            scratch_shapes=[
                pltpu.VMEM((2, D_MODEL, D_EXPERT), _F32), pltpu.VMEM((2, D_MODEL, D_EXPERT), _F32),
                pltpu.VMEM((2, D_EXPERT, D_MODEL), _F32),
                pltpu.VMEM((2, D_MODEL, D_EXPERT), _BF16), pltpu.VMEM((2, D_MODEL, D_EXPERT), _BF16),
                pltpu.VMEM((2, D_EXPERT, D_MODEL), _BF16),
                pltpu.SemaphoreType.DMA((2, 3)),
            ],
        ),
        compiler_params=pltpu.CompilerParams(
            dimension_semantics=("arbitrary",), vmem_limit_bytes=VMEM_LIMIT_BYTES),
        name="moe_ffn",
    )(*plan, xs, w_gate, w_up, w_down)


def _combine_kernel(pos_ref, x_ref, mod_ref, gfin_ref, y_hbm, o_ref, ybuf, sem):
    tc = x_ref.shape[0]
    for j in range(tc):
        pltpu.make_async_copy(y_hbm.at[pl.ds(pos_ref[j], 1)], ybuf.at[pl.ds(j, 1)], sem).start()
    pltpu.make_async_copy(y_hbm.at[pl.ds(0, tc)], ybuf, sem).wait()

    gt2 = mod_ref[0, 5:6, :]
    out = x_ref[...] + gt2 * ybuf[...]
    ms = jnp.mean(out * out, axis=-1, keepdims=True)
    o_ref[...] = out * lax.rsqrt(ms + EPS) * gfin_ref[...]


def _combine(pos, x2d, mod, g_final, y, seq):
    n_tok, d = x2d.shape
    tc = COMB_TC
    tiles_per_seq = seq // tc
    return pl.pallas_call(
        _combine_kernel,
        out_shape=jax.ShapeDtypeStruct((n_tok, d), _F32),
        grid=(n_tok // tc,),
        in_specs=[
            pl.BlockSpec((tc,), lambda i: (i,), memory_space=pltpu.SMEM),
            pl.BlockSpec((tc, d), lambda i: (i, 0)),
            pl.BlockSpec((1, 6, d), lambda i: (i // tiles_per_seq, 0, 0)),
            pl.BlockSpec((1, d), lambda i: (0, 0)),
            pl.BlockSpec(memory_space=pl.ANY),
        ],
        out_specs=pl.BlockSpec((tc, d), lambda i: (i, 0)),
        scratch_shapes=[pltpu.VMEM((tc, d), _F32), pltpu.SemaphoreType.DMA],
        compiler_params=pltpu.CompilerParams(
            dimension_semantics=("arbitrary",), vmem_limit_bytes=VMEM_LIMIT_BYTES),
        name="moe_combine",
    )(pos, x2d, mod, g_final, y)


def _pad_heads(w, heads, width, padded):
    lead = w.shape[:-1]
    w = w.reshape(lead + (heads, width))
    w = jnp.pad(w, [(0, 0)] * len(lead) + [(0, 0), (0, padded - width)])
    return w.reshape(lead + (heads * padded,))


def _block_diag(w):
    n_layers, n, bw, _ = w.shape
    eye = jnp.eye(n, dtype=w.dtype)
    return (eye[None, :, None, :, None] * w[:, :, :, None, :]).reshape(n_layers, n * bw, n * bw)


def _prep_params(w_in, conv_dw_w, conv_dw_b, conv_ln_g, conv_ln_b, lru_conv_w, lru_conv_b, lru_w_a,
                 lru_b_a, lru_w_i, lru_b_i, lru_lam, gla_w_gate, gla_b_gate, gla_norm_g, w_out, g_mix):
    sizes = [CONV_W, CONV_W, LRU_W, LRU_W, GLA_HEADS * GLA_DK, GLA_HEADS * GLA_DK, GLA_V, GLA_RANK, GLA_V]
    cv_v, cv_g, lr_x, lr_y, q, k, v, g_lr, og = jnp.split(w_in, np.cumsum(sizes)[:-1].tolist(), axis=-1)
    w_in_p = jnp.concatenate([
        cv_v, cv_g, lr_x, lr_y,
        _pad_heads(q, GLA_HEADS, GLA_DK, DK_PAD), _pad_heads(k, GLA_HEADS, GLA_DK, DK_PAD),
        _pad_heads(v, GLA_HEADS, GLA_DV, DV_PAD),
        jnp.pad(g_lr, ((0, 0), (0, 0), (0, LANES - GLA_RANK))),
        _pad_heads(og, GLA_HEADS, GLA_DV, DV_PAD)], axis=-1).astype(_BF16)
    n_layers = w_in.shape[0]
    wo_o = w_out[:, CONV_W + LRU_W:].reshape(n_layers, GLA_HEADS, GLA_DV, D_MODEL)
    wo_o = jnp.pad(wo_o, ((0, 0), (0, 0), (0, DV_PAD - GLA_DV), (0, 0))).reshape(n_layers, V_W, D_MODEL)
    w_out_p = jnp.concatenate([w_out[:, :CONV_W + LRU_W], wo_o], axis=1).astype(_BF16)
    w_gate_p = jnp.pad(_pad_heads(gla_w_gate, GLA_HEADS, GLA_DK, DK_PAD),
                       ((0, 0), (0, LANES - GLA_RANK), (0, 0))).astype(_BF16)
    row = lambda v: v[:, None, :]
    return [
        row(g_mix),
        w_in_p,
        jnp.pad(conv_dw_w, ((0, 0), (0, CONV_HIST - CONV_K), (0, 0))),
        row(conv_dw_b), row(conv_ln_g), row(conv_ln_b),
        lru_conv_w, row(lru_conv_b),
        jnp.concatenate([_block_diag(lru_w_a), _block_diag(lru_w_i)], axis=2).astype(_BF16),
        row(jnp.concatenate([lru_b_a, lru_b_i], axis=1)),
        row(lru_lam),
        w_gate_p,
        row(_pad_heads(gla_b_gate, GLA_HEADS, GLA_DK, DK_PAD)),
        row(_pad_heads(gla_norm_g, GLA_HEADS, GLA_DV, DV_PAD)),
        w_out_p,
    ]


def _bucket_layout(counts, n_tiles):
    tm = FFN_TM
    counts = counts.astype(jnp.int32)
    tiles = (counts + tm - 1) // tm
    tile_end = jnp.cumsum(tiles)
    tile_start = tile_end - tiles
    starts = (tile_start * tm).astype(jnp.int32)
    n_used = tile_end[-1]
    tile_idx = jnp.minimum(jnp.arange(n_tiles, dtype=jnp.int32), n_used - 1)
    bucket_of = lambda t: jnp.sum((t[:, None] >= tile_end[None, :]).astype(jnp.int32), axis=1)
    pair_lo = jnp.array([0, 0, 0, 1, 1, 2], jnp.int32)
    pair_hi = jnp.array([1, 2, 3, 2, 3, 3], jnp.int32)
    experts = lambda b: (b // N_PAIRS, pair_lo[b % N_PAIRS], pair_hi[b % N_PAIRS])

    tile_bucket = bucket_of(tile_idx)
    g, lo, hi = experts(tile_bucket)
    first = jnp.concatenate([jnp.ones((1,), bool), tile_bucket[1:] != tile_bucket[:-1]])
    prev = lambda v: jnp.concatenate([jnp.full((1,), -1, jnp.int32), v[:-1]])
    changed_lo = first & ((g != prev(g)) | (lo != prev(lo)))
    changed_hi = first & ((g != prev(g)) | (hi != prev(hi)))
    next_first_tile = tile_end[tile_bucket]
    has_next = next_first_tile < n_used
    ng, nlo, nhi = experts(bucket_of(jnp.minimum(next_first_tile, n_used - 1)))
    prefetch_lo = first & has_next & ((ng != g) | (nlo != lo))
    prefetch_hi = first & has_next & ((ng != g) | (nhi != hi))
    as_i32 = lambda v: v.astype(jnp.int32)
    ffn_plan = (g, lo, hi, as_i32(changed_lo), as_i32(changed_hi), ng, nlo, nhi,
                as_i32(prefetch_lo), as_i32(prefetch_hi), n_used.reshape(1).astype(jnp.int32))
    fill = jnp.concatenate([(jnp.maximum(tile_end - 1, 0) * tm).astype(jnp.int32), tiles, n_used[None]])
    return starts, fill, ffn_plan


def kernel(x, c, w_ada, b_ada, g_mix, w_in, conv_dw_w, conv_dw_b, conv_ln_g, conv_ln_b, lru_conv_w,
           lru_conv_b, lru_w_a, lru_b_a, lru_w_i, lru_b_i, lru_lam, gla_w_gate, gla_b_gate, gla_norm_g,
           w_out, g_ffn, w_route_group, b_route_group, w_route_expert, b_route_expert, w_gate, w_up,
           w_down, g_final):
    bsz, seq, d = x.shape
    n_layers = w_ada.shape[0]
    n_tok = bsz * seq
    assert d == D_MODEL and seq % max(MIX_TS, ROUTE_TT, COMB_TC) == 0 and MIX_TS % GLA_BLOCK == 0
    assert n_tok % PERM_TB == 0
    n_tiles = n_tok // FFN_TM + N_BUCKETS
    n_rows = n_tiles * FFN_TM

    mod_all = _modulation(c, w_ada, b_ada).reshape(n_layers, bsz, 6, d)

    params = _prep_params(w_in, conv_dw_w, conv_dw_b, conv_ln_g, conv_ln_b, lru_conv_w, lru_conv_b,
                          lru_w_a, lru_b_a, lru_w_i, lru_b_i, lru_lam, gla_w_gate, gla_b_gate, gla_norm_g,
                          w_out, g_mix)
    n_experts = N_GROUPS * EXPERTS_PER_GROUP
    w_r = jnp.concatenate(
        [w_route_group, w_route_expert.transpose(0, 2, 1, 3).reshape(n_layers, d, n_experts)], axis=2)
    w_r = jnp.pad(w_r, ((0, 0), (0, 0), (0, LANES - N_GROUPS - n_experts)))
    b_r = jnp.concatenate([b_route_group, b_route_expert.reshape(n_layers, n_experts)], axis=1)
    b_r = jnp.pad(b_r, ((0, 0), (0, LANES - N_GROUPS - n_experts)))[:, None, :]
    pending_moe = None
    for l in range(n_layers):
        mod = mod_all[l]
        x = _token_mix(x, mod, params, l, pending_moe)

        x2d = x.reshape(n_tok, d)
        xp, meta, counts = _route(x2d, mod, g_ffn[:, None, :], w_r, b_r, l, seq)
        bucket = meta[0].astype(jnp.int32)
        rank = meta[1].astype(jnp.int32)
        starts, fill, ffn_plan = _bucket_layout(counts[0, :N_BUCKETS], n_tiles)
        in_bucket = bucket[:, None] == jnp.arange(N_BUCKETS, dtype=jnp.int32)[None, :]
        pos = rank + jnp.sum(jnp.where(in_bucket, starts[None, :], 0), axis=1)

        xs = _permute(fill, pos, xp, n_rows)
        y = _expert_ffn(l, ffn_plan, xs, w_gate, w_up, w_down)
        pending_moe = (pos, mod, y)
    return _combine(pos, x2d, mod, g_final[None, :], y, seq).reshape(bsz, seq, d)
```

```python
import functools

import jax
import jax.numpy as jnp
import numpy as np
from jax import lax
from jax.experimental import pallas as pl
from jax.experimental.pallas import tpu as pltpu

D_MODEL = 1024
CONV_W = 256
LRU_W = 384
GLA_V = 384
CONV_K = 31
LRU_CONV_K = 4
LRU_BLOCKS = 6
LRU_BW = 64
LRU_C = 8.0
GLA_HEADS = 4
GLA_DV = 96
GLA_DK = 48
GLA_RANK = 16
GLA_TAU = 16.0
GLA_CHUNK = 64
N_GROUPS = 4
EXPERTS_PER_GROUP = 4
D_EXPERT = 512
EPS = 1e-6

LANES = 128
SUBLANES = 8
VMEM_LIMIT_BYTES = 56 * 1024 * 1024

DK_PAD = 64
DV_PAD = 128
QK_W = GLA_HEADS * DK_PAD
V_W = GLA_HEADS * DV_PAD
OFF_CVV = 0
OFF_CVG = OFF_CVV + CONV_W
OFF_LRX = OFF_CVG + CONV_W
OFF_LRY = OFF_LRX + LRU_W
OFF_Q = OFF_LRY + LRU_W
OFF_K = OFF_Q + QK_W
OFF_V = OFF_K + QK_W
OFF_GLR = OFF_V + V_W
OFF_OG = OFF_GLR + LANES
N_IN = OFF_OG + V_W
MIX_W = CONV_W + LRU_W + V_W

CONV_HIST = 32
LRU_HIST = 8
GLA_BLOCK = 256

N_PAIRS = 6
N_BUCKETS = N_GROUPS * N_PAIRS
ROW_W = D_MODEL + LANES

MIX_TS = 512
ROUTE_TT = 512
FFN_TM = 256
PERM_TB = 512
COMB_TC = 512
DMA_UNROLL = 8
GATHER_SLOTS = 3

_F32 = jnp.float32
_BF16 = jnp.bfloat16


def _sigmoid(x):
    return 1.0 / (1.0 + jnp.exp(-x))


def _silu(x):
    return x * _sigmoid(x)


def _dot(a, b):
    return jnp.dot(a, b, preferred_element_type=_F32)


def _dot_nt(a, b):
    return lax.dot_general(a, b, (((1,), (1,)), ((), ())), preferred_element_type=_F32)


def _split2(x):
    hi = x.astype(_BF16)
    return hi, (x - hi.astype(_F32)).astype(_BF16)


def _mod_kernel(c_ref, w_ref, b_ref, o_ref):
    c_act = _silu(c_ref[...])
    o_ref[0] = _dot(c_act.astype(_BF16), w_ref[0].astype(_BF16)) + b_ref[0]


def _modulation(c, w_ada, b_ada):
    n_layers, d, n = w_ada.shape
    bsz = c.shape[0]
    tn = 1536
    return pl.pallas_call(
        _mod_kernel,
        out_shape=jax.ShapeDtypeStruct((n_layers, bsz, n), _F32),
        grid=(n_layers, n // tn),
        in_specs=[
            pl.BlockSpec((bsz, d), lambda l, j: (0, 0)),
            pl.BlockSpec((1, d, tn), lambda l, j: (l, 0, j)),
            pl.BlockSpec((1, 1, tn), lambda l, j: (l, 0, j)),
        ],
        out_specs=pl.BlockSpec((1, bsz, tn), lambda l, j: (l, 0, j)),
        compiler_params=pltpu.CompilerParams(
            dimension_semantics=("arbitrary", "arbitrary"), vmem_limit_bytes=VMEM_LIMIT_BYTES),
        name="adaln_mod",
    )(c, w_ada, b_ada.reshape(n_layers, 1, n))


def _mix_kernel(x_ref, mod_ref, gmix_ref, win_ref, cw_ref, cb_ref, lng_ref, lnb_ref,
                lcw_ref, lcb_ref, wai_ref, bai_ref, lam_ref, wgate_ref, bgate_ref, gnorm_ref,
                wout_ref, *rest, fused_moe_gather):
    if fused_moe_gather:
        pos_cur, pos_nxt, pos_ahd, modp_ref, y_hbm, o_ref = rest[:6]
        ybuf, gsem = rest[-2:]
        rest = rest[6:-2]
    else:
        o_ref, rest = rest[0], rest[1:]
    ubuf, lbuf, hcar, state, tri_ref, tria_ref, csel_ref, causal_ref, hdiag_ref = rest
    ts = x_ref.shape[1]
    s_idx = pl.program_id(1)

    @pl.when(s_idx == 0)
    def _():
        ubuf[0:CONV_HIST, :] = jnp.zeros((CONV_HIST, CONV_W), _F32)
        lbuf[0:LRU_HIST, :] = jnp.zeros((LRU_HIST, LRU_W), _F32)
        hcar[...] = jnp.zeros_like(hcar)
        state[...] = jnp.zeros_like(state)
        ri = lax.broadcasted_iota(jnp.int32, (GLA_BLOCK, GLA_BLOCK), 0)
        ci = lax.broadcasted_iota(jnp.int32, (GLA_BLOCK, GLA_BLOCK), 1)
        same_chunk = ri // GLA_CHUNK == ci // GLA_CHUNK
        causal_f = jnp.where(same_chunk & (ci <= ri), 1.0, 0.0)
        causal_ref[...] = causal_f
        tri_ref[...] = causal_f.astype(_BF16)
        tria_ref[...] = jnp.where(same_chunk & (ci > ri), 1.0, 0.0).astype(_BF16)
        sel_r = lax.broadcasted_iota(jnp.int32, (LANES, GLA_BLOCK), 0)
        sel_c = lax.broadcasted_iota(jnp.int32, (LANES, GLA_BLOCK), 1)
        csel_ref[...] = jnp.where(sel_r == sel_c // GLA_CHUNK, 1.0, 0.0).astype(_BF16)
        st_row_head = lax.broadcasted_iota(jnp.int32, (QK_W, V_W), 0) // DK_PAD
        st_col_head = lax.broadcasted_iota(jnp.int32, (QK_W, V_W), 1) // DV_PAD
        hdiag_ref[...] = jnp.where(st_row_head == st_col_head, 1.0, 0.0)

    x = x_ref[0]
    if fused_moe_gather:
        step = pl.program_id(0) * pl.num_programs(1) + s_idx
        n_steps = pl.num_programs(0) * pl.num_programs(1)
        slot = step % GATHER_SLOTS

        def row_copy(pos_ref, j, to_slot):
            return pltpu.make_async_copy(y_hbm.at[pl.ds(pos_ref[j], 1)], ybuf.at[to_slot, pl.ds(j, 1)],
                                         gsem.at[to_slot])

        def wait_rows(of_slot):
            pltpu.make_async_copy(y_hbm.at[pl.ds(0, ts)], ybuf.at[of_slot], gsem.at[of_slot]).wait()

        @pl.when(step == 0)
        def _():
            def issue(j, carry):
                row_copy(pos_cur, j, 0).start()
                row_copy(pos_nxt, j, 1).start()
                return carry
            lax.fori_loop(0, ts, issue, 0, unroll=DMA_UNROLL)

        wait_rows(slot)
        x = x + modp_ref[0, 5:6, :] * ybuf[slot]

        ahead_slot = (step + GATHER_SLOTS - 1) % GATHER_SLOTS
        for j in range(ts):
            row_copy(pos_ahd, j, ahead_slot).start()

    sh1 = mod_ref[0, 0:1, :]
    gt1 = mod_ref[0, 2:3, :]
    scale = gmix_ref[...] * (1.0 + mod_ref[0, 1:2, :])
    ms = jnp.mean(x * x, axis=-1, keepdims=True)
    hb = (x * lax.rsqrt(ms + EPS) * scale + sh1).astype(_BF16)

    def proj(off, width):
        return _dot(hb, win_ref[:, off:off + width])

    out = {}
    _run_alternately(
        _conv_lru_steps(out, proj, cw_ref, cb_ref, lng_ref, lnb_ref, lcw_ref, lcb_ref, wai_ref, bai_ref,
                        lam_ref, ubuf, lbuf, hcar, ts),
        _gla_steps(out, proj, wgate_ref, bgate_ref, gnorm_ref, state, tri_ref, tria_ref, csel_ref,
                   causal_ref, hdiag_ref, ts))

    mixed = (_dot(out["o_g"].astype(_BF16), wout_ref[CONV_W + LRU_W:MIX_W, :])
             + _dot(out["r_out"].astype(_BF16), wout_ref[CONV_W:CONV_W + LRU_W, :])
             + _dot(out["u_out"].astype(_BF16), wout_ref[0:CONV_W, :]))
    o_ref[0] = x + gt1 * mixed

    if fused_moe_gather:
        @pl.when(step == n_steps - 1)
        def _():
            wait_rows((step + 1) % GATHER_SLOTS)
            wait_rows((step + 2) % GATHER_SLOTS)


def _run_alternately(*step_generators):
    clock = [0] * len(step_generators)
    live = list(range(len(step_generators)))
    while live:
        i = min(live, key=lambda j: clock[j])
        try:
            clock[i] += next(step_generators[i]) * (1 if i else 0.5)
        except StopIteration:
            live.remove(i)


def _conv_lru_steps(out, proj, cw_ref, cb_ref, lng_ref, lnb_ref, lcw_ref, lcb_ref, wai_ref, bai_ref,
                    lam_ref, ubuf, lbuf, hcar, ts):
    u = proj(OFF_CVV, CONV_W) * _sigmoid(proj(OFF_CVG, CONV_W))
    ubuf[CONV_HIST:CONV_HIST + ts, :] = u
    yield 500
    lbuf[LRU_HIST:LRU_HIST + ts, :] = proj(OFF_LRX, LRU_W)
    xb = jnp.broadcast_to(lcb_ref[...], (ts, LRU_W))
    for k in range(LRU_CONV_K):
        off = LRU_HIST - (LRU_CONV_K - 1) + k
        xb = xb + lcw_ref[k:k + 1, :] * lbuf[off:off + ts, :]
    lbuf[0:LRU_HIST, :] = lbuf[ts:ts + LRU_HIST, :]
    gates = _dot(xb.astype(_BF16), wai_ref[...]) + bai_ref[...]
    yield 800

    acc = jnp.broadcast_to(cb_ref[...], (ts, CONV_W))
    for r in range(SUBLANES):
        z = None
        for a_blk in range((CONV_K - 1 - r) // SUBLANES + 1):
            k = CONV_K - 1 - (SUBLANES * a_blk + r)
            off = CONV_HIST - SUBLANES * (a_blk + 1)
            term = cw_ref[k:k + 1, :] * ubuf[off:off + ts + SUBLANES, :]
            z = term if z is None else z + term
        acc = acc + z[SUBLANES - r:SUBLANES - r + ts]
        yield 200
    ubuf[0:CONV_HIST, :] = ubuf[ts:ts + CONV_HIST, :]
    mu = jnp.mean(acc, axis=-1, keepdims=True)
    cen = acc - mu
    var = jnp.mean(cen * cen, axis=-1, keepdims=True)
    out["u_out"] = _silu(cen * lax.rsqrt(var + EPS) * lng_ref[...] + lnb_ref[...])
    yield 250

    r_gate = _sigmoid(gates[:, 0:LRU_W])
    i_gate = _sigmoid(gates[:, LRU_W:2 * LRU_W])
    lam = lam_ref[...]
    softplus_neg_lam = jnp.maximum(-lam, 0.0) + jnp.log1p(jnp.exp(-jnp.abs(lam)))
    log_a = (-LRU_C) * r_gate * softplus_neg_lam
    a = jnp.exp(log_a)
    mult = jnp.sqrt(jnp.tanh(-log_a) * (a * a + 1.0))
    u_in = mult * (i_gate * xb)
    yield 600
    row = lax.broadcasted_iota(jnp.int32, a.shape, 0) % SUBLANES
    d = 1
    while d < SUBLANES:
        keep = row >= d
        a_sh = jnp.where(keep, pltpu.roll(a, d, axis=0), 1.0)
        u_sh = jnp.where(keep, pltpu.roll(u_in, d, axis=0), 0.0)
        u_in = u_in + a * u_sh
        a = a * a_sh
        d *= 2
        yield 200
    groups = []
    h_prev = hcar[0:1, :]
    for g in range(ts // SUBLANES):
        r0 = g * SUBLANES
        h_g = u_in[r0:r0 + SUBLANES] + a[r0:r0 + SUBLANES] * h_prev
        groups.append(h_g)
        h_prev = h_g[SUBLANES - 1:SUBLANES]
        if g % SUBLANES == SUBLANES - 1:
            yield 50
    h_lru = jnp.concatenate(groups, axis=0)
    hcar[...] = jnp.broadcast_to(h_prev, hcar.shape)
    out["r_out"] = h_lru * jax.nn.gelu(proj(OFF_LRY, LRU_W), approximate=True)


def _gla_steps(out, proj, wgate_ref, bgate_ref, gnorm_ref, state, tri_ref, tria_ref, csel_ref, causal_ref,
               hdiag_ref, ts):
    zg = proj(OFF_GLR, LANES)
    glog = _dot(zg.astype(_BF16), wgate_ref[...]) + bgate_ref[...]
    lg = (jnp.minimum(glog, 0.0) - jnp.log1p(jnp.exp(-jnp.abs(glog)))) * (1.0 / GLA_TAU)
    yield 300
    zq = proj(OFF_Q, QK_W) * (GLA_DK ** -0.5)
    zk = proj(OFF_K, QK_W)
    yield 500
    zv = proj(OFF_V, V_W)
    yield 500

    nblk = ts // GLA_BLOCK
    cpb = GLA_BLOCK // GLA_CHUNK
    tri = tri_ref[...]
    tri_after = tria_ref[...]
    chunk_sel = csel_ref[...]
    causal = causal_ref[...] > 0.5
    qk_lane_head = lax.broadcasted_iota(jnp.int32, (1, QK_W), 1) // DK_PAD
    t_lane_chunk = lax.broadcasted_iota(jnp.int32, (1, GLA_BLOCK), 1) // GLA_CHUNK

    o_blocks = []
    for blk in range(nblk):
        r0 = blk * GLA_BLOCK
        lg_b = lg[r0:r0 + GLA_BLOCK]
        p_hi, p_lo = _split2(lg_b)
        b = _dot(tri, p_hi) + _dot(tri, p_lo)
        b_rest = _dot(tri_after, p_hi) + _dot(tri_after, p_lo)
        b_tot = _dot(chunk_sel, p_hi) + _dot(chunk_sel, p_lo)
        yield 500
        q_in =zq[r0:r0 + GLA_BLOCK] * jnp.exp(b)
        k_blk = zk[r0:r0 + GLA_BLOCK]
        k_in = (k_blk * jnp.exp(-b)).astype(_BF16)
        k_out_t = (k_blk * jnp.exp(b_rest)).T.astype(_BF16)
        v_b = zv[r0:r0 + GLA_BLOCK].astype(_BF16)
        q_in_b = q_in.astype(_BF16)
        decay_cols = jnp.exp(b_tot).T

        o_heads = []
        for hd in range(GLA_HEADS):
            q_h = jnp.where(qk_lane_head == hd, q_in_b, jnp.zeros_like(q_in_b))
            sc = jnp.where(causal, _dot_nt(q_h, k_in), 0.0)
            o_heads.append(_dot(sc.astype(_BF16), v_b[:, hd * DV_PAD:(hd + 1) * DV_PAD]))
            yield 300
        o_intra = jnp.concatenate(o_heads, axis=1)

        o_inter = []
        st = state[...]
        for c in range(cpb):
            c0 = c * GLA_CHUNK
            o_inter.append(_dot(q_in_b[c0:c0 + GLA_CHUNK], st.astype(_BF16)))
            kv = _dot(jnp.where(t_lane_chunk == c, k_out_t, jnp.zeros_like(k_out_t)), v_b)
            st = st * decay_cols[:, c:c + 1] + kv * hdiag_ref[...]
            yield 400
        state[...] = st
        o_blocks.append(o_intra + jnp.concatenate(o_inter, axis=0))
    o = o_blocks[0] if nblk == 1 else jnp.concatenate(o_blocks, axis=0)

    og = proj(OFF_OG, V_W)
    o_parts = []
    for hd in range(GLA_HEADS):
        o_h = o[:, hd * DV_PAD:(hd + 1) * DV_PAD]
        ms_h = jnp.sum(o_h * o_h, axis=-1, keepdims=True) * (1.0 / GLA_DV)
        o_parts.append(o_h * lax.rsqrt(ms_h + EPS))
    o_n = jnp.concatenate(o_parts, axis=1) * gnorm_ref[...]
    out["o_g"] = o_n * _silu(og)


def _token_mix(x, mod, params, layer, moe=None):
    bsz, seq, d = x.shape
    ts = MIX_TS
    n_s = seq // ts
    full = lambda shape: pl.BlockSpec((None,) + shape, lambda b, s: (layer,) + (0,) * len(shape))
    in_specs = [
        pl.BlockSpec((1, ts, d), lambda b, s: (b, s, 0)),
        pl.BlockSpec((1, 6, d), lambda b, s: (b, 0, 0)),
        full((1, d)),
        full((d, N_IN)),
        full((CONV_HIST, CONV_W)), full((1, CONV_W)), full((1, CONV_W)), full((1, CONV_W)),
        full((LRU_CONV_K, LRU_W)), full((1, LRU_W)),
        full((LRU_W, 2 * LRU_W)), full((1, 2 * LRU_W)), full((1, LRU_W)),
        full((LANES, QK_W)), full((1, QK_W)), full((1, V_W)),
        full((MIX_W, d)),
    ]
    scratch_shapes = [
        pltpu.VMEM((CONV_HIST + ts, CONV_W), _F32),
        pltpu.VMEM((LRU_HIST + ts, LRU_W), _F32),
        pltpu.VMEM((SUBLANES, LRU_W), _F32),
        pltpu.VMEM((QK_W, V_W), _F32),
        pltpu.VMEM((GLA_BLOCK, GLA_BLOCK), _BF16),
        pltpu.VMEM((GLA_BLOCK, GLA_BLOCK), _BF16),
        pltpu.VMEM((LANES, GLA_BLOCK), _BF16),
        pltpu.VMEM((GLA_BLOCK, GLA_BLOCK), _F32),
        pltpu.VMEM((QK_W, V_W), _F32),
    ]
    args = [x, mod] + list(params)
    if moe is not None:
        pos, mod_prev, y = moe
        last = bsz * n_s - 1
        in_specs += [
            pl.BlockSpec((ts,), lambda b, s: (b * n_s + s,), memory_space=pltpu.SMEM),
            pl.BlockSpec((ts,), lambda b, s: (jnp.minimum(b * n_s + s + 1, last),), memory_space=pltpu.SMEM),
            pl.BlockSpec((ts,), lambda b, s: (jnp.minimum(b * n_s + s + 2, last),), memory_space=pltpu.SMEM),
            pl.BlockSpec((1, 6, d), lambda b, s: (b, 0, 0)),
            pl.BlockSpec(memory_space=pl.ANY),
        ]
        scratch_shapes += [pltpu.VMEM((GATHER_SLOTS, ts, d), _F32), pltpu.SemaphoreType.DMA((GATHER_SLOTS,))]
        args += [pos, pos, pos, mod_prev, y]
    return pl.pallas_call(
        functools.partial(_mix_kernel, fused_moe_gather=moe is not None),
        out_shape=jax.ShapeDtypeStruct(x.shape, _F32),
        grid=(bsz, n_s),
        in_specs=in_specs,
        out_specs=pl.BlockSpec((1, ts, d), lambda b, s: (b, s, 0)),
        scratch_shapes=scratch_shapes,
        compiler_params=pltpu.CompilerParams(
            dimension_semantics=("arbitrary", "arbitrary"), vmem_limit_bytes=VMEM_LIMIT_BYTES),
        name="token_mix",
    )(*args)


def _route_kernel(x_ref, mod_ref, g_ref, wr_ref, br_ref, xp_ref, meta_ref, cnt_ref, carry):
    tt = x_ref.shape[0]
    i = pl.program_id(0)

    @pl.when(i == 0)
    def _():
        carry[...] = jnp.zeros_like(carry)

    x = x_ref[...]
    sh2 = mod_ref[0, 3:4, :]
    sc2 = mod_ref[0, 4:5, :]
    ms = jnp.mean(x * x, axis=-1, keepdims=True)
    h = (x * lax.rsqrt(ms + EPS) * g_ref[...]) * (1.0 + sc2) + sh2

    h_hi, h_lo = _split2(h)
    w_hi, w_lo = _split2(wr_ref[...])
    w_cat = jnp.concatenate([w_hi, w_lo], axis=1)
    prod = _dot(h_hi, w_cat) + _dot(h_lo, w_cat)
    logits = prod[:, 0:LANES] + prod[:, LANES:2 * LANES] + br_ref[...]

    lane = lax.broadcasted_iota(jnp.int32, (tt, LANES), 1)
    lane_f = lane.astype(_F32)
    neg = -jnp.inf
    big = float(LANES)

    def first_argmax(vals, vmax):
        return jnp.min(jnp.where(vals == vmax, lane_f, big), axis=-1, keepdims=True).astype(jnp.int32)

    gl = jnp.where(lane < N_GROUPS, logits, neg)
    gmax = jnp.max(gl, axis=-1, keepdims=True)
    g_star = first_argmax(gl, gmax)
    p_sel = 1.0 / jnp.sum(jnp.exp(gl - gmax), axis=-1, keepdims=True)
    base = N_GROUPS + EXPERTS_PER_GROUP * g_star
    el = jnp.where((lane >= base) & (lane < base + EXPERTS_PER_GROUP), logits, neg)
    v0 = jnp.max(el, axis=-1, keepdims=True)
    i0 = first_argmax(el, v0)
    el2 = jnp.where(lane == i0, neg, el)
    v1 = jnp.max(el2, axis=-1, keepdims=True)
    i1 = first_argmax(el2, v1)
    ex = jnp.exp(v1 - v0)
    wt0 = p_sel / (1.0 + ex)
    wt1 = p_sel * ex / (1.0 + ex)
    e0 = i0 - base
    e1 = i1 - base
    e_lo = jnp.minimum(e0, e1)
    e_hi = jnp.maximum(e0, e1)
    w_lo = jnp.where(e0 < e1, wt0, wt1)
    w_hi = jnp.where(e0 < e1, wt1, wt0)
    pair = (e_lo * (2 * EXPERTS_PER_GROUP - 1 - e_lo)) // 2 + (e_hi - e_lo - 1)
    bucket = g_star * N_PAIRS + pair

    onehot = lane == bucket
    onehot_f = jnp.where(onehot, 1.0, 0.0)
    ri = lax.broadcasted_iota(jnp.int32, (tt, tt), 0)
    ci = lax.broadcasted_iota(jnp.int32, (tt, tt), 1)
    strict = jnp.where(ci < ri, 1.0, 0.0).astype(_BF16)
    prefix = _dot(strict, onehot_f.astype(_BF16)) + carry[0:1, :]
    rank = jnp.sum(jnp.where(onehot, prefix, 0.0), axis=-1, keepdims=True)
    carry[...] = carry[...] + jnp.sum(onehot_f, axis=0, keepdims=True)
    cnt_ref[...] = carry[...]

    meta = jnp.where(lane == 0, bucket.astype(_F32), jnp.where(lane == 1, rank, 0.0))
    meta_ref[...] = meta.T[0:SUBLANES, :]

    xp_ref[:, 0:D_MODEL] = h
    xp_ref[:, D_MODEL:ROW_W] = jnp.where(lane == 0, w_lo, jnp.where(lane == 1, w_hi, 0.0))


def _route(x2d, mod, g_ffn, w_r, b_r, layer, seq):
    n_tok, d = x2d.shape
    tt = ROUTE_TT
    tiles_per_seq = seq // tt
    per_layer = lambda shape: pl.BlockSpec((None,) + shape, lambda i: (layer, 0, 0))
    return pl.pallas_call(
        _route_kernel,
        out_shape=(
            jax.ShapeDtypeStruct((n_tok, ROW_W), _F32),
            jax.ShapeDtypeStruct((SUBLANES, n_tok), _F32),
            jax.ShapeDtypeStruct((SUBLANES, LANES), _F32),
        ),
        grid=(n_tok // tt,),
        in_specs=[
            pl.BlockSpec((tt, d), lambda i: (i, 0)),
            pl.BlockSpec((1, 6, d), lambda i: (i // tiles_per_seq, 0, 0)),
            per_layer((1, d)),
            per_layer((d, LANES)),
            per_layer((1, LANES)),
        ],
        out_specs=(
            pl.BlockSpec((tt, ROW_W), lambda i: (i, 0)),
            pl.BlockSpec((SUBLANES, tt), lambda i: (0, i)),
            pl.BlockSpec((SUBLANES, LANES), lambda i: (0, 0)),
        ),
        scratch_shapes=[pltpu.VMEM((SUBLANES, LANES), _F32)],
        compiler_params=pltpu.CompilerParams(
            dimension_semantics=("arbitrary",), vmem_limit_bytes=VMEM_LIMIT_BYTES),
        name="moe_route",
    )(x2d, mod, g_ffn, w_r, b_r)


def _permute_kernel(fill_ref, pos_ref, xp_ref, xs_hbm, zbuf, sem, zsem):
    tb = pos_ref.shape[0]

    @pl.when(pl.program_id(0) == 0)
    def _():
        zbuf[...] = jnp.zeros_like(zbuf)
        n_used = fill_ref[2 * N_BUCKETS]
        n_tiles = xs_hbm.shape[0] // FFN_TM
        for wait in (False, True):
            for b in range(N_BUCKETS):
                for row0, cond in ((fill_ref[b], fill_ref[N_BUCKETS + b] > 0),
                                   ((n_used + b) * FFN_TM, n_used + b < n_tiles)):
                    @pl.when(cond)
                    def _():
                        dst = xs_hbm.at[pl.ds(pl.multiple_of(row0, FFN_TM), FFN_TM)]
                        fill = pltpu.make_async_copy(zbuf, dst, zsem)
                        fill.wait() if wait else fill.start()

    for j in range(tb):
        pltpu.make_async_copy(xp_ref.at[pl.ds(j, 1)], xs_hbm.at[pl.ds(pos_ref[j], 1)], sem).start(priority=j % 2)
    pltpu.make_async_copy(xp_ref, xs_hbm.at[pl.ds(0, tb)], sem).wait()


def _permute(fill_starts, pos, xp, n_rows):
    n_tok = xp.shape[0]
    tb = PERM_TB
    return pl.pallas_call(
        _permute_kernel,
        out_shape=jax.ShapeDtypeStruct((n_rows, ROW_W), _F32),
        grid_spec=pltpu.PrefetchScalarGridSpec(
            num_scalar_prefetch=1,
            grid=(n_tok // tb,),
            in_specs=[
                pl.BlockSpec((tb,), lambda i, fs: (i,), memory_space=pltpu.SMEM),
                pl.BlockSpec((tb, ROW_W), lambda i, fs: (i, 0)),
            ],
            out_specs=pl.BlockSpec(memory_space=pl.ANY),
            scratch_shapes=[pltpu.VMEM((FFN_TM, ROW_W), _F32), pltpu.SemaphoreType.DMA,
                            pltpu.SemaphoreType.DMA],
        ),
        compiler_params=pltpu.CompilerParams(
            dimension_semantics=("arbitrary",), vmem_limit_bytes=VMEM_LIMIT_BYTES),
        name="moe_permute",
    )(fill_starts, pos, xp)


def _ffn_kernel(tg_ref, tlo_ref, thi_ref, clo_ref, chi_ref, ng_ref, nlo_ref, nhi_ref, plo_ref, phi_ref, nused_ref,
                xs_ref, wg_hbm, wu_hbm, wd_hbm, y_ref, sg, su, sd, agu, ad, sem, *, layer):
    i = pl.program_id(0)

    def weight_copies(slot, g, e):
        return (pltpu.make_async_copy(wg_hbm.at[layer, g, e], sg.at[slot], sem.at[slot, 0]),
                pltpu.make_async_copy(wu_hbm.at[layer, g, e], su.at[slot], sem.at[slot, 1]),
                pltpu.make_async_copy(wd_hbm.at[layer, g, e], sd.at[slot], sem.at[slot, 2]))

    @pl.when(i == 0)
    def _():
        for cp in weight_copies(0, tg_ref[0], tlo_ref[0]) + weight_copies(1, tg_ref[0], thi_ref[0]):
            cp.start()

    for slot, changed_ref, e_ref in ((0, clo_ref, tlo_ref), (1, chi_ref, thi_ref)):
        @pl.when(changed_ref[i] == 1)
        def _():
            for cp in weight_copies(slot, tg_ref[i], e_ref[i]):
                cp.wait()
            agu[slot, :, 0:D_EXPERT] = sg[slot].astype(_BF16)
            agu[slot, :, D_EXPERT:2 * D_EXPERT] = su[slot].astype(_BF16)
            ad[slot] = sd[slot].astype(_BF16)

    for slot, prefetch_ref, e_ref in ((0, plo_ref, nlo_ref), (1, phi_ref, nhi_ref)):
        @pl.when(prefetch_ref[i] == 1)
        def _():
            for cp in weight_copies(slot, ng_ref[i], e_ref[i]):
                cp.start()

    @pl.when(i < nused_ref[0])
    def _():
        xb = xs_ref[:, 0:D_MODEL].astype(_BF16)
        info = xs_ref[:, D_MODEL:ROW_W]

        def expert(slot):
            gate_up = _dot(xb, agu[slot])
            hid = _silu(gate_up[:, 0:D_EXPERT]) * gate_up[:, D_EXPERT:2 * D_EXPERT] * info[:, slot:slot + 1]
            return _dot(hid.astype(_BF16), ad[slot])

        y_ref[...] = expert(0) + expert(1)

    @pl.when(i >= nused_ref[0])
    def _():
        y_ref[...] = jnp.zeros_like(y_ref)


def _expert_ffn(layer, plan, xs, w_gate, w_up, w_down):
    n_rows = xs.shape[0]
    tm = FFN_TM
    n_tiles = n_rows // tm

    def row_map(i, *prefetch):
        n_used = prefetch[-1]
        return (jnp.minimum(i, n_used[0] - 1), 0)

    hbm = pl.BlockSpec(memory_space=pl.ANY)
    return pl.pallas_call(
        functools.partial(_ffn_kernel, layer=layer),
        out_shape=jax.ShapeDtypeStruct((n_rows, D_MODEL), _F32),
        grid_spec=pltpu.PrefetchScalarGridSpec(
            num_scalar_prefetch=len(plan),
            grid=(n_tiles,),
            in_specs=[pl.BlockSpec((tm, ROW_W), row_map), hbm, hbm, hbm],
            out_specs=pl.BlockSpec((tm, D_MODEL), lambda i, *prefetch: (i, 0)),
            scratch_shapes=[
                pltpu.VMEM((2, D_MODEL, D_EXPERT), _F32), pltpu.VMEM((2, D_MODEL, D_EXPERT), _F32),
                pltpu.VMEM((2, D_EXPERT, D_MODEL), _F32),
                pltpu.VMEM((2, D_MODEL, 2 * D_EXPERT), _BF16), pltpu.VMEM((2, D_EXPERT, D_MODEL), _BF16),
                pltpu.SemaphoreType.DMA((2, 3)),
            ],
        ),
        compiler_params=pltpu.CompilerParams(
            dimension_semantics=("arbitrary",), vmem_limit_bytes=VMEM_LIMIT_BYTES),
        name="moe_ffn",
    )(*plan, xs, w_gate, w_up, w_down)


def _combine_kernel(pos_ref, x_ref, mod_ref, gfin_ref, y_hbm, o_ref, ybuf, sem):
    tc = x_ref.shape[0]
    for j in range(tc):
        pltpu.make_async_copy(y_hbm.at[pl.ds(pos_ref[j], 1)], ybuf.at[pl.ds(j, 1)], sem).start(priority=j % 2)
    pltpu.make_async_copy(y_hbm.at[pl.ds(0, tc)], ybuf, sem).wait()

    gt2 = mod_ref[0, 5:6, :]
    out = x_ref[...] + gt2 * ybuf[...]
    ms = jnp.mean(out * out, axis=-1, keepdims=True)
    o_ref[...] = out * lax.rsqrt(ms + EPS) * gfin_ref[...]


def _combine(pos, x2d, mod, g_final, y, seq):
    n_tok, d = x2d.shape
    tc = COMB_TC
    tiles_per_seq = seq // tc
    return pl.pallas_call(
        _combine_kernel,
        out_shape=jax.ShapeDtypeStruct((n_tok, d), _F32),
        grid=(n_tok // tc,),
        in_specs=[
            pl.BlockSpec((tc,), lambda i: (i,), memory_space=pltpu.SMEM),
            pl.BlockSpec((tc, d), lambda i: (i, 0)),
            pl.BlockSpec((1, 6, d), lambda i: (i // tiles_per_seq, 0, 0)),
            pl.BlockSpec((1, d), lambda i: (0, 0)),
            pl.BlockSpec(memory_space=pl.ANY),
        ],
        out_specs=pl.BlockSpec((tc, d), lambda i: (i, 0)),
        scratch_shapes=[pltpu.VMEM((tc, d), _F32), pltpu.SemaphoreType.DMA],
        compiler_params=pltpu.CompilerParams(
            dimension_semantics=("arbitrary",), vmem_limit_bytes=VMEM_LIMIT_BYTES),
        name="moe_combine",
    )(pos, x2d, mod, g_final, y)


def _pad_heads(w, heads, width, padded):
    lead = w.shape[:-1]
    w = w.reshape(lead + (heads, width))
    w = jnp.pad(w, [(0, 0)] * len(lead) + [(0, 0), (0, padded - width)])
    return w.reshape(lead + (heads * padded,))


def _block_diag(w):
    n_layers, n, bw, _ = w.shape
    eye = jnp.eye(n, dtype=w.dtype)
    return (eye[None, :, None, :, None] * w[:, :, :, None, :]).reshape(n_layers, n * bw, n * bw)


def _prep_params(w_in, conv_dw_w, conv_dw_b, conv_ln_g, conv_ln_b, lru_conv_w, lru_conv_b, lru_w_a,
                 lru_b_a, lru_w_i, lru_b_i, lru_lam, gla_w_gate, gla_b_gate, gla_norm_g, w_out, g_mix):
    sizes = [CONV_W, CONV_W, LRU_W, LRU_W, GLA_HEADS * GLA_DK, GLA_HEADS * GLA_DK, GLA_V, GLA_RANK, GLA_V]
    cv_v, cv_g, lr_x, lr_y, q, k, v, g_lr, og = jnp.split(w_in, np.cumsum(sizes)[:-1].tolist(), axis=-1)
    w_in_p = jnp.concatenate([
        cv_v, cv_g, lr_x, lr_y,
        _pad_heads(q, GLA_HEADS, GLA_DK, DK_PAD), _pad_heads(k, GLA_HEADS, GLA_DK, DK_PAD),
        _pad_heads(v, GLA_HEADS, GLA_DV, DV_PAD),
        jnp.pad(g_lr, ((0, 0), (0, 0), (0, LANES - GLA_RANK))),
        _pad_heads(og, GLA_HEADS, GLA_DV, DV_PAD)], axis=-1).astype(_BF16)
    n_layers = w_in.shape[0]
    wo_o = w_out[:, CONV_W + LRU_W:].reshape(n_layers, GLA_HEADS, GLA_DV, D_MODEL)
    wo_o = jnp.pad(wo_o, ((0, 0), (0, 0), (0, DV_PAD - GLA_DV), (0, 0))).reshape(n_layers, V_W, D_MODEL)
    w_out_p = jnp.concatenate([w_out[:, :CONV_W + LRU_W], wo_o], axis=1).astype(_BF16)
    w_gate_p = jnp.pad(_pad_heads(gla_w_gate, GLA_HEADS, GLA_DK, DK_PAD),
                       ((0, 0), (0, LANES - GLA_RANK), (0, 0))).astype(_BF16)
    row = lambda v: v[:, None, :]
    return [
        row(g_mix),
        w_in_p,
        jnp.pad(conv_dw_w, ((0, 0), (0, CONV_HIST - CONV_K), (0, 0))),
        row(conv_dw_b), row(conv_ln_g), row(conv_ln_b),
        lru_conv_w, row(lru_conv_b),
        jnp.concatenate([_block_diag(lru_w_a), _block_diag(lru_w_i)], axis=2).astype(_BF16),
        row(jnp.concatenate([lru_b_a, lru_b_i], axis=1)),
        row(lru_lam),
        w_gate_p,
        row(_pad_heads(gla_b_gate, GLA_HEADS, GLA_DK, DK_PAD)),
        row(_pad_heads(gla_norm_g, GLA_HEADS, GLA_DV, DV_PAD)),
        w_out_p,
    ]


def _bucket_layout(counts, n_tiles):
    tm = FFN_TM
    counts = counts.astype(jnp.int32)
    tiles = (counts + tm - 1) // tm
    tile_end = jnp.cumsum(tiles)
    tile_start = tile_end - tiles
    starts = (tile_start * tm).astype(jnp.int32)
    n_used = tile_end[-1]
    tile_idx = jnp.minimum(jnp.arange(n_tiles, dtype=jnp.int32), n_used - 1)
    bucket_of = lambda t: jnp.sum((t[:, None] >= tile_end[None, :]).astype(jnp.int32), axis=1)
    pair_lo = jnp.array([0, 0, 0, 1, 1, 2], jnp.int32)
    pair_hi = jnp.array([1, 2, 3, 2, 3, 3], jnp.int32)
    experts = lambda b: (b // N_PAIRS, pair_lo[b % N_PAIRS], pair_hi[b % N_PAIRS])

    tile_bucket = bucket_of(tile_idx)
    g, lo, hi = experts(tile_bucket)
    first = jnp.concatenate([jnp.ones((1,), bool), tile_bucket[1:] != tile_bucket[:-1]])
    prev = lambda v: jnp.concatenate([jnp.full((1,), -1, jnp.int32), v[:-1]])
    changed_lo = first & ((g != prev(g)) | (lo != prev(lo)))
    changed_hi = first & ((g != prev(g)) | (hi != prev(hi)))
    next_first_tile = tile_end[tile_bucket]
    has_next = next_first_tile < n_used
    ng, nlo, nhi = experts(bucket_of(jnp.minimum(next_first_tile, n_used - 1)))
    prefetch_lo = first & has_next & ((ng != g) | (nlo != lo))
    prefetch_hi = first & has_next & ((ng != g) | (nhi != hi))
    as_i32 = lambda v: v.astype(jnp.int32)
    ffn_plan = (g, lo, hi, as_i32(changed_lo), as_i32(changed_hi), ng, nlo, nhi,
                as_i32(prefetch_lo), as_i32(prefetch_hi), n_used.reshape(1).astype(jnp.int32))
    fill = jnp.concatenate([(jnp.maximum(tile_end - 1, 0) * tm).astype(jnp.int32), tiles, n_used[None]])
    return starts, fill, ffn_plan


def kernel(x, c, w_ada, b_ada, g_mix, w_in, conv_dw_w, conv_dw_b, conv_ln_g, conv_ln_b, lru_conv_w,
           lru_conv_b, lru_w_a, lru_b_a, lru_w_i, lru_b_i, lru_lam, gla_w_gate, gla_b_gate, gla_norm_g,
           w_out, g_ffn, w_route_group, b_route_group, w_route_expert, b_route_expert, w_gate, w_up,
           w_down, g_final):
    bsz, seq, d = x.shape
    n_layers = w_ada.shape[0]
    n_tok = bsz * seq
    assert d == D_MODEL and seq % max(MIX_TS, ROUTE_TT, COMB_TC) == 0 and MIX_TS % GLA_BLOCK == 0
    assert n_tok % PERM_TB == 0
    n_tiles = n_tok // FFN_TM + N_BUCKETS
    n_rows = n_tiles * FFN_TM

    mod_all = _modulation(c, w_ada, b_ada).reshape(n_layers, bsz, 6, d)

    params = _prep_params(w_in, conv_dw_w, conv_dw_b, conv_ln_g, conv_ln_b, lru_conv_w, lru_conv_b,
                          lru_w_a, lru_b_a, lru_w_i, lru_b_i, lru_lam, gla_w_gate, gla_b_gate, gla_norm_g,
                          w_out, g_mix)
    n_experts = N_GROUPS * EXPERTS_PER_GROUP
    w_r = jnp.concatenate(
        [w_route_group, w_route_expert.transpose(0, 2, 1, 3).reshape(n_layers, d, n_experts)], axis=2)
    w_r = jnp.pad(w_r, ((0, 0), (0, 0), (0, LANES - N_GROUPS - n_experts)))
    b_r = jnp.concatenate([b_route_group, b_route_expert.reshape(n_layers, n_experts)], axis=1)
    b_r = jnp.pad(b_r, ((0, 0), (0, LANES - N_GROUPS - n_experts)))[:, None, :]
    pending_moe = None
    for l in range(n_layers):
        mod = mod_all[l]
        x = _token_mix(x, mod, params, l, pending_moe)

        x2d = x.reshape(n_tok, d)
        xp, meta, counts = _route(x2d, mod, g_ffn[:, None, :], w_r, b_r, l, seq)
        bucket = meta[0].astype(jnp.int32)
        rank = meta[1].astype(jnp.int32)
        starts, fill, ffn_plan = _bucket_layout(counts[0, :N_BUCKETS], n_tiles)
        in_bucket = bucket[:, None] == jnp.arange(N_BUCKETS, dtype=jnp.int32)[None, :]
        pos = rank + jnp.sum(jnp.where(in_bucket, starts[None, :], 0), axis=1)

        xs = _permute(fill, pos, xp, n_rows)
        y = _expert_ffn(l, ffn_plan, xs, w_gate, w_up, w_down)
        pending_moe = (pos, mod, y)
    return _combine(pos, x2d, mod, g_final[None, :], y, seq).reshape(bsz, seq, d)
```

```python
import functools

import jax
import jax.numpy as jnp
import numpy as np
from jax import lax
from jax.experimental import pallas as pl
from jax.experimental.pallas import tpu as pltpu

D_MODEL = 1024
CONV_W = 256
LRU_W = 384
GLA_V = 384
CONV_K = 31
LRU_CONV_K = 4
LRU_BLOCKS = 6
LRU_BW = 64
LRU_C = 8.0
GLA_HEADS = 4
GLA_DV = 96
GLA_DK = 48
GLA_RANK = 16
GLA_TAU = 16.0
GLA_CHUNK = 64
N_GROUPS = 4
EXPERTS_PER_GROUP = 4
D_EXPERT = 512
EPS = 1e-6

LANES = 128
SUBLANES = 8
VMEM_LIMIT_BYTES = 56 * 1024 * 1024

DK_PAD = 64
DV_PAD = 128
QK_W = GLA_HEADS * DK_PAD
V_W = GLA_HEADS * DV_PAD
OFF_CVV = 0
OFF_CVG = OFF_CVV + CONV_W
OFF_LRX = OFF_CVG + CONV_W
OFF_LRY = OFF_LRX + LRU_W
OFF_Q = OFF_LRY + LRU_W
OFF_K = OFF_Q + QK_W
OFF_V = OFF_K + QK_W
OFF_GLR = OFF_V + V_W
OFF_OG = OFF_GLR + LANES
N_IN = OFF_OG + V_W
MIX_W = CONV_W + LRU_W + V_W

CONV_HIST = 32
LRU_HIST = 8
GLA_BLOCK = 256

N_PAIRS = 6
N_BUCKETS = N_GROUPS * N_PAIRS
ROW_W = D_MODEL + LANES

MIX_TS = 512
ROUTE_TT = 512
FFN_TM = 256
PERM_TB = 512
COMB_TC = 512
DMA_UNROLL = 8
GATHER_SLOTS = 3

_F32 = jnp.float32
_BF16 = jnp.bfloat16


def _sigmoid(x):
    return 1.0 / (1.0 + jnp.exp(-x))


def _silu(x):
    return x * _sigmoid(x)


def _dot(a, b):
    return jnp.dot(a, b, preferred_element_type=_F32)


def _dot_nt(a, b):
    return lax.dot_general(a, b, (((1,), (1,)), ((), ())), preferred_element_type=_F32)


def _split2(x):
    hi = x.astype(_BF16)
    return hi, (x - hi.astype(_F32)).astype(_BF16)


def _mod_kernel(c_ref, w_ref, b_ref, o_ref):
    c_act = _silu(c_ref[...])
    o_ref[0] = _dot(c_act.astype(_BF16), w_ref[0].astype(_BF16)) + b_ref[0]


def _modulation(c, w_ada, b_ada):
    n_layers, d, n = w_ada.shape
    bsz = c.shape[0]
    tn = 1536
    return pl.pallas_call(
        _mod_kernel,
        out_shape=jax.ShapeDtypeStruct((n_layers, bsz, n), _F32),
        grid=(n_layers, n // tn),
        in_specs=[
            pl.BlockSpec((bsz, d), lambda l, j: (0, 0)),
            pl.BlockSpec((1, d, tn), lambda l, j: (l, 0, j)),
            pl.BlockSpec((1, 1, tn), lambda l, j: (l, 0, j)),
        ],
        out_specs=pl.BlockSpec((1, bsz, tn), lambda l, j: (l, 0, j)),
        compiler_params=pltpu.CompilerParams(
            dimension_semantics=("arbitrary", "arbitrary"), vmem_limit_bytes=VMEM_LIMIT_BYTES),
        name="adaln_mod",
    )(c, w_ada, b_ada.reshape(n_layers, 1, n))


def _mix_kernel(x_ref, mod_ref, gmix_ref, win_ref, cw_ref, cb_ref, lng_ref, lnb_ref,
                lcw_ref, lcb_ref, wai_ref, bai_ref, lam_ref, wgate_ref, bgate_ref, gnorm_ref,
                wout_ref, *rest, fused_moe_gather):
    if fused_moe_gather:
        pos_cur, pos_nxt, pos_ahd, modp_ref, y_hbm, o_ref = rest[:6]
        ybuf, gsem = rest[-2:]
        rest = rest[6:-2]
    else:
        o_ref, rest = rest[0], rest[1:]
    ubuf, lbuf, hcar, state, tri_ref, tria_ref, csel_ref, causal_ref, hdiag_ref = rest
    ts = x_ref.shape[1]
    s_idx = pl.program_id(1)

    @pl.when(s_idx == 0)
    def _():
        ubuf[0:CONV_HIST, :] = jnp.zeros((CONV_HIST, CONV_W), _F32)
        lbuf[0:LRU_HIST, :] = jnp.zeros((LRU_HIST, LRU_W), _F32)
        hcar[...] = jnp.zeros_like(hcar)
        state[...] = jnp.zeros_like(state)
        ri = lax.broadcasted_iota(jnp.int32, (GLA_BLOCK, GLA_BLOCK), 0)
        ci = lax.broadcasted_iota(jnp.int32, (GLA_BLOCK, GLA_BLOCK), 1)
        same_chunk = ri // GLA_CHUNK == ci // GLA_CHUNK
        causal_f = jnp.where(same_chunk & (ci <= ri), 1.0, 0.0)
        causal_ref[...] = causal_f
        tri_ref[...] = causal_f.astype(_BF16)
        tria_ref[...] = jnp.where(same_chunk & (ci > ri), 1.0, 0.0).astype(_BF16)
        sel_r = lax.broadcasted_iota(jnp.int32, (LANES, GLA_BLOCK), 0)
        sel_c = lax.broadcasted_iota(jnp.int32, (LANES, GLA_BLOCK), 1)
        csel_ref[...] = jnp.where(sel_r == sel_c // GLA_CHUNK, 1.0, 0.0).astype(_BF16)
        st_row_head = lax.broadcasted_iota(jnp.int32, (QK_W, V_W), 0) // DK_PAD
        st_col_head = lax.broadcasted_iota(jnp.int32, (QK_W, V_W), 1) // DV_PAD
        hdiag_ref[...] = jnp.where(st_row_head == st_col_head, 1.0, 0.0)

    x = x_ref[0]
    if fused_moe_gather:
        step = pl.program_id(0) * pl.num_programs(1) + s_idx
        n_steps = pl.num_programs(0) * pl.num_programs(1)
        slot = step % GATHER_SLOTS

        def row_copy(pos_ref, j, to_slot):
            return pltpu.make_async_copy(y_hbm.at[pl.ds(pos_ref[j], 1)], ybuf.at[to_slot, pl.ds(j, 1)],
                                         gsem.at[to_slot])

        def wait_rows(of_slot):
            pltpu.make_async_copy(y_hbm.at[pl.ds(0, ts)], ybuf.at[of_slot], gsem.at[of_slot]).wait()

        @pl.when(step == 0)
        def _():
            def issue(j, carry):
                row_copy(pos_cur, j, 0).start()
                row_copy(pos_nxt, j, 1).start()
                return carry
            lax.fori_loop(0, ts, issue, 0, unroll=DMA_UNROLL)

        wait_rows(slot)
        x = x + modp_ref[0, 5:6, :] * ybuf[slot]

        ahead_slot = (step + GATHER_SLOTS - 1) % GATHER_SLOTS
        for j in range(ts):
            row_copy(pos_ahd, j, ahead_slot).start()

    sh1 = mod_ref[0, 0:1, :]
    gt1 = mod_ref[0, 2:3, :]
    scale = gmix_ref[...] * (1.0 + mod_ref[0, 1:2, :])
    ms = jnp.mean(x * x, axis=-1, keepdims=True)
    hb = (x * lax.rsqrt(ms + EPS) * scale + sh1).astype(_BF16)

    def proj(off, width):
        return _dot(hb, win_ref[:, off:off + width])

    out = {}
    _run_alternately(
        _conv_lru_steps(out, proj, cw_ref, cb_ref, lng_ref, lnb_ref, lcw_ref, lcb_ref, wai_ref, bai_ref,
                        lam_ref, ubuf, lbuf, hcar, ts),
        _gla_steps(out, proj, wgate_ref, bgate_ref, gnorm_ref, state, tri_ref, tria_ref, csel_ref,
                   causal_ref, hdiag_ref, ts))

    mixed = (_dot(out["o_g"].astype(_BF16), wout_ref[CONV_W + LRU_W:MIX_W, :])
             + _dot(out["r_out"].astype(_BF16), wout_ref[CONV_W:CONV_W + LRU_W, :])
             + _dot(out["u_out"].astype(_BF16), wout_ref[0:CONV_W, :]))
    o_ref[0] = x + gt1 * mixed

    if fused_moe_gather:
        @pl.when(step == n_steps - 1)
        def _():
            wait_rows((step + 1) % GATHER_SLOTS)
            wait_rows((step + 2) % GATHER_SLOTS)


def _run_alternately(*step_generators):
    clock = [0] * len(step_generators)
    live = list(range(len(step_generators)))
    while live:
        i = min(live, key=lambda j: clock[j])
        try:
            clock[i] += next(step_generators[i]) * (1 if i else 0.5)
        except StopIteration:
            live.remove(i)


def _conv_lru_steps(out, proj, cw_ref, cb_ref, lng_ref, lnb_ref, lcw_ref, lcb_ref, wai_ref, bai_ref,
                    lam_ref, ubuf, lbuf, hcar, ts):
    u = proj(OFF_CVV, CONV_W) * _sigmoid(proj(OFF_CVG, CONV_W))
    ubuf[CONV_HIST:CONV_HIST + ts, :] = u
    yield 500
    lbuf[LRU_HIST:LRU_HIST + ts, :] = proj(OFF_LRX, LRU_W)
    xb = jnp.broadcast_to(lcb_ref[...], (ts, LRU_W))
    for k in range(LRU_CONV_K):
        off = LRU_HIST - (LRU_CONV_K - 1) + k
        xb = xb + lcw_ref[k:k + 1, :] * lbuf[off:off + ts, :]
    lbuf[0:LRU_HIST, :] = lbuf[ts:ts + LRU_HIST, :]
    gates = _dot(xb.astype(_BF16), wai_ref[...]) + bai_ref[...]
    yield 800

    acc = jnp.broadcast_to(cb_ref[...], (ts, CONV_W))
    for r in range(SUBLANES):
        z = None
        for a_blk in range((CONV_K - 1 - r) // SUBLANES + 1):
            k = CONV_K - 1 - (SUBLANES * a_blk + r)
            off = CONV_HIST - SUBLANES * (a_blk + 1)
            term = cw_ref[k:k + 1, :] * ubuf[off:off + ts + SUBLANES, :]
            z = term if z is None else z + term
        acc = acc + z[SUBLANES - r:SUBLANES - r + ts]
        yield 200
    ubuf[0:CONV_HIST, :] = ubuf[ts:ts + CONV_HIST, :]
    mu = jnp.mean(acc, axis=-1, keepdims=True)
    cen = acc - mu
    var = jnp.mean(cen * cen, axis=-1, keepdims=True)
    out["u_out"] = _silu(cen * lax.rsqrt(var + EPS) * lng_ref[...] + lnb_ref[...])
    yield 250

    r_gate = _sigmoid(gates[:, 0:LRU_W])
    i_gate = _sigmoid(gates[:, LRU_W:2 * LRU_W])
    lam = lam_ref[...]
    softplus_neg_lam = jnp.maximum(-lam, 0.0) + jnp.log1p(jnp.exp(-jnp.abs(lam)))
    log_a = (-LRU_C) * r_gate * softplus_neg_lam
    a = jnp.exp(log_a)
    mult = jnp.sqrt(jnp.tanh(-log_a) * (a * a + 1.0))
    u_in = mult * (i_gate * xb)
    yield 600
    row = lax.broadcasted_iota(jnp.int32, a.shape, 0) % SUBLANES
    d = 1
    while d < SUBLANES:
        keep = row >= d
        a_sh = jnp.where(keep, pltpu.roll(a, d, axis=0), 1.0)
        u_sh = jnp.where(keep, pltpu.roll(u_in, d, axis=0), 0.0)
        u_in = u_in + a * u_sh
        a = a * a_sh
        d *= 2
        yield 200
    groups = []
    h_prev = hcar[0:1, :]
    for g in range(ts // SUBLANES):
        r0 = g * SUBLANES
        h_g = u_in[r0:r0 + SUBLANES] + a[r0:r0 + SUBLANES] * h_prev
        groups.append(h_g)
        h_prev = h_g[SUBLANES - 1:SUBLANES]
        if g % SUBLANES == SUBLANES - 1:
            yield 50
    h_lru = jnp.concatenate(groups, axis=0)
    hcar[...] = jnp.broadcast_to(h_prev, hcar.shape)
    out["r_out"] = h_lru * jax.nn.gelu(proj(OFF_LRY, LRU_W), approximate=True)


def _gla_steps(out, proj, wgate_ref, bgate_ref, gnorm_ref, state, tri_ref, tria_ref, csel_ref, causal_ref,
               hdiag_ref, ts):
    zg = proj(OFF_GLR, LANES)
    glog = _dot(zg.astype(_BF16), wgate_ref[...]) + bgate_ref[...]
    lg = (jnp.minimum(glog, 0.0) - jnp.log1p(jnp.exp(-jnp.abs(glog)))) * (1.0 / GLA_TAU)
    yield 300
    zq = proj(OFF_Q, QK_W) * (GLA_DK ** -0.5)
    zk = proj(OFF_K, QK_W)
    yield 500
    zv = proj(OFF_V, V_W)
    yield 500

    nblk = ts // GLA_BLOCK
    cpb = GLA_BLOCK // GLA_CHUNK
    tri = tri_ref[...]
    tri_after = tria_ref[...]
    chunk_sel = csel_ref[...]
    causal = causal_ref[...] > 0.5
    qk_lane_head = lax.broadcasted_iota(jnp.int32, (1, QK_W), 1) // DK_PAD
    t_lane_chunk = lax.broadcasted_iota(jnp.int32, (1, GLA_BLOCK), 1) // GLA_CHUNK

    o_blocks = []
    for blk in range(nblk):
        r0 = blk * GLA_BLOCK
        lg_b = lg[r0:r0 + GLA_BLOCK]
        p_hi, p_lo = _split2(lg_b)
        b = _dot(tri, p_hi) + _dot(tri, p_lo)
        b_rest = _dot(tri_after, p_hi) + _dot(tri_after, p_lo)
        b_tot = _dot(chunk_sel, p_hi) + _dot(chunk_sel, p_lo)
        yield 500
        q_in =zq[r0:r0 + GLA_BLOCK] * jnp.exp(b)
        k_blk = zk[r0:r0 + GLA_BLOCK]
        k_in = (k_blk * jnp.exp(-b)).astype(_BF16)
        k_out_t = (k_blk * jnp.exp(b_rest)).T.astype(_BF16)
        v_b = zv[r0:r0 + GLA_BLOCK].astype(_BF16)
        q_in_b = q_in.astype(_BF16)
        decay_cols = jnp.exp(b_tot).T

        o_heads = []
        for hd in range(GLA_HEADS):
            q_h = jnp.where(qk_lane_head == hd, q_in_b, jnp.zeros_like(q_in_b))
            sc = jnp.where(causal, _dot_nt(q_h, k_in), 0.0)
            o_heads.append(_dot(sc.astype(_BF16), v_b[:, hd * DV_PAD:(hd + 1) * DV_PAD]))
            yield 300
        o_intra = jnp.concatenate(o_heads, axis=1)

        o_inter = []
        st = state[...]
        for c in range(cpb):
            c0 = c * GLA_CHUNK
            o_inter.append(_dot(q_in_b[c0:c0 + GLA_CHUNK], st.astype(_BF16)))
            kv = _dot(jnp.where(t_lane_chunk == c, k_out_t, jnp.zeros_like(k_out_t)), v_b)
            st = st * decay_cols[:, c:c + 1] + kv * hdiag_ref[...]
            yield 400
        state[...] = st
        o_blocks.append(o_intra + jnp.concatenate(o_inter, axis=0))
    o = o_blocks[0] if nblk == 1 else jnp.concatenate(o_blocks, axis=0)

    og = proj(OFF_OG, V_W)
    o_parts = []
    for hd in range(GLA_HEADS):
        o_h = o[:, hd * DV_PAD:(hd + 1) * DV_PAD]
        ms_h = jnp.sum(o_h * o_h, axis=-1, keepdims=True) * (1.0 / GLA_DV)
        o_parts.append(o_h * lax.rsqrt(ms_h + EPS))
    o_n = jnp.concatenate(o_parts, axis=1) * gnorm_ref[...]
    out["o_g"] = o_n * _silu(og)


def _token_mix(x, mod, params, layer, moe=None):
    bsz, seq, d = x.shape
    ts = MIX_TS
    n_s = seq // ts
    full = lambda shape: pl.BlockSpec((None,) + shape, lambda b, s: (layer,) + (0,) * len(shape))
    in_specs = [
        pl.BlockSpec((1, ts, d), lambda b, s: (b, s, 0)),
        pl.BlockSpec((1, 6, d), lambda b, s: (b, 0, 0)),
        full((1, d)),
        full((d, N_IN)),
        full((CONV_HIST, CONV_W)), full((1, CONV_W)), full((1, CONV_W)), full((1, CONV_W)),
        full((LRU_CONV_K, LRU_W)), full((1, LRU_W)),
        full((LRU_W, 2 * LRU_W)), full((1, 2 * LRU_W)), full((1, LRU_W)),
        full((LANES, QK_W)), full((1, QK_W)), full((1, V_W)),
        full((MIX_W, d)),
    ]
    scratch_shapes = [
        pltpu.VMEM((CONV_HIST + ts, CONV_W), _F32),
        pltpu.VMEM((LRU_HIST + ts, LRU_W), _F32),
        pltpu.VMEM((SUBLANES, LRU_W), _F32),
        pltpu.VMEM((QK_W, V_W), _F32),
        pltpu.VMEM((GLA_BLOCK, GLA_BLOCK), _BF16),
        pltpu.VMEM((GLA_BLOCK, GLA_BLOCK), _BF16),
        pltpu.VMEM((LANES, GLA_BLOCK), _BF16),
        pltpu.VMEM((GLA_BLOCK, GLA_BLOCK), _F32),
        pltpu.VMEM((QK_W, V_W), _F32),
    ]
    args = [x, mod] + list(params)
    if moe is not None:
        pos, mod_prev, y = moe
        last = bsz * n_s - 1
        in_specs += [
            pl.BlockSpec((ts,), lambda b, s: (b * n_s + s,), memory_space=pltpu.SMEM),
            pl.BlockSpec((ts,), lambda b, s: (jnp.minimum(b * n_s + s + 1, last),), memory_space=pltpu.SMEM),
            pl.BlockSpec((ts,), lambda b, s: (jnp.minimum(b * n_s + s + 2, last),), memory_space=pltpu.SMEM),
            pl.BlockSpec((1, 6, d), lambda b, s: (b, 0, 0)),
            pl.BlockSpec(memory_space=pl.ANY),
        ]
        scratch_shapes += [pltpu.VMEM((GATHER_SLOTS, ts, d), _F32), pltpu.SemaphoreType.DMA((GATHER_SLOTS,))]
        args += [pos, pos, pos, mod_prev, y]
    return pl.pallas_call(
        functools.partial(_mix_kernel, fused_moe_gather=moe is not None),
        out_shape=jax.ShapeDtypeStruct(x.shape, _F32),
        grid=(bsz, n_s),
        in_specs=in_specs,
        out_specs=pl.BlockSpec((1, ts, d), lambda b, s: (b, s, 0)),
        scratch_shapes=scratch_shapes,
        compiler_params=pltpu.CompilerParams(
            dimension_semantics=("arbitrary", "arbitrary"), vmem_limit_bytes=VMEM_LIMIT_BYTES),
        name="token_mix",
    )(*args)


def _route_kernel(x_ref, mod_ref, g_ref, wr_ref, br_ref, xp_ref, meta_ref, cnt_ref, carry):
    tt = x_ref.shape[0]
    i = pl.program_id(0)

    @pl.when(i == 0)
    def _():
        carry[...] = jnp.zeros_like(carry)

    x = x_ref[...]
    sh2 = mod_ref[0, 3:4, :]
    sc2 = mod_ref[0, 4:5, :]
    ms = jnp.mean(x * x, axis=-1, keepdims=True)
    h = (x * lax.rsqrt(ms + EPS) * g_ref[...]) * (1.0 + sc2) + sh2

    h_hi, h_lo = _split2(h)
    w_hi, w_lo = _split2(wr_ref[...])
    w_cat = jnp.concatenate([w_hi, w_lo], axis=1)
    prod = _dot(h_hi, w_cat) + _dot(h_lo, w_cat)
    logits = prod[:, 0:LANES] + prod[:, LANES:2 * LANES] + br_ref[...]

    lane = lax.broadcasted_iota(jnp.int32, (tt, LANES), 1)
    lane_f = lane.astype(_F32)
    neg = -jnp.inf
    big = float(LANES)

    def first_argmax(vals, vmax):
        return jnp.min(jnp.where(vals == vmax, lane_f, big), axis=-1, keepdims=True).astype(jnp.int32)

    gl = jnp.where(lane < N_GROUPS, logits, neg)
    gmax = jnp.max(gl, axis=-1, keepdims=True)
    g_star = first_argmax(gl, gmax)
    p_sel = 1.0 / jnp.sum(jnp.exp(gl - gmax), axis=-1, keepdims=True)
    base = N_GROUPS + EXPERTS_PER_GROUP * g_star
    el = jnp.where((lane >= base) & (lane < base + EXPERTS_PER_GROUP), logits, neg)
    v0 = jnp.max(el, axis=-1, keepdims=True)
    i0 = first_argmax(el, v0)
    el2 = jnp.where(lane == i0, neg, el)
    v1 = jnp.max(el2, axis=-1, keepdims=True)
    i1 = first_argmax(el2, v1)
    ex = jnp.exp(v1 - v0)
    wt0 = p_sel / (1.0 + ex)
    wt1 = p_sel * ex / (1.0 + ex)
    e0 = i0 - base
    e1 = i1 - base
    e_lo = jnp.minimum(e0, e1)
    e_hi = jnp.maximum(e0, e1)
    w_lo = jnp.where(e0 < e1, wt0, wt1)
    w_hi = jnp.where(e0 < e1, wt1, wt0)
    pair = (e_lo * (2 * EXPERTS_PER_GROUP - 1 - e_lo)) // 2 + (e_hi - e_lo - 1)
    bucket = g_star * N_PAIRS + pair

    onehot = lane == bucket
    onehot_f = jnp.where(onehot, 1.0, 0.0)
    ri = lax.broadcasted_iota(jnp.int32, (tt, tt), 0)
    ci = lax.broadcasted_iota(jnp.int32, (tt, tt), 1)
    strict = jnp.where(ci < ri, 1.0, 0.0).astype(_BF16)
    prefix = _dot(strict, onehot_f.astype(_BF16)) + carry[0:1, :]
    rank = jnp.sum(jnp.where(onehot, prefix, 0.0), axis=-1, keepdims=True)
    carry[...] = carry[...] + jnp.sum(onehot_f, axis=0, keepdims=True)
    cnt_ref[...] = carry[...]

    meta = jnp.where(lane == 0, bucket.astype(_F32), jnp.where(lane == 1, rank, 0.0))
    meta_ref[...] = meta.T[0:SUBLANES, :]

    xp_ref[:, 0:D_MODEL] = h
    xp_ref[:, D_MODEL:ROW_W] = jnp.where(lane == 0, w_lo, jnp.where(lane == 1, w_hi, 0.0))


def _route(x2d, mod, g_ffn, w_r, b_r, layer, seq):
    n_tok, d = x2d.shape
    tt = ROUTE_TT
    tiles_per_seq = seq // tt
    per_layer = lambda shape: pl.BlockSpec((None,) + shape, lambda i: (layer, 0, 0))
    return pl.pallas_call(
        _route_kernel,
        out_shape=(
            jax.ShapeDtypeStruct((n_tok, ROW_W), _F32),
            jax.ShapeDtypeStruct((SUBLANES, n_tok), _F32),
            jax.ShapeDtypeStruct((SUBLANES, LANES), _F32),
        ),
        grid=(n_tok // tt,),
        in_specs=[
            pl.BlockSpec((tt, d), lambda i: (i, 0)),
            pl.BlockSpec((1, 6, d), lambda i: (i // tiles_per_seq, 0, 0)),
            per_layer((1, d)),
            per_layer((d, LANES)),
            per_layer((1, LANES)),
        ],
        out_specs=(
            pl.BlockSpec((tt, ROW_W), lambda i: (i, 0)),
            pl.BlockSpec((SUBLANES, tt), lambda i: (0, i)),
            pl.BlockSpec((SUBLANES, LANES), lambda i: (0, 0)),
        ),
        scratch_shapes=[pltpu.VMEM((SUBLANES, LANES), _F32)],
        compiler_params=pltpu.CompilerParams(
            dimension_semantics=("arbitrary",), vmem_limit_bytes=VMEM_LIMIT_BYTES),
        name="moe_route",
    )(x2d, mod, g_ffn, w_r, b_r)


def _permute_kernel(fill_ref, pos_ref, xp_ref, xs_hbm, zbuf, sem, zsem):
    tb = pos_ref.shape[0]

    @pl.when(pl.program_id(0) == 0)
    def _():
        zbuf[...] = jnp.zeros_like(zbuf)
        n_used = fill_ref[2 * N_BUCKETS]
        n_tiles = xs_hbm.shape[0] // FFN_TM
        for wait in (False, True):
            for b in range(N_BUCKETS):
                for row0, cond in ((fill_ref[b], fill_ref[N_BUCKETS + b] > 0),
                                   ((n_used + b) * FFN_TM, n_used + b < n_tiles)):
                    @pl.when(cond)
                    def _():
                        dst = xs_hbm.at[pl.ds(pl.multiple_of(row0, FFN_TM), FFN_TM)]
                        fill = pltpu.make_async_copy(zbuf, dst, zsem)
                        fill.wait() if wait else fill.start()

    for j in range(tb):
        pltpu.make_async_copy(xp_ref.at[pl.ds(j, 1)], xs_hbm.at[pl.ds(pos_ref[j], 1)], sem).start(priority=j % 2)
    pltpu.make_async_copy(xp_ref, xs_hbm.at[pl.ds(0, tb)], sem).wait()


def _permute(fill_starts, pos, xp, n_rows):
    n_tok = xp.shape[0]
    tb = PERM_TB
    return pl.pallas_call(
        _permute_kernel,
        out_shape=jax.ShapeDtypeStruct((n_rows, ROW_W), _F32),
        grid_spec=pltpu.PrefetchScalarGridSpec(
            num_scalar_prefetch=1,
            grid=(n_tok // tb,),
            in_specs=[
                pl.BlockSpec((tb,), lambda i, fs: (i,), memory_space=pltpu.SMEM),
                pl.BlockSpec((tb, ROW_W), lambda i, fs: (i, 0)),
            ],
            out_specs=pl.BlockSpec(memory_space=pl.ANY),
            scratch_shapes=[pltpu.VMEM((FFN_TM, ROW_W), _F32), pltpu.SemaphoreType.DMA,
                            pltpu.SemaphoreType.DMA],
        ),
        compiler_params=pltpu.CompilerParams(
            dimension_semantics=("arbitrary",), vmem_limit_bytes=VMEM_LIMIT_BYTES),
        name="moe_permute",
    )(fill_starts, pos, xp)


def _ffn_kernel(tg_ref, tlo_ref, thi_ref, clo_ref, chi_ref, ng_ref, nlo_ref, nhi_ref, plo_ref, phi_ref, nused_ref,
                xs_ref, wg_hbm, wu_hbm, wd_hbm, y_ref, sg, su, sd, agu, ad, sem, *, layer):
    i = pl.program_id(0)

    def weight_copies(slot, g, e):
        return (pltpu.make_async_copy(wg_hbm.at[layer, g, e], sg.at[slot], sem.at[slot, 0]),
                pltpu.make_async_copy(wu_hbm.at[layer, g, e], su.at[slot], sem.at[slot, 1]),
                pltpu.make_async_copy(wd_hbm.at[layer, g, e], sd.at[slot], sem.at[slot, 2]))

    @pl.when(i == 0)
    def _():
        for cp in weight_copies(0, tg_ref[0], tlo_ref[0]) + weight_copies(1, tg_ref[0], thi_ref[0]):
            cp.start()

    for slot, changed_ref, e_ref in ((0, clo_ref, tlo_ref), (1, chi_ref, thi_ref)):
        @pl.when(changed_ref[i] == 1)
        def _():
            for cp in weight_copies(slot, tg_ref[i], e_ref[i]):
                cp.wait()
            agu[slot, :, 0:D_EXPERT] = sg[slot].astype(_BF16)
            agu[slot, :, D_EXPERT:2 * D_EXPERT] = su[slot].astype(_BF16)
            ad[slot] = sd[slot].astype(_BF16)

    for slot, prefetch_ref, e_ref in ((0, plo_ref, nlo_ref), (1, phi_ref, nhi_ref)):
        @pl.when(prefetch_ref[i] == 1)
        def _():
            for cp in weight_copies(slot, ng_ref[i], e_ref[i]):
                cp.start()

    @pl.when(i < nused_ref[0])
    def _():
        xb = xs_ref[:, 0:D_MODEL].astype(_BF16)
        info = xs_ref[:, D_MODEL:ROW_W]

        def expert(slot):
            gate_up = _dot(xb, agu[slot])
            hid = _silu(gate_up[:, 0:D_EXPERT]) * gate_up[:, D_EXPERT:2 * D_EXPERT] * info[:, slot:slot + 1]
            return _dot(hid.astype(_BF16), ad[slot])

        y_ref[...] = expert(0) + expert(1)

    @pl.when(i >= nused_ref[0])
    def _():
        y_ref[...] = jnp.zeros_like(y_ref)


def _expert_ffn(layer, plan, xs, w_gate, w_up, w_down):
    n_rows = xs.shape[0]
    tm = FFN_TM
    n_tiles = n_rows // tm

    def row_map(i, *prefetch):
        n_used = prefetch[-1]
        return (jnp.minimum(i, n_used[0] - 1), 0)

    hbm = pl.BlockSpec(memory_space=pl.ANY)
    return pl.pallas_call(
        functools.partial(_ffn_kernel, layer=layer),
        out_shape=jax.ShapeDtypeStruct((n_rows, D_MODEL), _F32),
        grid_spec=pltpu.PrefetchScalarGridSpec(
            num_scalar_prefetch=len(plan),
            grid=(n_tiles,),
            in_specs=[pl.BlockSpec((tm, ROW_W), row_map), hbm, hbm, hbm],
            out_specs=pl.BlockSpec((tm, D_MODEL), lambda i, *prefetch: (i, 0)),
            scratch_shapes=[
                pltpu.VMEM((2, D_MODEL, D_EXPERT), _F32), pltpu.VMEM((2, D_MODEL, D_EXPERT), _F32),
                pltpu.VMEM((2, D_EXPERT, D_MODEL), _F32),
                pltpu.VMEM((2, D_MODEL, 2 * D_EXPERT), _BF16), pltpu.VMEM((2, D_EXPERT, D_MODEL), _BF16),
                pltpu.SemaphoreType.DMA((2, 3)),
            ],
        ),
        compiler_params=pltpu.CompilerParams(
            dimension_semantics=("arbitrary",), vmem_limit_bytes=VMEM_LIMIT_BYTES),
        name="moe_ffn",
    )(*plan, xs, w_gate, w_up, w_down)


def _combine_kernel(pos_ref, x_ref, mod_ref, gfin_ref, y_hbm, o_ref, ybuf, sem):
    tc = x_ref.shape[0]
    for j in range(tc):
        pltpu.make_async_copy(y_hbm.at[pl.ds(pos_ref[j], 1)], ybuf.at[pl.ds(j, 1)], sem).start(priority=j % 2)
    pltpu.make_async_copy(y_hbm.at[pl.ds(0, tc)], ybuf, sem).wait()

    gt2 = mod_ref[0, 5:6, :]
    out = x_ref[...] + gt2 * ybuf[...]
    ms = jnp.mean(out * out, axis=-1, keepdims=True)
    o_ref[...] = out * lax.rsqrt(ms + EPS) * gfin_ref[...]


def _combine(pos, x2d, mod, g_final, y, seq):
    n_tok, d = x2d.shape
    tc = COMB_TC
    tiles_per_seq = seq // tc
    return pl.pallas_call(
        _combine_kernel,
        out_shape=jax.ShapeDtypeStruct((n_tok, d), _F32),
        grid=(n_tok // tc,),
        in_specs=[
            pl.BlockSpec((tc,), lambda i: (i,), memory_space=pltpu.SMEM),
            pl.BlockSpec((tc, d), lambda i: (i, 0)),
            pl.BlockSpec((1, 6, d), lambda i: (i // tiles_per_seq, 0, 0)),
            pl.BlockSpec((1, d), lambda i: (0, 0)),
            pl.BlockSpec(memory_space=pl.ANY),
        ],
        out_specs=pl.BlockSpec((tc, d), lambda i: (i, 0)),
        scratch_shapes=[pltpu.VMEM((tc, d), _F32), pltpu.SemaphoreType.DMA],
        compiler_params=pltpu.CompilerParams(
            dimension_semantics=("arbitrary",), vmem_limit_bytes=VMEM_LIMIT_BYTES),
        name="moe_combine",
    )(pos, x2d, mod, g_final, y)


def _pad_heads(w, heads, width, padded):
    lead = w.shape[:-1]
    w = w.reshape(lead + (heads, width))
    w = jnp.pad(w, [(0, 0)] * len(lead) + [(0, 0), (0, padded - width)])
    return w.reshape(lead + (heads * padded,))


def _block_diag(w):
    n_layers, n, bw, _ = w.shape
    eye = jnp.eye(n, dtype=w.dtype)
    return (eye[None, :, None, :, None] * w[:, :, :, None, :]).reshape(n_layers, n * bw, n * bw)


def _prep_params(w_in, conv_dw_w, conv_dw_b, conv_ln_g, conv_ln_b, lru_conv_w, lru_conv_b, lru_w_a,
                 lru_b_a, lru_w_i, lru_b_i, lru_lam, gla_w_gate, gla_b_gate, gla_norm_g, w_out, g_mix):
    sizes = [CONV_W, CONV_W, LRU_W, LRU_W, GLA_HEADS * GLA_DK, GLA_HEADS * GLA_DK, GLA_V, GLA_RANK, GLA_V]
    cv_v, cv_g, lr_x, lr_y, q, k, v, g_lr, og = jnp.split(w_in, np.cumsum(sizes)[:-1].tolist(), axis=-1)
    w_in_p = jnp.concatenate([
        cv_v, cv_g, lr_x, lr_y,
        _pad_heads(q, GLA_HEADS, GLA_DK, DK_PAD), _pad_heads(k, GLA_HEADS, GLA_DK, DK_PAD),
        _pad_heads(v, GLA_HEADS, GLA_DV, DV_PAD),
        jnp.pad(g_lr, ((0, 0), (0, 0), (0, LANES - GLA_RANK))),
        _pad_heads(og, GLA_HEADS, GLA_DV, DV_PAD)], axis=-1).astype(_BF16)
    n_layers = w_in.shape[0]
    wo_o = w_out[:, CONV_W + LRU_W:].reshape(n_layers, GLA_HEADS, GLA_DV, D_MODEL)
    wo_o = jnp.pad(wo_o, ((0, 0), (0, 0), (0, DV_PAD - GLA_DV), (0, 0))).reshape(n_layers, V_W, D_MODEL)
    w_out_p = jnp.concatenate([w_out[:, :CONV_W + LRU_W], wo_o], axis=1).astype(_BF16)
    w_gate_p = jnp.pad(_pad_heads(gla_w_gate, GLA_HEADS, GLA_DK, DK_PAD),
                       ((0, 0), (0, LANES - GLA_RANK), (0, 0))).astype(_BF16)
    row = lambda v: v[:, None, :]
    return [
        row(g_mix),
        w_in_p,
        jnp.pad(conv_dw_w, ((0, 0), (0, CONV_HIST - CONV_K), (0, 0))),
        row(conv_dw_b), row(conv_ln_g), row(conv_ln_b),
        lru_conv_w, row(lru_conv_b),
        jnp.concatenate([_block_diag(lru_w_a), _block_diag(lru_w_i)], axis=2).astype(_BF16),
        row(jnp.concatenate([lru_b_a, lru_b_i], axis=1)),
        row(lru_lam),
        w_gate_p,
        row(_pad_heads(gla_b_gate, GLA_HEADS, GLA_DK, DK_PAD)),
        row(_pad_heads(gla_norm_g, GLA_HEADS, GLA_DV, DV_PAD)),
        w_out_p,
    ]


def _bucket_layout(counts, n_tiles):
    tm = FFN_TM
    counts = counts.astype(jnp.int32)
    tiles = (counts + tm - 1) // tm
    upto = jnp.arange(N_BUCKETS)[:, None] <= jnp.arange(N_BUCKETS)[None, :]
    tile_end = jnp.sum(jnp.where(upto, tiles[:, None], 0), axis=0)
    starts = ((tile_end - tiles) * tm).astype(jnp.int32)
    n_used = tile_end[-1]
    tile_idx = jnp.minimum(jnp.arange(n_tiles, dtype=jnp.int32), n_used - 1)

    def bucket_of(t):
        return jnp.sum((t[:, None] >= tile_end[None, :]).astype(jnp.int32), axis=1)

    def experts(b):
        pair = b % N_PAIRS
        lo = (pair >= 3).astype(jnp.int32) + (pair >= 5).astype(jnp.int32)
        hi = jnp.where(pair < 3, pair + 1, jnp.where(pair < 5, pair - 1, 3))
        return b // N_PAIRS, lo, hi

    tile_bucket = bucket_of(tile_idx)
    prev_bucket = jnp.where(tile_idx >= 1, bucket_of(tile_idx - 1), -1)
    g, lo, hi = experts(tile_bucket)
    pg, plo, phi = experts(prev_bucket)
    first = (tile_bucket != prev_bucket) & (jnp.arange(n_tiles) < n_used)
    fresh = prev_bucket < 0
    changed_lo = first & (fresh | (g != pg) | (lo != plo))
    changed_hi = first & (fresh | (g != pg) | (hi != phi))
    next_first_tile = jnp.min(jnp.where(tile_end[None, :] > tile_idx[:, None], tile_end[None, :], n_tiles), axis=1)
    has_next = next_first_tile < n_used
    ng, nlo, nhi = experts(bucket_of(jnp.minimum(next_first_tile, n_used - 1)))
    prefetch_lo = first & has_next & ((ng != g) | (nlo != lo))
    prefetch_hi = first & has_next & ((ng != g) | (nhi != hi))
    as_i32 = lambda v: v.astype(jnp.int32)
    ffn_plan = (g, lo, hi, as_i32(changed_lo), as_i32(changed_hi), ng, nlo, nhi,
                as_i32(prefetch_lo), as_i32(prefetch_hi), n_used.reshape(1).astype(jnp.int32))
    fill = jnp.concatenate([(jnp.maximum(tile_end - 1, 0) * tm).astype(jnp.int32), tiles, n_used[None]])
    return starts, fill, ffn_plan


def kernel(x, c, w_ada, b_ada, g_mix, w_in, conv_dw_w, conv_dw_b, conv_ln_g, conv_ln_b, lru_conv_w,
           lru_conv_b, lru_w_a, lru_b_a, lru_w_i, lru_b_i, lru_lam, gla_w_gate, gla_b_gate, gla_norm_g,
           w_out, g_ffn, w_route_group, b_route_group, w_route_expert, b_route_expert, w_gate, w_up,
           w_down, g_final):
    bsz, seq, d = x.shape
    n_layers = w_ada.shape[0]
    n_tok = bsz * seq
    assert d == D_MODEL and seq % max(MIX_TS, ROUTE_TT, COMB_TC) == 0 and MIX_TS % GLA_BLOCK == 0
    assert n_tok % PERM_TB == 0
    n_tiles = n_tok // FFN_TM + N_BUCKETS
    n_rows = n_tiles * FFN_TM

    mod_all = _modulation(c, w_ada, b_ada).reshape(n_layers, bsz, 6, d)

    params = _prep_params(w_in, conv_dw_w, conv_dw_b, conv_ln_g, conv_ln_b, lru_conv_w, lru_conv_b,
                          lru_w_a, lru_b_a, lru_w_i, lru_b_i, lru_lam, gla_w_gate, gla_b_gate, gla_norm_g,
                          w_out, g_mix)
    n_experts = N_GROUPS * EXPERTS_PER_GROUP
    w_r = jnp.concatenate(
        [w_route_group, w_route_expert.transpose(0, 2, 1, 3).reshape(n_layers, d, n_experts)], axis=2)
    w_r = jnp.pad(w_r, ((0, 0), (0, 0), (0, LANES - N_GROUPS - n_experts)))
    b_r = jnp.concatenate([b_route_group, b_route_expert.reshape(n_layers, n_experts)], axis=1)
    b_r = jnp.pad(b_r, ((0, 0), (0, LANES - N_GROUPS - n_experts)))[:, None, :]
    pending_moe = None
    for l in range(n_layers):
        mod = mod_all[l]
        x = _token_mix(x, mod, params, l, pending_moe)

        x2d = x.reshape(n_tok, d)
        xp, meta, counts = _route(x2d, mod, g_ffn[:, None, :], w_r, b_r, l, seq)
        bucket = meta[0].astype(jnp.int32)
        rank = meta[1].astype(jnp.int32)
        starts, fill, ffn_plan = _bucket_layout(counts[0, :N_BUCKETS], n_tiles)
        in_bucket = bucket[:, None] == jnp.arange(N_BUCKETS, dtype=jnp.int32)[None, :]
        pos = rank + jnp.sum(jnp.where(in_bucket, starts[None, :], 0), axis=1)

        xs = _permute(fill, pos, xp, n_rows)
        y = _expert_ffn(l, ffn_plan, xs, w_gate, w_up, w_down)
        pending_moe = (pos, mod, y)
    return _combine(pos, x2d, mod, g_final[None, :], y, seq).reshape(bsz, seq, d)
```

```python
import functools

import jax
import jax.numpy as jnp
import numpy as np
from jax import lax
from jax.experimental import pallas as pl
from jax.experimental.pallas import tpu as pltpu

D_MODEL = 1024
CONV_W = 256
LRU_W = 384
GLA_V = 384
CONV_K = 31
LRU_CONV_K = 4
LRU_BLOCKS = 6
LRU_BW = 64
LRU_C = 8.0
GLA_HEADS = 4
GLA_DV = 96
GLA_DK = 48
GLA_RANK = 16
GLA_TAU = 16.0
GLA_CHUNK = 64
N_GROUPS = 4
EXPERTS_PER_GROUP = 4
D_EXPERT = 512
EPS = 1e-6

LANES = 128
SUBLANES = 8
VMEM_LIMIT_BYTES = 56 * 1024 * 1024

DK_PAD = 64
DV_PAD = 128
QK_W = GLA_HEADS * DK_PAD
V_W = GLA_HEADS * DV_PAD
OFF_CVV = 0
OFF_CVG = OFF_CVV + CONV_W
OFF_LRX = OFF_CVG + CONV_W
OFF_LRY = OFF_LRX + LRU_W
OFF_Q = OFF_LRY + LRU_W
OFF_K = OFF_Q + QK_W
OFF_V = OFF_K + QK_W
OFF_GLR = OFF_V + V_W
OFF_OG = OFF_GLR + LANES
N_IN = OFF_OG + V_W
MIX_W = CONV_W + LRU_W + V_W

CONV_HIST = 32
LRU_HIST = 8
GLA_BLOCK = 256

N_PAIRS = 6
N_BUCKETS = N_GROUPS * N_PAIRS
ROW_W = D_MODEL + LANES

MIX_TS = 512
ROUTE_TT = 512
FFN_TM = 256
PERM_TB = 1024
COMB_TC = 1024
DMA_UNROLL = 8
GATHER_SLOTS = 3

_F32 = jnp.float32
_BF16 = jnp.bfloat16


def _sigmoid(x):
    return 1.0 / (1.0 + jnp.exp(-x))


def _silu(x):
    return x * _sigmoid(x)


def _dot(a, b):
    return jnp.dot(a, b, preferred_element_type=_F32)


def _dot_nt(a, b):
    return lax.dot_general(a, b, (((1,), (1,)), ((), ())), preferred_element_type=_F32)


def _split2(x):
    hi = x.astype(_BF16)
    return hi, (x - hi.astype(_F32)).astype(_BF16)


def _mod_kernel(c_ref, w_ref, b_ref, o_ref):
    c_act = _silu(c_ref[...])
    o_ref[0] = _dot(c_act.astype(_BF16), w_ref[0].astype(_BF16)) + b_ref[0]


def _modulation(c, w_ada, b_ada):
    n_layers, d, n = w_ada.shape
    bsz = c.shape[0]
    tn = 1536
    return pl.pallas_call(
        _mod_kernel,
        out_shape=jax.ShapeDtypeStruct((n_layers, bsz, n), _F32),
        grid=(n_layers, n // tn),
        in_specs=[
            pl.BlockSpec((bsz, d), lambda l, j: (0, 0)),
            pl.BlockSpec((1, d, tn), lambda l, j: (l, 0, j)),
            pl.BlockSpec((1, 1, tn), lambda l, j: (l, 0, j)),
        ],
        out_specs=pl.BlockSpec((1, bsz, tn), lambda l, j: (l, 0, j)),
        compiler_params=pltpu.CompilerParams(
            dimension_semantics=("arbitrary", "arbitrary"), vmem_limit_bytes=VMEM_LIMIT_BYTES),
        name="adaln_mod",
    )(c, w_ada, b_ada.reshape(n_layers, 1, n))


def _mix_kernel(x_ref, mod_ref, gmix_ref, win_ref, cw_ref, cb_ref, lng_ref, lnb_ref,
                lcw_ref, lcb_ref, wai_ref, bai_ref, lam_ref, wgate_ref, bgate_ref, gnorm_ref,
                wout_ref, *rest, fused_moe_gather):
    if fused_moe_gather:
        pos_cur, pos_nxt, pos_ahd, modp_ref, y_hbm, o_ref = rest[:6]
        ybuf, gsem = rest[-2:]
        rest = rest[6:-2]
    else:
        o_ref, rest = rest[0], rest[1:]
    ubuf, lbuf, hcar, state, tri_ref, tria_ref, csel_ref, causal_ref, hdiag_ref = rest
    ts = x_ref.shape[1]
    s_idx = pl.program_id(1)

    @pl.when(s_idx == 0)
    def _():
        ubuf[0:CONV_HIST, :] = jnp.zeros((CONV_HIST, CONV_W), _F32)
        lbuf[0:LRU_HIST, :] = jnp.zeros((LRU_HIST, LRU_W), _F32)
        hcar[...] = jnp.zeros_like(hcar)
        state[...] = jnp.zeros_like(state)
        ri = lax.broadcasted_iota(jnp.int32, (GLA_BLOCK, GLA_BLOCK), 0)
        ci = lax.broadcasted_iota(jnp.int32, (GLA_BLOCK, GLA_BLOCK), 1)
        same_chunk = ri // GLA_CHUNK == ci // GLA_CHUNK
        causal_f = jnp.where(same_chunk & (ci <= ri), 1.0, 0.0)
        causal_ref[...] = causal_f
        tri_ref[...] = causal_f.astype(_BF16)
        tria_ref[...] = jnp.where(same_chunk & (ci > ri), 1.0, 0.0).astype(_BF16)
        sel_r = lax.broadcasted_iota(jnp.int32, (LANES, GLA_BLOCK), 0)
        sel_c = lax.broadcasted_iota(jnp.int32, (LANES, GLA_BLOCK), 1)
        csel_ref[...] = jnp.where(sel_r == sel_c // GLA_CHUNK, 1.0, 0.0).astype(_BF16)
        st_row_head = lax.broadcasted_iota(jnp.int32, (QK_W, V_W), 0) // DK_PAD
        st_col_head = lax.broadcasted_iota(jnp.int32, (QK_W, V_W), 1) // DV_PAD
        hdiag_ref[...] = jnp.where(st_row_head == st_col_head, 1.0, 0.0)

    x = x_ref[0]
    if fused_moe_gather:
        step = pl.program_id(0) * pl.num_programs(1) + s_idx
        n_steps = pl.num_programs(0) * pl.num_programs(1)
        slot = step % GATHER_SLOTS

        def row_copy(pos_ref, j, to_slot):
            return pltpu.make_async_copy(y_hbm.at[pl.ds(pos_ref[j], 1)], ybuf.at[to_slot, pl.ds(j, 1)],
                                         gsem.at[to_slot])

        def wait_rows(of_slot):
            pltpu.make_async_copy(y_hbm.at[pl.ds(0, ts)], ybuf.at[of_slot], gsem.at[of_slot]).wait()

        @pl.when(step == 0)
        def _():
            def issue(j, carry):
                row_copy(pos_cur, j, 0).start()
                row_copy(pos_nxt, j, 1).start()
                return carry
            lax.fori_loop(0, ts, issue, 0, unroll=DMA_UNROLL)

        wait_rows(slot)
        x = x + modp_ref[0, 5:6, :] * ybuf[slot]

        ahead_slot = (step + GATHER_SLOTS - 1) % GATHER_SLOTS
        for j in range(ts):
            row_copy(pos_ahd, j, ahead_slot).start()

    sh1 = mod_ref[0, 0:1, :]
    gt1 = mod_ref[0, 2:3, :]
    scale = gmix_ref[...] * (1.0 + mod_ref[0, 1:2, :])
    ms = jnp.mean(x * x, axis=-1, keepdims=True)
    hb = (x * lax.rsqrt(ms + EPS) * scale + sh1).astype(_BF16)

    def proj(off, width):
        return _dot(hb, win_ref[:, off:off + width])

    out = {}
    _run_alternately(
        _conv_lru_steps(out, proj, cw_ref, cb_ref, lng_ref, lnb_ref, lcw_ref, lcb_ref, wai_ref, bai_ref,
                        lam_ref, ubuf, lbuf, hcar, ts),
        _gla_steps(out, proj, wgate_ref, bgate_ref, gnorm_ref, state, tri_ref, tria_ref, csel_ref,
                   causal_ref, hdiag_ref, ts))

    mixed = (_dot(out["o_g"].astype(_BF16), wout_ref[CONV_W + LRU_W:MIX_W, :])
             + _dot(out["r_out"].astype(_BF16), wout_ref[CONV_W:CONV_W + LRU_W, :])
             + _dot(out["u_out"].astype(_BF16), wout_ref[0:CONV_W, :]))
    o_ref[0] = x + gt1 * mixed

    if fused_moe_gather:
        @pl.when(step == n_steps - 1)
        def _():
            wait_rows((step + 1) % GATHER_SLOTS)
            wait_rows((step + 2) % GATHER_SLOTS)


def _run_alternately(*step_generators):
    clock = [0] * len(step_generators)
    live = list(range(len(step_generators)))
    while live:
        i = min(live, key=lambda j: clock[j])
        try:
            clock[i] += next(step_generators[i]) * (1 if i else 0.5)
        except StopIteration:
            live.remove(i)


def _conv_lru_steps(out, proj, cw_ref, cb_ref, lng_ref, lnb_ref, lcw_ref, lcb_ref, wai_ref, bai_ref,
                    lam_ref, ubuf, lbuf, hcar, ts):
    u = proj(OFF_CVV, CONV_W) * _sigmoid(proj(OFF_CVG, CONV_W))
    ubuf[CONV_HIST:CONV_HIST + ts, :] = u
    yield 500
    lbuf[LRU_HIST:LRU_HIST + ts, :] = proj(OFF_LRX, LRU_W)
    xb = jnp.broadcast_to(lcb_ref[...], (ts, LRU_W))
    for k in range(LRU_CONV_K):
        off = LRU_HIST - (LRU_CONV_K - 1) + k
        xb = xb + lcw_ref[k:k + 1, :] * lbuf[off:off + ts, :]
    lbuf[0:LRU_HIST, :] = lbuf[ts:ts + LRU_HIST, :]
    gates = _dot(xb.astype(_BF16), wai_ref[...]) + bai_ref[...]
    yield 800

    acc = jnp.broadcast_to(cb_ref[...], (ts, CONV_W))
    for r in range(SUBLANES):
        z = None
        for a_blk in range((CONV_K - 1 - r) // SUBLANES + 1):
            k = CONV_K - 1 - (SUBLANES * a_blk + r)
            off = CONV_HIST - SUBLANES * (a_blk + 1)
            term = cw_ref[k:k + 1, :] * ubuf[off:off + ts + SUBLANES, :]
            z = term if z is None else z + term
        acc = acc + z[SUBLANES - r:SUBLANES - r + ts]
        yield 200
    ubuf[0:CONV_HIST, :] = ubuf[ts:ts + CONV_HIST, :]
    mu = jnp.mean(acc, axis=-1, keepdims=True)
    cen = acc - mu
    var = jnp.mean(cen * cen, axis=-1, keepdims=True)
    out["u_out"] = _silu(cen * lax.rsqrt(var + EPS) * lng_ref[...] + lnb_ref[...])
    yield 250

    r_gate = _sigmoid(gates[:, 0:LRU_W])
    i_gate = _sigmoid(gates[:, LRU_W:2 * LRU_W])
    lam = lam_ref[...]
    softplus_neg_lam = jnp.maximum(-lam, 0.0) + jnp.log1p(jnp.exp(-jnp.abs(lam)))
    log_a = (-LRU_C) * r_gate * softplus_neg_lam
    a = jnp.exp(log_a)
    mult = jnp.sqrt(jnp.tanh(-log_a) * (a * a + 1.0))
    u_in = mult * (i_gate * xb)
    yield 600
    row = lax.broadcasted_iota(jnp.int32, a.shape, 0) % SUBLANES
    d = 1
    while d < SUBLANES:
        keep = row >= d
        a_sh = jnp.where(keep, pltpu.roll(a, d, axis=0), 1.0)
        u_sh = jnp.where(keep, pltpu.roll(u_in, d, axis=0), 0.0)
        u_in = u_in + a * u_sh
        a = a * a_sh
        d *= 2
        yield 200
    groups = []
    h_prev = hcar[0:1, :]
    for g in range(ts // SUBLANES):
        r0 = g * SUBLANES
        h_g = u_in[r0:r0 + SUBLANES] + a[r0:r0 + SUBLANES] * h_prev
        groups.append(h_g)
        h_prev = h_g[SUBLANES - 1:SUBLANES]
        if g % SUBLANES == SUBLANES - 1:
            yield 50
    h_lru = jnp.concatenate(groups, axis=0)
    hcar[...] = jnp.broadcast_to(h_prev, hcar.shape)
    out["r_out"] = h_lru * jax.nn.gelu(proj(OFF_LRY, LRU_W), approximate=True)


def _gla_steps(out, proj, wgate_ref, bgate_ref, gnorm_ref, state, tri_ref, tria_ref, csel_ref, causal_ref,
               hdiag_ref, ts):
    zg = proj(OFF_GLR, LANES)
    glog = _dot(zg.astype(_BF16), wgate_ref[...]) + bgate_ref[...]
    lg = (jnp.minimum(glog, 0.0) - jnp.log1p(jnp.exp(-jnp.abs(glog)))) * (1.0 / GLA_TAU)
    yield 300
    zq = proj(OFF_Q, QK_W) * (GLA_DK ** -0.5)
    zk = proj(OFF_K, QK_W)
    yield 500
    zv = proj(OFF_V, V_W)
    yield 500

    nblk = ts // GLA_BLOCK
    cpb = GLA_BLOCK // GLA_CHUNK
    tri = tri_ref[...]
    tri_after = tria_ref[...]
    chunk_sel = csel_ref[...]
    causal = causal_ref[...] > 0.5
    qk_lane_head = lax.broadcasted_iota(jnp.int32, (1, QK_W), 1) // DK_PAD
    t_lane_chunk = lax.broadcasted_iota(jnp.int32, (1, GLA_BLOCK), 1) // GLA_CHUNK

    o_blocks = []
    for blk in range(nblk):
        r0 = blk * GLA_BLOCK
        lg_b = lg[r0:r0 + GLA_BLOCK]
        p_hi, p_lo = _split2(lg_b)
        b = _dot(tri, p_hi) + _dot(tri, p_lo)
        b_rest = _dot(tri_after, p_hi) + _dot(tri_after, p_lo)
        b_tot = _dot(chunk_sel, p_hi) + _dot(chunk_sel, p_lo)
        yield 500
        q_in =zq[r0:r0 + GLA_BLOCK] * jnp.exp(b)
        k_blk = zk[r0:r0 + GLA_BLOCK]
        k_in = (k_blk * jnp.exp(-b)).astype(_BF16)
        k_out_t = (k_blk * jnp.exp(b_rest)).T.astype(_BF16)
        v_b = zv[r0:r0 + GLA_BLOCK].astype(_BF16)
        q_in_b = q_in.astype(_BF16)
        decay_cols = jnp.exp(b_tot).T

        o_heads = []
        for hd in range(GLA_HEADS):
            q_h = jnp.where(qk_lane_head == hd, q_in_b, jnp.zeros_like(q_in_b))
            sc = jnp.where(causal, _dot_nt(q_h, k_in), 0.0)
            o_heads.append(_dot(sc.astype(_BF16), v_b[:, hd * DV_PAD:(hd + 1) * DV_PAD]))
            yield 300
        o_intra = jnp.concatenate(o_heads, axis=1)

        o_inter = []
        st = state[...]
        for c in range(cpb):
            c0 = c * GLA_CHUNK
            o_inter.append(_dot(q_in_b[c0:c0 + GLA_CHUNK], st.astype(_BF16)))
            kv = _dot(jnp.where(t_lane_chunk == c, k_out_t, jnp.zeros_like(k_out_t)), v_b)
            st = st * decay_cols[:, c:c + 1] + kv * hdiag_ref[...]
            yield 400
        state[...] = st
        o_blocks.append(o_intra + jnp.concatenate(o_inter, axis=0))
    o = o_blocks[0] if nblk == 1 else jnp.concatenate(o_blocks, axis=0)

    og = proj(OFF_OG, V_W)
    o_parts = []
    for hd in range(GLA_HEADS):
        o_h = o[:, hd * DV_PAD:(hd + 1) * DV_PAD]
        ms_h = jnp.sum(o_h * o_h, axis=-1, keepdims=True) * (1.0 / GLA_DV)
        o_parts.append(o_h * lax.rsqrt(ms_h + EPS))
    o_n = jnp.concatenate(o_parts, axis=1) * gnorm_ref[...]
    out["o_g"] = o_n * _silu(og)


def _token_mix(x, mod, params, layer, moe=None):
    bsz, seq, d = x.shape
    ts = MIX_TS
    n_s = seq // ts
    full = lambda shape: pl.BlockSpec((None,) + shape, lambda b, s: (layer,) + (0,) * len(shape))
    in_specs = [
        pl.BlockSpec((1, ts, d), lambda b, s: (b, s, 0)),
        pl.BlockSpec((1, 6, d), lambda b, s: (b, 0, 0)),
        full((1, d)),
        full((d, N_IN)),
        full((CONV_HIST, CONV_W)), full((1, CONV_W)), full((1, CONV_W)), full((1, CONV_W)),
        full((LRU_CONV_K, LRU_W)), full((1, LRU_W)),
        full((LRU_W, 2 * LRU_W)), full((1, 2 * LRU_W)), full((1, LRU_W)),
        full((LANES, QK_W)), full((1, QK_W)), full((1, V_W)),
        full((MIX_W, d)),
    ]
    scratch_shapes = [
        pltpu.VMEM((CONV_HIST + ts, CONV_W), _F32),
        pltpu.VMEM((LRU_HIST + ts, LRU_W), _F32),
        pltpu.VMEM((SUBLANES, LRU_W), _F32),
        pltpu.VMEM((QK_W, V_W), _F32),
        pltpu.VMEM((GLA_BLOCK, GLA_BLOCK), _BF16),
        pltpu.VMEM((GLA_BLOCK, GLA_BLOCK), _BF16),
        pltpu.VMEM((LANES, GLA_BLOCK), _BF16),
        pltpu.VMEM((GLA_BLOCK, GLA_BLOCK), _F32),
        pltpu.VMEM((QK_W, V_W), _F32),
    ]
    args = [x, mod] + list(params)
    if moe is not None:
        pos, mod_prev, y = moe
        last = bsz * n_s - 1
        in_specs += [
            pl.BlockSpec((ts,), lambda b, s: (b * n_s + s,), memory_space=pltpu.SMEM),
            pl.BlockSpec((ts,), lambda b, s: (jnp.minimum(b * n_s + s + 1, last),), memory_space=pltpu.SMEM),
            pl.BlockSpec((ts,), lambda b, s: (jnp.minimum(b * n_s + s + 2, last),), memory_space=pltpu.SMEM),
            pl.BlockSpec((1, 6, d), lambda b, s: (b, 0, 0)),
            pl.BlockSpec(memory_space=pl.ANY),
        ]
        scratch_shapes += [pltpu.VMEM((GATHER_SLOTS, ts, d), _F32), pltpu.SemaphoreType.DMA((GATHER_SLOTS,))]
        args += [pos, pos, pos, mod_prev, y]
    return pl.pallas_call(
        functools.partial(_mix_kernel, fused_moe_gather=moe is not None),
        out_shape=jax.ShapeDtypeStruct(x.shape, _F32),
        grid=(bsz, n_s),
        in_specs=in_specs,
        out_specs=pl.BlockSpec((1, ts, d), lambda b, s: (b, s, 0)),
        scratch_shapes=scratch_shapes,
        compiler_params=pltpu.CompilerParams(
            dimension_semantics=("arbitrary", "arbitrary"), vmem_limit_bytes=VMEM_LIMIT_BYTES),
        name="token_mix",
    )(*args)


def _route_kernel(x_ref, mod_ref, g_ref, wr_ref, br_ref, xp_ref, meta_ref, cnt_ref, carry):
    tt = x_ref.shape[0]
    i = pl.program_id(0)

    @pl.when(i == 0)
    def _():
        carry[...] = jnp.zeros_like(carry)

    x = x_ref[...]
    sh2 = mod_ref[0, 3:4, :]
    sc2 = mod_ref[0, 4:5, :]
    ms = jnp.mean(x * x, axis=-1, keepdims=True)
    h = (x * lax.rsqrt(ms + EPS) * g_ref[...]) * (1.0 + sc2) + sh2

    h_hi, h_lo = _split2(h)
    w_hi, w_lo = _split2(wr_ref[...])
    w_cat = jnp.concatenate([w_hi, w_lo], axis=1)
    prod = _dot(h_hi, w_cat) + _dot(h_lo, w_cat)
    logits = prod[:, 0:LANES] + prod[:, LANES:2 * LANES] + br_ref[...]

    lane = lax.broadcasted_iota(jnp.int32, (tt, LANES), 1)
    lane_f = lane.astype(_F32)
    neg = -jnp.inf
    big = float(LANES)

    def first_argmax(vals, vmax):
        return jnp.min(jnp.where(vals == vmax, lane_f, big), axis=-1, keepdims=True).astype(jnp.int32)

    gl = jnp.where(lane < N_GROUPS, logits, neg)
    gmax = jnp.max(gl, axis=-1, keepdims=True)
    g_star = first_argmax(gl, gmax)
    p_sel = 1.0 / jnp.sum(jnp.exp(gl - gmax), axis=-1, keepdims=True)
    base = N_GROUPS + EXPERTS_PER_GROUP * g_star
    el = jnp.where((lane >= base) & (lane < base + EXPERTS_PER_GROUP), logits, neg)
    v0 = jnp.max(el, axis=-1, keepdims=True)
    i0 = first_argmax(el, v0)
    el2 = jnp.where(lane == i0, neg, el)
    v1 = jnp.max(el2, axis=-1, keepdims=True)
    i1 = first_argmax(el2, v1)
    ex = jnp.exp(v1 - v0)
    wt0 = p_sel / (1.0 + ex)
    wt1 = p_sel * ex / (1.0 + ex)
    e0 = i0 - base
    e1 = i1 - base
    e_lo = jnp.minimum(e0, e1)
    e_hi = jnp.maximum(e0, e1)
    w_lo = jnp.where(e0 < e1, wt0, wt1)
    w_hi = jnp.where(e0 < e1, wt1, wt0)
    pair = (e_lo * (2 * EXPERTS_PER_GROUP - 1 - e_lo)) // 2 + (e_hi - e_lo - 1)
    bucket = g_star * N_PAIRS + pair

    onehot = lane == bucket
    onehot_f = jnp.where(onehot, 1.0, 0.0)
    ri = lax.broadcasted_iota(jnp.int32, (tt, tt), 0)
    ci = lax.broadcasted_iota(jnp.int32, (tt, tt), 1)
    strict = jnp.where(ci < ri, 1.0, 0.0).astype(_BF16)
    prefix = _dot(strict, onehot_f.astype(_BF16)) + carry[0:1, :]
    rank = jnp.sum(jnp.where(onehot, prefix, 0.0), axis=-1, keepdims=True)
    carry[...] = carry[...] + jnp.sum(onehot_f, axis=0, keepdims=True)
    cnt_ref[...] = carry[...]

    meta = jnp.where(lane == 0, bucket.astype(_F32), jnp.where(lane == 1, rank, 0.0))
    meta_ref[...] = meta.T[0:SUBLANES, :]

    xp_ref[:, 0:D_MODEL] = h
    xp_ref[:, D_MODEL:ROW_W] = jnp.where(lane == 0, w_lo, jnp.where(lane == 1, w_hi, 0.0))


def _route(x2d, mod, g_ffn, w_r, b_r, layer, seq):
    n_tok, d = x2d.shape
    tt = ROUTE_TT
    tiles_per_seq = seq // tt
    per_layer = lambda shape: pl.BlockSpec((None,) + shape, lambda i: (layer, 0, 0))
    return pl.pallas_call(
        _route_kernel,
        out_shape=(
            jax.ShapeDtypeStruct((n_tok, ROW_W), _F32),
            jax.ShapeDtypeStruct((SUBLANES, n_tok), _F32),
            jax.ShapeDtypeStruct((SUBLANES, LANES), _F32),
        ),
        grid=(n_tok // tt,),
        in_specs=[
            pl.BlockSpec((tt, d), lambda i: (i, 0)),
            pl.BlockSpec((1, 6, d), lambda i: (i // tiles_per_seq, 0, 0)),
            per_layer((1, d)),
            per_layer((d, LANES)),
            per_layer((1, LANES)),
        ],
        out_specs=(
            pl.BlockSpec((tt, ROW_W), lambda i: (i, 0)),
            pl.BlockSpec((SUBLANES, tt), lambda i: (0, i)),
            pl.BlockSpec((SUBLANES, LANES), lambda i: (0, 0)),
        ),
        scratch_shapes=[pltpu.VMEM((SUBLANES, LANES), _F32)],
        compiler_params=pltpu.CompilerParams(
            dimension_semantics=("arbitrary",), vmem_limit_bytes=VMEM_LIMIT_BYTES),
        name="moe_route",
    )(x2d, mod, g_ffn, w_r, b_r)


def _permute_kernel(fill_ref, pos_ref, xp_ref, xs_hbm, zbuf, sem, zsem):
    tb = pos_ref.shape[0]

    @pl.when(pl.program_id(0) == 0)
    def _():
        zbuf[...] = jnp.zeros_like(zbuf)
        n_used = fill_ref[2 * N_BUCKETS]
        n_tiles = xs_hbm.shape[0] // FFN_TM
        for wait in (False, True):
            for b in range(N_BUCKETS):
                for row0, cond in ((fill_ref[b], fill_ref[N_BUCKETS + b] > 0),
                                   ((n_used + b) * FFN_TM, n_used + b < n_tiles)):
                    @pl.when(cond)
                    def _():
                        dst = xs_hbm.at[pl.ds(pl.multiple_of(row0, FFN_TM), FFN_TM)]
                        fill = pltpu.make_async_copy(zbuf, dst, zsem)
                        fill.wait() if wait else fill.start()

    for j in range(tb):
        pltpu.make_async_copy(xp_ref.at[pl.ds(j, 1)], xs_hbm.at[pl.ds(pos_ref[j], 1)], sem).start(priority=j % 2)
    pltpu.make_async_copy(xp_ref, xs_hbm.at[pl.ds(0, tb)], sem).wait()


def _permute(fill_starts, pos, xp, n_rows):
    n_tok = xp.shape[0]
    tb = PERM_TB
    return pl.pallas_call(
        _permute_kernel,
        out_shape=jax.ShapeDtypeStruct((n_rows, ROW_W), _F32),
        grid_spec=pltpu.PrefetchScalarGridSpec(
            num_scalar_prefetch=1,
            grid=(n_tok // tb,),
            in_specs=[
                pl.BlockSpec((tb,), lambda i, fs: (i,), memory_space=pltpu.SMEM),
                pl.BlockSpec((tb, ROW_W), lambda i, fs: (i, 0)),
            ],
            out_specs=pl.BlockSpec(memory_space=pl.ANY),
            scratch_shapes=[pltpu.VMEM((FFN_TM, ROW_W), _F32), pltpu.SemaphoreType.DMA,
                            pltpu.SemaphoreType.DMA],
        ),
        compiler_params=pltpu.CompilerParams(
            dimension_semantics=("arbitrary",), vmem_limit_bytes=VMEM_LIMIT_BYTES),
        name="moe_permute",
    )(fill_starts, pos, xp)


def _ffn_kernel(tg_ref, tlo_ref, thi_ref, clo_ref, chi_ref, ng_ref, nlo_ref, nhi_ref, plo_ref, phi_ref, nused_ref,
                xs_ref, wg_hbm, wu_hbm, wd_hbm, y_ref, sg, su, sd, agu, ad, sem, *, layer):
    i = pl.program_id(0)

    def weight_copies(slot, g, e):
        return (pltpu.make_async_copy(wg_hbm.at[layer, g, e], sg.at[slot], sem.at[slot, 0]),
                pltpu.make_async_copy(wu_hbm.at[layer, g, e], su.at[slot], sem.at[slot, 1]),
                pltpu.make_async_copy(wd_hbm.at[layer, g, e], sd.at[slot], sem.at[slot, 2]))

    @pl.when(i == 0)
    def _():
        for cp in weight_copies(0, tg_ref[0], tlo_ref[0]) + weight_copies(1, tg_ref[0], thi_ref[0]):
            cp.start()

    for slot, changed_ref, e_ref in ((0, clo_ref, tlo_ref), (1, chi_ref, thi_ref)):
        @pl.when(changed_ref[i] == 1)
        def _():
            for cp in weight_copies(slot, tg_ref[i], e_ref[i]):
                cp.wait()
            agu[slot, :, 0:D_EXPERT] = sg[slot].astype(_BF16)
            agu[slot, :, D_EXPERT:2 * D_EXPERT] = su[slot].astype(_BF16)
            ad[slot] = sd[slot].astype(_BF16)

    for slot, prefetch_ref, e_ref in ((0, plo_ref, nlo_ref), (1, phi_ref, nhi_ref)):
        @pl.when(prefetch_ref[i] == 1)
        def _():
            for cp in weight_copies(slot, ng_ref[i], e_ref[i]):
                cp.start()

    @pl.when(i < nused_ref[0])
    def _():
        xb = xs_ref[:, 0:D_MODEL].astype(_BF16)
        info = xs_ref[:, D_MODEL:ROW_W]

        def expert(slot):
            gate_up = _dot(xb, agu[slot])
            hid = _silu(gate_up[:, 0:D_EXPERT]) * gate_up[:, D_EXPERT:2 * D_EXPERT] * info[:, slot:slot + 1]
            return _dot(hid.astype(_BF16), ad[slot])

        y_ref[...] = expert(0) + expert(1)

    @pl.when(i >= nused_ref[0])
    def _():
        y_ref[...] = jnp.zeros_like(y_ref)


def _expert_ffn(layer, plan, xs, w_gate, w_up, w_down):
    n_rows = xs.shape[0]
    tm = FFN_TM
    n_tiles = n_rows // tm

    def row_map(i, *prefetch):
        n_used = prefetch[-1]
        return (jnp.minimum(i, n_used[0] - 1), 0)

    hbm = pl.BlockSpec(memory_space=pl.ANY)
    return pl.pallas_call(
        functools.partial(_ffn_kernel, layer=layer),
        out_shape=jax.ShapeDtypeStruct((n_rows, D_MODEL), _F32),
        grid_spec=pltpu.PrefetchScalarGridSpec(
            num_scalar_prefetch=len(plan),
            grid=(n_tiles,),
            in_specs=[pl.BlockSpec((tm, ROW_W), row_map), hbm, hbm, hbm],
            out_specs=pl.BlockSpec((tm, D_MODEL), lambda i, *prefetch: (i, 0)),
            scratch_shapes=[
                pltpu.VMEM((2, D_MODEL, D_EXPERT), _F32), pltpu.VMEM((2, D_MODEL, D_EXPERT), _F32),
                pltpu.VMEM((2, D_EXPERT, D_MODEL), _F32),
                pltpu.VMEM((2, D_MODEL, 2 * D_EXPERT), _BF16), pltpu.VMEM((2, D_EXPERT, D_MODEL), _BF16),
                pltpu.SemaphoreType.DMA((2, 3)),
            ],
        ),
        compiler_params=pltpu.CompilerParams(
            dimension_semantics=("arbitrary",), vmem_limit_bytes=VMEM_LIMIT_BYTES),
        name="moe_ffn",
    )(*plan, xs, w_gate, w_up, w_down)


def _combine_kernel(pos_ref, x_ref, mod_ref, gfin_ref, y_hbm, o_ref, ybuf, sem):
    tc = x_ref.shape[0]
    for j in range(tc):
        pltpu.make_async_copy(y_hbm.at[pl.ds(pos_ref[j], 1)], ybuf.at[pl.ds(j, 1)], sem).start(priority=j % 2)
    pltpu.make_async_copy(y_hbm.at[pl.ds(0, tc)], ybuf, sem).wait()

    gt2 = mod_ref[0, 5:6, :]
    out = x_ref[...] + gt2 * ybuf[...]
    ms = jnp.mean(out * out, axis=-1, keepdims=True)
    o_ref[...] = out * lax.rsqrt(ms + EPS) * gfin_ref[...]


def _combine(pos, x2d, mod, g_final, y, seq):
    n_tok, d = x2d.shape
    tc = COMB_TC
    tiles_per_seq = seq // tc
    return pl.pallas_call(
        _combine_kernel,
        out_shape=jax.ShapeDtypeStruct((n_tok, d), _F32),
        grid=(n_tok // tc,),
        in_specs=[
            pl.BlockSpec((tc,), lambda i: (i,), memory_space=pltpu.SMEM),
            pl.BlockSpec((tc, d), lambda i: (i, 0)),
            pl.BlockSpec((1, 6, d), lambda i: (i // tiles_per_seq, 0, 0)),
            pl.BlockSpec((1, d), lambda i: (0, 0)),
            pl.BlockSpec(memory_space=pl.ANY),
        ],
        out_specs=pl.BlockSpec((tc, d), lambda i: (i, 0)),
        scratch_shapes=[pltpu.VMEM((tc, d), _F32), pltpu.SemaphoreType.DMA],
        compiler_params=pltpu.CompilerParams(
            dimension_semantics=("arbitrary",), vmem_limit_bytes=VMEM_LIMIT_BYTES),
        name="moe_combine",
    )(pos, x2d, mod, g_final, y)


def _pad_heads(w, heads, width, padded):
    lead = w.shape[:-1]
    w = w.reshape(lead + (heads, width))
    w = jnp.pad(w, [(0, 0)] * len(lead) + [(0, 0), (0, padded - width)])
    return w.reshape(lead + (heads * padded,))


def _block_diag(w):
    n_layers, n, bw, _ = w.shape
    eye = jnp.eye(n, dtype=w.dtype)
    return (eye[None, :, None, :, None] * w[:, :, :, None, :]).reshape(n_layers, n * bw, n * bw)


def _prep_params(w_in, conv_dw_w, conv_dw_b, conv_ln_g, conv_ln_b, lru_conv_w, lru_conv_b, lru_w_a,
                 lru_b_a, lru_w_i, lru_b_i, lru_lam, gla_w_gate, gla_b_gate, gla_norm_g, w_out, g_mix):
    sizes = [CONV_W, CONV_W, LRU_W, LRU_W, GLA_HEADS * GLA_DK, GLA_HEADS * GLA_DK, GLA_V, GLA_RANK, GLA_V]
    cv_v, cv_g, lr_x, lr_y, q, k, v, g_lr, og = jnp.split(w_in, np.cumsum(sizes)[:-1].tolist(), axis=-1)
    w_in_p = jnp.concatenate([
        cv_v, cv_g, lr_x, lr_y,
        _pad_heads(q, GLA_HEADS, GLA_DK, DK_PAD), _pad_heads(k, GLA_HEADS, GLA_DK, DK_PAD),
        _pad_heads(v, GLA_HEADS, GLA_DV, DV_PAD),
        jnp.pad(g_lr, ((0, 0), (0, 0), (0, LANES - GLA_RANK))),
        _pad_heads(og, GLA_HEADS, GLA_DV, DV_PAD)], axis=-1).astype(_BF16)
    n_layers = w_in.shape[0]
    wo_o = w_out[:, CONV_W + LRU_W:].reshape(n_layers, GLA_HEADS, GLA_DV, D_MODEL)
    wo_o = jnp.pad(wo_o, ((0, 0), (0, 0), (0, DV_PAD - GLA_DV), (0, 0))).reshape(n_layers, V_W, D_MODEL)
    w_out_p = jnp.concatenate([w_out[:, :CONV_W + LRU_W], wo_o], axis=1).astype(_BF16)
    w_gate_p = jnp.pad(_pad_heads(gla_w_gate, GLA_HEADS, GLA_DK, DK_PAD),
                       ((0, 0), (0, LANES - GLA_RANK), (0, 0))).astype(_BF16)
    row = lambda v: v[:, None, :]
    return [
        row(g_mix),
        w_in_p,
        jnp.pad(conv_dw_w, ((0, 0), (0, CONV_HIST - CONV_K), (0, 0))),
        row(conv_dw_b), row(conv_ln_g), row(conv_ln_b),
        lru_conv_w, row(lru_conv_b),
        jnp.concatenate([_block_diag(lru_w_a), _block_diag(lru_w_i)], axis=2).astype(_BF16),
        row(jnp.concatenate([lru_b_a, lru_b_i], axis=1)),
        row(lru_lam),
        w_gate_p,
        row(_pad_heads(gla_b_gate, GLA_HEADS, GLA_DK, DK_PAD)),
        row(_pad_heads(gla_norm_g, GLA_HEADS, GLA_DV, DV_PAD)),
        w_out_p,
    ]


def _bucket_layout(counts, n_tiles):
    tm = FFN_TM
    counts = counts.astype(jnp.int32)
    tiles = (counts + tm - 1) // tm
    upto = jnp.arange(N_BUCKETS)[:, None] <= jnp.arange(N_BUCKETS)[None, :]
    tile_end = jnp.sum(jnp.where(upto, tiles[:, None], 0), axis=0)
    starts = ((tile_end - tiles) * tm).astype(jnp.int32)
    n_used = tile_end[-1]
    tile_idx = jnp.minimum(jnp.arange(n_tiles, dtype=jnp.int32), n_used - 1)

    def bucket_of(t):
        return jnp.sum((t[:, None] >= tile_end[None, :]).astype(jnp.int32), axis=1)

    def experts(b):
        pair = b % N_PAIRS
        lo = (pair >= 3).astype(jnp.int32) + (pair >= 5).astype(jnp.int32)
        hi = jnp.where(pair < 3, pair + 1, jnp.where(pair < 5, pair - 1, 3))
        return b // N_PAIRS, lo, hi

    tile_bucket = bucket_of(tile_idx)
    prev_bucket = jnp.where(tile_idx >= 1, bucket_of(tile_idx - 1), -1)
    g, lo, hi = experts(tile_bucket)
    pg, plo, phi = experts(prev_bucket)
    first = (tile_bucket != prev_bucket) & (jnp.arange(n_tiles) < n_used)
    fresh = prev_bucket < 0
    changed_lo = first & (fresh | (g != pg) | (lo != plo))
    changed_hi = first & (fresh | (g != pg) | (hi != phi))
    next_first_tile = jnp.min(jnp.where(tile_end[None, :] > tile_idx[:, None], tile_end[None, :], n_tiles), axis=1)
    has_next = next_first_tile < n_used
    ng, nlo, nhi = experts(bucket_of(jnp.minimum(next_first_tile, n_used - 1)))
    prefetch_lo = first & has_next & ((ng != g) | (nlo != lo))
    prefetch_hi = first & has_next & ((ng != g) | (nhi != hi))
    as_i32 = lambda v: v.astype(jnp.int32)
    ffn_plan = (g, lo, hi, as_i32(changed_lo), as_i32(changed_hi), ng, nlo, nhi,
                as_i32(prefetch_lo), as_i32(prefetch_hi), n_used.reshape(1).astype(jnp.int32))
    fill = jnp.concatenate([(jnp.maximum(tile_end - 1, 0) * tm).astype(jnp.int32), tiles, n_used[None]])
    return starts, fill, ffn_plan


def kernel(x, c, w_ada, b_ada, g_mix, w_in, conv_dw_w, conv_dw_b, conv_ln_g, conv_ln_b, lru_conv_w,
           lru_conv_b, lru_w_a, lru_b_a, lru_w_i, lru_b_i, lru_lam, gla_w_gate, gla_b_gate, gla_norm_g,
           w_out, g_ffn, w_route_group, b_route_group, w_route_expert, b_route_expert, w_gate, w_up,
           w_down, g_final):
    bsz, seq, d = x.shape
    n_layers = w_ada.shape[0]
    n_tok = bsz * seq
    assert d == D_MODEL and seq % max(MIX_TS, ROUTE_TT, COMB_TC) == 0 and MIX_TS % GLA_BLOCK == 0
    assert n_tok % PERM_TB == 0
    n_tiles = n_tok // FFN_TM + N_BUCKETS
    n_rows = n_tiles * FFN_TM

    mod_all = _modulation(c, w_ada, b_ada).reshape(n_layers, bsz, 6, d)

    params = _prep_params(w_in, conv_dw_w, conv_dw_b, conv_ln_g, conv_ln_b, lru_conv_w, lru_conv_b,
                          lru_w_a, lru_b_a, lru_w_i, lru_b_i, lru_lam, gla_w_gate, gla_b_gate, gla_norm_g,
                          w_out, g_mix)
    n_experts = N_GROUPS * EXPERTS_PER_GROUP
    w_r = jnp.concatenate(
        [w_route_group, w_route_expert.transpose(0, 2, 1, 3).reshape(n_layers, d, n_experts)], axis=2)
    w_r = jnp.pad(w_r, ((0, 0), (0, 0), (0, LANES - N_GROUPS - n_experts)))
    b_r = jnp.concatenate([b_route_group, b_route_expert.reshape(n_layers, n_experts)], axis=1)
    b_r = jnp.pad(b_r, ((0, 0), (0, LANES - N_GROUPS - n_experts)))[:, None, :]
    pending_moe = None
    for l in range(n_layers):
        mod = mod_all[l]
        x = _token_mix(x, mod, params, l, pending_moe)

        x2d = x.reshape(n_tok, d)
        xp, meta, counts = _route(x2d, mod, g_ffn[:, None, :], w_r, b_r, l, seq)
        bucket = meta[0].astype(jnp.int32)
        rank = meta[1].astype(jnp.int32)
        starts, fill, ffn_plan = _bucket_layout(counts[0, :N_BUCKETS], n_tiles)
        in_bucket = bucket[:, None] == jnp.arange(N_BUCKETS, dtype=jnp.int32)[None, :]
        pos = rank + jnp.sum(jnp.where(in_bucket, starts[None, :], 0), axis=1)

        xs = _permute(fill, pos, xp, n_rows)
        y = _expert_ffn(l, ffn_plan, xs, w_gate, w_up, w_down)
        pending_moe = (pos, mod, y)
    return _combine(pos, x2d, mod, g_final[None, :], y, seq).reshape(bsz, seq, d)
```

```python
import functools

import jax
import jax.numpy as jnp
import numpy as np
from jax import lax
from jax.experimental import pallas as pl
from jax.experimental.pallas import tpu as pltpu

D_MODEL = 1024
CONV_W = 256
LRU_W = 384
GLA_V = 384
CONV_K = 31
LRU_CONV_K = 4
LRU_BLOCKS = 6
LRU_BW = 64
LRU_C = 8.0
GLA_HEADS = 4
GLA_DV = 96
GLA_DK = 48
GLA_RANK = 16
GLA_TAU = 16.0
GLA_CHUNK = 64
N_GROUPS = 4
EXPERTS_PER_GROUP = 4
D_EXPERT = 512
EPS = 1e-6

LANES = 128
SUBLANES = 8
VMEM_LIMIT_BYTES = 56 * 1024 * 1024

DK_PAD = 64
DV_PAD = 128
QK_W = GLA_HEADS * DK_PAD
V_W = GLA_HEADS * DV_PAD
OFF_CVV = 0
OFF_CVG = OFF_CVV + CONV_W
OFF_LRX = OFF_CVG + CONV_W
OFF_LRY = OFF_LRX + LRU_W
OFF_Q = OFF_LRY + LRU_W
OFF_K = OFF_Q + QK_W
OFF_V = OFF_K + QK_W
OFF_GLR = OFF_V + V_W
OFF_OG = OFF_GLR + LANES
N_IN = OFF_OG + V_W
MIX_W = CONV_W + LRU_W + V_W

CONV_HIST = 32
LRU_HIST = 8
GLA_BLOCK = 256

N_PAIRS = 6
N_BUCKETS = N_GROUPS * N_PAIRS
ROW_W = D_MODEL + LANES

MIX_TS = 512
ROUTE_TT = 512
FFN_TM = 256
PERM_TB = 2048
COMB_TC = 2048
DMA_UNROLL = 8
GATHER_SLOTS = 3

_F32 = jnp.float32
_BF16 = jnp.bfloat16


def _sigmoid(x):
    return 1.0 / (1.0 + jnp.exp(-x))


def _silu(x):
    return x * _sigmoid(x)


def _dot(a, b):
    return jnp.dot(a, b, preferred_element_type=_F32)


def _dot_nt(a, b):
    return lax.dot_general(a, b, (((1,), (1,)), ((), ())), preferred_element_type=_F32)


def _split2(x):
    hi = x.astype(_BF16)
    return hi, (x - hi.astype(_F32)).astype(_BF16)


def _mod_kernel(c_ref, w_ref, b_ref, o_ref):
    c_act = _silu(c_ref[...])
    o_ref[0] = _dot(c_act.astype(_BF16), w_ref[0].astype(_BF16)) + b_ref[0]


def _modulation(c, w_ada, b_ada):
    n_layers, d, n = w_ada.shape
    bsz = c.shape[0]
    tn = 1536
    return pl.pallas_call(
        _mod_kernel,
        out_shape=jax.ShapeDtypeStruct((n_layers, bsz, n), _F32),
        grid=(n_layers, n // tn),
        in_specs=[
            pl.BlockSpec((bsz, d), lambda l, j: (0, 0)),
            pl.BlockSpec((1, d, tn), lambda l, j: (l, 0, j)),
            pl.BlockSpec((1, 1, tn), lambda l, j: (l, 0, j)),
        ],
        out_specs=pl.BlockSpec((1, bsz, tn), lambda l, j: (l, 0, j)),
        compiler_params=pltpu.CompilerParams(
            dimension_semantics=("arbitrary", "arbitrary"), vmem_limit_bytes=VMEM_LIMIT_BYTES),
        name="adaln_mod",
    )(c, w_ada, b_ada.reshape(n_layers, 1, n))


def _mix_kernel(x_ref, mod_ref, gmix_ref, win_ref, cw_ref, cb_ref, lng_ref, lnb_ref,
                lcw_ref, lcb_ref, wai_ref, bai_ref, lam_ref, wgate_ref, bgate_ref, gnorm_ref,
                wout_ref, *rest, fused_moe_gather):
    if fused_moe_gather:
        pos_cur, pos_nxt, pos_ahd, modp_ref, y_hbm, o_ref = rest[:6]
        ybuf, gsem = rest[-2:]
        rest = rest[6:-2]
    else:
        o_ref, rest = rest[0], rest[1:]
    ubuf, lbuf, hcar, state, tri_ref, tria_ref, csel_ref, causal_ref, hdiag_ref = rest
    ts = x_ref.shape[1]
    s_idx = pl.program_id(1)

    @pl.when(s_idx == 0)
    def _():
        ubuf[0:CONV_HIST, :] = jnp.zeros((CONV_HIST, CONV_W), _F32)
        lbuf[0:LRU_HIST, :] = jnp.zeros((LRU_HIST, LRU_W), _F32)
        hcar[...] = jnp.zeros_like(hcar)
        state[...] = jnp.zeros_like(state)
        ri = lax.broadcasted_iota(jnp.int32, (GLA_BLOCK, GLA_BLOCK), 0)
        ci = lax.broadcasted_iota(jnp.int32, (GLA_BLOCK, GLA_BLOCK), 1)
        same_chunk = ri // GLA_CHUNK == ci // GLA_CHUNK
        causal_f = jnp.where(same_chunk & (ci <= ri), 1.0, 0.0)
        causal_ref[...] = causal_f
        tri_ref[...] = causal_f.astype(_BF16)
        tria_ref[...] = jnp.where(same_chunk & (ci > ri), 1.0, 0.0).astype(_BF16)
        sel_r = lax.broadcasted_iota(jnp.int32, (LANES, GLA_BLOCK), 0)
        sel_c = lax.broadcasted_iota(jnp.int32, (LANES, GLA_BLOCK), 1)
        csel_ref[...] = jnp.where(sel_r == sel_c // GLA_CHUNK, 1.0, 0.0).astype(_BF16)
        st_row_head = lax.broadcasted_iota(jnp.int32, (QK_W, V_W), 0) // DK_PAD
        st_col_head = lax.broadcasted_iota(jnp.int32, (QK_W, V_W), 1) // DV_PAD
        hdiag_ref[...] = jnp.where(st_row_head == st_col_head, 1.0, 0.0)

    x = x_ref[0]
    if fused_moe_gather:
        step = pl.program_id(0) * pl.num_programs(1) + s_idx
        n_steps = pl.num_programs(0) * pl.num_programs(1)
        slot = step % GATHER_SLOTS

        def row_copy(pos_ref, j, to_slot):
            return pltpu.make_async_copy(y_hbm.at[pl.ds(pos_ref[j], 1)], ybuf.at[to_slot, pl.ds(j, 1)],
                                         gsem.at[to_slot])

        def wait_rows(of_slot):
            pltpu.make_async_copy(y_hbm.at[pl.ds(0, ts)], ybuf.at[of_slot], gsem.at[of_slot]).wait()

        @pl.when(step == 0)
        def _():
            def issue(j, carry):
                row_copy(pos_cur, j, 0).start()
                row_copy(pos_nxt, j, 1).start()
                return carry
            lax.fori_loop(0, ts, issue, 0, unroll=DMA_UNROLL)

        wait_rows(slot)
        x = x + modp_ref[0, 5:6, :] * ybuf[slot]

        ahead_slot = (step + GATHER_SLOTS - 1) % GATHER_SLOTS
        for j in range(ts):
            row_copy(pos_ahd, j, ahead_slot).start()

    sh1 = mod_ref[0, 0:1, :]
    gt1 = mod_ref[0, 2:3, :]
    scale = gmix_ref[...] * (1.0 + mod_ref[0, 1:2, :])
    ms = jnp.mean(x * x, axis=-1, keepdims=True)
    hb = (x * lax.rsqrt(ms + EPS) * scale + sh1).astype(_BF16)

    def proj(off, width):
        return _dot(hb, win_ref[:, off:off + width])

    out = {}
    _run_alternately(
        _conv_lru_steps(out, proj, cw_ref, cb_ref, lng_ref, lnb_ref, lcw_ref, lcb_ref, wai_ref, bai_ref,
                        lam_ref, ubuf, lbuf, hcar, ts),
        _gla_steps(out, proj, wgate_ref, bgate_ref, gnorm_ref, state, tri_ref, tria_ref, csel_ref,
                   causal_ref, hdiag_ref, ts))

    mixed = (_dot(out["o_g"].astype(_BF16), wout_ref[CONV_W + LRU_W:MIX_W, :])
             + _dot(out["r_out"].astype(_BF16), wout_ref[CONV_W:CONV_W + LRU_W, :])
             + _dot(out["u_out"].astype(_BF16), wout_ref[0:CONV_W, :]))
    o_ref[0] = x + gt1 * mixed

    if fused_moe_gather:
        @pl.when(step == n_steps - 1)
        def _():
            wait_rows((step + 1) % GATHER_SLOTS)
            wait_rows((step + 2) % GATHER_SLOTS)


def _run_alternately(*step_generators):
    clock = [0] * len(step_generators)
    live = list(range(len(step_generators)))
    while live:
        i = min(live, key=lambda j: clock[j])
        try:
            clock[i] += next(step_generators[i]) * (1 if i else 0.5)
        except StopIteration:
            live.remove(i)


def _conv_lru_steps(out, proj, cw_ref, cb_ref, lng_ref, lnb_ref, lcw_ref, lcb_ref, wai_ref, bai_ref,
                    lam_ref, ubuf, lbuf, hcar, ts):
    u = proj(OFF_CVV, CONV_W) * _sigmoid(proj(OFF_CVG, CONV_W))
    ubuf[CONV_HIST:CONV_HIST + ts, :] = u
    yield 500
    lbuf[LRU_HIST:LRU_HIST + ts, :] = proj(OFF_LRX, LRU_W)
    xb = jnp.broadcast_to(lcb_ref[...], (ts, LRU_W))
    for k in range(LRU_CONV_K):
        off = LRU_HIST - (LRU_CONV_K - 1) + k
        xb = xb + lcw_ref[k:k + 1, :] * lbuf[off:off + ts, :]
    lbuf[0:LRU_HIST, :] = lbuf[ts:ts + LRU_HIST, :]
    gates = _dot(xb.astype(_BF16), wai_ref[...]) + bai_ref[...]
    yield 800

    acc = jnp.broadcast_to(cb_ref[...], (ts, CONV_W))
    for r in range(SUBLANES):
        z = None
        for a_blk in range((CONV_K - 1 - r) // SUBLANES + 1):
            k = CONV_K - 1 - (SUBLANES * a_blk + r)
            off = CONV_HIST - SUBLANES * (a_blk + 1)
            term = cw_ref[k:k + 1, :] * ubuf[off:off + ts + SUBLANES, :]
            z = term if z is None else z + term
        acc = acc + z[SUBLANES - r:SUBLANES - r + ts]
        yield 200
    ubuf[0:CONV_HIST, :] = ubuf[ts:ts + CONV_HIST, :]
    mu = jnp.mean(acc, axis=-1, keepdims=True)
    cen = acc - mu
    var = jnp.mean(cen * cen, axis=-1, keepdims=True)
    out["u_out"] = _silu(cen * lax.rsqrt(var + EPS) * lng_ref[...] + lnb_ref[...])
    yield 250

    r_gate = _sigmoid(gates[:, 0:LRU_W])
    i_gate = _sigmoid(gates[:, LRU_W:2 * LRU_W])
    lam = lam_ref[...]
    softplus_neg_lam = jnp.maximum(-lam, 0.0) + jnp.log1p(jnp.exp(-jnp.abs(lam)))
    log_a = (-LRU_C) * r_gate * softplus_neg_lam
    a = jnp.exp(log_a)
    mult = jnp.sqrt(jnp.tanh(-log_a) * (a * a + 1.0))
    u_in = mult * (i_gate * xb)
    yield 600
    row = lax.broadcasted_iota(jnp.int32, a.shape, 0) % SUBLANES
    d = 1
    while d < SUBLANES:
        keep = row >= d
        a_sh = jnp.where(keep, pltpu.roll(a, d, axis=0), 1.0)
        u_sh = jnp.where(keep, pltpu.roll(u_in, d, axis=0), 0.0)
        u_in = u_in + a * u_sh
        a = a * a_sh
        d *= 2
        yield 200
    groups = []
    h_prev = hcar[0:1, :]
    for g in range(ts // SUBLANES):
        r0 = g * SUBLANES
        h_g = u_in[r0:r0 + SUBLANES] + a[r0:r0 + SUBLANES] * h_prev
        groups.append(h_g)
        h_prev = h_g[SUBLANES - 1:SUBLANES]
        if g % SUBLANES == SUBLANES - 1:
            yield 50
    h_lru = jnp.concatenate(groups, axis=0)
    hcar[...] = jnp.broadcast_to(h_prev, hcar.shape)
    out["r_out"] = h_lru * jax.nn.gelu(proj(OFF_LRY, LRU_W), approximate=True)


def _gla_steps(out, proj, wgate_ref, bgate_ref, gnorm_ref, state, tri_ref, tria_ref, csel_ref, causal_ref,
               hdiag_ref, ts):
    zg = proj(OFF_GLR, LANES)
    glog = _dot(zg.astype(_BF16), wgate_ref[...]) + bgate_ref[...]
    lg = (jnp.minimum(glog, 0.0) - jnp.log1p(jnp.exp(-jnp.abs(glog)))) * (1.0 / GLA_TAU)
    yield 300
    zq = proj(OFF_Q, QK_W) * (GLA_DK ** -0.5)
    zk = proj(OFF_K, QK_W)
    yield 500
    zv = proj(OFF_V, V_W)
    yield 500

    nblk = ts // GLA_BLOCK
    cpb = GLA_BLOCK // GLA_CHUNK
    tri = tri_ref[...]
    tri_after = tria_ref[...]
    chunk_sel = csel_ref[...]
    causal = causal_ref[...] > 0.5
    qk_lane_head = lax.broadcasted_iota(jnp.int32, (1, QK_W), 1) // DK_PAD
    t_lane_chunk = lax.broadcasted_iota(jnp.int32, (1, GLA_BLOCK), 1) // GLA_CHUNK

    o_blocks = []
    for blk in range(nblk):
        r0 = blk * GLA_BLOCK
        lg_b = lg[r0:r0 + GLA_BLOCK]
        p_hi, p_lo = _split2(lg_b)
        b = _dot(tri, p_hi) + _dot(tri, p_lo)
        b_rest = _dot(tri_after, p_hi) + _dot(tri_after, p_lo)
        b_tot = _dot(chunk_sel, p_hi) + _dot(chunk_sel, p_lo)
        yield 500
        q_in =zq[r0:r0 + GLA_BLOCK] * jnp.exp(b)
        k_blk = zk[r0:r0 + GLA_BLOCK]
        k_in = (k_blk * jnp.exp(-b)).astype(_BF16)
        k_out_t = (k_blk * jnp.exp(b_rest)).T.astype(_BF16)
        v_b = zv[r0:r0 + GLA_BLOCK].astype(_BF16)
        q_in_b = q_in.astype(_BF16)
        decay_cols = jnp.exp(b_tot).T

        o_heads = []
        for hd in range(GLA_HEADS):
            q_h = jnp.where(qk_lane_head == hd, q_in_b, jnp.zeros_like(q_in_b))
            sc = jnp.where(causal, _dot_nt(q_h, k_in), 0.0)
            o_heads.append(_dot(sc.astype(_BF16), v_b[:, hd * DV_PAD:(hd + 1) * DV_PAD]))
            yield 300
        o_intra = jnp.concatenate(o_heads, axis=1)

        o_inter = []
        st = state[...]
        for c in range(cpb):
            c0 = c * GLA_CHUNK
            o_inter.append(_dot(q_in_b[c0:c0 + GLA_CHUNK], st.astype(_BF16)))
            kv = _dot(jnp.where(t_lane_chunk == c, k_out_t, jnp.zeros_like(k_out_t)), v_b)
            st = st * decay_cols[:, c:c + 1] + kv * hdiag_ref[...]
            yield 400
        state[...] = st
        o_blocks.append(o_intra + jnp.concatenate(o_inter, axis=0))
    o = o_blocks[0] if nblk == 1 else jnp.concatenate(o_blocks, axis=0)

    og = proj(OFF_OG, V_W)
    o_parts = []
    for hd in range(GLA_HEADS):
        o_h = o[:, hd * DV_PAD:(hd + 1) * DV_PAD]
        ms_h = jnp.sum(o_h * o_h, axis=-1, keepdims=True) * (1.0 / GLA_DV)
        o_parts.append(o_h * lax.rsqrt(ms_h + EPS))
    o_n = jnp.concatenate(o_parts, axis=1) * gnorm_ref[...]
    out["o_g"] = o_n * _silu(og)


def _token_mix(x, mod, params, layer, moe=None):
    bsz, seq, d = x.shape
    ts = MIX_TS
    n_s = seq // ts
    full = lambda shape: pl.BlockSpec((None,) + shape, lambda b, s: (layer,) + (0,) * len(shape))
    in_specs = [
        pl.BlockSpec((1, ts, d), lambda b, s: (b, s, 0)),
        pl.BlockSpec((1, 6, d), lambda b, s: (b, 0, 0)),
        full((1, d)),
        full((d, N_IN)),
        full((CONV_HIST, CONV_W)), full((1, CONV_W)), full((1, CONV_W)), full((1, CONV_W)),
        full((LRU_CONV_K, LRU_W)), full((1, LRU_W)),
        full((LRU_W, 2 * LRU_W)), full((1, 2 * LRU_W)), full((1, LRU_W)),
        full((LANES, QK_W)), full((1, QK_W)), full((1, V_W)),
        full((MIX_W, d)),
    ]
    scratch_shapes = [
        pltpu.VMEM((CONV_HIST + ts, CONV_W), _F32),
        pltpu.VMEM((LRU_HIST + ts, LRU_W), _F32),
        pltpu.VMEM((SUBLANES, LRU_W), _F32),
        pltpu.VMEM((QK_W, V_W), _F32),
        pltpu.VMEM((GLA_BLOCK, GLA_BLOCK), _BF16),
        pltpu.VMEM((GLA_BLOCK, GLA_BLOCK), _BF16),
        pltpu.VMEM((LANES, GLA_BLOCK), _BF16),
        pltpu.VMEM((GLA_BLOCK, GLA_BLOCK), _F32),
        pltpu.VMEM((QK_W, V_W), _F32),
    ]
    args = [x, mod] + list(params)
    if moe is not None:
        pos, mod_prev, y = moe
        last = bsz * n_s - 1
        in_specs += [
            pl.BlockSpec((ts,), lambda b, s: (b * n_s + s,), memory_space=pltpu.SMEM),
            pl.BlockSpec((ts,), lambda b, s: (jnp.minimum(b * n_s + s + 1, last),), memory_space=pltpu.SMEM),
            pl.BlockSpec((ts,), lambda b, s: (jnp.minimum(b * n_s + s + 2, last),), memory_space=pltpu.SMEM),
            pl.BlockSpec((1, 6, d), lambda b, s: (b, 0, 0)),
            pl.BlockSpec(memory_space=pl.ANY),
        ]
        scratch_shapes += [pltpu.VMEM((GATHER_SLOTS, ts, d), _F32), pltpu.SemaphoreType.DMA((GATHER_SLOTS,))]
        args += [pos, pos, pos, mod_prev, y]
    return pl.pallas_call(
        functools.partial(_mix_kernel, fused_moe_gather=moe is not None),
        out_shape=jax.ShapeDtypeStruct(x.shape, _F32),
        grid=(bsz, n_s),
        in_specs=in_specs,
        out_specs=pl.BlockSpec((1, ts, d), lambda b, s: (b, s, 0)),
        scratch_shapes=scratch_shapes,
        compiler_params=pltpu.CompilerParams(
            dimension_semantics=("arbitrary", "arbitrary"), vmem_limit_bytes=VMEM_LIMIT_BYTES),
        name="token_mix",
    )(*args)


def _route_kernel(x_ref, mod_ref, g_ref, wr_ref, br_ref, xp_ref, meta_ref, cnt_ref, carry):
    tt = x_ref.shape[0]
    i = pl.program_id(0)

    @pl.when(i == 0)
    def _():
        carry[...] = jnp.zeros_like(carry)

    x = x_ref[...]
    sh2 = mod_ref[0, 3:4, :]
    sc2 = mod_ref[0, 4:5, :]
    ms = jnp.mean(x * x, axis=-1, keepdims=True)
    h = (x * lax.rsqrt(ms + EPS) * g_ref[...]) * (1.0 + sc2) + sh2

    h_hi, h_lo = _split2(h)
    w_hi, w_lo = _split2(wr_ref[...])
    w_cat = jnp.concatenate([w_hi, w_lo], axis=1)
    prod = _dot(h_hi, w_cat) + _dot(h_lo, w_cat)
    logits = prod[:, 0:LANES] + prod[:, LANES:2 * LANES] + br_ref[...]

    lane = lax.broadcasted_iota(jnp.int32, (tt, LANES), 1)
    lane_f = lane.astype(_F32)
    neg = -jnp.inf
    big = float(LANES)

    def first_argmax(vals, vmax):
        return jnp.min(jnp.where(vals == vmax, lane_f, big), axis=-1, keepdims=True).astype(jnp.int32)

    gl = jnp.where(lane < N_GROUPS, logits, neg)
    gmax = jnp.max(gl, axis=-1, keepdims=True)
    g_star = first_argmax(gl, gmax)
    p_sel = 1.0 / jnp.sum(jnp.exp(gl - gmax), axis=-1, keepdims=True)
    base = N_GROUPS + EXPERTS_PER_GROUP * g_star
    el = jnp.where((lane >= base) & (lane < base + EXPERTS_PER_GROUP), logits, neg)
    v0 = jnp.max(el, axis=-1, keepdims=True)
    i0 = first_argmax(el, v0)
    el2 = jnp.where(lane == i0, neg, el)
    v1 = jnp.max(el2, axis=-1, keepdims=True)
    i1 = first_argmax(el2, v1)
    ex = jnp.exp(v1 - v0)
    wt0 = p_sel / (1.0 + ex)
    wt1 = p_sel * ex / (1.0 + ex)
    e0 = i0 - base
    e1 = i1 - base
    e_lo = jnp.minimum(e0, e1)
    e_hi = jnp.maximum(e0, e1)
    w_lo = jnp.where(e0 < e1, wt0, wt1)
    w_hi = jnp.where(e0 < e1, wt1, wt0)
    pair = (e_lo * (2 * EXPERTS_PER_GROUP - 1 - e_lo)) // 2 + (e_hi - e_lo - 1)
    bucket = g_star * N_PAIRS + pair

    onehot = lane == bucket
    onehot_f = jnp.where(onehot, 1.0, 0.0)
    ri = lax.broadcasted_iota(jnp.int32, (tt, tt), 0)
    ci = lax.broadcasted_iota(jnp.int32, (tt, tt), 1)
    strict = jnp.where(ci < ri, 1.0, 0.0).astype(_BF16)
    prefix = _dot(strict, onehot_f.astype(_BF16)) + carry[0:1, :]
    rank = jnp.sum(jnp.where(onehot, prefix, 0.0), axis=-1, keepdims=True)
    carry[...] = carry[...] + jnp.sum(onehot_f, axis=0, keepdims=True)
    cnt_ref[...] = carry[...]

    meta = jnp.where(lane == 0, bucket.astype(_F32), jnp.where(lane == 1, rank, 0.0))
    meta_ref[...] = meta.T[0:SUBLANES, :]

    xp_ref[:, 0:D_MODEL] = h
    xp_ref[:, D_MODEL:ROW_W] = jnp.where(lane == 0, w_lo, jnp.where(lane == 1, w_hi, 0.0))


def _route(x2d, mod, g_ffn, w_r, b_r, layer, seq):
    n_tok, d = x2d.shape
    tt = ROUTE_TT
    tiles_per_seq = seq // tt
    per_layer = lambda shape: pl.BlockSpec((None,) + shape, lambda i: (layer, 0, 0))
    return pl.pallas_call(
        _route_kernel,
        out_shape=(
            jax.ShapeDtypeStruct((n_tok, ROW_W), _F32),
            jax.ShapeDtypeStruct((SUBLANES, n_tok), _F32),
            jax.ShapeDtypeStruct((SUBLANES, LANES), _F32),
        ),
        grid=(n_tok // tt,),
        in_specs=[
            pl.BlockSpec((tt, d), lambda i: (i, 0)),
            pl.BlockSpec((1, 6, d), lambda i: (i // tiles_per_seq, 0, 0)),
            per_layer((1, d)),
            per_layer((d, LANES)),
            per_layer((1, LANES)),
        ],
        out_specs=(
            pl.BlockSpec((tt, ROW_W), lambda i: (i, 0)),
            pl.BlockSpec((SUBLANES, tt), lambda i: (0, i)),
            pl.BlockSpec((SUBLANES, LANES), lambda i: (0, 0)),
        ),
        scratch_shapes=[pltpu.VMEM((SUBLANES, LANES), _F32)],
        compiler_params=pltpu.CompilerParams(
            dimension_semantics=("arbitrary",), vmem_limit_bytes=VMEM_LIMIT_BYTES),
        name="moe_route",
    )(x2d, mod, g_ffn, w_r, b_r)


def _permute_kernel(fill_ref, pos_ref, xp_ref, xs_hbm, zbuf, sem, zsem):
    tb = pos_ref.shape[0]

    @pl.when(pl.program_id(0) == 0)
    def _():
        zbuf[...] = jnp.zeros_like(zbuf)
        n_used = fill_ref[2 * N_BUCKETS]
        n_tiles = xs_hbm.shape[0] // FFN_TM
        for wait in (False, True):
            for b in range(N_BUCKETS):
                for row0, cond in ((fill_ref[b], fill_ref[N_BUCKETS + b] > 0),
                                   ((n_used + b) * FFN_TM, n_used + b < n_tiles)):
                    @pl.when(cond)
                    def _():
                        dst = xs_hbm.at[pl.ds(pl.multiple_of(row0, FFN_TM), FFN_TM)]
                        fill = pltpu.make_async_copy(zbuf, dst, zsem)
                        fill.wait() if wait else fill.start()

    for j in range(tb):
        pltpu.make_async_copy(xp_ref.at[pl.ds(j, 1)], xs_hbm.at[pl.ds(pos_ref[j], 1)], sem).start(priority=j % 2)
    pltpu.make_async_copy(xp_ref, xs_hbm.at[pl.ds(0, tb)], sem).wait()


def _permute(fill_starts, pos, xp, n_rows):
    n_tok = xp.shape[0]
    tb = PERM_TB
    return pl.pallas_call(
        _permute_kernel,
        out_shape=jax.ShapeDtypeStruct((n_rows, ROW_W), _F32),
        grid_spec=pltpu.PrefetchScalarGridSpec(
            num_scalar_prefetch=1,
            grid=(n_tok // tb,),
            in_specs=[
                pl.BlockSpec((tb,), lambda i, fs: (i,), memory_space=pltpu.SMEM),
                pl.BlockSpec((tb, ROW_W), lambda i, fs: (i, 0)),
            ],
            out_specs=pl.BlockSpec(memory_space=pl.ANY),
            scratch_shapes=[pltpu.VMEM((FFN_TM, ROW_W), _F32), pltpu.SemaphoreType.DMA,
                            pltpu.SemaphoreType.DMA],
        ),
        compiler_params=pltpu.CompilerParams(
            dimension_semantics=("arbitrary",), vmem_limit_bytes=VMEM_LIMIT_BYTES),
        name="moe_permute",
    )(fill_starts, pos, xp)


def _ffn_kernel(tg_ref, tlo_ref, thi_ref, clo_ref, chi_ref, ng_ref, nlo_ref, nhi_ref, plo_ref, phi_ref, nused_ref,
                xs_ref, wg_hbm, wu_hbm, wd_hbm, y_ref, sg, su, sd, agu, ad, sem, *, layer):
    i = pl.program_id(0)

    def weight_copies(slot, g, e):
        return (pltpu.make_async_copy(wg_hbm.at[layer, g, e], sg.at[slot], sem.at[slot, 0]),
                pltpu.make_async_copy(wu_hbm.at[layer, g, e], su.at[slot], sem.at[slot, 1]),
                pltpu.make_async_copy(wd_hbm.at[layer, g, e], sd.at[slot], sem.at[slot, 2]))

    @pl.when(i == 0)
    def _():
        for cp in weight_copies(0, tg_ref[0], tlo_ref[0]) + weight_copies(1, tg_ref[0], thi_ref[0]):
            cp.start()

    for slot, changed_ref, e_ref in ((0, clo_ref, tlo_ref), (1, chi_ref, thi_ref)):
        @pl.when(changed_ref[i] == 1)
        def _():
            for cp in weight_copies(slot, tg_ref[i], e_ref[i]):
                cp.wait()
            agu[slot, :, 0:D_EXPERT] = sg[slot].astype(_BF16)
            agu[slot, :, D_EXPERT:2 * D_EXPERT] = su[slot].astype(_BF16)
            ad[slot] = sd[slot].astype(_BF16)

    for slot, prefetch_ref, e_ref in ((0, plo_ref, nlo_ref), (1, phi_ref, nhi_ref)):
        @pl.when(prefetch_ref[i] == 1)
        def _():
            for cp in weight_copies(slot, ng_ref[i], e_ref[i]):
                cp.start()

    @pl.when(i < nused_ref[0])
    def _():
        xb = xs_ref[:, 0:D_MODEL].astype(_BF16)
        info = xs_ref[:, D_MODEL:ROW_W]

        def expert(slot):
            gate_up = _dot(xb, agu[slot])
            hid = _silu(gate_up[:, 0:D_EXPERT]) * gate_up[:, D_EXPERT:2 * D_EXPERT] * info[:, slot:slot + 1]
            return _dot(hid.astype(_BF16), ad[slot])

        y_ref[...] = expert(0) + expert(1)

    @pl.when(i >= nused_ref[0])
    def _():
        y_ref[...] = jnp.zeros_like(y_ref)


def _expert_ffn(layer, plan, xs, w_gate, w_up, w_down):
    n_rows = xs.shape[0]
    tm = FFN_TM
    n_tiles = n_rows // tm

    def row_map(i, *prefetch):
        n_used = prefetch[-1]
        return (jnp.minimum(i, n_used[0] - 1), 0)

    hbm = pl.BlockSpec(memory_space=pl.ANY)
    return pl.pallas_call(
        functools.partial(_ffn_kernel, layer=layer),
        out_shape=jax.ShapeDtypeStruct((n_rows, D_MODEL), _F32),
        grid_spec=pltpu.PrefetchScalarGridSpec(
            num_scalar_prefetch=len(plan),
            grid=(n_tiles,),
            in_specs=[pl.BlockSpec((tm, ROW_W), row_map), hbm, hbm, hbm],
            out_specs=pl.BlockSpec((tm, D_MODEL), lambda i, *prefetch: (i, 0)),
            scratch_shapes=[
                pltpu.VMEM((2, D_MODEL, D_EXPERT), _F32), pltpu.VMEM((2, D_MODEL, D_EXPERT), _F32),
                pltpu.VMEM((2, D_EXPERT, D_MODEL), _F32),
                pltpu.VMEM((2, D_MODEL, 2 * D_EXPERT), _BF16), pltpu.VMEM((2, D_EXPERT, D_MODEL), _BF16),
                pltpu.SemaphoreType.DMA((2, 3)),
            ],
        ),
        compiler_params=pltpu.CompilerParams(
            dimension_semantics=("arbitrary",), vmem_limit_bytes=VMEM_LIMIT_BYTES),
        name="moe_ffn",
    )(*plan, xs, w_gate, w_up, w_down)


def _combine_kernel(pos_ref, x_ref, mod_ref, gfin_ref, y_hbm, o_ref, ybuf, sem):
    tc = x_ref.shape[0]
    for j in range(tc):
        pltpu.make_async_copy(y_hbm.at[pl.ds(pos_ref[j], 1)], ybuf.at[pl.ds(j, 1)], sem).start(priority=j % 2)
    pltpu.make_async_copy(y_hbm.at[pl.ds(0, tc)], ybuf, sem).wait()

    gt2 = mod_ref[0, 5:6, :]
    out = x_ref[...] + gt2 * ybuf[...]
    ms = jnp.mean(out * out, axis=-1, keepdims=True)
    o_ref[...] = out * lax.rsqrt(ms + EPS) * gfin_ref[...]


def _combine(pos, x2d, mod, g_final, y, seq):
    n_tok, d = x2d.shape
    tc = COMB_TC
    tiles_per_seq = seq // tc
    return pl.pallas_call(
        _combine_kernel,
        out_shape=jax.ShapeDtypeStruct((n_tok, d), _F32),
        grid=(n_tok // tc,),
        in_specs=[
            pl.BlockSpec((tc,), lambda i: (i,), memory_space=pltpu.SMEM),
            pl.BlockSpec((tc, d), lambda i: (i, 0)),
            pl.BlockSpec((1, 6, d), lambda i: (i // tiles_per_seq, 0, 0)),
            pl.BlockSpec((1, d), lambda i: (0, 0)),
            pl.BlockSpec(memory_space=pl.ANY),
        ],
        out_specs=pl.BlockSpec((tc, d), lambda i: (i, 0)),
        scratch_shapes=[pltpu.VMEM((tc, d), _F32), pltpu.SemaphoreType.DMA],
        compiler_params=pltpu.CompilerParams(
            dimension_semantics=("arbitrary",), vmem_limit_bytes=VMEM_LIMIT_BYTES),
        name="moe_combine",
    )(pos, x2d, mod, g_final, y)


def _pad_heads(w, heads, width, padded):
    lead = w.shape[:-1]
    w = w.reshape(lead + (heads, width))
    w = jnp.pad(w, [(0, 0)] * len(lead) + [(0, 0), (0, padded - width)])
    return w.reshape(lead + (heads * padded,))


def _block_diag(w):
    n_layers, n, bw, _ = w.shape
    eye = jnp.eye(n, dtype=w.dtype)
    return (eye[None, :, None, :, None] * w[:, :, :, None, :]).reshape(n_layers, n * bw, n * bw)


def _prep_params(w_in, conv_dw_w, conv_dw_b, conv_ln_g, conv_ln_b, lru_conv_w, lru_conv_b, lru_w_a,
                 lru_b_a, lru_w_i, lru_b_i, lru_lam, gla_w_gate, gla_b_gate, gla_norm_g, w_out, g_mix):
    sizes = [CONV_W, CONV_W, LRU_W, LRU_W, GLA_HEADS * GLA_DK, GLA_HEADS * GLA_DK, GLA_V, GLA_RANK, GLA_V]
    cv_v, cv_g, lr_x, lr_y, q, k, v, g_lr, og = jnp.split(w_in, np.cumsum(sizes)[:-1].tolist(), axis=-1)
    w_in_p = jnp.concatenate([
        cv_v, cv_g, lr_x, lr_y,
        _pad_heads(q, GLA_HEADS, GLA_DK, DK_PAD), _pad_heads(k, GLA_HEADS, GLA_DK, DK_PAD),
        _pad_heads(v, GLA_HEADS, GLA_DV, DV_PAD),
        jnp.pad(g_lr, ((0, 0), (0, 0), (0, LANES - GLA_RANK))),
        _pad_heads(og, GLA_HEADS, GLA_DV, DV_PAD)], axis=-1).astype(_BF16)
    n_layers = w_in.shape[0]
    wo_o = w_out[:, CONV_W + LRU_W:].reshape(n_layers, GLA_HEADS, GLA_DV, D_MODEL)
    wo_o = jnp.pad(wo_o, ((0, 0), (0, 0), (0, DV_PAD - GLA_DV), (0, 0))).reshape(n_layers, V_W, D_MODEL)
    w_out_p = jnp.concatenate([w_out[:, :CONV_W + LRU_W], wo_o], axis=1).astype(_BF16)
    w_gate_p = jnp.pad(_pad_heads(gla_w_gate, GLA_HEADS, GLA_DK, DK_PAD),
                       ((0, 0), (0, LANES - GLA_RANK), (0, 0))).astype(_BF16)
    row = lambda v: v[:, None, :]
    return [
        row(g_mix),
        w_in_p,
        jnp.pad(conv_dw_w, ((0, 0), (0, CONV_HIST - CONV_K), (0, 0))),
        row(conv_dw_b), row(conv_ln_g), row(conv_ln_b),
        lru_conv_w, row(lru_conv_b),
        jnp.concatenate([_block_diag(lru_w_a), _block_diag(lru_w_i)], axis=2).astype(_BF16),
        row(jnp.concatenate([lru_b_a, lru_b_i], axis=1)),
        row(lru_lam),
        w_gate_p,
        row(_pad_heads(gla_b_gate, GLA_HEADS, GLA_DK, DK_PAD)),
        row(_pad_heads(gla_norm_g, GLA_HEADS, GLA_DV, DV_PAD)),
        w_out_p,
    ]


def _bucket_layout(counts, n_tiles):
    tm = FFN_TM
    counts = counts.astype(jnp.int32)
    tiles = (counts + tm - 1) // tm
    upto = jnp.arange(N_BUCKETS)[:, None] <= jnp.arange(N_BUCKETS)[None, :]
    tile_end = jnp.sum(jnp.where(upto, tiles[:, None], 0), axis=0)
    starts = ((tile_end - tiles) * tm).astype(jnp.int32)
    n_used = tile_end[-1]
    tile_idx = jnp.minimum(jnp.arange(n_tiles, dtype=jnp.int32), n_used - 1)

    def bucket_of(t):
        return jnp.sum((t[:, None] >= tile_end[None, :]).astype(jnp.int32), axis=1)

    def experts(b):
        pair = b % N_PAIRS
        lo = (pair >= 3).astype(jnp.int32) + (pair >= 5).astype(jnp.int32)
        hi = jnp.where(pair < 3, pair + 1, jnp.where(pair < 5, pair - 1, 3))
        return b // N_PAIRS, lo, hi

    tile_bucket = bucket_of(tile_idx)
    prev_bucket = jnp.where(tile_idx >= 1, bucket_of(tile_idx - 1), -1)
    g, lo, hi = experts(tile_bucket)
    pg, plo, phi = experts(prev_bucket)
    first = (tile_bucket != prev_bucket) & (jnp.arange(n_tiles) < n_used)
    fresh = prev_bucket < 0
    changed_lo = first & (fresh | (g != pg) | (lo != plo))
    changed_hi = first & (fresh | (g != pg) | (hi != phi))
    next_first_tile = jnp.min(jnp.where(tile_end[None, :] > tile_idx[:, None], tile_end[None, :], n_tiles), axis=1)
    has_next = next_first_tile < n_used
    ng, nlo, nhi = experts(bucket_of(jnp.minimum(next_first_tile, n_used - 1)))
    prefetch_lo = first & has_next & ((ng != g) | (nlo != lo))
    prefetch_hi = first & has_next & ((ng != g) | (nhi != hi))
    as_i32 = lambda v: v.astype(jnp.int32)
    ffn_plan = (g, lo, hi, as_i32(changed_lo), as_i32(changed_hi), ng, nlo, nhi,
                as_i32(prefetch_lo), as_i32(prefetch_hi), n_used.reshape(1).astype(jnp.int32))
    fill = jnp.concatenate([(jnp.maximum(tile_end - 1, 0) * tm).astype(jnp.int32), tiles, n_used[None]])
    return starts, fill, ffn_plan


def kernel(x, c, w_ada, b_ada, g_mix, w_in, conv_dw_w, conv_dw_b, conv_ln_g, conv_ln_b, lru_conv_w,
           lru_conv_b, lru_w_a, lru_b_a, lru_w_i, lru_b_i, lru_lam, gla_w_gate, gla_b_gate, gla_norm_g,
           w_out, g_ffn, w_route_group, b_route_group, w_route_expert, b_route_expert, w_gate, w_up,
           w_down, g_final):
    bsz, seq, d = x.shape
    n_layers = w_ada.shape[0]
    n_tok = bsz * seq
    assert d == D_MODEL and seq % max(MIX_TS, ROUTE_TT, COMB_TC) == 0 and MIX_TS % GLA_BLOCK == 0
    assert n_tok % PERM_TB == 0
    n_tiles = n_tok // FFN_TM + N_BUCKETS
    n_rows = n_tiles * FFN_TM

    mod_all = _modulation(c, w_ada, b_ada).reshape(n_layers, bsz, 6, d)

    params = _prep_params(w_in, conv_dw_w, conv_dw_b, conv_ln_g, conv_ln_b, lru_conv_w, lru_conv_b,
                          lru_w_a, lru_b_a, lru_w_i, lru_b_i, lru_lam, gla_w_gate, gla_b_gate, gla_norm_g,
                          w_out, g_mix)
    n_experts = N_GROUPS * EXPERTS_PER_GROUP
    w_r = jnp.concatenate(
        [w_route_group, w_route_expert.transpose(0, 2, 1, 3).reshape(n_layers, d, n_experts)], axis=2)
    w_r = jnp.pad(w_r, ((0, 0), (0, 0), (0, LANES - N_GROUPS - n_experts)))
    b_r = jnp.concatenate([b_route_group, b_route_expert.reshape(n_layers, n_experts)], axis=1)
    b_r = jnp.pad(b_r, ((0, 0), (0, LANES - N_GROUPS - n_experts)))[:, None, :]
    pending_moe = None
    for l in range(n_layers):
        mod = mod_all[l]
        x = _token_mix(x, mod, params, l, pending_moe)

        x2d = x.reshape(n_tok, d)
        xp, meta, counts = _route(x2d, mod, g_ffn[:, None, :], w_r, b_r, l, seq)
        bucket = meta[0].astype(jnp.int32)
        rank = meta[1].astype(jnp.int32)
        starts, fill, ffn_plan = _bucket_layout(counts[0, :N_BUCKETS], n_tiles)
        in_bucket = bucket[:, None] == jnp.arange(N_BUCKETS, dtype=jnp.int32)[None, :]
        pos = rank + jnp.sum(jnp.where(in_bucket, starts[None, :], 0), axis=1)

        xs = _permute(fill, pos, xp, n_rows)
        y = _expert_ffn(l, ffn_plan, xs, w_gate, w_up, w_down)
        pending_moe = (pos, mod, y)
    return _combine(pos, x2d, mod, g_final[None, :], y, seq).reshape(bsz, seq, d)
```

```python
import functools

import jax
import jax.numpy as jnp
import numpy as np
from jax import lax
from jax.experimental import pallas as pl
from jax.experimental.pallas import tpu as pltpu

D_MODEL = 1024
CONV_W = 256
LRU_W = 384
GLA_V = 384
CONV_K = 31
LRU_CONV_K = 4
LRU_BLOCKS = 6
LRU_BW = 64
LRU_C = 8.0
GLA_HEADS = 4
GLA_DV = 96
GLA_DK = 48
GLA_RANK = 16
GLA_TAU = 16.0
GLA_CHUNK = 64
N_GROUPS = 4
EXPERTS_PER_GROUP = 4
D_EXPERT = 512
EPS = 1e-6

LANES = 128
SUBLANES = 8
VMEM_LIMIT_BYTES = 56 * 1024 * 1024

DK_PAD = 64
DV_PAD = 128
QK_W = GLA_HEADS * DK_PAD
V_W = GLA_HEADS * DV_PAD
OFF_CVV = 0
OFF_CVG = OFF_CVV + CONV_W
OFF_LRX = OFF_CVG + CONV_W
OFF_LRY = OFF_LRX + LRU_W
OFF_Q = OFF_LRY + LRU_W
OFF_K = OFF_Q + QK_W
OFF_V = OFF_K + QK_W
OFF_GLR = OFF_V + V_W
OFF_OG = OFF_GLR + LANES
N_IN = OFF_OG + V_W
MIX_W = CONV_W + LRU_W + V_W

CONV_HIST = 32
LRU_HIST = 8
GLA_BLOCK = 256

N_PAIRS = 6
N_BUCKETS = N_GROUPS * N_PAIRS
ROW_W = D_MODEL + LANES

MIX_TS = 512
ROUTE_TT = 512
FFN_TM = 256
PERM_TB = 2048
COMB_TC = 2048
DMA_UNROLL = 8
GATHER_SLOTS = 3

_F32 = jnp.float32
_BF16 = jnp.bfloat16


def _sigmoid(x):
    return 1.0 / (1.0 + jnp.exp(-x))


def _silu(x):
    return x * _sigmoid(x)


def _dot(a, b):
    return jnp.dot(a, b, preferred_element_type=_F32)


def _dot_nt(a, b):
    return lax.dot_general(a, b, (((1,), (1,)), ((), ())), preferred_element_type=_F32)


def _split2(x):
    hi = x.astype(_BF16)
    return hi, (x - hi.astype(_F32)).astype(_BF16)


def _mod_kernel(c_ref, w_ref, b_ref, o_ref):
    c_act = _silu(c_ref[...])
    o_ref[0] = _dot(c_act.astype(_BF16), w_ref[0].astype(_BF16)) + b_ref[0]


def _modulation(c, w_ada, b_ada):
    n_layers, d, n = w_ada.shape
    bsz = c.shape[0]
    tn = 1536
    return pl.pallas_call(
        _mod_kernel,
        out_shape=jax.ShapeDtypeStruct((n_layers, bsz, n), _F32),
        grid=(n_layers, n // tn),
        in_specs=[
            pl.BlockSpec((bsz, d), lambda l, j: (0, 0)),
            pl.BlockSpec((1, d, tn), lambda l, j: (l, 0, j)),
            pl.BlockSpec((1, 1, tn), lambda l, j: (l, 0, j)),
        ],
        out_specs=pl.BlockSpec((1, bsz, tn), lambda l, j: (l, 0, j)),
        compiler_params=pltpu.CompilerParams(
            dimension_semantics=("arbitrary", "arbitrary"), vmem_limit_bytes=VMEM_LIMIT_BYTES),
        name="adaln_mod",
    )(c, w_ada, b_ada.reshape(n_layers, 1, n))


def _mix_kernel(x_ref, mod_ref, gmix_ref, win_ref, cw_ref, cb_ref, lng_ref, lnb_ref,
                lcw_ref, lcb_ref, wai_ref, bai_ref, lam_ref, wgate_ref, bgate_ref, gnorm_ref,
                wout_ref, *rest, fused_moe_gather):
    if fused_moe_gather:
        pos_cur, pos_nxt, pos_ahd, modp_ref, y_hbm, o_ref = rest[:6]
        ybuf, gsem = rest[-2:]
        rest = rest[6:-2]
    else:
        o_ref, rest = rest[0], rest[1:]
    ubuf, lbuf, hcar, state, tri_ref, tria_ref, csel_ref, causal_ref, hdiag_ref = rest
    ts = x_ref.shape[1]
    s_idx = pl.program_id(1)

    @pl.when(s_idx == 0)
    def _():
        ubuf[0:CONV_HIST, :] = jnp.zeros((CONV_HIST, CONV_W), _F32)
        lbuf[0:LRU_HIST, :] = jnp.zeros((LRU_HIST, LRU_W), _F32)
        hcar[...] = jnp.zeros_like(hcar)
        state[...] = jnp.zeros_like(state)
        ri = lax.broadcasted_iota(jnp.int32, (GLA_BLOCK, GLA_BLOCK), 0)
        ci = lax.broadcasted_iota(jnp.int32, (GLA_BLOCK, GLA_BLOCK), 1)
        same_chunk = ri // GLA_CHUNK == ci // GLA_CHUNK
        causal_f = jnp.where(same_chunk & (ci <= ri), 1.0, 0.0)
        causal_ref[...] = causal_f
        tri_ref[...] = causal_f.astype(_BF16)
        tria_ref[...] = jnp.where(same_chunk & (ci > ri), 1.0, 0.0).astype(_BF16)
        sel_r = lax.broadcasted_iota(jnp.int32, (LANES, GLA_BLOCK), 0)
        sel_c = lax.broadcasted_iota(jnp.int32, (LANES, GLA_BLOCK), 1)
        csel_ref[...] = jnp.where(sel_r == sel_c // GLA_CHUNK, 1.0, 0.0).astype(_BF16)
        st_row_head = lax.broadcasted_iota(jnp.int32, (QK_W, V_W), 0) // DK_PAD
        st_col_head = lax.broadcasted_iota(jnp.int32, (QK_W, V_W), 1) // DV_PAD
        hdiag_ref[...] = jnp.where(st_row_head == st_col_head, 1.0, 0.0)

    x = x_ref[0]
    if fused_moe_gather:
        step = pl.program_id(0) * pl.num_programs(1) + s_idx
        n_steps = pl.num_programs(0) * pl.num_programs(1)
        slot = step % GATHER_SLOTS

        def row_copy(pos_ref, j, to_slot):
            return pltpu.make_async_copy(y_hbm.at[pl.ds(pos_ref[j], 1)], ybuf.at[to_slot, pl.ds(j, 1)],
                                         gsem.at[to_slot])

        def wait_rows(of_slot):
            pltpu.make_async_copy(y_hbm.at[pl.ds(0, ts)], ybuf.at[of_slot], gsem.at[of_slot]).wait()

        @pl.when(step == 0)
        def _():
            def issue(j, carry):
                row_copy(pos_cur, j, 0).start()
                row_copy(pos_nxt, j, 1).start()
                return carry
            lax.fori_loop(0, ts, issue, 0, unroll=DMA_UNROLL)

        wait_rows(slot)
        x = x + modp_ref[0, 5:6, :] * ybuf[slot]

        ahead_slot = (step + GATHER_SLOTS - 1) % GATHER_SLOTS
        for j in range(ts):
            row_copy(pos_ahd, j, ahead_slot).start()

    sh1 = mod_ref[0, 0:1, :]
    gt1 = mod_ref[0, 2:3, :]
    scale = gmix_ref[...] * (1.0 + mod_ref[0, 1:2, :])
    ms = jnp.mean(x * x, axis=-1, keepdims=True)
    hb = (x * lax.rsqrt(ms + EPS) * scale + sh1).astype(_BF16)

    def proj(off, width):
        return _dot(hb, win_ref[:, off:off + width])

    out = {}
    _run_alternately(
        _conv_lru_steps(out, proj, cw_ref, cb_ref, lng_ref, lnb_ref, lcw_ref, lcb_ref, wai_ref, bai_ref,
                        lam_ref, ubuf, lbuf, hcar, ts),
        _gla_steps(out, proj, wgate_ref, bgate_ref, gnorm_ref, state, tri_ref, tria_ref, csel_ref,
                   causal_ref, hdiag_ref, ts))

    mixed = (_dot(out["o_g"].astype(_BF16), wout_ref[CONV_W + LRU_W:MIX_W, :])
             + _dot(out["r_out"].astype(_BF16), wout_ref[CONV_W:CONV_W + LRU_W, :])
             + _dot(out["u_out"].astype(_BF16), wout_ref[0:CONV_W, :]))
    o_ref[0] = x + gt1 * mixed

    if fused_moe_gather:
        @pl.when(step == n_steps - 1)
        def _():
            wait_rows((step + 1) % GATHER_SLOTS)
            wait_rows((step + 2) % GATHER_SLOTS)


def _run_alternately(*step_generators):
    clock = [0] * len(step_generators)
    live = list(range(len(step_generators)))
    while live:
        i = min(live, key=lambda j: clock[j])
        try:
            clock[i] += next(step_generators[i]) * (1 if i else 0.5)
        except StopIteration:
            live.remove(i)


def _conv_lru_steps(out, proj, cw_ref, cb_ref, lng_ref, lnb_ref, lcw_ref, lcb_ref, wai_ref, bai_ref,
                    lam_ref, ubuf, lbuf, hcar, ts):
    u = proj(OFF_CVV, CONV_W) * _sigmoid(proj(OFF_CVG, CONV_W))
    ubuf[CONV_HIST:CONV_HIST + ts, :] = u
    yield 500
    lbuf[LRU_HIST:LRU_HIST + ts, :] = proj(OFF_LRX, LRU_W)
    xb = jnp.broadcast_to(lcb_ref[...], (ts, LRU_W))
    for k in range(LRU_CONV_K):
        off = LRU_HIST - (LRU_CONV_K - 1) + k
        xb = xb + lcw_ref[k:k + 1, :] * lbuf[off:off + ts, :]
    lbuf[0:LRU_HIST, :] = lbuf[ts:ts + LRU_HIST, :]
    gates = _dot(xb.astype(_BF16), wai_ref[...]) + bai_ref[...]
    yield 800

    acc = jnp.broadcast_to(cb_ref[...], (ts, CONV_W))
    for r in range(SUBLANES):
        z = None
        for a_blk in range((CONV_K - 1 - r) // SUBLANES + 1):
            k = CONV_K - 1 - (SUBLANES * a_blk + r)
            off = CONV_HIST - SUBLANES * (a_blk + 1)
            term = cw_ref[k:k + 1, :] * ubuf[off:off + ts + SUBLANES, :]
            z = term if z is None else z + term
        acc = acc + z[SUBLANES - r:SUBLANES - r + ts]
        yield 200
    ubuf[0:CONV_HIST, :] = ubuf[ts:ts + CONV_HIST, :]
    mu = jnp.mean(acc, axis=-1, keepdims=True)
    cen = acc - mu
    var = jnp.mean(cen * cen, axis=-1, keepdims=True)
    out["u_out"] = _silu(cen * lax.rsqrt(var + EPS) * lng_ref[...] + lnb_ref[...])
    yield 250

    r_gate = _sigmoid(gates[:, 0:LRU_W])
    i_gate = _sigmoid(gates[:, LRU_W:2 * LRU_W])
    lam = lam_ref[...]
    softplus_neg_lam = jnp.maximum(-lam, 0.0) + jnp.log1p(jnp.exp(-jnp.abs(lam)))
    log_a = (-LRU_C) * r_gate * softplus_neg_lam
    a = jnp.exp(log_a)
    mult = jnp.sqrt(jnp.tanh(-log_a) * (a * a + 1.0))
    u_in = mult * (i_gate * xb)
    yield 600
    row = lax.broadcasted_iota(jnp.int32, a.shape, 0) % SUBLANES
    d = 1
    while d < SUBLANES:
        keep = row >= d
        a_sh = jnp.where(keep, pltpu.roll(a, d, axis=0), 1.0)
        u_sh = jnp.where(keep, pltpu.roll(u_in, d, axis=0), 0.0)
        u_in = u_in + a * u_sh
        a = a * a_sh
        d *= 2
        yield 200
    groups = []
    h_prev = hcar[0:1, :]
    for g in range(ts // SUBLANES):
        r0 = g * SUBLANES
        h_g = u_in[r0:r0 + SUBLANES] + a[r0:r0 + SUBLANES] * h_prev
        groups.append(h_g)
        h_prev = h_g[SUBLANES - 1:SUBLANES]
        if g % SUBLANES == SUBLANES - 1:
            yield 50
    h_lru = jnp.concatenate(groups, axis=0)
    hcar[...] = jnp.broadcast_to(h_prev, hcar.shape)
    out["r_out"] = h_lru * jax.nn.gelu(proj(OFF_LRY, LRU_W), approximate=True)


def _gla_steps(out, proj, wgate_ref, bgate_ref, gnorm_ref, state, tri_ref, tria_ref, csel_ref, causal_ref,
               hdiag_ref, ts):
    zg = proj(OFF_GLR, LANES)
    glog = _dot(zg.astype(_BF16), wgate_ref[...]) + bgate_ref[...]
    lg = (jnp.minimum(glog, 0.0) - jnp.log1p(jnp.exp(-jnp.abs(glog)))) * (1.0 / GLA_TAU)
    yield 300
    zq = proj(OFF_Q, QK_W) * (GLA_DK ** -0.5)
    zk = proj(OFF_K, QK_W)
    yield 500
    zv = proj(OFF_V, V_W)
    yield 500

    nblk = ts // GLA_BLOCK
    cpb = GLA_BLOCK // GLA_CHUNK
    tri = tri_ref[...]
    tri_after = tria_ref[...]
    chunk_sel = csel_ref[...]
    causal = causal_ref[...] > 0.5
    qk_lane_head = lax.broadcasted_iota(jnp.int32, (1, QK_W), 1) // DK_PAD
    t_lane_chunk = lax.broadcasted_iota(jnp.int32, (1, GLA_BLOCK), 1) // GLA_CHUNK

    o_blocks = []
    for blk in range(nblk):
        r0 = blk * GLA_BLOCK
        lg_b = lg[r0:r0 + GLA_BLOCK]
        p_hi, p_lo = _split2(lg_b)
        b = _dot(tri, p_hi) + _dot(tri, p_lo)
        b_rest = _dot(tri_after, p_hi) + _dot(tri_after, p_lo)
        b_tot = _dot(chunk_sel, p_hi) + _dot(chunk_sel, p_lo)
        yield 500
        q_in =zq[r0:r0 + GLA_BLOCK] * jnp.exp(b)
        k_blk = zk[r0:r0 + GLA_BLOCK]
        k_in = (k_blk * jnp.exp(-b)).astype(_BF16)
        k_out_t = (k_blk * jnp.exp(b_rest)).T.astype(_BF16)
        v_b = zv[r0:r0 + GLA_BLOCK].astype(_BF16)
        q_in_b = q_in.astype(_BF16)
        decay_cols = jnp.exp(b_tot).T

        o_heads = []
        for hd in range(GLA_HEADS):
            q_h = jnp.where(qk_lane_head == hd, q_in_b, jnp.zeros_like(q_in_b))
            sc = jnp.where(causal, _dot_nt(q_h, k_in), 0.0)
            o_heads.append(_dot(sc.astype(_BF16), v_b[:, hd * DV_PAD:(hd + 1) * DV_PAD]))
            yield 300
        o_intra = jnp.concatenate(o_heads, axis=1)

        o_inter = []
        st = state[...]
        for c in range(cpb):
            c0 = c * GLA_CHUNK
            o_inter.append(_dot(q_in_b[c0:c0 + GLA_CHUNK], st.astype(_BF16)))
            kv = _dot(jnp.where(t_lane_chunk == c, k_out_t, jnp.zeros_like(k_out_t)), v_b)
            st = st * decay_cols[:, c:c + 1] + kv * hdiag_ref[...]
            yield 400
        state[...] = st
        o_blocks.append(o_intra + jnp.concatenate(o_inter, axis=0))
    o = o_blocks[0] if nblk == 1 else jnp.concatenate(o_blocks, axis=0)

    og = proj(OFF_OG, V_W)
    o_parts = []
    for hd in range(GLA_HEADS):
        o_h = o[:, hd * DV_PAD:(hd + 1) * DV_PAD]
        ms_h = jnp.sum(o_h * o_h, axis=-1, keepdims=True) * (1.0 / GLA_DV)
        o_parts.append(o_h * lax.rsqrt(ms_h + EPS))
    o_n = jnp.concatenate(o_parts, axis=1) * gnorm_ref[...]
    out["o_g"] = o_n * _silu(og)


def _token_mix(x, mod, params, layer, moe=None):
    bsz, seq, d = x.shape
    ts = MIX_TS
    n_s = seq // ts
    full = lambda shape: pl.BlockSpec((None,) + shape, lambda b, s: (layer,) + (0,) * len(shape))
    in_specs = [
        pl.BlockSpec((1, ts, d), lambda b, s: (b, s, 0)),
        pl.BlockSpec((1, 6, d), lambda b, s: (b, 0, 0)),
        full((1, d)),
        full((d, N_IN)),
        full((CONV_HIST, CONV_W)), full((1, CONV_W)), full((1, CONV_W)), full((1, CONV_W)),
        full((LRU_CONV_K, LRU_W)), full((1, LRU_W)),
        full((LRU_W, 2 * LRU_W)), full((1, 2 * LRU_W)), full((1, LRU_W)),
        full((LANES, QK_W)), full((1, QK_W)), full((1, V_W)),
        full((MIX_W, d)),
    ]
    scratch_shapes = [
        pltpu.VMEM((CONV_HIST + ts, CONV_W), _F32),
        pltpu.VMEM((LRU_HIST + ts, LRU_W), _F32),
        pltpu.VMEM((SUBLANES, LRU_W), _F32),
        pltpu.VMEM((QK_W, V_W), _F32),
        pltpu.VMEM((GLA_BLOCK, GLA_BLOCK), _BF16),
        pltpu.VMEM((GLA_BLOCK, GLA_BLOCK), _BF16),
        pltpu.VMEM((LANES, GLA_BLOCK), _BF16),
        pltpu.VMEM((GLA_BLOCK, GLA_BLOCK), _F32),
        pltpu.VMEM((QK_W, V_W), _F32),
    ]
    args = [x, mod] + list(params)
    if moe is not None:
        pos, mod_prev, y = moe
        last = bsz * n_s - 1
        in_specs += [
            pl.BlockSpec((ts,), lambda b, s: (b * n_s + s,), memory_space=pltpu.SMEM),
            pl.BlockSpec((ts,), lambda b, s: (jnp.minimum(b * n_s + s + 1, last),), memory_space=pltpu.SMEM),
            pl.BlockSpec((ts,), lambda b, s: (jnp.minimum(b * n_s + s + 2, last),), memory_space=pltpu.SMEM),
            pl.BlockSpec((1, 6, d), lambda b, s: (b, 0, 0)),
            pl.BlockSpec(memory_space=pl.ANY),
        ]
        scratch_shapes += [pltpu.VMEM((GATHER_SLOTS, ts, d), _F32), pltpu.SemaphoreType.DMA((GATHER_SLOTS,))]
        args += [pos, pos, pos, mod_prev, y]
    return pl.pallas_call(
        functools.partial(_mix_kernel, fused_moe_gather=moe is not None),
        out_shape=jax.ShapeDtypeStruct(x.shape, _F32),
        grid=(bsz, n_s),
        in_specs=in_specs,
        out_specs=pl.BlockSpec((1, ts, d), lambda b, s: (b, s, 0)),
        scratch_shapes=scratch_shapes,
        compiler_params=pltpu.CompilerParams(
            dimension_semantics=("arbitrary", "arbitrary"), vmem_limit_bytes=VMEM_LIMIT_BYTES),
        name="token_mix",
    )(*args)


def _route_kernel(x_ref, mod_ref, g_ref, wr_ref, br_ref, xp_ref, meta_ref, cnt_ref, carry, strict_ref):
    tt = x_ref.shape[0]
    i = pl.program_id(0)

    @pl.when(i == 0)
    def _():
        carry[...] = jnp.zeros_like(carry)
        ri = lax.broadcasted_iota(jnp.int32, (tt, tt), 0)
        ci = lax.broadcasted_iota(jnp.int32, (tt, tt), 1)
        strict_ref[...] = jnp.where(ci < ri, 1.0, 0.0).astype(_BF16)

    x = x_ref[...]
    sh2 = mod_ref[0, 3:4, :]
    sc2 = mod_ref[0, 4:5, :]
    ms = jnp.mean(x * x, axis=-1, keepdims=True)
    h = (x * lax.rsqrt(ms + EPS) * g_ref[...]) * (1.0 + sc2) + sh2

    h_hi, h_lo = _split2(h)
    w_hi, w_lo = _split2(wr_ref[...])
    w_cat = jnp.concatenate([w_hi, w_lo], axis=1)
    prod = _dot(h_hi, w_cat) + _dot(h_lo, w_cat)
    logits = prod[:, 0:LANES] + prod[:, LANES:2 * LANES] + br_ref[...]

    lane = lax.broadcasted_iota(jnp.int32, (tt, LANES), 1)
    lane_f = lane.astype(_F32)
    neg = -jnp.inf
    big = float(LANES)

    def first_argmax(vals, vmax):
        return jnp.min(jnp.where(vals == vmax, lane_f, big), axis=-1, keepdims=True).astype(jnp.int32)

    gl = jnp.where(lane < N_GROUPS, logits, neg)
    gmax = jnp.max(gl, axis=-1, keepdims=True)
    g_star = first_argmax(gl, gmax)
    p_sel = 1.0 / jnp.sum(jnp.exp(gl - gmax), axis=-1, keepdims=True)
    base = N_GROUPS + EXPERTS_PER_GROUP * g_star
    el = jnp.where((lane >= base) & (lane < base + EXPERTS_PER_GROUP), logits, neg)
    v0 = jnp.max(el, axis=-1, keepdims=True)
    i0 = first_argmax(el, v0)
    el2 = jnp.where(lane == i0, neg, el)
    v1 = jnp.max(el2, axis=-1, keepdims=True)
    i1 = first_argmax(el2, v1)
    ex = jnp.exp(v1 - v0)
    wt0 = p_sel / (1.0 + ex)
    wt1 = p_sel * ex / (1.0 + ex)
    e0 = i0 - base
    e1 = i1 - base
    e_lo = jnp.minimum(e0, e1)
    e_hi = jnp.maximum(e0, e1)
    w_lo = jnp.where(e0 < e1, wt0, wt1)
    w_hi = jnp.where(e0 < e1, wt1, wt0)
    pair = (e_lo * (2 * EXPERTS_PER_GROUP - 1 - e_lo)) // 2 + (e_hi - e_lo - 1)
    bucket = g_star * N_PAIRS + pair

    onehot = lane == bucket
    onehot_f = jnp.where(onehot, 1.0, 0.0)
    prefix = _dot(strict_ref[...], onehot_f.astype(_BF16)) + carry[0:1, :]
    rank = jnp.sum(jnp.where(onehot, prefix, 0.0), axis=-1, keepdims=True)
    carry[...] = carry[...] + jnp.sum(onehot_f, axis=0, keepdims=True)
    cnt_ref[...] = carry[...]

    meta = jnp.where(lane == 0, bucket.astype(_F32), jnp.where(lane == 1, rank, 0.0))
    meta_ref[...] = meta.T[0:SUBLANES, :]

    xp_ref[:, 0:D_MODEL] = h
    xp_ref[:, D_MODEL:ROW_W] = jnp.where(lane == 0, w_lo, jnp.where(lane == 1, w_hi, 0.0))


def _route(x2d, mod, g_ffn, w_r, b_r, layer, seq):
    n_tok, d = x2d.shape
    tt = ROUTE_TT
    tiles_per_seq = seq // tt
    per_layer = lambda shape: pl.BlockSpec((None,) + shape, lambda i: (layer, 0, 0))
    return pl.pallas_call(
        _route_kernel,
        out_shape=(
            jax.ShapeDtypeStruct((n_tok, ROW_W), _F32),
            jax.ShapeDtypeStruct((SUBLANES, n_tok), _F32),
            jax.ShapeDtypeStruct((SUBLANES, LANES), _F32),
        ),
        grid=(n_tok // tt,),
        in_specs=[
            pl.BlockSpec((tt, d), lambda i: (i, 0)),
            pl.BlockSpec((1, 6, d), lambda i: (i // tiles_per_seq, 0, 0)),
            per_layer((1, d)),
            per_layer((d, LANES)),
            per_layer((1, LANES)),
        ],
        out_specs=(
            pl.BlockSpec((tt, ROW_W), lambda i: (i, 0)),
            pl.BlockSpec((SUBLANES, tt), lambda i: (0, i)),
            pl.BlockSpec((SUBLANES, LANES), lambda i: (0, 0)),
        ),
        scratch_shapes=[pltpu.VMEM((SUBLANES, LANES), _F32), pltpu.VMEM((tt, tt), _BF16)],
        compiler_params=pltpu.CompilerParams(
            dimension_semantics=("arbitrary",), vmem_limit_bytes=VMEM_LIMIT_BYTES),
        name="moe_route",
    )(x2d, mod, g_ffn, w_r, b_r)


def _permute_kernel(fill_ref, pos_ref, xp_ref, xs_hbm, zbuf, sem, zsem):
    tb = pos_ref.shape[0]

    @pl.when(pl.program_id(0) == 0)
    def _():
        zbuf[...] = jnp.zeros_like(zbuf)
        n_used = fill_ref[2 * N_BUCKETS]
        n_tiles = xs_hbm.shape[0] // FFN_TM
        for wait in (False, True):
            for b in range(N_BUCKETS):
                for row0, cond in ((fill_ref[b], fill_ref[N_BUCKETS + b] > 0),
                                   ((n_used + b) * FFN_TM, n_used + b < n_tiles)):
                    @pl.when(cond)
                    def _():
                        dst = xs_hbm.at[pl.ds(pl.multiple_of(row0, FFN_TM), FFN_TM)]
                        fill = pltpu.make_async_copy(zbuf, dst, zsem)
                        fill.wait() if wait else fill.start()

    for j in range(tb):
        pltpu.make_async_copy(xp_ref.at[pl.ds(j, 1)], xs_hbm.at[pl.ds(pos_ref[j], 1)], sem).start(priority=j % 2)
    pltpu.make_async_copy(xp_ref, xs_hbm.at[pl.ds(0, tb)], sem).wait()


def _permute(fill_starts, pos, xp, n_rows):
    n_tok = xp.shape[0]
    tb = PERM_TB
    return pl.pallas_call(
        _permute_kernel,
        out_shape=jax.ShapeDtypeStruct((n_rows, ROW_W), _F32),
        grid_spec=pltpu.PrefetchScalarGridSpec(
            num_scalar_prefetch=1,
            grid=(n_tok // tb,),
            in_specs=[
                pl.BlockSpec((tb,), lambda i, fs: (i,), memory_space=pltpu.SMEM),
                pl.BlockSpec((tb, ROW_W), lambda i, fs: (i, 0)),
            ],
            out_specs=pl.BlockSpec(memory_space=pl.ANY),
            scratch_shapes=[pltpu.VMEM((FFN_TM, ROW_W), _F32), pltpu.SemaphoreType.DMA,
                            pltpu.SemaphoreType.DMA],
        ),
        compiler_params=pltpu.CompilerParams(
            dimension_semantics=("arbitrary",), vmem_limit_bytes=VMEM_LIMIT_BYTES),
        name="moe_permute",
    )(fill_starts, pos, xp)


def _ffn_kernel(tg_ref, tlo_ref, thi_ref, clo_ref, chi_ref, ng_ref, nlo_ref, nhi_ref, plo_ref, phi_ref, nused_ref,
                xs_ref, wg_hbm, wu_hbm, wd_hbm, y_ref, sg, su, sd, agu, ad, sem, *, layer):
    i = pl.program_id(0)

    def weight_copies(slot, g, e):
        return (pltpu.make_async_copy(wg_hbm.at[layer, g, e], sg.at[slot], sem.at[slot, 0]),
                pltpu.make_async_copy(wu_hbm.at[layer, g, e], su.at[slot], sem.at[slot, 1]),
                pltpu.make_async_copy(wd_hbm.at[layer, g, e], sd.at[slot], sem.at[slot, 2]))

    @pl.when(i == 0)
    def _():
        for cp in weight_copies(0, tg_ref[0], tlo_ref[0]) + weight_copies(1, tg_ref[0], thi_ref[0]):
            cp.start()

    for slot, changed_ref, e_ref in ((0, clo_ref, tlo_ref), (1, chi_ref, thi_ref)):
        @pl.when(changed_ref[i] == 1)
        def _():
            for cp in weight_copies(slot, tg_ref[i], e_ref[i]):
                cp.wait()
            agu[slot, :, 0:D_EXPERT] = sg[slot].astype(_BF16)
            agu[slot, :, D_EXPERT:2 * D_EXPERT] = su[slot].astype(_BF16)
            ad[slot] = sd[slot].astype(_BF16)

    for slot, prefetch_ref, e_ref in ((0, plo_ref, nlo_ref), (1, phi_ref, nhi_ref)):
        @pl.when(prefetch_ref[i] == 1)
        def _():
            for cp in weight_copies(slot, ng_ref[i], e_ref[i]):
                cp.start()

    @pl.when(i < nused_ref[0])
    def _():
        xb = xs_ref[:, 0:D_MODEL].astype(_BF16)
        info = xs_ref[:, D_MODEL:ROW_W]

        def expert(slot):
            gate_up = _dot(xb, agu[slot])
            hid = _silu(gate_up[:, 0:D_EXPERT]) * gate_up[:, D_EXPERT:2 * D_EXPERT] * info[:, slot:slot + 1]
            return _dot(hid.astype(_BF16), ad[slot])

        y_ref[...] = expert(0) + expert(1)

    @pl.when(i >= nused_ref[0])
    def _():
        y_ref[...] = jnp.zeros_like(y_ref)


def _expert_ffn(layer, plan, xs, w_gate, w_up, w_down):
    n_rows = xs.shape[0]
    tm = FFN_TM
    n_tiles = n_rows // tm

    def row_map(i, *prefetch):
        n_used = prefetch[-1]
        return (jnp.minimum(i, n_used[0] - 1), 0)

    hbm = pl.BlockSpec(memory_space=pl.ANY)
    return pl.pallas_call(
        functools.partial(_ffn_kernel, layer=layer),
        out_shape=jax.ShapeDtypeStruct((n_rows, D_MODEL), _F32),
        grid_spec=pltpu.PrefetchScalarGridSpec(
            num_scalar_prefetch=len(plan),
            grid=(n_tiles,),
            in_specs=[pl.BlockSpec((tm, ROW_W), row_map), hbm, hbm, hbm],
            out_specs=pl.BlockSpec((tm, D_MODEL), lambda i, *prefetch: (i, 0)),
            scratch_shapes=[
                pltpu.VMEM((2, D_MODEL, D_EXPERT), _F32), pltpu.VMEM((2, D_MODEL, D_EXPERT), _F32),
                pltpu.VMEM((2, D_EXPERT, D_MODEL), _F32),
                pltpu.VMEM((2, D_MODEL, 2 * D_EXPERT), _BF16), pltpu.VMEM((2, D_EXPERT, D_MODEL), _BF16),
                pltpu.SemaphoreType.DMA((2, 3)),
            ],
        ),
        compiler_params=pltpu.CompilerParams(
            dimension_semantics=("arbitrary",), vmem_limit_bytes=VMEM_LIMIT_BYTES),
        name="moe_ffn",
    )(*plan, xs, w_gate, w_up, w_down)


def _combine_kernel(pos_ref, x_ref, mod_ref, gfin_ref, y_hbm, o_ref, ybuf, sem):
    tc = x_ref.shape[0]
    for j in range(tc):
        pltpu.make_async_copy(y_hbm.at[pl.ds(pos_ref[j], 1)], ybuf.at[pl.ds(j, 1)], sem).start(priority=j % 2)
    pltpu.make_async_copy(y_hbm.at[pl.ds(0, tc)], ybuf, sem).wait()

    gt2 = mod_ref[0, 5:6, :]
    out = x_ref[...] + gt2 * ybuf[...]
    ms = jnp.mean(out * out, axis=-1, keepdims=True)
    o_ref[...] = out * lax.rsqrt(ms + EPS) * gfin_ref[...]


def _combine(pos, x2d, mod, g_final, y, seq):
    n_tok, d = x2d.shape
    tc = COMB_TC
    tiles_per_seq = seq // tc
    return pl.pallas_call(
        _combine_kernel,
        out_shape=jax.ShapeDtypeStruct((n_tok, d), _F32),
        grid=(n_tok // tc,),
        in_specs=[
            pl.BlockSpec((tc,), lambda i: (i,), memory_space=pltpu.SMEM),
            pl.BlockSpec((tc, d), lambda i: (i, 0)),
            pl.BlockSpec((1, 6, d), lambda i: (i // tiles_per_seq, 0, 0)),
            pl.BlockSpec((1, d), lambda i: (0, 0)),
            pl.BlockSpec(memory_space=pl.ANY),
        ],
        out_specs=pl.BlockSpec((tc, d), lambda i: (i, 0)),
        scratch_shapes=[pltpu.VMEM((tc, d), _F32), pltpu.SemaphoreType.DMA],
        compiler_params=pltpu.CompilerParams(
            dimension_semantics=("arbitrary",), vmem_limit_bytes=VMEM_LIMIT_BYTES),
        name="moe_combine",
    )(pos, x2d, mod, g_final, y)


def _pad_heads(w, heads, width, padded):
    lead = w.shape[:-1]
    w = w.reshape(lead + (heads, width))
    w = jnp.pad(w, [(0, 0)] * len(lead) + [(0, 0), (0, padded - width)])
    return w.reshape(lead + (heads * padded,))


def _block_diag(w):
    n_layers, n, bw, _ = w.shape
    eye = jnp.eye(n, dtype=w.dtype)
    return (eye[None, :, None, :, None] * w[:, :, :, None, :]).reshape(n_layers, n * bw, n * bw)


def _prep_params(w_in, conv_dw_w, conv_dw_b, conv_ln_g, conv_ln_b, lru_conv_w, lru_conv_b, lru_w_a,
                 lru_b_a, lru_w_i, lru_b_i, lru_lam, gla_w_gate, gla_b_gate, gla_norm_g, w_out, g_mix):
    sizes = [CONV_W, CONV_W, LRU_W, LRU_W, GLA_HEADS * GLA_DK, GLA_HEADS * GLA_DK, GLA_V, GLA_RANK, GLA_V]
    cv_v, cv_g, lr_x, lr_y, q, k, v, g_lr, og = jnp.split(w_in, np.cumsum(sizes)[:-1].tolist(), axis=-1)
    w_in_p = jnp.concatenate([
        cv_v, cv_g, lr_x, lr_y,
        _pad_heads(q, GLA_HEADS, GLA_DK, DK_PAD), _pad_heads(k, GLA_HEADS, GLA_DK, DK_PAD),
        _pad_heads(v, GLA_HEADS, GLA_DV, DV_PAD),
        jnp.pad(g_lr, ((0, 0), (0, 0), (0, LANES - GLA_RANK))),
        _pad_heads(og, GLA_HEADS, GLA_DV, DV_PAD)], axis=-1).astype(_BF16)
    n_layers = w_in.shape[0]
    wo_o = w_out[:, CONV_W + LRU_W:].reshape(n_layers, GLA_HEADS, GLA_DV, D_MODEL)
    wo_o = jnp.pad(wo_o, ((0, 0), (0, 0), (0, DV_PAD - GLA_DV), (0, 0))).reshape(n_layers, V_W, D_MODEL)
    w_out_p = jnp.concatenate([w_out[:, :CONV_W + LRU_W], wo_o], axis=1).astype(_BF16)
    w_gate_p = jnp.pad(_pad_heads(gla_w_gate, GLA_HEADS, GLA_DK, DK_PAD),
                       ((0, 0), (0, LANES - GLA_RANK), (0, 0))).astype(_BF16)
    row = lambda v: v[:, None, :]
    return [
        row(g_mix),
        w_in_p,
        jnp.pad(conv_dw_w, ((0, 0), (0, CONV_HIST - CONV_K), (0, 0))),
        row(conv_dw_b), row(conv_ln_g), row(conv_ln_b),
        lru_conv_w, row(lru_conv_b),
        jnp.concatenate([_block_diag(lru_w_a), _block_diag(lru_w_i)], axis=2).astype(_BF16),
        row(jnp.concatenate([lru_b_a, lru_b_i], axis=1)),
        row(lru_lam),
        w_gate_p,
        row(_pad_heads(gla_b_gate, GLA_HEADS, GLA_DK, DK_PAD)),
        row(_pad_heads(gla_norm_g, GLA_HEADS, GLA_DV, DV_PAD)),
        w_out_p,
    ]


def _bucket_layout(counts, n_tiles):
    tm = FFN_TM
    counts = counts.astype(jnp.int32)
    tiles = (counts + tm - 1) // tm
    upto = jnp.arange(N_BUCKETS)[:, None] <= jnp.arange(N_BUCKETS)[None, :]
    tile_end = jnp.sum(jnp.where(upto, tiles[:, None], 0), axis=0)
    starts = ((tile_end - tiles) * tm).astype(jnp.int32)
    n_used = tile_end[-1]
    tile_idx = jnp.minimum(jnp.arange(n_tiles, dtype=jnp.int32), n_used - 1)

    def bucket_of(t):
        return jnp.sum((t[:, None] >= tile_end[None, :]).astype(jnp.int32), axis=1)

    def experts(b):
        pair = b % N_PAIRS
        lo = (pair >= 3).astype(jnp.int32) + (pair >= 5).astype(jnp.int32)
        hi = jnp.where(pair < 3, pair + 1, jnp.where(pair < 5, pair - 1, 3))
        return b // N_PAIRS, lo, hi

    tile_bucket = bucket_of(tile_idx)
    prev_bucket = jnp.where(tile_idx >= 1, bucket_of(tile_idx - 1), -1)
    g, lo, hi = experts(tile_bucket)
    pg, plo, phi = experts(prev_bucket)
    first = (tile_bucket != prev_bucket) & (jnp.arange(n_tiles) < n_used)
    fresh = prev_bucket < 0
    changed_lo = first & (fresh | (g != pg) | (lo != plo))
    changed_hi = first & (fresh | (g != pg) | (hi != phi))
    next_first_tile = jnp.min(jnp.where(tile_end[None, :] > tile_idx[:, None], tile_end[None, :], n_tiles), axis=1)
    has_next = next_first_tile < n_used
    ng, nlo, nhi = experts(bucket_of(jnp.minimum(next_first_tile, n_used - 1)))
    prefetch_lo = first & has_next & ((ng != g) | (nlo != lo))
    prefetch_hi = first & has_next & ((ng != g) | (nhi != hi))
    as_i32 = lambda v: v.astype(jnp.int32)
    ffn_plan = (g, lo, hi, as_i32(changed_lo), as_i32(changed_hi), ng, nlo, nhi,
                as_i32(prefetch_lo), as_i32(prefetch_hi), n_used.reshape(1).astype(jnp.int32))
    fill = jnp.concatenate([(jnp.maximum(tile_end - 1, 0) * tm).astype(jnp.int32), tiles, n_used[None]])
    return starts, fill, ffn_plan


def kernel(x, c, w_ada, b_ada, g_mix, w_in, conv_dw_w, conv_dw_b, conv_ln_g, conv_ln_b, lru_conv_w,
           lru_conv_b, lru_w_a, lru_b_a, lru_w_i, lru_b_i, lru_lam, gla_w_gate, gla_b_gate, gla_norm_g,
           w_out, g_ffn, w_route_group, b_route_group, w_route_expert, b_route_expert, w_gate, w_up,
           w_down, g_final):
    bsz, seq, d = x.shape
    n_layers = w_ada.shape[0]
    n_tok = bsz * seq
    assert d == D_MODEL and seq % max(MIX_TS, ROUTE_TT, COMB_TC) == 0 and MIX_TS % GLA_BLOCK == 0
    assert n_tok % PERM_TB == 0
    n_tiles = n_tok // FFN_TM + N_BUCKETS
    n_rows = n_tiles * FFN_TM

    mod_all = _modulation(c, w_ada, b_ada).reshape(n_layers, bsz, 6, d)

    params = _prep_params(w_in, conv_dw_w, conv_dw_b, conv_ln_g, conv_ln_b, lru_conv_w, lru_conv_b,
                          lru_w_a, lru_b_a, lru_w_i, lru_b_i, lru_lam, gla_w_gate, gla_b_gate, gla_norm_g,
                          w_out, g_mix)
    n_experts = N_GROUPS * EXPERTS_PER_GROUP
    w_r = jnp.concatenate(
        [w_route_group, w_route_expert.transpose(0, 2, 1, 3).reshape(n_layers, d, n_experts)], axis=2)
    w_r = jnp.pad(w_r, ((0, 0), (0, 0), (0, LANES - N_GROUPS - n_experts)))
    b_r = jnp.concatenate([b_route_group, b_route_expert.reshape(n_layers, n_experts)], axis=1)
    b_r = jnp.pad(b_r, ((0, 0), (0, LANES - N_GROUPS - n_experts)))[:, None, :]
    pending_moe = None
    for l in range(n_layers):
        mod = mod_all[l]
        x = _token_mix(x, mod, params, l, pending_moe)

        x2d = x.reshape(n_tok, d)
        xp, meta, counts = _route(x2d, mod, g_ffn[:, None, :], w_r, b_r, l, seq)
        bucket = meta[0].astype(jnp.int32)
        rank = meta[1].astype(jnp.int32)
        starts, fill, ffn_plan = _bucket_layout(counts[0, :N_BUCKETS], n_tiles)
        in_bucket = bucket[:, None] == jnp.arange(N_BUCKETS, dtype=jnp.int32)[None, :]
        pos = rank + jnp.sum(jnp.where(in_bucket, starts[None, :], 0), axis=1)

        xs = _permute(fill, pos, xp, n_rows)
        y = _expert_ffn(l, ffn_plan, xs, w_gate, w_up, w_down)
        pending_moe = (pos, mod, y)
    return _combine(pos, x2d, mod, g_final[None, :], y, seq).reshape(bsz, seq, d)
```

```python
import functools

import jax
import jax.numpy as jnp
import numpy as np
from jax import lax
from jax.experimental import pallas as pl
from jax.experimental.pallas import tpu as pltpu

D_MODEL = 1024
CONV_W = 256
LRU_W = 384
GLA_V = 384
CONV_K = 31
LRU_CONV_K = 4
LRU_BLOCKS = 6
LRU_BW = 64
LRU_C = 8.0
GLA_HEADS = 4
GLA_DV = 96
GLA_DK = 48
GLA_RANK = 16
GLA_TAU = 16.0
GLA_CHUNK = 64
N_GROUPS = 4
EXPERTS_PER_GROUP = 4
D_EXPERT = 512
EPS = 1e-6

LANES = 128
SUBLANES = 8
VMEM_LIMIT_BYTES = 56 * 1024 * 1024

DK_PAD = 64
DV_PAD = 128
QK_W = GLA_HEADS * DK_PAD
V_W = GLA_HEADS * DV_PAD
OFF_CVV = 0
OFF_CVG = OFF_CVV + CONV_W
OFF_LRX = OFF_CVG + CONV_W
OFF_LRY = OFF_LRX + LRU_W
OFF_Q = OFF_LRY + LRU_W
OFF_K = OFF_Q + QK_W
OFF_V = OFF_K + QK_W
OFF_GLR = OFF_V + V_W
OFF_OG = OFF_GLR + LANES
N_IN = OFF_OG + V_W
MIX_W = CONV_W + LRU_W + V_W

CONV_HIST = 32
LRU_HIST = 8
GLA_BLOCK = 256

N_PAIRS = 6
N_BUCKETS = N_GROUPS * N_PAIRS
ROW_W = D_MODEL + LANES

MIX_TS = 512
ROUTE_TT = 512
ROUTE_SPLIT = 2
FFN_TM = 256
PERM_TB = 2048
COMB_TC = 2048
DMA_UNROLL = 8
GATHER_SLOTS = 3

_F32 = jnp.float32
_BF16 = jnp.bfloat16


def _sigmoid(x):
    return 1.0 / (1.0 + jnp.exp(-x))


def _silu(x):
    return x * _sigmoid(x)


def _dot(a, b):
    return jnp.dot(a, b, preferred_element_type=_F32)


def _dot_nt(a, b):
    return lax.dot_general(a, b, (((1,), (1,)), ((), ())), preferred_element_type=_F32)


def _split2(x):
    hi = x.astype(_BF16)
    return hi, (x - hi.astype(_F32)).astype(_BF16)


def _mod_kernel(c_ref, w_ref, b_ref, o_ref):
    c_act = _silu(c_ref[...])
    o_ref[0] = _dot(c_act.astype(_BF16), w_ref[0].astype(_BF16)) + b_ref[0]


def _modulation(c, w_ada, b_ada):
    n_layers, d, n = w_ada.shape
    bsz = c.shape[0]
    tn = 1536
    return pl.pallas_call(
        _mod_kernel,
        out_shape=jax.ShapeDtypeStruct((n_layers, bsz, n), _F32),
        grid=(n_layers, n // tn),
        in_specs=[
            pl.BlockSpec((bsz, d), lambda l, j: (0, 0)),
            pl.BlockSpec((1, d, tn), lambda l, j: (l, 0, j)),
            pl.BlockSpec((1, 1, tn), lambda l, j: (l, 0, j)),
        ],
        out_specs=pl.BlockSpec((1, bsz, tn), lambda l, j: (l, 0, j)),
        compiler_params=pltpu.CompilerParams(
            dimension_semantics=("arbitrary", "arbitrary"), vmem_limit_bytes=VMEM_LIMIT_BYTES),
        name="adaln_mod",
    )(c, w_ada, b_ada.reshape(n_layers, 1, n))


def _mix_kernel(x_ref, mod_ref, gmix_ref, win_ref, cw_ref, cb_ref, lng_ref, lnb_ref,
                lcw_ref, lcb_ref, wai_ref, bai_ref, lam_ref, wgate_ref, bgate_ref, gnorm_ref,
                wout_ref, *rest, fused_moe_gather):
    if fused_moe_gather:
        pos_cur, pos_nxt, pos_ahd, modp_ref, y_hbm, o_ref = rest[:6]
        ybuf, gsem = rest[-2:]
        rest = rest[6:-2]
    else:
        o_ref, rest = rest[0], rest[1:]
    ubuf, lbuf, hcar, state, tri_ref, tria_ref, csel_ref, causal_ref, hdiag_ref = rest
    ts = x_ref.shape[1]
    s_idx = pl.program_id(1)

    @pl.when(s_idx == 0)
    def _():
        ubuf[0:CONV_HIST, :] = jnp.zeros((CONV_HIST, CONV_W), _F32)
        lbuf[0:LRU_HIST, :] = jnp.zeros((LRU_HIST, LRU_W), _F32)
        hcar[...] = jnp.zeros_like(hcar)
        state[...] = jnp.zeros_like(state)
        ri = lax.broadcasted_iota(jnp.int32, (GLA_BLOCK, GLA_BLOCK), 0)
        ci = lax.broadcasted_iota(jnp.int32, (GLA_BLOCK, GLA_BLOCK), 1)
        same_chunk = ri // GLA_CHUNK == ci // GLA_CHUNK
        causal_f = jnp.where(same_chunk & (ci <= ri), 1.0, 0.0)
        causal_ref[...] = causal_f
        tri_ref[...] = causal_f.astype(_BF16)
        tria_ref[...] = jnp.where(same_chunk & (ci > ri), 1.0, 0.0).astype(_BF16)
        sel_r = lax.broadcasted_iota(jnp.int32, (LANES, GLA_BLOCK), 0)
        sel_c = lax.broadcasted_iota(jnp.int32, (LANES, GLA_BLOCK), 1)
        csel_ref[...] = jnp.where(sel_r == sel_c // GLA_CHUNK, 1.0, 0.0).astype(_BF16)
        st_row_head = lax.broadcasted_iota(jnp.int32, (QK_W, V_W), 0) // DK_PAD
        st_col_head = lax.broadcasted_iota(jnp.int32, (QK_W, V_W), 1) // DV_PAD
        hdiag_ref[...] = jnp.where(st_row_head == st_col_head, 1.0, 0.0)

    x = x_ref[0]
    if fused_moe_gather:
        step = pl.program_id(0) * pl.num_programs(1) + s_idx
        n_steps = pl.num_programs(0) * pl.num_programs(1)
        slot = step % GATHER_SLOTS

        def row_copy(pos_ref, j, to_slot):
            return pltpu.make_async_copy(y_hbm.at[pl.ds(pos_ref[j], 1)], ybuf.at[to_slot, pl.ds(j, 1)],
                                         gsem.at[to_slot])

        def wait_rows(of_slot):
            pltpu.make_async_copy(y_hbm.at[pl.ds(0, ts)], ybuf.at[of_slot], gsem.at[of_slot]).wait()

        @pl.when(step == 0)
        def _():
            def issue(j, carry):
                row_copy(pos_cur, j, 0).start()
                row_copy(pos_nxt, j, 1).start()
                return carry
            lax.fori_loop(0, ts, issue, 0, unroll=DMA_UNROLL)

        wait_rows(slot)
        x = x + modp_ref[0, 5:6, :] * ybuf[slot]

        ahead_slot = (step + GATHER_SLOTS - 1) % GATHER_SLOTS
        for j in range(ts):
            row_copy(pos_ahd, j, ahead_slot).start()

    sh1 = mod_ref[0, 0:1, :]
    gt1 = mod_ref[0, 2:3, :]
    scale = gmix_ref[...] * (1.0 + mod_ref[0, 1:2, :])
    ms = jnp.mean(x * x, axis=-1, keepdims=True)
    hb = (x * lax.rsqrt(ms + EPS) * scale + sh1).astype(_BF16)

    def proj(off, width):
        return _dot(hb, win_ref[:, off:off + width])

    out = {}
    _run_alternately(
        _conv_lru_steps(out, proj, cw_ref, cb_ref, lng_ref, lnb_ref, lcw_ref, lcb_ref, wai_ref, bai_ref,
                        lam_ref, ubuf, lbuf, hcar, ts),
        _gla_steps(out, proj, wgate_ref, bgate_ref, gnorm_ref, state, tri_ref, tria_ref, csel_ref,
                   causal_ref, hdiag_ref, ts))

    mixed = (_dot(out["o_g"].astype(_BF16), wout_ref[CONV_W + LRU_W:MIX_W, :])
             + _dot(out["r_out"].astype(_BF16), wout_ref[CONV_W:CONV_W + LRU_W, :])
             + _dot(out["u_out"].astype(_BF16), wout_ref[0:CONV_W, :]))
    o_ref[0] = x + gt1 * mixed

    if fused_moe_gather:
        @pl.when(step == n_steps - 1)
        def _():
            wait_rows((step + 1) % GATHER_SLOTS)
            wait_rows((step + 2) % GATHER_SLOTS)


def _run_alternately(*step_generators):
    clock = [0] * len(step_generators)
    live = list(range(len(step_generators)))
    while live:
        i = min(live, key=lambda j: clock[j])
        try:
            clock[i] += next(step_generators[i]) * (1 if i else 0.5)
        except StopIteration:
            live.remove(i)


def _conv_lru_steps(out, proj, cw_ref, cb_ref, lng_ref, lnb_ref, lcw_ref, lcb_ref, wai_ref, bai_ref,
                    lam_ref, ubuf, lbuf, hcar, ts):
    u = proj(OFF_CVV, CONV_W) * _sigmoid(proj(OFF_CVG, CONV_W))
    ubuf[CONV_HIST:CONV_HIST + ts, :] = u
    yield 500
    lbuf[LRU_HIST:LRU_HIST + ts, :] = proj(OFF_LRX, LRU_W)
    xb = jnp.broadcast_to(lcb_ref[...], (ts, LRU_W))
    for k in range(LRU_CONV_K):
        off = LRU_HIST - (LRU_CONV_K - 1) + k
        xb = xb + lcw_ref[k:k + 1, :] * lbuf[off:off + ts, :]
    lbuf[0:LRU_HIST, :] = lbuf[ts:ts + LRU_HIST, :]
    gates = _dot(xb.astype(_BF16), wai_ref[...]) + bai_ref[...]
    yield 800

    acc = jnp.broadcast_to(cb_ref[...], (ts, CONV_W))
    for r in range(SUBLANES):
        z = None
        for a_blk in range((CONV_K - 1 - r) // SUBLANES + 1):
            k = CONV_K - 1 - (SUBLANES * a_blk + r)
            off = CONV_HIST - SUBLANES * (a_blk + 1)
            term = cw_ref[k:k + 1, :] * ubuf[off:off + ts + SUBLANES, :]
            z = term if z is None else z + term
        acc = acc + z[SUBLANES - r:SUBLANES - r + ts]
        yield 200
    ubuf[0:CONV_HIST, :] = ubuf[ts:ts + CONV_HIST, :]
    mu = jnp.mean(acc, axis=-1, keepdims=True)
    cen = acc - mu
    var = jnp.mean(cen * cen, axis=-1, keepdims=True)
    out["u_out"] = _silu(cen * lax.rsqrt(var + EPS) * lng_ref[...] + lnb_ref[...])
    yield 250

    r_gate = _sigmoid(gates[:, 0:LRU_W])
    i_gate = _sigmoid(gates[:, LRU_W:2 * LRU_W])
    lam = lam_ref[...]
    softplus_neg_lam = jnp.maximum(-lam, 0.0) + jnp.log1p(jnp.exp(-jnp.abs(lam)))
    log_a = (-LRU_C) * r_gate * softplus_neg_lam
    a = jnp.exp(log_a)
    mult = jnp.sqrt(jnp.tanh(-log_a) * (a * a + 1.0))
    u_in = mult * (i_gate * xb)
    yield 600
    row = lax.broadcasted_iota(jnp.int32, a.shape, 0) % SUBLANES
    d = 1
    while d < SUBLANES:
        keep = row >= d
        a_sh = jnp.where(keep, pltpu.roll(a, d, axis=0), 1.0)
        u_sh = jnp.where(keep, pltpu.roll(u_in, d, axis=0), 0.0)
        u_in = u_in + a * u_sh
        a = a * a_sh
        d *= 2
        yield 200
    groups = []
    h_prev = hcar[0:1, :]
    for g in range(ts // SUBLANES):
        r0 = g * SUBLANES
        h_g = u_in[r0:r0 + SUBLANES] + a[r0:r0 + SUBLANES] * h_prev
        groups.append(h_g)
        h_prev = h_g[SUBLANES - 1:SUBLANES]
        if g % SUBLANES == SUBLANES - 1:
            yield 50
    h_lru = jnp.concatenate(groups, axis=0)
    hcar[...] = jnp.broadcast_to(h_prev, hcar.shape)
    out["r_out"] = h_lru * jax.nn.gelu(proj(OFF_LRY, LRU_W), approximate=True)


def _gla_steps(out, proj, wgate_ref, bgate_ref, gnorm_ref, state, tri_ref, tria_ref, csel_ref, causal_ref,
               hdiag_ref, ts):
    zg = proj(OFF_GLR, LANES)
    glog = _dot(zg.astype(_BF16), wgate_ref[...]) + bgate_ref[...]
    lg = (jnp.minimum(glog, 0.0) - jnp.log1p(jnp.exp(-jnp.abs(glog)))) * (1.0 / GLA_TAU)
    yield 300
    zq = proj(OFF_Q, QK_W) * (GLA_DK ** -0.5)
    zk = proj(OFF_K, QK_W)
    yield 500
    zv = proj(OFF_V, V_W)
    yield 500

    nblk = ts // GLA_BLOCK
    cpb = GLA_BLOCK // GLA_CHUNK
    tri = tri_ref[...]
    tri_after = tria_ref[...]
    chunk_sel = csel_ref[...]
    causal = causal_ref[...] > 0.5
    qk_lane_head = lax.broadcasted_iota(jnp.int32, (1, QK_W), 1) // DK_PAD
    t_lane_chunk = lax.broadcasted_iota(jnp.int32, (1, GLA_BLOCK), 1) // GLA_CHUNK

    o_blocks = []
    for blk in range(nblk):
        r0 = blk * GLA_BLOCK
        lg_b = lg[r0:r0 + GLA_BLOCK]
        p_hi, p_lo = _split2(lg_b)
        b = _dot(tri, p_hi) + _dot(tri, p_lo)
        b_rest = _dot(tri_after, p_hi) + _dot(tri_after, p_lo)
        b_tot = _dot(chunk_sel, p_hi) + _dot(chunk_sel, p_lo)
        yield 500
        q_in =zq[r0:r0 + GLA_BLOCK] * jnp.exp(b)
        k_blk = zk[r0:r0 + GLA_BLOCK]
        k_in = (k_blk * jnp.exp(-b)).astype(_BF16)
        k_out_t = (k_blk * jnp.exp(b_rest)).T.astype(_BF16)
        v_b = zv[r0:r0 + GLA_BLOCK].astype(_BF16)
        q_in_b = q_in.astype(_BF16)
        decay_cols = jnp.exp(b_tot).T

        o_heads = []
        for hd in range(GLA_HEADS):
            q_h = jnp.where(qk_lane_head == hd, q_in_b, jnp.zeros_like(q_in_b))
            sc = jnp.where(causal, _dot_nt(q_h, k_in), 0.0)
            o_heads.append(_dot(sc.astype(_BF16), v_b[:, hd * DV_PAD:(hd + 1) * DV_PAD]))
            yield 300
        o_intra = jnp.concatenate(o_heads, axis=1)

        o_inter = []
        st = state[...]
        for c in range(cpb):
            c0 = c * GLA_CHUNK
            o_inter.append(_dot(q_in_b[c0:c0 + GLA_CHUNK], st.astype(_BF16)))
            kv = _dot(jnp.where(t_lane_chunk == c, k_out_t, jnp.zeros_like(k_out_t)), v_b)
            st = st * decay_cols[:, c:c + 1] + kv * hdiag_ref[...]
            yield 400
        state[...] = st
        o_blocks.append(o_intra + jnp.concatenate(o_inter, axis=0))
    o = o_blocks[0] if nblk == 1 else jnp.concatenate(o_blocks, axis=0)

    og = proj(OFF_OG, V_W)
    o_parts = []
    for hd in range(GLA_HEADS):
        o_h = o[:, hd * DV_PAD:(hd + 1) * DV_PAD]
        ms_h = jnp.sum(o_h * o_h, axis=-1, keepdims=True) * (1.0 / GLA_DV)
        o_parts.append(o_h * lax.rsqrt(ms_h + EPS))
    o_n = jnp.concatenate(o_parts, axis=1) * gnorm_ref[...]
    out["o_g"] = o_n * _silu(og)


def _token_mix(x, mod, params, layer, moe=None):
    bsz, seq, d = x.shape
    ts = MIX_TS
    n_s = seq // ts
    full = lambda shape: pl.BlockSpec((None,) + shape, lambda b, s: (layer,) + (0,) * len(shape))
    in_specs = [
        pl.BlockSpec((1, ts, d), lambda b, s: (b, s, 0)),
        pl.BlockSpec((1, 6, d), lambda b, s: (b, 0, 0)),
        full((1, d)),
        full((d, N_IN)),
        full((CONV_HIST, CONV_W)), full((1, CONV_W)), full((1, CONV_W)), full((1, CONV_W)),
        full((LRU_CONV_K, LRU_W)), full((1, LRU_W)),
        full((LRU_W, 2 * LRU_W)), full((1, 2 * LRU_W)), full((1, LRU_W)),
        full((LANES, QK_W)), full((1, QK_W)), full((1, V_W)),
        full((MIX_W, d)),
    ]
    scratch_shapes = [
        pltpu.VMEM((CONV_HIST + ts, CONV_W), _F32),
        pltpu.VMEM((LRU_HIST + ts, LRU_W), _F32),
        pltpu.VMEM((SUBLANES, LRU_W), _F32),
        pltpu.VMEM((QK_W, V_W), _F32),
        pltpu.VMEM((GLA_BLOCK, GLA_BLOCK), _BF16),
        pltpu.VMEM((GLA_BLOCK, GLA_BLOCK), _BF16),
        pltpu.VMEM((LANES, GLA_BLOCK), _BF16),
        pltpu.VMEM((GLA_BLOCK, GLA_BLOCK), _F32),
        pltpu.VMEM((QK_W, V_W), _F32),
    ]
    args = [x, mod] + list(params)
    if moe is not None:
        pos, mod_prev, y = moe
        last = bsz * n_s - 1
        in_specs += [
            pl.BlockSpec((ts,), lambda b, s: (b * n_s + s,), memory_space=pltpu.SMEM),
            pl.BlockSpec((ts,), lambda b, s: (jnp.minimum(b * n_s + s + 1, last),), memory_space=pltpu.SMEM),
            pl.BlockSpec((ts,), lambda b, s: (jnp.minimum(b * n_s + s + 2, last),), memory_space=pltpu.SMEM),
            pl.BlockSpec((1, 6, d), lambda b, s: (b, 0, 0)),
            pl.BlockSpec(memory_space=pl.ANY),
        ]
        scratch_shapes += [pltpu.VMEM((GATHER_SLOTS, ts, d), _F32), pltpu.SemaphoreType.DMA((GATHER_SLOTS,))]
        args += [pos, pos, pos, mod_prev, y]
    return pl.pallas_call(
        functools.partial(_mix_kernel, fused_moe_gather=moe is not None),
        out_shape=jax.ShapeDtypeStruct(x.shape, _F32),
        grid=(bsz, n_s),
        in_specs=in_specs,
        out_specs=pl.BlockSpec((1, ts, d), lambda b, s: (b, s, 0)),
        scratch_shapes=scratch_shapes,
        compiler_params=pltpu.CompilerParams(
            dimension_semantics=("arbitrary", "arbitrary"), vmem_limit_bytes=VMEM_LIMIT_BYTES),
        name="token_mix",
    )(*args)


def _route_kernel(x_ref, mod_ref, g_ref, wr_ref, br_ref, xp_ref, meta_ref, cnt_ref, carry, strict_ref):
    tt = x_ref.shape[0]
    i = pl.program_id(0)

    @pl.when(i == 0)
    def _():
        carry[...] = jnp.zeros_like(carry)
        ri = lax.broadcasted_iota(jnp.int32, (tt, tt), 0)
        ci = lax.broadcasted_iota(jnp.int32, (tt, tt), 1)
        strict_ref[...] = jnp.where(ci < ri, 1.0, 0.0).astype(_BF16)

    sh2 = mod_ref[0, 3:4, :]
    scale = g_ref[...] * (1.0 + mod_ref[0, 4:5, :])
    w_hi, w_lo = _split2(wr_ref[...])
    w_cat = jnp.concatenate([w_hi, w_lo], axis=1)
    n_half = tt // ROUTE_SPLIT
    counts_before = [carry[0:1, :]]

    def route_rows(r0):
        x = x_ref[r0:r0 + n_half, :]
        ms = jnp.mean(x * x, axis=-1, keepdims=True)
        h = x * lax.rsqrt(ms + EPS) * scale + sh2
        xp_ref[r0:r0 + n_half, 0:D_MODEL] = h
        h_hi, h_lo = _split2(h)
        yield
        prod = _dot(h_hi, w_cat) + _dot(h_lo, w_cat)
        logits = prod[:, 0:LANES] + prod[:, LANES:2 * LANES] + br_ref[...]
        yield

        lane = lax.broadcasted_iota(jnp.int32, (n_half, LANES), 1)
        lane_f = lane.astype(_F32)
        neg = -jnp.inf
        big = float(LANES)

        def first_argmax(vals, vmax):
            return jnp.min(jnp.where(vals == vmax, lane_f, big), axis=-1, keepdims=True).astype(jnp.int32)

        gl = jnp.where(lane < N_GROUPS, logits, neg)
        gmax = jnp.max(gl, axis=-1, keepdims=True)
        g_star = first_argmax(gl, gmax)
        p_sel = 1.0 / jnp.sum(jnp.exp(gl - gmax), axis=-1, keepdims=True)
        base = N_GROUPS + EXPERTS_PER_GROUP * g_star
        el = jnp.where((lane >= base) & (lane < base + EXPERTS_PER_GROUP), logits, neg)
        v0 = jnp.max(el, axis=-1, keepdims=True)
        i0 = first_argmax(el, v0)
        el2 = jnp.where(lane == i0, neg, el)
        v1 = jnp.max(el2, axis=-1, keepdims=True)
        i1 = first_argmax(el2, v1)
        ex = jnp.exp(v1 - v0)
        wt0 = p_sel / (1.0 + ex)
        wt1 = p_sel * ex / (1.0 + ex)
        e0 = i0 - base
        e1 = i1 - base
        e_lo = jnp.minimum(e0, e1)
        e_hi = jnp.maximum(e0, e1)
        w_lo = jnp.where(e0 < e1, wt0, wt1)
        w_hi = jnp.where(e0 < e1, wt1, wt0)
        pair = (e_lo * (2 * EXPERTS_PER_GROUP - 1 - e_lo)) // 2 + (e_hi - e_lo - 1)
        bucket = g_star * N_PAIRS + pair
        yield

        onehot = lane == bucket
        onehot_f = jnp.where(onehot, 1.0, 0.0)
        before = counts_before[-1]
        prefix = _dot(strict_ref[0:n_half, 0:n_half], onehot_f.astype(_BF16)) + before
        rank = jnp.sum(jnp.where(onehot, prefix, 0.0), axis=-1, keepdims=True)
        counts_before.append(before + jnp.sum(onehot_f, axis=0, keepdims=True))

        meta = jnp.where(lane == 0, bucket.astype(_F32), jnp.where(lane == 1, rank, 0.0))
        meta_ref[:, r0:r0 + n_half] = meta.T[0:SUBLANES, :]
        xp_ref[r0:r0 + n_half, D_MODEL:ROW_W] = jnp.where(lane == 0, w_lo, jnp.where(lane == 1, w_hi, 0.0))

    blocks = [route_rows(k * n_half) for k in range(ROUTE_SPLIT)]
    while blocks:
        for blk in list(blocks):
            if next(blk, "done") == "done":
                blocks.remove(blk)
    total = jnp.broadcast_to(counts_before[-1], carry.shape)
    carry[...] = total
    cnt_ref[...] = total


def _route(x2d, mod, g_ffn, w_r, b_r, layer, seq):
    n_tok, d = x2d.shape
    tt = ROUTE_TT
    tiles_per_seq = seq // tt
    per_layer = lambda shape: pl.BlockSpec((None,) + shape, lambda i: (layer, 0, 0))
    return pl.pallas_call(
        _route_kernel,
        out_shape=(
            jax.ShapeDtypeStruct((n_tok, ROW_W), _F32),
            jax.ShapeDtypeStruct((SUBLANES, n_tok), _F32),
            jax.ShapeDtypeStruct((SUBLANES, LANES), _F32),
        ),
        grid=(n_tok // tt,),
        in_specs=[
            pl.BlockSpec((tt, d), lambda i: (i, 0)),
            pl.BlockSpec((1, 6, d), lambda i: (i // tiles_per_seq, 0, 0)),
            per_layer((1, d)),
            per_layer((d, LANES)),
            per_layer((1, LANES)),
        ],
        out_specs=(
            pl.BlockSpec((tt, ROW_W), lambda i: (i, 0)),
            pl.BlockSpec((SUBLANES, tt), lambda i: (0, i)),
            pl.BlockSpec((SUBLANES, LANES), lambda i: (0, 0)),
        ),
        scratch_shapes=[pltpu.VMEM((SUBLANES, LANES), _F32), pltpu.VMEM((tt, tt), _BF16)],
        compiler_params=pltpu.CompilerParams(
            dimension_semantics=("arbitrary",), vmem_limit_bytes=VMEM_LIMIT_BYTES),
        name="moe_route",
    )(x2d, mod, g_ffn, w_r, b_r)


def _permute_kernel(fill_ref, pos_ref, xp_ref, xs_hbm, zbuf, sem, zsem):
    tb = pos_ref.shape[0]

    @pl.when(pl.program_id(0) == 0)
    def _():
        zbuf[...] = jnp.zeros_like(zbuf)
        n_used = fill_ref[2 * N_BUCKETS]
        n_tiles = xs_hbm.shape[0] // FFN_TM
        for wait in (False, True):
            for b in range(N_BUCKETS):
                for row0, cond in ((fill_ref[b], fill_ref[N_BUCKETS + b] > 0),
                                   ((n_used + b) * FFN_TM, n_used + b < n_tiles)):
                    @pl.when(cond)
                    def _():
                        dst = xs_hbm.at[pl.ds(pl.multiple_of(row0, FFN_TM), FFN_TM)]
                        fill = pltpu.make_async_copy(zbuf, dst, zsem)
                        fill.wait() if wait else fill.start()

    for j in range(tb):
        pltpu.make_async_copy(xp_ref.at[pl.ds(j, 1)], xs_hbm.at[pl.ds(pos_ref[j], 1)], sem).start(priority=j % 2)
    pltpu.make_async_copy(xp_ref, xs_hbm.at[pl.ds(0, tb)], sem).wait()


def _permute(fill_starts, pos, xp, n_rows):
    n_tok = xp.shape[0]
    tb = PERM_TB
    return pl.pallas_call(
        _permute_kernel,
        out_shape=jax.ShapeDtypeStruct((n_rows, ROW_W), _F32),
        grid_spec=pltpu.PrefetchScalarGridSpec(
            num_scalar_prefetch=1,
            grid=(n_tok // tb,),
            in_specs=[
                pl.BlockSpec((tb,), lambda i, fs: (i,), memory_space=pltpu.SMEM),
                pl.BlockSpec((tb, ROW_W), lambda i, fs: (i, 0)),
            ],
            out_specs=pl.BlockSpec(memory_space=pl.ANY),
            scratch_shapes=[pltpu.VMEM((FFN_TM, ROW_W), _F32), pltpu.SemaphoreType.DMA,
                            pltpu.SemaphoreType.DMA],
        ),
        compiler_params=pltpu.CompilerParams(
            dimension_semantics=("arbitrary",), vmem_limit_bytes=VMEM_LIMIT_BYTES),
        name="moe_permute",
    )(fill_starts, pos, xp)


def _ffn_kernel(tg_ref, tlo_ref, thi_ref, clo_ref, chi_ref, ng_ref, nlo_ref, nhi_ref, plo_ref, phi_ref, nused_ref,
                xs_ref, wg_hbm, wu_hbm, wd_hbm, y_ref, sg, su, sd, agu, ad, sem, *, layer):
    i = pl.program_id(0)

    def weight_copies(slot, g, e):
        return (pltpu.make_async_copy(wg_hbm.at[layer, g, e], sg.at[slot], sem.at[slot, 0]),
                pltpu.make_async_copy(wu_hbm.at[layer, g, e], su.at[slot], sem.at[slot, 1]),
                pltpu.make_async_copy(wd_hbm.at[layer, g, e], sd.at[slot], sem.at[slot, 2]))

    @pl.when(i == 0)
    def _():
        for cp in weight_copies(0, tg_ref[0], tlo_ref[0]) + weight_copies(1, tg_ref[0], thi_ref[0]):
            cp.start()

    for slot, changed_ref, e_ref in ((0, clo_ref, tlo_ref), (1, chi_ref, thi_ref)):
        @pl.when(changed_ref[i] == 1)
        def _():
            for cp in weight_copies(slot, tg_ref[i], e_ref[i]):
                cp.wait()
            agu[slot, :, 0:D_EXPERT] = sg[slot].astype(_BF16)
            agu[slot, :, D_EXPERT:2 * D_EXPERT] = su[slot].astype(_BF16)
            ad[slot] = sd[slot].astype(_BF16)

    for slot, prefetch_ref, e_ref in ((0, plo_ref, nlo_ref), (1, phi_ref, nhi_ref)):
        @pl.when(prefetch_ref[i] == 1)
        def _():
            for cp in weight_copies(slot, ng_ref[i], e_ref[i]):
                cp.start()

    @pl.when(i < nused_ref[0])
    def _():
        xb = xs_ref[:, 0:D_MODEL].astype(_BF16)
        info = xs_ref[:, D_MODEL:ROW_W]

        def expert(slot):
            gate_up = _dot(xb, agu[slot])
            hid = _silu(gate_up[:, 0:D_EXPERT]) * gate_up[:, D_EXPERT:2 * D_EXPERT] * info[:, slot:slot + 1]
            return _dot(hid.astype(_BF16), ad[slot])

        y_ref[...] = expert(0) + expert(1)

    @pl.when(i >= nused_ref[0])
    def _():
        y_ref[...] = jnp.zeros_like(y_ref)


def _expert_ffn(layer, plan, xs, w_gate, w_up, w_down):
    n_rows = xs.shape[0]
    tm = FFN_TM
    n_tiles = n_rows // tm

    def row_map(i, *prefetch):
        n_used = prefetch[-1]
        return (jnp.minimum(i, n_used[0] - 1), 0)

    hbm = pl.BlockSpec(memory_space=pl.ANY)
    return pl.pallas_call(
        functools.partial(_ffn_kernel, layer=layer),
        out_shape=jax.ShapeDtypeStruct((n_rows, D_MODEL), _F32),
        grid_spec=pltpu.PrefetchScalarGridSpec(
            num_scalar_prefetch=len(plan),
            grid=(n_tiles,),
            in_specs=[pl.BlockSpec((tm, ROW_W), row_map), hbm, hbm, hbm],
            out_specs=pl.BlockSpec((tm, D_MODEL), lambda i, *prefetch: (i, 0)),
            scratch_shapes=[
                pltpu.VMEM((2, D_MODEL, D_EXPERT), _F32), pltpu.VMEM((2, D_MODEL, D_EXPERT), _F32),
                pltpu.VMEM((2, D_EXPERT, D_MODEL), _F32),
                pltpu.VMEM((2, D_MODEL, 2 * D_EXPERT), _BF16), pltpu.VMEM((2, D_EXPERT, D_MODEL), _BF16),
                pltpu.SemaphoreType.DMA((2, 3)),
            ],
        ),
        compiler_params=pltpu.CompilerParams(
            dimension_semantics=("arbitrary",), vmem_limit_bytes=VMEM_LIMIT_BYTES),
        name="moe_ffn",
    )(*plan, xs, w_gate, w_up, w_down)


def _combine_kernel(pos_ref, x_ref, mod_ref, gfin_ref, y_hbm, o_ref, ybuf, sem):
    tc = x_ref.shape[0]
    for j in range(tc):
        pltpu.make_async_copy(y_hbm.at[pl.ds(pos_ref[j], 1)], ybuf.at[pl.ds(j, 1)], sem).start(priority=j % 2)
    pltpu.make_async_copy(y_hbm.at[pl.ds(0, tc)], ybuf, sem).wait()

    gt2 = mod_ref[0, 5:6, :]
    out = x_ref[...] + gt2 * ybuf[...]
    ms = jnp.mean(out * out, axis=-1, keepdims=True)
    o_ref[...] = out * lax.rsqrt(ms + EPS) * gfin_ref[...]


def _combine(pos, x2d, mod, g_final, y, seq):
    n_tok, d = x2d.shape
    tc = COMB_TC
    tiles_per_seq = seq // tc
    return pl.pallas_call(
        _combine_kernel,
        out_shape=jax.ShapeDtypeStruct((n_tok, d), _F32),
        grid=(n_tok // tc,),
        in_specs=[
            pl.BlockSpec((tc,), lambda i: (i,), memory_space=pltpu.SMEM),
            pl.BlockSpec((tc, d), lambda i: (i, 0)),
            pl.BlockSpec((1, 6, d), lambda i: (i // tiles_per_seq, 0, 0)),
            pl.BlockSpec((1, d), lambda i: (0, 0)),
            pl.BlockSpec(memory_space=pl.ANY),
        ],
        out_specs=pl.BlockSpec((tc, d), lambda i: (i, 0)),
        scratch_shapes=[pltpu.VMEM((tc, d), _F32), pltpu.SemaphoreType.DMA],
        compiler_params=pltpu.CompilerParams(
            dimension_semantics=("arbitrary",), vmem_limit_bytes=VMEM_LIMIT_BYTES),
        name="moe_combine",
    )(pos, x2d, mod, g_final, y)


def _pad_heads(w, heads, width, padded):
    lead = w.shape[:-1]
    w = w.reshape(lead + (heads, width))
    w = jnp.pad(w, [(0, 0)] * len(lead) + [(0, 0), (0, padded - width)])
    return w.reshape(lead + (heads * padded,))


def _block_diag(w):
    n_layers, n, bw, _ = w.shape
    eye = jnp.eye(n, dtype=w.dtype)
    return (eye[None, :, None, :, None] * w[:, :, :, None, :]).reshape(n_layers, n * bw, n * bw)


def _prep_params(w_in, conv_dw_w, conv_dw_b, conv_ln_g, conv_ln_b, lru_conv_w, lru_conv_b, lru_w_a,
                 lru_b_a, lru_w_i, lru_b_i, lru_lam, gla_w_gate, gla_b_gate, gla_norm_g, w_out, g_mix):
    sizes = [CONV_W, CONV_W, LRU_W, LRU_W, GLA_HEADS * GLA_DK, GLA_HEADS * GLA_DK, GLA_V, GLA_RANK, GLA_V]
    cv_v, cv_g, lr_x, lr_y, q, k, v, g_lr, og = jnp.split(w_in, np.cumsum(sizes)[:-1].tolist(), axis=-1)
    w_in_p = jnp.concatenate([
        cv_v, cv_g, lr_x, lr_y,
        _pad_heads(q, GLA_HEADS, GLA_DK, DK_PAD), _pad_heads(k, GLA_HEADS, GLA_DK, DK_PAD),
        _pad_heads(v, GLA_HEADS, GLA_DV, DV_PAD),
        jnp.pad(g_lr, ((0, 0), (0, 0), (0, LANES - GLA_RANK))),
        _pad_heads(og, GLA_HEADS, GLA_DV, DV_PAD)], axis=-1).astype(_BF16)
    n_layers = w_in.shape[0]
    wo_o = w_out[:, CONV_W + LRU_W:].reshape(n_layers, GLA_HEADS, GLA_DV, D_MODEL)
    wo_o = jnp.pad(wo_o, ((0, 0), (0, 0), (0, DV_PAD - GLA_DV), (0, 0))).reshape(n_layers, V_W, D_MODEL)
    w_out_p = jnp.concatenate([w_out[:, :CONV_W + LRU_W], wo_o], axis=1).astype(_BF16)
    w_gate_p = jnp.pad(_pad_heads(gla_w_gate, GLA_HEADS, GLA_DK, DK_PAD),
                       ((0, 0), (0, LANES - GLA_RANK), (0, 0))).astype(_BF16)
    row = lambda v: v[:, None, :]
    return [
        row(g_mix),
        w_in_p,
        jnp.pad(conv_dw_w, ((0, 0), (0, CONV_HIST - CONV_K), (0, 0))),
        row(conv_dw_b), row(conv_ln_g), row(conv_ln_b),
        lru_conv_w, row(lru_conv_b),
        jnp.concatenate([_block_diag(lru_w_a), _block_diag(lru_w_i)], axis=2).astype(_BF16),
        row(jnp.concatenate([lru_b_a, lru_b_i], axis=1)),
        row(lru_lam),
        w_gate_p,
        row(_pad_heads(gla_b_gate, GLA_HEADS, GLA_DK, DK_PAD)),
        row(_pad_heads(gla_norm_g, GLA_HEADS, GLA_DV, DV_PAD)),
        w_out_p,
    ]


def _bucket_layout(counts, n_tiles):
    tm = FFN_TM
    counts = counts.astype(jnp.int32)
    tiles = (counts + tm - 1) // tm
    upto = jnp.arange(N_BUCKETS)[:, None] <= jnp.arange(N_BUCKETS)[None, :]
    tile_end = jnp.sum(jnp.where(upto, tiles[:, None], 0), axis=0)
    starts = ((tile_end - tiles) * tm).astype(jnp.int32)
    n_used = tile_end[-1]
    tile_idx = jnp.minimum(jnp.arange(n_tiles, dtype=jnp.int32), n_used - 1)

    def bucket_of(t):
        return jnp.sum((t[:, None] >= tile_end[None, :]).astype(jnp.int32), axis=1)

    def experts(b):
        pair = b % N_PAIRS
        lo = (pair >= 3).astype(jnp.int32) + (pair >= 5).astype(jnp.int32)
        hi = jnp.where(pair < 3, pair + 1, jnp.where(pair < 5, pair - 1, 3))
        return b // N_PAIRS, lo, hi

    tile_bucket = bucket_of(tile_idx)
    prev_bucket = jnp.where(tile_idx >= 1, bucket_of(tile_idx - 1), -1)
    g, lo, hi = experts(tile_bucket)
    pg, plo, phi = experts(prev_bucket)
    first = (tile_bucket != prev_bucket) & (jnp.arange(n_tiles) < n_used)
    fresh = prev_bucket < 0
    changed_lo = first & (fresh | (g != pg) | (lo != plo))
    changed_hi = first & (fresh | (g != pg) | (hi != phi))
    next_first_tile = jnp.min(jnp.where(tile_end[None, :] > tile_idx[:, None], tile_end[None, :], n_tiles), axis=1)
    has_next = next_first_tile < n_used
    ng, nlo, nhi = experts(bucket_of(jnp.minimum(next_first_tile, n_used - 1)))
    prefetch_lo = first & has_next & ((ng != g) | (nlo != lo))
    prefetch_hi = first & has_next & ((ng != g) | (nhi != hi))
    as_i32 = lambda v: v.astype(jnp.int32)
    ffn_plan = (g, lo, hi, as_i32(changed_lo), as_i32(changed_hi), ng, nlo, nhi,
                as_i32(prefetch_lo), as_i32(prefetch_hi), n_used.reshape(1).astype(jnp.int32))
    fill = jnp.concatenate([(jnp.maximum(tile_end - 1, 0) * tm).astype(jnp.int32), tiles, n_used[None]])
    return starts, fill, ffn_plan


def kernel(x, c, w_ada, b_ada, g_mix, w_in, conv_dw_w, conv_dw_b, conv_ln_g, conv_ln_b, lru_conv_w,
           lru_conv_b, lru_w_a, lru_b_a, lru_w_i, lru_b_i, lru_lam, gla_w_gate, gla_b_gate, gla_norm_g,
           w_out, g_ffn, w_route_group, b_route_group, w_route_expert, b_route_expert, w_gate, w_up,
           w_down, g_final):
    bsz, seq, d = x.shape
    n_layers = w_ada.shape[0]
    n_tok = bsz * seq
    assert d == D_MODEL and seq % max(MIX_TS, ROUTE_TT, COMB_TC) == 0 and MIX_TS % GLA_BLOCK == 0
    assert n_tok % PERM_TB == 0
    n_tiles = n_tok // FFN_TM + N_BUCKETS
    n_rows = n_tiles * FFN_TM

    mod_all = _modulation(c, w_ada, b_ada).reshape(n_layers, bsz, 6, d)

    params = _prep_params(w_in, conv_dw_w, conv_dw_b, conv_ln_g, conv_ln_b, lru_conv_w, lru_conv_b,
                          lru_w_a, lru_b_a, lru_w_i, lru_b_i, lru_lam, gla_w_gate, gla_b_gate, gla_norm_g,
                          w_out, g_mix)
    n_experts = N_GROUPS * EXPERTS_PER_GROUP
    w_r = jnp.concatenate(
        [w_route_group, w_route_expert.transpose(0, 2, 1, 3).reshape(n_layers, d, n_experts)], axis=2)
    w_r = jnp.pad(w_r, ((0, 0), (0, 0), (0, LANES - N_GROUPS - n_experts)))
    b_r = jnp.concatenate([b_route_group, b_route_expert.reshape(n_layers, n_experts)], axis=1)
    b_r = jnp.pad(b_r, ((0, 0), (0, LANES - N_GROUPS - n_experts)))[:, None, :]
    pending_moe = None
    for l in range(n_layers):
        mod = mod_all[l]
        x = _token_mix(x, mod, params, l, pending_moe)

        x2d = x.reshape(n_tok, d)
        xp, meta, counts = _route(x2d, mod, g_ffn[:, None, :], w_r, b_r, l, seq)
        bucket = meta[0].astype(jnp.int32)
        rank = meta[1].astype(jnp.int32)
        starts, fill, ffn_plan = _bucket_layout(counts[0, :N_BUCKETS], n_tiles)
        in_bucket = bucket[:, None] == jnp.arange(N_BUCKETS, dtype=jnp.int32)[None, :]
        pos = rank + jnp.sum(jnp.where(in_bucket, starts[None, :], 0), axis=1)

        xs = _permute(fill, pos, xp, n_rows)
        y = _expert_ffn(l, ffn_plan, xs, w_gate, w_up, w_down)
        pending_moe = (pos, mod, y)
    return _combine(pos, x2d, mod, g_final[None, :], y, seq).reshape(bsz, seq, d)
```

```python
import functools

import jax
import jax.numpy as jnp
import numpy as np
from jax import lax
from jax.experimental import pallas as pl
from jax.experimental.pallas import tpu as pltpu

D_MODEL = 1024
CONV_W = 256
LRU_W = 384
GLA_V = 384
CONV_K = 31
LRU_CONV_K = 4
LRU_BLOCKS = 6
LRU_BW = 64
LRU_C = 8.0
GLA_HEADS = 4
GLA_DV = 96
GLA_DK = 48
GLA_RANK = 16
GLA_TAU = 16.0
GLA_CHUNK = 64
N_GROUPS = 4
EXPERTS_PER_GROUP = 4
D_EXPERT = 512
EPS = 1e-6

LANES = 128
SUBLANES = 8
VMEM_LIMIT_BYTES = 56 * 1024 * 1024

DK_PAD = 64
DV_PAD = 128
QK_W = GLA_HEADS * DK_PAD
V_W = GLA_HEADS * DV_PAD
OFF_CVV = 0
OFF_CVG = OFF_CVV + CONV_W
OFF_LRX = OFF_CVG + CONV_W
OFF_LRY = OFF_LRX + LRU_W
OFF_Q = OFF_LRY + LRU_W
OFF_K = OFF_Q + QK_W
OFF_V = OFF_K + QK_W
OFF_GLR = OFF_V + V_W
OFF_OG = OFF_GLR + LANES
N_IN = OFF_OG + V_W
MIX_W = CONV_W + LRU_W + V_W

CONV_HIST = 32
LRU_HIST = 8
GLA_BLOCK = 256

N_PAIRS = 6
N_BUCKETS = N_GROUPS * N_PAIRS
ROW_W = D_MODEL + LANES

MIX_TS = 512
ROUTE_TT = 512
ROUTE_SPLIT = 4
FFN_TM = 256
PERM_TB = 2048
COMB_TC = 2048
DMA_UNROLL = 8
GATHER_SLOTS = 3

_F32 = jnp.float32
_BF16 = jnp.bfloat16


def _sigmoid(x):
    return 1.0 / (1.0 + jnp.exp(-x))


def _silu(x):
    return x * _sigmoid(x)


def _dot(a, b):
    return jnp.dot(a, b, preferred_element_type=_F32)


def _dot_nt(a, b):
    return lax.dot_general(a, b, (((1,), (1,)), ((), ())), preferred_element_type=_F32)


def _split2(x):
    hi = x.astype(_BF16)
    return hi, (x - hi.astype(_F32)).astype(_BF16)


def _mod_kernel(c_ref, w_ref, b_ref, o_ref):
    c_act = _silu(c_ref[...])
    o_ref[0] = _dot(c_act.astype(_BF16), w_ref[0].astype(_BF16)) + b_ref[0]


def _modulation(c, w_ada, b_ada):
    n_layers, d, n = w_ada.shape
    bsz = c.shape[0]
    tn = 1536
    return pl.pallas_call(
        _mod_kernel,
        out_shape=jax.ShapeDtypeStruct((n_layers, bsz, n), _F32),
        grid=(n_layers, n // tn),
        in_specs=[
            pl.BlockSpec((bsz, d), lambda l, j: (0, 0)),
            pl.BlockSpec((1, d, tn), lambda l, j: (l, 0, j)),
            pl.BlockSpec((1, 1, tn), lambda l, j: (l, 0, j)),
        ],
        out_specs=pl.BlockSpec((1, bsz, tn), lambda l, j: (l, 0, j)),
        compiler_params=pltpu.CompilerParams(
            dimension_semantics=("arbitrary", "arbitrary"), vmem_limit_bytes=VMEM_LIMIT_BYTES),
        name="adaln_mod",
    )(c, w_ada, b_ada.reshape(n_layers, 1, n))


def _mix_kernel(x_ref, mod_ref, gmix_ref, win_ref, cw_ref, cb_ref, lng_ref, lnb_ref,
                lcw_ref, lcb_ref, wai_ref, bai_ref, lam_ref, wgate_ref, bgate_ref, gnorm_ref,
                wout_ref, *rest, fused_moe_gather):
    if fused_moe_gather:
        pos_cur, pos_nxt, pos_ahd, modp_ref, y_hbm, o_ref = rest[:6]
        ybuf, gsem = rest[-2:]
        rest = rest[6:-2]
    else:
        o_ref, rest = rest[0], rest[1:]
    ubuf, lbuf, hcar, state, tri_ref, tria_ref, csel_ref, causal_ref, hdiag_ref = rest
    ts = x_ref.shape[1]
    s_idx = pl.program_id(1)

    @pl.when(s_idx == 0)
    def _():
        ubuf[0:CONV_HIST, :] = jnp.zeros((CONV_HIST, CONV_W), _F32)
        lbuf[0:LRU_HIST, :] = jnp.zeros((LRU_HIST, LRU_W), _F32)
        hcar[...] = jnp.zeros_like(hcar)
        state[...] = jnp.zeros_like(state)
        ri = lax.broadcasted_iota(jnp.int32, (GLA_BLOCK, GLA_BLOCK), 0)
        ci = lax.broadcasted_iota(jnp.int32, (GLA_BLOCK, GLA_BLOCK), 1)
        same_chunk = ri // GLA_CHUNK == ci // GLA_CHUNK
        causal_f = jnp.where(same_chunk & (ci <= ri), 1.0, 0.0)
        causal_ref[...] = causal_f
        tri_ref[...] = causal_f.astype(_BF16)
        tria_ref[...] = jnp.where(same_chunk & (ci > ri), 1.0, 0.0).astype(_BF16)
        sel_r = lax.broadcasted_iota(jnp.int32, (LANES, GLA_BLOCK), 0)
        sel_c = lax.broadcasted_iota(jnp.int32, (LANES, GLA_BLOCK), 1)
        csel_ref[...] = jnp.where(sel_r == sel_c // GLA_CHUNK, 1.0, 0.0).astype(_BF16)
        st_row_head = lax.broadcasted_iota(jnp.int32, (QK_W, V_W), 0) // DK_PAD
        st_col_head = lax.broadcasted_iota(jnp.int32, (QK_W, V_W), 1) // DV_PAD
        hdiag_ref[...] = jnp.where(st_row_head == st_col_head, 1.0, 0.0)

    x = x_ref[0]
    if fused_moe_gather:
        step = pl.program_id(0) * pl.num_programs(1) + s_idx
        n_steps = pl.num_programs(0) * pl.num_programs(1)
        slot = step % GATHER_SLOTS

        def row_copy(pos_ref, j, to_slot):
            return pltpu.make_async_copy(y_hbm.at[pl.ds(pos_ref[j], 1)], ybuf.at[to_slot, pl.ds(j, 1)],
                                         gsem.at[to_slot])

        def wait_rows(of_slot):
            pltpu.make_async_copy(y_hbm.at[pl.ds(0, ts)], ybuf.at[of_slot], gsem.at[of_slot]).wait()

        @pl.when(step == 0)
        def _():
            def issue(j, carry):
                row_copy(pos_cur, j, 0).start()
                row_copy(pos_nxt, j, 1).start()
                return carry
            lax.fori_loop(0, ts, issue, 0, unroll=DMA_UNROLL)

        wait_rows(slot)
        x = x + modp_ref[0, 5:6, :] * ybuf[slot]

        ahead_slot = (step + GATHER_SLOTS - 1) % GATHER_SLOTS
        for j in range(ts):
            row_copy(pos_ahd, j, ahead_slot).start()

    sh1 = mod_ref[0, 0:1, :]
    gt1 = mod_ref[0, 2:3, :]
    scale = gmix_ref[...] * (1.0 + mod_ref[0, 1:2, :])
    ms = jnp.mean(x * x, axis=-1, keepdims=True)
    hb = (x * lax.rsqrt(ms + EPS) * scale + sh1).astype(_BF16)

    def proj(off, width):
        return _dot(hb, win_ref[:, off:off + width])

    out = {}
    _run_alternately(
        _conv_lru_steps(out, proj, cw_ref, cb_ref, lng_ref, lnb_ref, lcw_ref, lcb_ref, wai_ref, bai_ref,
                        lam_ref, ubuf, lbuf, hcar, ts),
        _gla_steps(out, proj, wgate_ref, bgate_ref, gnorm_ref, state, tri_ref, tria_ref, csel_ref,
                   causal_ref, hdiag_ref, ts))

    mixed = (_dot(out["o_g"].astype(_BF16), wout_ref[CONV_W + LRU_W:MIX_W, :])
             + _dot(out["r_out"].astype(_BF16), wout_ref[CONV_W:CONV_W + LRU_W, :])
             + _dot(out["u_out"].astype(_BF16), wout_ref[0:CONV_W, :]))
    o_ref[0] = x + gt1 * mixed

    if fused_moe_gather:
        @pl.when(step == n_steps - 1)
        def _():
            wait_rows((step + 1) % GATHER_SLOTS)
            wait_rows((step + 2) % GATHER_SLOTS)


def _run_alternately(*step_generators):
    clock = [0] * len(step_generators)
    live = list(range(len(step_generators)))
    while live:
        i = min(live, key=lambda j: clock[j])
        try:
            clock[i] += next(step_generators[i]) * (1 if i else 0.5)
        except StopIteration:
            live.remove(i)


def _conv_lru_steps(out, proj, cw_ref, cb_ref, lng_ref, lnb_ref, lcw_ref, lcb_ref, wai_ref, bai_ref,
                    lam_ref, ubuf, lbuf, hcar, ts):
    u = proj(OFF_CVV, CONV_W) * _sigmoid(proj(OFF_CVG, CONV_W))
    ubuf[CONV_HIST:CONV_HIST + ts, :] = u
    yield 500
    lbuf[LRU_HIST:LRU_HIST + ts, :] = proj(OFF_LRX, LRU_W)
    xb = jnp.broadcast_to(lcb_ref[...], (ts, LRU_W))
    for k in range(LRU_CONV_K):
        off = LRU_HIST - (LRU_CONV_K - 1) + k
        xb = xb + lcw_ref[k:k + 1, :] * lbuf[off:off + ts, :]
    lbuf[0:LRU_HIST, :] = lbuf[ts:ts + LRU_HIST, :]
    gates = _dot(xb.astype(_BF16), wai_ref[...]) + bai_ref[...]
    yield 800

    acc = jnp.broadcast_to(cb_ref[...], (ts, CONV_W))
    for r in range(SUBLANES):
        z = None
        for a_blk in range((CONV_K - 1 - r) // SUBLANES + 1):
            k = CONV_K - 1 - (SUBLANES * a_blk + r)
            off = CONV_HIST - SUBLANES * (a_blk + 1)
            term = cw_ref[k:k + 1, :] * ubuf[off:off + ts + SUBLANES, :]
            z = term if z is None else z + term
        acc = acc + z[SUBLANES - r:SUBLANES - r + ts]
        yield 200
    ubuf[0:CONV_HIST, :] = ubuf[ts:ts + CONV_HIST, :]
    mu = jnp.mean(acc, axis=-1, keepdims=True)
    cen = acc - mu
    var = jnp.mean(cen * cen, axis=-1, keepdims=True)
    out["u_out"] = _silu(cen * lax.rsqrt(var + EPS) * lng_ref[...] + lnb_ref[...])
    yield 250

    r_gate = _sigmoid(gates[:, 0:LRU_W])
    i_gate = _sigmoid(gates[:, LRU_W:2 * LRU_W])
    lam = lam_ref[...]
    softplus_neg_lam = jnp.maximum(-lam, 0.0) + jnp.log1p(jnp.exp(-jnp.abs(lam)))
    log_a = (-LRU_C) * r_gate * softplus_neg_lam
    a = jnp.exp(log_a)
    mult = jnp.sqrt(jnp.tanh(-log_a) * (a * a + 1.0))
    u_in = mult * (i_gate * xb)
    yield 600
    row = lax.broadcasted_iota(jnp.int32, a.shape, 0) % SUBLANES
    d = 1
    while d < SUBLANES:
        keep = row >= d
        a_sh = jnp.where(keep, pltpu.roll(a, d, axis=0), 1.0)
        u_sh = jnp.where(keep, pltpu.roll(u_in, d, axis=0), 0.0)
        u_in = u_in + a * u_sh
        a = a * a_sh
        d *= 2
        yield 200
    groups = []
    h_prev = hcar[0:1, :]
    for g in range(ts // SUBLANES):
        r0 = g * SUBLANES
        h_g = u_in[r0:r0 + SUBLANES] + a[r0:r0 + SUBLANES] * h_prev
        groups.append(h_g)
        h_prev = h_g[SUBLANES - 1:SUBLANES]
        if g % SUBLANES == SUBLANES - 1:
            yield 50
    h_lru = jnp.concatenate(groups, axis=0)
    hcar[...] = jnp.broadcast_to(h_prev, hcar.shape)
    out["r_out"] = h_lru * jax.nn.gelu(proj(OFF_LRY, LRU_W), approximate=True)


def _gla_steps(out, proj, wgate_ref, bgate_ref, gnorm_ref, state, tri_ref, tria_ref, csel_ref, causal_ref,
               hdiag_ref, ts):
    zg = proj(OFF_GLR, LANES)
    glog = _dot(zg.astype(_BF16), wgate_ref[...]) + bgate_ref[...]
    lg = (jnp.minimum(glog, 0.0) - jnp.log1p(jnp.exp(-jnp.abs(glog)))) * (1.0 / GLA_TAU)
    yield 300
    zq = proj(OFF_Q, QK_W) * (GLA_DK ** -0.5)
    zk = proj(OFF_K, QK_W)
    yield 500
    zv = proj(OFF_V, V_W)
    yield 500

    nblk = ts // GLA_BLOCK
    cpb = GLA_BLOCK // GLA_CHUNK
    tri = tri_ref[...]
    tri_after = tria_ref[...]
    chunk_sel = csel_ref[...]
    causal = causal_ref[...] > 0.5
    qk_lane_head = lax.broadcasted_iota(jnp.int32, (1, QK_W), 1) // DK_PAD
    t_lane_chunk = lax.broadcasted_iota(jnp.int32, (1, GLA_BLOCK), 1) // GLA_CHUNK

    o_blocks = []
    for blk in range(nblk):
        r0 = blk * GLA_BLOCK
        lg_b = lg[r0:r0 + GLA_BLOCK]
        p_hi, p_lo = _split2(lg_b)
        b = _dot(tri, p_hi) + _dot(tri, p_lo)
        b_rest = _dot(tri_after, p_hi) + _dot(tri_after, p_lo)
        b_tot = _dot(chunk_sel, p_hi) + _dot(chunk_sel, p_lo)
        yield 500
        q_in =zq[r0:r0 + GLA_BLOCK] * jnp.exp(b)
        k_blk = zk[r0:r0 + GLA_BLOCK]
        k_in = (k_blk * jnp.exp(-b)).astype(_BF16)
        k_out_t = (k_blk * jnp.exp(b_rest)).T.astype(_BF16)
        v_b = zv[r0:r0 + GLA_BLOCK].astype(_BF16)
        q_in_b = q_in.astype(_BF16)
        decay_cols = jnp.exp(b_tot).T

        o_heads = []
        for hd in range(GLA_HEADS):
            q_h = jnp.where(qk_lane_head == hd, q_in_b, jnp.zeros_like(q_in_b))
            sc = jnp.where(causal, _dot_nt(q_h, k_in), 0.0)
            o_heads.append(_dot(sc.astype(_BF16), v_b[:, hd * DV_PAD:(hd + 1) * DV_PAD]))
            yield 300
        o_intra = jnp.concatenate(o_heads, axis=1)

        o_inter = []
        st = state[...]
        for c in range(cpb):
            c0 = c * GLA_CHUNK
            o_inter.append(_dot(q_in_b[c0:c0 + GLA_CHUNK], st.astype(_BF16)))
            kv = _dot(jnp.where(t_lane_chunk == c, k_out_t, jnp.zeros_like(k_out_t)), v_b)
            st = st * decay_cols[:, c:c + 1] + kv * hdiag_ref[...]
            yield 400
        state[...] = st
        o_blocks.append(o_intra + jnp.concatenate(o_inter, axis=0))
    o = o_blocks[0] if nblk == 1 else jnp.concatenate(o_blocks, axis=0)

    og = proj(OFF_OG, V_W)
    o_parts = []
    for hd in range(GLA_HEADS):
        o_h = o[:, hd * DV_PAD:(hd + 1) * DV_PAD]
        ms_h = jnp.sum(o_h * o_h, axis=-1, keepdims=True) * (1.0 / GLA_DV)
        o_parts.append(o_h * lax.rsqrt(ms_h + EPS))
    o_n = jnp.concatenate(o_parts, axis=1) * gnorm_ref[...]
    out["o_g"] = o_n * _silu(og)


def _token_mix(x, mod, params, layer, moe=None):
    bsz, seq, d = x.shape
    ts = MIX_TS
    n_s = seq // ts
    full = lambda shape: pl.BlockSpec((None,) + shape, lambda b, s: (layer,) + (0,) * len(shape))
    in_specs = [
        pl.BlockSpec((1, ts, d), lambda b, s: (b, s, 0)),
        pl.BlockSpec((1, 6, d), lambda b, s: (b, 0, 0)),
        full((1, d)),
        full((d, N_IN)),
        full((CONV_HIST, CONV_W)), full((1, CONV_W)), full((1, CONV_W)), full((1, CONV_W)),
        full((LRU_CONV_K, LRU_W)), full((1, LRU_W)),
        full((LRU_W, 2 * LRU_W)), full((1, 2 * LRU_W)), full((1, LRU_W)),
        full((LANES, QK_W)), full((1, QK_W)), full((1, V_W)),
        full((MIX_W, d)),
    ]
    scratch_shapes = [
        pltpu.VMEM((CONV_HIST + ts, CONV_W), _F32),
        pltpu.VMEM((LRU_HIST + ts, LRU_W), _F32),
        pltpu.VMEM((SUBLANES, LRU_W), _F32),
        pltpu.VMEM((QK_W, V_W), _F32),
        pltpu.VMEM((GLA_BLOCK, GLA_BLOCK), _BF16),
        pltpu.VMEM((GLA_BLOCK, GLA_BLOCK), _BF16),
        pltpu.VMEM((LANES, GLA_BLOCK), _BF16),
        pltpu.VMEM((GLA_BLOCK, GLA_BLOCK), _F32),
        pltpu.VMEM((QK_W, V_W), _F32),
    ]
    args = [x, mod] + list(params)
    if moe is not None:
        pos, mod_prev, y = moe
        last = bsz * n_s - 1
        in_specs += [
            pl.BlockSpec((ts,), lambda b, s: (b * n_s + s,), memory_space=pltpu.SMEM),
            pl.BlockSpec((ts,), lambda b, s: (jnp.minimum(b * n_s + s + 1, last),), memory_space=pltpu.SMEM),
            pl.BlockSpec((ts,), lambda b, s: (jnp.minimum(b * n_s + s + 2, last),), memory_space=pltpu.SMEM),
            pl.BlockSpec((1, 6, d), lambda b, s: (b, 0, 0)),
            pl.BlockSpec(memory_space=pl.ANY),
        ]
        scratch_shapes += [pltpu.VMEM((GATHER_SLOTS, ts, d), _F32), pltpu.SemaphoreType.DMA((GATHER_SLOTS,))]
        args += [pos, pos, pos, mod_prev, y]
    return pl.pallas_call(
        functools.partial(_mix_kernel, fused_moe_gather=moe is not None),
        out_shape=jax.ShapeDtypeStruct(x.shape, _F32),
        grid=(bsz, n_s),
        in_specs=in_specs,
        out_specs=pl.BlockSpec((1, ts, d), lambda b, s: (b, s, 0)),
        scratch_shapes=scratch_shapes,
        compiler_params=pltpu.CompilerParams(
            dimension_semantics=("arbitrary", "arbitrary"), vmem_limit_bytes=VMEM_LIMIT_BYTES),
        name="token_mix",
    )(*args)


def _route_kernel(x_ref, mod_ref, g_ref, wr_ref, br_ref, xp_ref, meta_ref, cnt_ref, carry, strict_ref):
    tt = x_ref.shape[0]
    i = pl.program_id(0)

    @pl.when(i == 0)
    def _():
        carry[...] = jnp.zeros_like(carry)
        ri = lax.broadcasted_iota(jnp.int32, (tt, tt), 0)
        ci = lax.broadcasted_iota(jnp.int32, (tt, tt), 1)
        strict_ref[...] = jnp.where(ci < ri, 1.0, 0.0).astype(_BF16)

    sh2 = mod_ref[0, 3:4, :]
    scale = g_ref[...] * (1.0 + mod_ref[0, 4:5, :])
    w_hi, w_lo = _split2(wr_ref[...])
    w_cat = jnp.concatenate([w_hi, w_lo], axis=1)
    n_half = tt // ROUTE_SPLIT
    counts_before = [carry[0:1, :]]

    def route_rows(r0):
        x = x_ref[r0:r0 + n_half, :]
        ms = jnp.mean(x * x, axis=-1, keepdims=True)
        h = x * lax.rsqrt(ms + EPS) * scale + sh2
        xp_ref[r0:r0 + n_half, 0:D_MODEL] = h
        h_hi, h_lo = _split2(h)
        yield
        prod = _dot(h_hi, w_cat) + _dot(h_lo, w_cat)
        logits = prod[:, 0:LANES] + prod[:, LANES:2 * LANES] + br_ref[...]
        yield

        lane = lax.broadcasted_iota(jnp.int32, (n_half, LANES), 1)
        lane_f = lane.astype(_F32)
        neg = -jnp.inf
        big = float(LANES)

        def first_argmax(vals, vmax):
            return jnp.min(jnp.where(vals == vmax, lane_f, big), axis=-1, keepdims=True).astype(jnp.int32)

        gl = jnp.where(lane < N_GROUPS, logits, neg)
        gmax = jnp.max(gl, axis=-1, keepdims=True)
        g_star = first_argmax(gl, gmax)
        p_sel = 1.0 / jnp.sum(jnp.exp(gl - gmax), axis=-1, keepdims=True)
        base = N_GROUPS + EXPERTS_PER_GROUP * g_star
        el = jnp.where((lane >= base) & (lane < base + EXPERTS_PER_GROUP), logits, neg)
        v0 = jnp.max(el, axis=-1, keepdims=True)
        i0 = first_argmax(el, v0)
        el2 = jnp.where(lane == i0, neg, el)
        v1 = jnp.max(el2, axis=-1, keepdims=True)
        i1 = first_argmax(el2, v1)
        ex = jnp.exp(v1 - v0)
        wt0 = p_sel / (1.0 + ex)
        wt1 = p_sel * ex / (1.0 + ex)
        e0 = i0 - base
        e1 = i1 - base
        e_lo = jnp.minimum(e0, e1)
        e_hi = jnp.maximum(e0, e1)
        w_lo = jnp.where(e0 < e1, wt0, wt1)
        w_hi = jnp.where(e0 < e1, wt1, wt0)
        pair = (e_lo * (2 * EXPERTS_PER_GROUP - 1 - e_lo)) // 2 + (e_hi - e_lo - 1)
        bucket = g_star * N_PAIRS + pair
        yield

        onehot = lane == bucket
        onehot_f = jnp.where(onehot, 1.0, 0.0)
        before = counts_before[-1]
        prefix = _dot(strict_ref[0:n_half, 0:n_half], onehot_f.astype(_BF16)) + before
        rank = jnp.sum(jnp.where(onehot, prefix, 0.0), axis=-1, keepdims=True)
        counts_before.append(before + jnp.sum(onehot_f, axis=0, keepdims=True))

        meta = jnp.where(lane == 0, bucket.astype(_F32), jnp.where(lane == 1, rank, 0.0))
        meta_ref[:, r0:r0 + n_half] = meta.T[0:SUBLANES, :]
        xp_ref[r0:r0 + n_half, D_MODEL:ROW_W] = jnp.where(lane == 0, w_lo, jnp.where(lane == 1, w_hi, 0.0))

    blocks = [route_rows(k * n_half) for k in range(ROUTE_SPLIT)]
    while blocks:
        for blk in list(blocks):
            if next(blk, "done") == "done":
                blocks.remove(blk)
    total = jnp.broadcast_to(counts_before[-1], carry.shape)
    carry[...] = total
    cnt_ref[...] = total


def _route(x2d, mod, g_ffn, w_r, b_r, layer, seq):
    n_tok, d = x2d.shape
    tt = ROUTE_TT
    tiles_per_seq = seq // tt
    per_layer = lambda shape: pl.BlockSpec((None,) + shape, lambda i: (layer, 0, 0))
    return pl.pallas_call(
        _route_kernel,
        out_shape=(
            jax.ShapeDtypeStruct((n_tok, ROW_W), _F32),
            jax.ShapeDtypeStruct((SUBLANES, n_tok), _F32),
            jax.ShapeDtypeStruct((SUBLANES, LANES), _F32),
        ),
        grid=(n_tok // tt,),
        in_specs=[
            pl.BlockSpec((tt, d), lambda i: (i, 0)),
            pl.BlockSpec((1, 6, d), lambda i: (i // tiles_per_seq, 0, 0)),
            per_layer((1, d)),
            per_layer((d, LANES)),
            per_layer((1, LANES)),
        ],
        out_specs=(
            pl.BlockSpec((tt, ROW_W), lambda i: (i, 0)),
            pl.BlockSpec((SUBLANES, tt), lambda i: (0, i)),
            pl.BlockSpec((SUBLANES, LANES), lambda i: (0, 0)),
        ),
        scratch_shapes=[pltpu.VMEM((SUBLANES, LANES), _F32), pltpu.VMEM((tt, tt), _BF16)],
        compiler_params=pltpu.CompilerParams(
            dimension_semantics=("arbitrary",), vmem_limit_bytes=VMEM_LIMIT_BYTES),
        name="moe_route",
    )(x2d, mod, g_ffn, w_r, b_r)


def _permute_kernel(fill_ref, pos_ref, xp_ref, xs_hbm, zbuf, sem, zsem):
    tb = pos_ref.shape[0]

    @pl.when(pl.program_id(0) == 0)
    def _():
        zbuf[...] = jnp.zeros_like(zbuf)
        n_used = fill_ref[2 * N_BUCKETS]
        n_tiles = xs_hbm.shape[0] // FFN_TM
        for wait in (False, True):
            for b in range(N_BUCKETS):
                for row0, cond in ((fill_ref[b], fill_ref[N_BUCKETS + b] > 0),
                                   ((n_used + b) * FFN_TM, n_used + b < n_tiles)):
                    @pl.when(cond)
                    def _():
                        dst = xs_hbm.at[pl.ds(pl.multiple_of(row0, FFN_TM), FFN_TM)]
                        fill = pltpu.make_async_copy(zbuf, dst, zsem)
                        fill.wait() if wait else fill.start()

    for j in range(tb):
        pltpu.make_async_copy(xp_ref.at[pl.ds(j, 1)], xs_hbm.at[pl.ds(pos_ref[j], 1)], sem).start(priority=j % 2)
    pltpu.make_async_copy(xp_ref, xs_hbm.at[pl.ds(0, tb)], sem).wait()


def _permute(fill_starts, pos, xp, n_rows):
    n_tok = xp.shape[0]
    tb = PERM_TB
    return pl.pallas_call(
        _permute_kernel,
        out_shape=jax.ShapeDtypeStruct((n_rows, ROW_W), _F32),
        grid_spec=pltpu.PrefetchScalarGridSpec(
            num_scalar_prefetch=1,
            grid=(n_tok // tb,),
            in_specs=[
                pl.BlockSpec((tb,), lambda i, fs: (i,), memory_space=pltpu.SMEM),
                pl.BlockSpec((tb, ROW_W), lambda i, fs: (i, 0)),
            ],
            out_specs=pl.BlockSpec(memory_space=pl.ANY),
            scratch_shapes=[pltpu.VMEM((FFN_TM, ROW_W), _F32), pltpu.SemaphoreType.DMA,
                            pltpu.SemaphoreType.DMA],
        ),
        compiler_params=pltpu.CompilerParams(
            dimension_semantics=("arbitrary",), vmem_limit_bytes=VMEM_LIMIT_BYTES),
        name="moe_permute",
    )(fill_starts, pos, xp)


def _ffn_kernel(tg_ref, tlo_ref, thi_ref, clo_ref, chi_ref, ng_ref, nlo_ref, nhi_ref, plo_ref, phi_ref, nused_ref,
                xs_ref, wg_hbm, wu_hbm, wd_hbm, y_ref, sg, su, sd, agu, ad, sem, *, layer):
    i = pl.program_id(0)

    def weight_copies(slot, g, e):
        return (pltpu.make_async_copy(wg_hbm.at[layer, g, e], sg.at[slot], sem.at[slot, 0]),
                pltpu.make_async_copy(wu_hbm.at[layer, g, e], su.at[slot], sem.at[slot, 1]),
                pltpu.make_async_copy(wd_hbm.at[layer, g, e], sd.at[slot], sem.at[slot, 2]))

    @pl.when(i == 0)
    def _():
        for cp in weight_copies(0, tg_ref[0], tlo_ref[0]) + weight_copies(1, tg_ref[0], thi_ref[0]):
            cp.start()

    for slot, changed_ref, e_ref in ((0, clo_ref, tlo_ref), (1, chi_ref, thi_ref)):
        @pl.when(changed_ref[i] == 1)
        def _():
            for cp in weight_copies(slot, tg_ref[i], e_ref[i]):
                cp.wait()
            agu[slot, :, 0:D_EXPERT] = sg[slot].astype(_BF16)
            agu[slot, :, D_EXPERT:2 * D_EXPERT] = su[slot].astype(_BF16)
            ad[slot] = sd[slot].astype(_BF16)

    for slot, prefetch_ref, e_ref in ((0, plo_ref, nlo_ref), (1, phi_ref, nhi_ref)):
        @pl.when(prefetch_ref[i] == 1)
        def _():
            for cp in weight_copies(slot, ng_ref[i], e_ref[i]):
                cp.start()

    @pl.when(i < nused_ref[0])
    def _():
        xb = xs_ref[:, 0:D_MODEL].astype(_BF16)
        info = xs_ref[:, D_MODEL:ROW_W]

        def expert(slot):
            gate_up = _dot(xb, agu[slot])
            hid = _silu(gate_up[:, 0:D_EXPERT]) * gate_up[:, D_EXPERT:2 * D_EXPERT] * info[:, slot:slot + 1]
            return _dot(hid.astype(_BF16), ad[slot])

        y_ref[...] = expert(0) + expert(1)

    @pl.when(i >= nused_ref[0])
    def _():
        y_ref[...] = jnp.zeros_like(y_ref)


def _expert_ffn(layer, plan, xs, w_gate, w_up, w_down):
    n_rows = xs.shape[0]
    tm = FFN_TM
    n_tiles = n_rows // tm

    def row_map(i, *prefetch):
        n_used = prefetch[-1]
        return (jnp.minimum(i, n_used[0] - 1), 0)

    hbm = pl.BlockSpec(memory_space=pl.ANY)
    return pl.pallas_call(
        functools.partial(_ffn_kernel, layer=layer),
        out_shape=jax.ShapeDtypeStruct((n_rows, D_MODEL), _F32),
        grid_spec=pltpu.PrefetchScalarGridSpec(
            num_scalar_prefetch=len(plan),
            grid=(n_tiles,),
            in_specs=[pl.BlockSpec((tm, ROW_W), row_map), hbm, hbm, hbm],
            out_specs=pl.BlockSpec((tm, D_MODEL), lambda i, *prefetch: (i, 0)),
            scratch_shapes=[
                pltpu.VMEM((2, D_MODEL, D_EXPERT), _F32), pltpu.VMEM((2, D_MODEL, D_EXPERT), _F32),
                pltpu.VMEM((2, D_EXPERT, D_MODEL), _F32),
                pltpu.VMEM((2, D_MODEL, 2 * D_EXPERT), _BF16), pltpu.VMEM((2, D_EXPERT, D_MODEL), _BF16),
                pltpu.SemaphoreType.DMA((2, 3)),
            ],
        ),
        compiler_params=pltpu.CompilerParams(
            dimension_semantics=("arbitrary",), vmem_limit_bytes=VMEM_LIMIT_BYTES),
        name="moe_ffn",
    )(*plan, xs, w_gate, w_up, w_down)


def _combine_kernel(pos_ref, x_ref, mod_ref, gfin_ref, y_hbm, o_ref, ybuf, sem):
    tc = x_ref.shape[0]
    for j in range(tc):
        pltpu.make_async_copy(y_hbm.at[pl.ds(pos_ref[j], 1)], ybuf.at[pl.ds(j, 1)], sem).start(priority=j % 2)
    pltpu.make_async_copy(y_hbm.at[pl.ds(0, tc)], ybuf, sem).wait()

    gt2 = mod_ref[0, 5:6, :]
    out = x_ref[...] + gt2 * ybuf[...]
    ms = jnp.mean(out * out, axis=-1, keepdims=True)
    o_ref[...] = out * lax.rsqrt(ms + EPS) * gfin_ref[...]


def _combine(pos, x2d, mod, g_final, y, seq):
    n_tok, d = x2d.shape
    tc = COMB_TC
    tiles_per_seq = seq // tc
    return pl.pallas_call(
        _combine_kernel,
        out_shape=jax.ShapeDtypeStruct((n_tok, d), _F32),
        grid=(n_tok // tc,),
        in_specs=[
            pl.BlockSpec((tc,), lambda i: (i,), memory_space=pltpu.SMEM),
            pl.BlockSpec((tc, d), lambda i: (i, 0)),
            pl.BlockSpec((1, 6, d), lambda i: (i // tiles_per_seq, 0, 0)),
            pl.BlockSpec((1, d), lambda i: (0, 0)),
            pl.BlockSpec(memory_space=pl.ANY),
        ],
        out_specs=pl.BlockSpec((tc, d), lambda i: (i, 0)),
        scratch_shapes=[pltpu.VMEM((tc, d), _F32), pltpu.SemaphoreType.DMA],
        compiler_params=pltpu.CompilerParams(
            dimension_semantics=("arbitrary",), vmem_limit_bytes=VMEM_LIMIT_BYTES),
        name="moe_combine",
    )(pos, x2d, mod, g_final, y)


def _pad_heads(w, heads, width, padded):
    lead = w.shape[:-1]
    w = w.reshape(lead + (heads, width))
    w = jnp.pad(w, [(0, 0)] * len(lead) + [(0, 0), (0, padded - width)])
    return w.reshape(lead + (heads * padded,))


def _block_diag(w):
    n_layers, n, bw, _ = w.shape
    eye = jnp.eye(n, dtype=w.dtype)
    return (eye[None, :, None, :, None] * w[:, :, :, None, :]).reshape(n_layers, n * bw, n * bw)


def _prep_params(w_in, conv_dw_w, conv_dw_b, conv_ln_g, conv_ln_b, lru_conv_w, lru_conv_b, lru_w_a,
                 lru_b_a, lru_w_i, lru_b_i, lru_lam, gla_w_gate, gla_b_gate, gla_norm_g, w_out, g_mix):
    sizes = [CONV_W, CONV_W, LRU_W, LRU_W, GLA_HEADS * GLA_DK, GLA_HEADS * GLA_DK, GLA_V, GLA_RANK, GLA_V]
    cv_v, cv_g, lr_x, lr_y, q, k, v, g_lr, og = jnp.split(w_in, np.cumsum(sizes)[:-1].tolist(), axis=-1)
    w_in_p = jnp.concatenate([
        cv_v, cv_g, lr_x, lr_y,
        _pad_heads(q, GLA_HEADS, GLA_DK, DK_PAD), _pad_heads(k, GLA_HEADS, GLA_DK, DK_PAD),
        _pad_heads(v, GLA_HEADS, GLA_DV, DV_PAD),
        jnp.pad(g_lr, ((0, 0), (0, 0), (0, LANES - GLA_RANK))),
        _pad_heads(og, GLA_HEADS, GLA_DV, DV_PAD)], axis=-1).astype(_BF16)
    n_layers = w_in.shape[0]
    wo_o = w_out[:, CONV_W + LRU_W:].reshape(n_layers, GLA_HEADS, GLA_DV, D_MODEL)
    wo_o = jnp.pad(wo_o, ((0, 0), (0, 0), (0, DV_PAD - GLA_DV), (0, 0))).reshape(n_layers, V_W, D_MODEL)
    w_out_p = jnp.concatenate([w_out[:, :CONV_W + LRU_W], wo_o], axis=1).astype(_BF16)
    w_gate_p = jnp.pad(_pad_heads(gla_w_gate, GLA_HEADS, GLA_DK, DK_PAD),
                       ((0, 0), (0, LANES - GLA_RANK), (0, 0))).astype(_BF16)
    row = lambda v: v[:, None, :]
    return [
        row(g_mix),
        w_in_p,
        jnp.pad(conv_dw_w, ((0, 0), (0, CONV_HIST - CONV_K), (0, 0))),
        row(conv_dw_b), row(conv_ln_g), row(conv_ln_b),
        lru_conv_w, row(lru_conv_b),
        jnp.concatenate([_block_diag(lru_w_a), _block_diag(lru_w_i)], axis=2).astype(_BF16),
        row(jnp.concatenate([lru_b_a, lru_b_i], axis=1)),
        row(lru_lam),
        w_gate_p,
        row(_pad_heads(gla_b_gate, GLA_HEADS, GLA_DK, DK_PAD)),
        row(_pad_heads(gla_norm_g, GLA_HEADS, GLA_DV, DV_PAD)),
        w_out_p,
    ]


def _bucket_layout(counts, n_tiles):
    tm = FFN_TM
    counts = counts.astype(jnp.int32)
    tiles = (counts + tm - 1) // tm
    upto = jnp.arange(N_BUCKETS)[:, None] <= jnp.arange(N_BUCKETS)[None, :]
    tile_end = jnp.sum(jnp.where(upto, tiles[:, None], 0), axis=0)
    starts = ((tile_end - tiles) * tm).astype(jnp.int32)
    n_used = tile_end[-1]
    tile_idx = jnp.minimum(jnp.arange(n_tiles, dtype=jnp.int32), n_used - 1)

    def bucket_of(t):
        return jnp.sum((t[:, None] >= tile_end[None, :]).astype(jnp.int32), axis=1)

    def experts(b):
        pair = b % N_PAIRS
        lo = (pair >= 3).astype(jnp.int32) + (pair >= 5).astype(jnp.int32)
        hi = jnp.where(pair < 3, pair + 1, jnp.where(pair < 5, pair - 1, 3))
        return b // N_PAIRS, lo, hi

    tile_bucket = bucket_of(tile_idx)
    prev_bucket = jnp.where(tile_idx >= 1, bucket_of(tile_idx - 1), -1)
    g, lo, hi = experts(tile_bucket)
    pg, plo, phi = experts(prev_bucket)
    first = (tile_bucket != prev_bucket) & (jnp.arange(n_tiles) < n_used)
    fresh = prev_bucket < 0
    changed_lo = first & (fresh | (g != pg) | (lo != plo))
    changed_hi = first & (fresh | (g != pg) | (hi != phi))
    next_first_tile = jnp.min(jnp.where(tile_end[None, :] > tile_idx[:, None], tile_end[None, :], n_tiles), axis=1)
    has_next = next_first_tile < n_used
    ng, nlo, nhi = experts(bucket_of(jnp.minimum(next_first_tile, n_used - 1)))
    prefetch_lo = first & has_next & ((ng != g) | (nlo != lo))
    prefetch_hi = first & has_next & ((ng != g) | (nhi != hi))
    as_i32 = lambda v: v.astype(jnp.int32)
    ffn_plan = (g, lo, hi, as_i32(changed_lo), as_i32(changed_hi), ng, nlo, nhi,
                as_i32(prefetch_lo), as_i32(prefetch_hi), n_used.reshape(1).astype(jnp.int32))
    fill = jnp.concatenate([(jnp.maximum(tile_end - 1, 0) * tm).astype(jnp.int32), tiles, n_used[None]])
    return starts, fill, ffn_plan


def kernel(x, c, w_ada, b_ada, g_mix, w_in, conv_dw_w, conv_dw_b, conv_ln_g, conv_ln_b, lru_conv_w,
           lru_conv_b, lru_w_a, lru_b_a, lru_w_i, lru_b_i, lru_lam, gla_w_gate, gla_b_gate, gla_norm_g,
           w_out, g_ffn, w_route_group, b_route_group, w_route_expert, b_route_expert, w_gate, w_up,
           w_down, g_final):
    bsz, seq, d = x.shape
    n_layers = w_ada.shape[0]
    n_tok = bsz * seq
    assert d == D_MODEL and seq % max(MIX_TS, ROUTE_TT, COMB_TC) == 0 and MIX_TS % GLA_BLOCK == 0
    assert n_tok % PERM_TB == 0
    n_tiles = n_tok // FFN_TM + N_BUCKETS
    n_rows = n_tiles * FFN_TM

    mod_all = _modulation(c, w_ada, b_ada).reshape(n_layers, bsz, 6, d)

    params = _prep_params(w_in, conv_dw_w, conv_dw_b, conv_ln_g, conv_ln_b, lru_conv_w, lru_conv_b,
                          lru_w_a, lru_b_a, lru_w_i, lru_b_i, lru_lam, gla_w_gate, gla_b_gate, gla_norm_g,
                          w_out, g_mix)
    n_experts = N_GROUPS * EXPERTS_PER_GROUP
    w_r = jnp.concatenate(
        [w_route_group, w_route_expert.transpose(0, 2, 1, 3).reshape(n_layers, d, n_experts)], axis=2)
    w_r = jnp.pad(w_r, ((0, 0), (0, 0), (0, LANES - N_GROUPS - n_experts)))
    b_r = jnp.concatenate([b_route_group, b_route_expert.reshape(n_layers, n_experts)], axis=1)
    b_r = jnp.pad(b_r, ((0, 0), (0, LANES - N_GROUPS - n_experts)))[:, None, :]
    pending_moe = None
    for l in range(n_layers):
        mod = mod_all[l]
        x = _token_mix(x, mod, params, l, pending_moe)

        x2d = x.reshape(n_tok, d)
        xp, meta, counts = _route(x2d, mod, g_ffn[:, None, :], w_r, b_r, l, seq)
        bucket = meta[0].astype(jnp.int32)
        rank = meta[1].astype(jnp.int32)
        starts, fill, ffn_plan = _bucket_layout(counts[0, :N_BUCKETS], n_tiles)
        in_bucket = bucket[:, None] == jnp.arange(N_BUCKETS, dtype=jnp.int32)[None, :]
        pos = rank + jnp.sum(jnp.where(in_bucket, starts[None, :], 0), axis=1)

        xs = _permute(fill, pos, xp, n_rows)
        y = _expert_ffn(l, ffn_plan, xs, w_gate, w_up, w_down)
        pending_moe = (pos, mod, y)
    return _combine(pos, x2d, mod, g_final[None, :], y, seq).reshape(bsz, seq, d)
```

```python
import functools

import jax
import jax.numpy as jnp
import numpy as np
from jax import lax
from jax.experimental import pallas as pl
from jax.experimental.pallas import tpu as pltpu

D_MODEL = 1024
CONV_W = 256
LRU_W = 384
GLA_V = 384
CONV_K = 31
LRU_CONV_K = 4
LRU_BLOCKS = 6
LRU_BW = 64
LRU_C = 8.0
GLA_HEADS = 4
GLA_DV = 96
GLA_DK = 48
GLA_RANK = 16
GLA_TAU = 16.0
GLA_CHUNK = 64
N_GROUPS = 4
EXPERTS_PER_GROUP = 4
D_EXPERT = 512
EPS = 1e-6

LANES = 128
SUBLANES = 8
VMEM_LIMIT_BYTES = 56 * 1024 * 1024

DK_PAD = 64
DV_PAD = 128
QK_W = GLA_HEADS * DK_PAD
V_W = GLA_HEADS * DV_PAD
OFF_CVV = 0
OFF_CVG = OFF_CVV + CONV_W
OFF_LRX = OFF_CVG + CONV_W
OFF_LRY = OFF_LRX + LRU_W
OFF_Q = OFF_LRY + LRU_W
OFF_K = OFF_Q + QK_W
OFF_V = OFF_K + QK_W
OFF_GLR = OFF_V + V_W
OFF_OG = OFF_GLR + LANES
N_IN = OFF_OG + V_W
MIX_W = CONV_W + LRU_W + V_W

CONV_HIST = 32
LRU_HIST = 8
GLA_BLOCK = 256

N_PAIRS = 6
N_BUCKETS = N_GROUPS * N_PAIRS
ROW_W = D_MODEL + LANES

MIX_TS = 512
ROUTE_TT = 1024
ROUTE_SPLIT = 8
FFN_TM = 256
PERM_TB = 2048
COMB_TC = 2048
DMA_UNROLL = 8
GATHER_SLOTS = 3

_F32 = jnp.float32
_BF16 = jnp.bfloat16


def _sigmoid(x):
    return 1.0 / (1.0 + jnp.exp(-x))


def _silu(x):
    return x * _sigmoid(x)


def _dot(a, b):
    return jnp.dot(a, b, preferred_element_type=_F32)


def _dot_nt(a, b):
    return lax.dot_general(a, b, (((1,), (1,)), ((), ())), preferred_element_type=_F32)


def _split2(x):
    hi = x.astype(_BF16)
    return hi, (x - hi.astype(_F32)).astype(_BF16)


def _mod_kernel(c_ref, w_ref, b_ref, o_ref):
    c_act = _silu(c_ref[...])
    o_ref[0] = _dot(c_act.astype(_BF16), w_ref[0].astype(_BF16)) + b_ref[0]


def _modulation(c, w_ada, b_ada):
    n_layers, d, n = w_ada.shape
    bsz = c.shape[0]
    tn = 1536
    return pl.pallas_call(
        _mod_kernel,
        out_shape=jax.ShapeDtypeStruct((n_layers, bsz, n), _F32),
        grid=(n_layers, n // tn),
        in_specs=[
            pl.BlockSpec((bsz, d), lambda l, j: (0, 0)),
            pl.BlockSpec((1, d, tn), lambda l, j: (l, 0, j)),
            pl.BlockSpec((1, 1, tn), lambda l, j: (l, 0, j)),
        ],
        out_specs=pl.BlockSpec((1, bsz, tn), lambda l, j: (l, 0, j)),
        compiler_params=pltpu.CompilerParams(
            dimension_semantics=("arbitrary", "arbitrary"), vmem_limit_bytes=VMEM_LIMIT_BYTES),
        name="adaln_mod",
    )(c, w_ada, b_ada.reshape(n_layers, 1, n))


def _mix_kernel(x_ref, mod_ref, gmix_ref, win_ref, cw_ref, cb_ref, lng_ref, lnb_ref,
                lcw_ref, lcb_ref, wai_ref, bai_ref, lam_ref, wgate_ref, bgate_ref, gnorm_ref,
                wout_ref, *rest, fused_moe_gather):
    if fused_moe_gather:
        pos_cur, pos_nxt, pos_ahd, modp_ref, y_hbm, o_ref = rest[:6]
        ybuf, gsem = rest[-2:]
        rest = rest[6:-2]
    else:
        o_ref, rest = rest[0], rest[1:]
    ubuf, lbuf, hcar, state, tri_ref, tria_ref, csel_ref, causal_ref, hdiag_ref = rest
    ts = x_ref.shape[1]
    s_idx = pl.program_id(1)

    @pl.when(s_idx == 0)
    def _():
        ubuf[0:CONV_HIST, :] = jnp.zeros((CONV_HIST, CONV_W), _F32)
        lbuf[0:LRU_HIST, :] = jnp.zeros((LRU_HIST, LRU_W), _F32)
        hcar[...] = jnp.zeros_like(hcar)
        state[...] = jnp.zeros_like(state)
        ri = lax.broadcasted_iota(jnp.int32, (GLA_BLOCK, GLA_BLOCK), 0)
        ci = lax.broadcasted_iota(jnp.int32, (GLA_BLOCK, GLA_BLOCK), 1)
        same_chunk = ri // GLA_CHUNK == ci // GLA_CHUNK
        causal_f = jnp.where(same_chunk & (ci <= ri), 1.0, 0.0)
        causal_ref[...] = causal_f
        tri_ref[...] = causal_f.astype(_BF16)
        tria_ref[...] = jnp.where(same_chunk & (ci > ri), 1.0, 0.0).astype(_BF16)
        sel_r = lax.broadcasted_iota(jnp.int32, (LANES, GLA_BLOCK), 0)
        sel_c = lax.broadcasted_iota(jnp.int32, (LANES, GLA_BLOCK), 1)
        csel_ref[...] = jnp.where(sel_r == sel_c // GLA_CHUNK, 1.0, 0.0).astype(_BF16)
        st_row_head = lax.broadcasted_iota(jnp.int32, (QK_W, V_W), 0) // DK_PAD
        st_col_head = lax.broadcasted_iota(jnp.int32, (QK_W, V_W), 1) // DV_PAD
        hdiag_ref[...] = jnp.where(st_row_head == st_col_head, 1.0, 0.0)

    x = x_ref[0]
    if fused_moe_gather:
        step = pl.program_id(0) * pl.num_programs(1) + s_idx
        n_steps = pl.num_programs(0) * pl.num_programs(1)
        slot = step % GATHER_SLOTS

        def row_copy(pos_ref, j, to_slot):
            return pltpu.make_async_copy(y_hbm.at[pl.ds(pos_ref[j], 1)], ybuf.at[to_slot, pl.ds(j, 1)],
                                         gsem.at[to_slot])

        def wait_rows(of_slot):
            pltpu.make_async_copy(y_hbm.at[pl.ds(0, ts)], ybuf.at[of_slot], gsem.at[of_slot]).wait()

        @pl.when(step == 0)
        def _():
            def issue(j, carry):
                row_copy(pos_cur, j, 0).start()
                row_copy(pos_nxt, j, 1).start()
                return carry
            lax.fori_loop(0, ts, issue, 0, unroll=DMA_UNROLL)

        wait_rows(slot)
        x = x + modp_ref[0, 5:6, :] * ybuf[slot]

        ahead_slot = (step + GATHER_SLOTS - 1) % GATHER_SLOTS
        for j in range(ts):
            row_copy(pos_ahd, j, ahead_slot).start()

    sh1 = mod_ref[0, 0:1, :]
    gt1 = mod_ref[0, 2:3, :]
    scale = gmix_ref[...] * (1.0 + mod_ref[0, 1:2, :])
    ms = jnp.mean(x * x, axis=-1, keepdims=True)
    hb = (x * lax.rsqrt(ms + EPS) * scale + sh1).astype(_BF16)

    def proj(off, width):
        return _dot(hb, win_ref[:, off:off + width])

    out = {}
    _run_alternately(
        _conv_lru_steps(out, proj, cw_ref, cb_ref, lng_ref, lnb_ref, lcw_ref, lcb_ref, wai_ref, bai_ref,
                        lam_ref, ubuf, lbuf, hcar, ts),
        _gla_steps(out, proj, wgate_ref, bgate_ref, gnorm_ref, state, tri_ref, tria_ref, csel_ref,
                   causal_ref, hdiag_ref, ts))

    mixed = (_dot(out["o_g"].astype(_BF16), wout_ref[CONV_W + LRU_W:MIX_W, :])
             + _dot(out["r_out"].astype(_BF16), wout_ref[CONV_W:CONV_W + LRU_W, :])
             + _dot(out["u_out"].astype(_BF16), wout_ref[0:CONV_W, :]))
    o_ref[0] = x + gt1 * mixed

    if fused_moe_gather:
        @pl.when(step == n_steps - 1)
        def _():
            wait_rows((step + 1) % GATHER_SLOTS)
            wait_rows((step + 2) % GATHER_SLOTS)


def _run_alternately(*step_generators):
    clock = [0] * len(step_generators)
    live = list(range(len(step_generators)))
    while live:
        i = min(live, key=lambda j: clock[j])
        try:
            clock[i] += next(step_generators[i]) * (1 if i else 0.5)
        except StopIteration:
            live.remove(i)


def _conv_lru_steps(out, proj, cw_ref, cb_ref, lng_ref, lnb_ref, lcw_ref, lcb_ref, wai_ref, bai_ref,
                    lam_ref, ubuf, lbuf, hcar, ts):
    u = proj(OFF_CVV, CONV_W) * _sigmoid(proj(OFF_CVG, CONV_W))
    ubuf[CONV_HIST:CONV_HIST + ts, :] = u
    yield 500
    lbuf[LRU_HIST:LRU_HIST + ts, :] = proj(OFF_LRX, LRU_W)
    xb = jnp.broadcast_to(lcb_ref[...], (ts, LRU_W))
    for k in range(LRU_CONV_K):
        off = LRU_HIST - (LRU_CONV_K - 1) + k
        xb = xb + lcw_ref[k:k + 1, :] * lbuf[off:off + ts, :]
    lbuf[0:LRU_HIST, :] = lbuf[ts:ts + LRU_HIST, :]
    gates = _dot(xb.astype(_BF16), wai_ref[...]) + bai_ref[...]
    yield 800

    acc = jnp.broadcast_to(cb_ref[...], (ts, CONV_W))
    for r in range(SUBLANES):
        z = None
        for a_blk in range((CONV_K - 1 - r) // SUBLANES + 1):
            k = CONV_K - 1 - (SUBLANES * a_blk + r)
            off = CONV_HIST - SUBLANES * (a_blk + 1)
            term = cw_ref[k:k + 1, :] * ubuf[off:off + ts + SUBLANES, :]
            z = term if z is None else z + term
        acc = acc + z[SUBLANES - r:SUBLANES - r + ts]
        yield 200
    ubuf[0:CONV_HIST, :] = ubuf[ts:ts + CONV_HIST, :]
    mu = jnp.mean(acc, axis=-1, keepdims=True)
    cen = acc - mu
    var = jnp.mean(cen * cen, axis=-1, keepdims=True)
    out["u_out"] = _silu(cen * lax.rsqrt(var + EPS) * lng_ref[...] + lnb_ref[...])
    yield 250

    r_gate = _sigmoid(gates[:, 0:LRU_W])
    i_gate = _sigmoid(gates[:, LRU_W:2 * LRU_W])
    lam = lam_ref[...]
    softplus_neg_lam = jnp.maximum(-lam, 0.0) + jnp.log1p(jnp.exp(-jnp.abs(lam)))
    log_a = (-LRU_C) * r_gate * softplus_neg_lam
    a = jnp.exp(log_a)
    mult = jnp.sqrt(jnp.tanh(-log_a) * (a * a + 1.0))
    u_in = mult * (i_gate * xb)
    yield 600
    row = lax.broadcasted_iota(jnp.int32, a.shape, 0) % SUBLANES
    d = 1
    while d < SUBLANES:
        keep = row >= d
        a_sh = jnp.where(keep, pltpu.roll(a, d, axis=0), 1.0)
        u_sh = jnp.where(keep, pltpu.roll(u_in, d, axis=0), 0.0)
        u_in = u_in + a * u_sh
        a = a * a_sh
        d *= 2
        yield 200
    groups = []
    h_prev = hcar[0:1, :]
    for g in range(ts // SUBLANES):
        r0 = g * SUBLANES
        h_g = u_in[r0:r0 + SUBLANES] + a[r0:r0 + SUBLANES] * h_prev
        groups.append(h_g)
        h_prev = h_g[SUBLANES - 1:SUBLANES]
        if g % SUBLANES == SUBLANES - 1:
            yield 50
    h_lru = jnp.concatenate(groups, axis=0)
    hcar[...] = jnp.broadcast_to(h_prev, hcar.shape)
    out["r_out"] = h_lru * jax.nn.gelu(proj(OFF_LRY, LRU_W), approximate=True)


def _gla_steps(out, proj, wgate_ref, bgate_ref, gnorm_ref, state, tri_ref, tria_ref, csel_ref, causal_ref,
               hdiag_ref, ts):
    zg = proj(OFF_GLR, LANES)
    glog = _dot(zg.astype(_BF16), wgate_ref[...]) + bgate_ref[...]
    lg = (jnp.minimum(glog, 0.0) - jnp.log1p(jnp.exp(-jnp.abs(glog)))) * (1.0 / GLA_TAU)
    yield 300
    zq = proj(OFF_Q, QK_W) * (GLA_DK ** -0.5)
    zk = proj(OFF_K, QK_W)
    yield 500
    zv = proj(OFF_V, V_W)
    yield 500

    nblk = ts // GLA_BLOCK
    cpb = GLA_BLOCK // GLA_CHUNK
    tri = tri_ref[...]
    tri_after = tria_ref[...]
    chunk_sel = csel_ref[...]
    causal = causal_ref[...] > 0.5
    qk_lane_head = lax.broadcasted_iota(jnp.int32, (1, QK_W), 1) // DK_PAD
    t_lane_chunk = lax.broadcasted_iota(jnp.int32, (1, GLA_BLOCK), 1) // GLA_CHUNK

    o_blocks = []
    for blk in range(nblk):
        r0 = blk * GLA_BLOCK
        lg_b = lg[r0:r0 + GLA_BLOCK]
        p_hi, p_lo = _split2(lg_b)
        b = _dot(tri, p_hi) + _dot(tri, p_lo)
        b_rest = _dot(tri_after, p_hi) + _dot(tri_after, p_lo)
        b_tot = _dot(chunk_sel, p_hi) + _dot(chunk_sel, p_lo)
        yield 500
        q_in =zq[r0:r0 + GLA_BLOCK] * jnp.exp(b)
        k_blk = zk[r0:r0 + GLA_BLOCK]
        k_in = (k_blk * jnp.exp(-b)).astype(_BF16)
        k_out_t = (k_blk * jnp.exp(b_rest)).T.astype(_BF16)
        v_b = zv[r0:r0 + GLA_BLOCK].astype(_BF16)
        q_in_b = q_in.astype(_BF16)
        decay_cols = jnp.exp(b_tot).T

        o_heads = []
        for hd in range(GLA_HEADS):
            q_h = jnp.where(qk_lane_head == hd, q_in_b, jnp.zeros_like(q_in_b))
            sc = jnp.where(causal, _dot_nt(q_h, k_in), 0.0)
            o_heads.append(_dot(sc.astype(_BF16), v_b[:, hd * DV_PAD:(hd + 1) * DV_PAD]))
            yield 300
        o_intra = jnp.concatenate(o_heads, axis=1)

        o_inter = []
        st = state[...]
        for c in range(cpb):
            c0 = c * GLA_CHUNK
            o_inter.append(_dot(q_in_b[c0:c0 + GLA_CHUNK], st.astype(_BF16)))
            kv = _dot(jnp.where(t_lane_chunk == c, k_out_t, jnp.zeros_like(k_out_t)), v_b)
            st = st * decay_cols[:, c:c + 1] + kv * hdiag_ref[...]
            yield 400
        state[...] = st
        o_blocks.append(o_intra + jnp.concatenate(o_inter, axis=0))
    o = o_blocks[0] if nblk == 1 else jnp.concatenate(o_blocks, axis=0)

    og = proj(OFF_OG, V_W)
    o_parts = []
    for hd in range(GLA_HEADS):
        o_h = o[:, hd * DV_PAD:(hd + 1) * DV_PAD]
        ms_h = jnp.sum(o_h * o_h, axis=-1, keepdims=True) * (1.0 / GLA_DV)
        o_parts.append(o_h * lax.rsqrt(ms_h + EPS))
    o_n = jnp.concatenate(o_parts, axis=1) * gnorm_ref[...]
    out["o_g"] = o_n * _silu(og)


def _token_mix(x, mod, params, layer, moe=None):
    bsz, seq, d = x.shape
    ts = MIX_TS
    n_s = seq // ts
    full = lambda shape: pl.BlockSpec((None,) + shape, lambda b, s: (layer,) + (0,) * len(shape))
    in_specs = [
        pl.BlockSpec((1, ts, d), lambda b, s: (b, s, 0)),
        pl.BlockSpec((1, 6, d), lambda b, s: (b, 0, 0)),
        full((1, d)),
        full((d, N_IN)),
        full((CONV_HIST, CONV_W)), full((1, CONV_W)), full((1, CONV_W)), full((1, CONV_W)),
        full((LRU_CONV_K, LRU_W)), full((1, LRU_W)),
        full((LRU_W, 2 * LRU_W)), full((1, 2 * LRU_W)), full((1, LRU_W)),
        full((LANES, QK_W)), full((1, QK_W)), full((1, V_W)),
        full((MIX_W, d)),
    ]
    scratch_shapes = [
        pltpu.VMEM((CONV_HIST + ts, CONV_W), _F32),
        pltpu.VMEM((LRU_HIST + ts, LRU_W), _F32),
        pltpu.VMEM((SUBLANES, LRU_W), _F32),
        pltpu.VMEM((QK_W, V_W), _F32),
        pltpu.VMEM((GLA_BLOCK, GLA_BLOCK), _BF16),
        pltpu.VMEM((GLA_BLOCK, GLA_BLOCK), _BF16),
        pltpu.VMEM((LANES, GLA_BLOCK), _BF16),
        pltpu.VMEM((GLA_BLOCK, GLA_BLOCK), _F32),
        pltpu.VMEM((QK_W, V_W), _F32),
    ]
    args = [x, mod] + list(params)
    if moe is not None:
        pos, mod_prev, y = moe
        last = bsz * n_s - 1
        in_specs += [
            pl.BlockSpec((ts,), lambda b, s: (b * n_s + s,), memory_space=pltpu.SMEM),
            pl.BlockSpec((ts,), lambda b, s: (jnp.minimum(b * n_s + s + 1, last),), memory_space=pltpu.SMEM),
            pl.BlockSpec((ts,), lambda b, s: (jnp.minimum(b * n_s + s + 2, last),), memory_space=pltpu.SMEM),
            pl.BlockSpec((1, 6, d), lambda b, s: (b, 0, 0)),
            pl.BlockSpec(memory_space=pl.ANY),
        ]
        scratch_shapes += [pltpu.VMEM((GATHER_SLOTS, ts, d), _F32), pltpu.SemaphoreType.DMA((GATHER_SLOTS,))]
        args += [pos, pos, pos, mod_prev, y]
    return pl.pallas_call(
        functools.partial(_mix_kernel, fused_moe_gather=moe is not None),
        out_shape=jax.ShapeDtypeStruct(x.shape, _F32),
        grid=(bsz, n_s),
        in_specs=in_specs,
        out_specs=pl.BlockSpec((1, ts, d), lambda b, s: (b, s, 0)),
        scratch_shapes=scratch_shapes,
        compiler_params=pltpu.CompilerParams(
            dimension_semantics=("arbitrary", "arbitrary"), vmem_limit_bytes=VMEM_LIMIT_BYTES),
        name="token_mix",
    )(*args)


def _route_kernel(x_ref, mod_ref, g_ref, wr_ref, br_ref, xp_ref, meta_ref, cnt_ref, carry, strict_ref):
    tt = x_ref.shape[0]
    i = pl.program_id(0)

    @pl.when(i == 0)
    def _():
        carry[...] = jnp.zeros_like(carry)
        ri = lax.broadcasted_iota(jnp.int32, (tt, tt), 0)
        ci = lax.broadcasted_iota(jnp.int32, (tt, tt), 1)
        strict_ref[...] = jnp.where(ci < ri, 1.0, 0.0).astype(_BF16)

    sh2 = mod_ref[0, 3:4, :]
    scale = g_ref[...] * (1.0 + mod_ref[0, 4:5, :])
    w_hi, w_lo = _split2(wr_ref[...])
    w_cat = jnp.concatenate([w_hi, w_lo], axis=1)
    n_half = tt // ROUTE_SPLIT
    counts_before = [carry[0:1, :]]

    def route_rows(r0):
        x = x_ref[r0:r0 + n_half, :]
        ms = jnp.mean(x * x, axis=-1, keepdims=True)
        h = x * lax.rsqrt(ms + EPS) * scale + sh2
        xp_ref[r0:r0 + n_half, 0:D_MODEL] = h
        h_hi, h_lo = _split2(h)
        yield
        prod = _dot(h_hi, w_cat) + _dot(h_lo, w_cat)
        logits = prod[:, 0:LANES] + prod[:, LANES:2 * LANES] + br_ref[...]
        yield

        lane = lax.broadcasted_iota(jnp.int32, (n_half, LANES), 1)
        lane_f = lane.astype(_F32)
        neg = -jnp.inf
        big = float(LANES)

        def first_argmax(vals, vmax):
            return jnp.min(jnp.where(vals == vmax, lane_f, big), axis=-1, keepdims=True).astype(jnp.int32)

        gl = jnp.where(lane < N_GROUPS, logits, neg)
        gmax = jnp.max(gl, axis=-1, keepdims=True)
        g_star = first_argmax(gl, gmax)
        p_sel = 1.0 / jnp.sum(jnp.exp(gl - gmax), axis=-1, keepdims=True)
        base = N_GROUPS + EXPERTS_PER_GROUP * g_star
        el = jnp.where((lane >= base) & (lane < base + EXPERTS_PER_GROUP), logits, neg)
        v0 = jnp.max(el, axis=-1, keepdims=True)
        i0 = first_argmax(el, v0)
        el2 = jnp.where(lane == i0, neg, el)
        v1 = jnp.max(el2, axis=-1, keepdims=True)
        i1 = first_argmax(el2, v1)
        ex = jnp.exp(v1 - v0)
        wt0 = p_sel / (1.0 + ex)
        wt1 = p_sel * ex / (1.0 + ex)
        e0 = i0 - base
        e1 = i1 - base
        e_lo = jnp.minimum(e0, e1)
        e_hi = jnp.maximum(e0, e1)
        w_lo = jnp.where(e0 < e1, wt0, wt1)
        w_hi = jnp.where(e0 < e1, wt1, wt0)
        pair = (e_lo * (2 * EXPERTS_PER_GROUP - 1 - e_lo)) // 2 + (e_hi - e_lo - 1)
        bucket = g_star * N_PAIRS + pair
        yield

        onehot = lane == bucket
        onehot_f = jnp.where(onehot, 1.0, 0.0)
        before = counts_before[-1]
        prefix = _dot(strict_ref[0:n_half, 0:n_half], onehot_f.astype(_BF16)) + before
        rank = jnp.sum(jnp.where(onehot, prefix, 0.0), axis=-1, keepdims=True)
        counts_before.append(before + jnp.sum(onehot_f, axis=0, keepdims=True))

        meta = jnp.where(lane == 0, bucket.astype(_F32), jnp.where(lane == 1, rank, 0.0))
        meta_ref[:, r0:r0 + n_half] = meta.T[0:SUBLANES, :]
        xp_ref[r0:r0 + n_half, D_MODEL:ROW_W] = jnp.where(lane == 0, w_lo, jnp.where(lane == 1, w_hi, 0.0))

    blocks = [route_rows(k * n_half) for k in range(ROUTE_SPLIT)]
    while blocks:
        for blk in list(blocks):
            if next(blk, "done") == "done":
                blocks.remove(blk)
    total = jnp.broadcast_to(counts_before[-1], carry.shape)
    carry[...] = total
    cnt_ref[...] = total


def _route(x2d, mod, g_ffn, w_r, b_r, layer, seq):
    n_tok, d = x2d.shape
    tt = ROUTE_TT
    tiles_per_seq = seq // tt
    per_layer = lambda shape: pl.BlockSpec((None,) + shape, lambda i: (layer, 0, 0))
    return pl.pallas_call(
        _route_kernel,
        out_shape=(
            jax.ShapeDtypeStruct((n_tok, ROW_W), _F32),
            jax.ShapeDtypeStruct((SUBLANES, n_tok), _F32),
            jax.ShapeDtypeStruct((SUBLANES, LANES), _F32),
        ),
        grid=(n_tok // tt,),
        in_specs=[
            pl.BlockSpec((tt, d), lambda i: (i, 0)),
            pl.BlockSpec((1, 6, d), lambda i: (i // tiles_per_seq, 0, 0)),
            per_layer((1, d)),
            per_layer((d, LANES)),
            per_layer((1, LANES)),
        ],
        out_specs=(
            pl.BlockSpec((tt, ROW_W), lambda i: (i, 0)),
            pl.BlockSpec((SUBLANES, tt), lambda i: (0, i)),
            pl.BlockSpec((SUBLANES, LANES), lambda i: (0, 0)),
        ),
        scratch_shapes=[pltpu.VMEM((SUBLANES, LANES), _F32), pltpu.VMEM((tt, tt), _BF16)],
        compiler_params=pltpu.CompilerParams(
            dimension_semantics=("arbitrary",), vmem_limit_bytes=VMEM_LIMIT_BYTES),
        name="moe_route",
    )(x2d, mod, g_ffn, w_r, b_r)


def _permute_kernel(fill_ref, pos_ref, xp_ref, xs_hbm, zbuf, sem, zsem):
    tb = pos_ref.shape[0]

    @pl.when(pl.program_id(0) == 0)
    def _():
        zbuf[...] = jnp.zeros_like(zbuf)
        n_used = fill_ref[2 * N_BUCKETS]
        n_tiles = xs_hbm.shape[0] // FFN_TM
        for wait in (False, True):
            for b in range(N_BUCKETS):
                for row0, cond in ((fill_ref[b], fill_ref[N_BUCKETS + b] > 0),
                                   ((n_used + b) * FFN_TM, n_used + b < n_tiles)):
                    @pl.when(cond)
                    def _():
                        dst = xs_hbm.at[pl.ds(pl.multiple_of(row0, FFN_TM), FFN_TM)]
                        fill = pltpu.make_async_copy(zbuf, dst, zsem)
                        fill.wait() if wait else fill.start()

    for j in range(tb):
        pltpu.make_async_copy(xp_ref.at[pl.ds(j, 1)], xs_hbm.at[pl.ds(pos_ref[j], 1)], sem).start(priority=j % 2)
    pltpu.make_async_copy(xp_ref, xs_hbm.at[pl.ds(0, tb)], sem).wait()


def _permute(fill_starts, pos, xp, n_rows):
    n_tok = xp.shape[0]
    tb = PERM_TB
    return pl.pallas_call(
        _permute_kernel,
        out_shape=jax.ShapeDtypeStruct((n_rows, ROW_W), _F32),
        grid_spec=pltpu.PrefetchScalarGridSpec(
            num_scalar_prefetch=1,
            grid=(n_tok // tb,),
            in_specs=[
                pl.BlockSpec((tb,), lambda i, fs: (i,), memory_space=pltpu.SMEM),
                pl.BlockSpec((tb, ROW_W), lambda i, fs: (i, 0)),
            ],
            out_specs=pl.BlockSpec(memory_space=pl.ANY),
            scratch_shapes=[pltpu.VMEM((FFN_TM, ROW_W), _F32), pltpu.SemaphoreType.DMA,
                            pltpu.SemaphoreType.DMA],
        ),
        compiler_params=pltpu.CompilerParams(
            dimension_semantics=("arbitrary",), vmem_limit_bytes=VMEM_LIMIT_BYTES),
        name="moe_permute",
    )(fill_starts, pos, xp)


def _ffn_kernel(tg_ref, tlo_ref, thi_ref, clo_ref, chi_ref, ng_ref, nlo_ref, nhi_ref, plo_ref, phi_ref, nused_ref,
                xs_ref, wg_hbm, wu_hbm, wd_hbm, y_ref, sg, su, sd, agu, ad, sem, *, layer):
    i = pl.program_id(0)

    def weight_copies(slot, g, e):
        return (pltpu.make_async_copy(wg_hbm.at[layer, g, e], sg.at[slot], sem.at[slot, 0]),
                pltpu.make_async_copy(wu_hbm.at[layer, g, e], su.at[slot], sem.at[slot, 1]),
                pltpu.make_async_copy(wd_hbm.at[layer, g, e], sd.at[slot], sem.at[slot, 2]))

    @pl.when(i == 0)
    def _():
        for cp in weight_copies(0, tg_ref[0], tlo_ref[0]) + weight_copies(1, tg_ref[0], thi_ref[0]):
            cp.start()

    for slot, changed_ref, e_ref in ((0, clo_ref, tlo_ref), (1, chi_ref, thi_ref)):
        @pl.when(changed_ref[i] == 1)
        def _():
            for cp in weight_copies(slot, tg_ref[i], e_ref[i]):
                cp.wait()
            agu[slot, :, 0:D_EXPERT] = sg[slot].astype(_BF16)
            agu[slot, :, D_EXPERT:2 * D_EXPERT] = su[slot].astype(_BF16)
            ad[slot] = sd[slot].astype(_BF16)

    for slot, prefetch_ref, e_ref in ((0, plo_ref, nlo_ref), (1, phi_ref, nhi_ref)):
        @pl.when(prefetch_ref[i] == 1)
        def _():
            for cp in weight_copies(slot, ng_ref[i], e_ref[i]):
                cp.start()

    @pl.when(i < nused_ref[0])
    def _():
        xb = xs_ref[:, 0:D_MODEL].astype(_BF16)
        info = xs_ref[:, D_MODEL:ROW_W]

        def expert(slot):
            gate_up = _dot(xb, agu[slot])
            hid = _silu(gate_up[:, 0:D_EXPERT]) * gate_up[:, D_EXPERT:2 * D_EXPERT] * info[:, slot:slot + 1]
            return _dot(hid.astype(_BF16), ad[slot])

        y_ref[...] = expert(0) + expert(1)

    @pl.when(i >= nused_ref[0])
    def _():
        y_ref[...] = jnp.zeros_like(y_ref)


def _expert_ffn(layer, plan, xs, w_gate, w_up, w_down):
    n_rows = xs.shape[0]
    tm = FFN_TM
    n_tiles = n_rows // tm

    def row_map(i, *prefetch):
        n_used = prefetch[-1]
        return (jnp.minimum(i, n_used[0] - 1), 0)

    hbm = pl.BlockSpec(memory_space=pl.ANY)
    return pl.pallas_call(
        functools.partial(_ffn_kernel, layer=layer),
        out_shape=jax.ShapeDtypeStruct((n_rows, D_MODEL), _F32),
        grid_spec=pltpu.PrefetchScalarGridSpec(
            num_scalar_prefetch=len(plan),
            grid=(n_tiles,),
            in_specs=[pl.BlockSpec((tm, ROW_W), row_map), hbm, hbm, hbm],
            out_specs=pl.BlockSpec((tm, D_MODEL), lambda i, *prefetch: (i, 0)),
            scratch_shapes=[
                pltpu.VMEM((2, D_MODEL, D_EXPERT), _F32), pltpu.VMEM((2, D_MODEL, D_EXPERT), _F32),
                pltpu.VMEM((2, D_EXPERT, D_MODEL), _F32),
                pltpu.VMEM((2, D_MODEL, 2 * D_EXPERT), _BF16), pltpu.VMEM((2, D_EXPERT, D_MODEL), _BF16),
                pltpu.SemaphoreType.DMA((2, 3)),
            ],
        ),
        compiler_params=pltpu.CompilerParams(
            dimension_semantics=("arbitrary",), vmem_limit_bytes=VMEM_LIMIT_BYTES),
        name="moe_ffn",
    )(*plan, xs, w_gate, w_up, w_down)


def _combine_kernel(pos_ref, x_ref, mod_ref, gfin_ref, y_hbm, o_ref, ybuf, sem):
    tc = x_ref.shape[0]
    for j in range(tc):
        pltpu.make_async_copy(y_hbm.at[pl.ds(pos_ref[j], 1)], ybuf.at[pl.ds(j, 1)], sem).start(priority=j % 2)
    pltpu.make_async_copy(y_hbm.at[pl.ds(0, tc)], ybuf, sem).wait()

    gt2 = mod_ref[0, 5:6, :]
    out = x_ref[...] + gt2 * ybuf[...]
    ms = jnp.mean(out * out, axis=-1, keepdims=True)
    o_ref[...] = out * lax.rsqrt(ms + EPS) * gfin_ref[...]


def _combine(pos, x2d, mod, g_final, y, seq):
    n_tok, d = x2d.shape
    tc = COMB_TC
    tiles_per_seq = seq // tc
    return pl.pallas_call(
        _combine_kernel,
        out_shape=jax.ShapeDtypeStruct((n_tok, d), _F32),
        grid=(n_tok // tc,),
        in_specs=[
            pl.BlockSpec((tc,), lambda i: (i,), memory_space=pltpu.SMEM),
            pl.BlockSpec((tc, d), lambda i: (i, 0)),
            pl.BlockSpec((1, 6, d), lambda i: (i // tiles_per_seq, 0, 0)),
            pl.BlockSpec((1, d), lambda i: (0, 0)),
            pl.BlockSpec(memory_space=pl.ANY),
        ],
        out_specs=pl.BlockSpec((tc, d), lambda i: (i, 0)),
        scratch_shapes=[pltpu.VMEM((tc, d), _F32), pltpu.SemaphoreType.DMA],
        compiler_params=pltpu.CompilerParams(
            dimension_semantics=("arbitrary",), vmem_limit_bytes=VMEM_LIMIT_BYTES),
        name="moe_combine",
    )(pos, x2d, mod, g_final, y)


def _pad_heads(w, heads, width, padded):
    lead = w.shape[:-1]
    w = w.reshape(lead + (heads, width))
    w = jnp.pad(w, [(0, 0)] * len(lead) + [(0, 0), (0, padded - width)])
    return w.reshape(lead + (heads * padded,))


def _block_diag(w):
    n_layers, n, bw, _ = w.shape
    eye = jnp.eye(n, dtype=w.dtype)
    return (eye[None, :, None, :, None] * w[:, :, :, None, :]).reshape(n_layers, n * bw, n * bw)


def _prep_params(w_in, conv_dw_w, conv_dw_b, conv_ln_g, conv_ln_b, lru_conv_w, lru_conv_b, lru_w_a,
                 lru_b_a, lru_w_i, lru_b_i, lru_lam, gla_w_gate, gla_b_gate, gla_norm_g, w_out, g_mix):
    sizes = [CONV_W, CONV_W, LRU_W, LRU_W, GLA_HEADS * GLA_DK, GLA_HEADS * GLA_DK, GLA_V, GLA_RANK, GLA_V]
    cv_v, cv_g, lr_x, lr_y, q, k, v, g_lr, og = jnp.split(w_in, np.cumsum(sizes)[:-1].tolist(), axis=-1)
    w_in_p = jnp.concatenate([
        cv_v, cv_g, lr_x, lr_y,
        _pad_heads(q, GLA_HEADS, GLA_DK, DK_PAD), _pad_heads(k, GLA_HEADS, GLA_DK, DK_PAD),
        _pad_heads(v, GLA_HEADS, GLA_DV, DV_PAD),
        jnp.pad(g_lr, ((0, 0), (0, 0), (0, LANES - GLA_RANK))),
        _pad_heads(og, GLA_HEADS, GLA_DV, DV_PAD)], axis=-1).astype(_BF16)
    n_layers = w_in.shape[0]
    wo_o = w_out[:, CONV_W + LRU_W:].reshape(n_layers, GLA_HEADS, GLA_DV, D_MODEL)
    wo_o = jnp.pad(wo_o, ((0, 0), (0, 0), (0, DV_PAD - GLA_DV), (0, 0))).reshape(n_layers, V_W, D_MODEL)
    w_out_p = jnp.concatenate([w_out[:, :CONV_W + LRU_W], wo_o], axis=1).astype(_BF16)
    w_gate_p = jnp.pad(_pad_heads(gla_w_gate, GLA_HEADS, GLA_DK, DK_PAD),
                       ((0, 0), (0, LANES - GLA_RANK), (0, 0))).astype(_BF16)
    row = lambda v: v[:, None, :]
    return [
        row(g_mix),
        w_in_p,
        jnp.pad(conv_dw_w, ((0, 0), (0, CONV_HIST - CONV_K), (0, 0))),
        row(conv_dw_b), row(conv_ln_g), row(conv_ln_b),
        lru_conv_w, row(lru_conv_b),
        jnp.concatenate([_block_diag(lru_w_a), _block_diag(lru_w_i)], axis=2).astype(_BF16),
        row(jnp.concatenate([lru_b_a, lru_b_i], axis=1)),
        row(lru_lam),
        w_gate_p,
        row(_pad_heads(gla_b_gate, GLA_HEADS, GLA_DK, DK_PAD)),
        row(_pad_heads(gla_norm_g, GLA_HEADS, GLA_DV, DV_PAD)),
        w_out_p,
    ]


def _bucket_layout(counts, n_tiles):
    tm = FFN_TM
    counts = counts.astype(jnp.int32)
    tiles = (counts + tm - 1) // tm
    upto = jnp.arange(N_BUCKETS)[:, None] <= jnp.arange(N_BUCKETS)[None, :]
    tile_end = jnp.sum(jnp.where(upto, tiles[:, None], 0), axis=0)
    starts = ((tile_end - tiles) * tm).astype(jnp.int32)
    n_used = tile_end[-1]
    tile_idx = jnp.minimum(jnp.arange(n_tiles, dtype=jnp.int32), n_used - 1)

    def bucket_of(t):
        return jnp.sum((t[:, None] >= tile_end[None, :]).astype(jnp.int32), axis=1)

    def experts(b):
        pair = b % N_PAIRS
        lo = (pair >= 3).astype(jnp.int32) + (pair >= 5).astype(jnp.int32)
        hi = jnp.where(pair < 3, pair + 1, jnp.where(pair < 5, pair - 1, 3))
        return b // N_PAIRS, lo, hi

    tile_bucket = bucket_of(tile_idx)
    prev_bucket = jnp.where(tile_idx >= 1, bucket_of(tile_idx - 1), -1)
    g, lo, hi = experts(tile_bucket)
    pg, plo, phi = experts(prev_bucket)
    first = (tile_bucket != prev_bucket) & (jnp.arange(n_tiles) < n_used)
    fresh = prev_bucket < 0
    changed_lo = first & (fresh | (g != pg) | (lo != plo))
    changed_hi = first & (fresh | (g != pg) | (hi != phi))
    next_first_tile = jnp.min(jnp.where(tile_end[None, :] > tile_idx[:, None], tile_end[None, :], n_tiles), axis=1)
    has_next = next_first_tile < n_used
    ng, nlo, nhi = experts(bucket_of(jnp.minimum(next_first_tile, n_used - 1)))
    prefetch_lo = first & has_next & ((ng != g) | (nlo != lo))
    prefetch_hi = first & has_next & ((ng != g) | (nhi != hi))
    as_i32 = lambda v: v.astype(jnp.int32)
    ffn_plan = (g, lo, hi, as_i32(changed_lo), as_i32(changed_hi), ng, nlo, nhi,
                as_i32(prefetch_lo), as_i32(prefetch_hi), n_used.reshape(1).astype(jnp.int32))
    fill = jnp.concatenate([(jnp.maximum(tile_end - 1, 0) * tm).astype(jnp.int32), tiles, n_used[None]])
    return starts, fill, ffn_plan


def kernel(x, c, w_ada, b_ada, g_mix, w_in, conv_dw_w, conv_dw_b, conv_ln_g, conv_ln_b, lru_conv_w,
           lru_conv_b, lru_w_a, lru_b_a, lru_w_i, lru_b_i, lru_lam, gla_w_gate, gla_b_gate, gla_norm_g,
           w_out, g_ffn, w_route_group, b_route_group, w_route_expert, b_route_expert, w_gate, w_up,
           w_down, g_final):
    bsz, seq, d = x.shape
    n_layers = w_ada.shape[0]
    n_tok = bsz * seq
    assert d == D_MODEL and seq % max(MIX_TS, ROUTE_TT, COMB_TC) == 0 and MIX_TS % GLA_BLOCK == 0
    assert n_tok % PERM_TB == 0
    n_tiles = n_tok // FFN_TM + N_BUCKETS
    n_rows = n_tiles * FFN_TM

    mod_all = _modulation(c, w_ada, b_ada).reshape(n_layers, bsz, 6, d)

    params = _prep_params(w_in, conv_dw_w, conv_dw_b, conv_ln_g, conv_ln_b, lru_conv_w, lru_conv_b,
                          lru_w_a, lru_b_a, lru_w_i, lru_b_i, lru_lam, gla_w_gate, gla_b_gate, gla_norm_g,
                          w_out, g_mix)
    n_experts = N_GROUPS * EXPERTS_PER_GROUP
    w_r = jnp.concatenate(
        [w_route_group, w_route_expert.transpose(0, 2, 1, 3).reshape(n_layers, d, n_experts)], axis=2)
    w_r = jnp.pad(w_r, ((0, 0), (0, 0), (0, LANES - N_GROUPS - n_experts)))
    b_r = jnp.concatenate([b_route_group, b_route_expert.reshape(n_layers, n_experts)], axis=1)
    b_r = jnp.pad(b_r, ((0, 0), (0, LANES - N_GROUPS - n_experts)))[:, None, :]
    pending_moe = None
    for l in range(n_layers):
        mod = mod_all[l]
        x = _token_mix(x, mod, params, l, pending_moe)

        x2d = x.reshape(n_tok, d)
        xp, meta, counts = _route(x2d, mod, g_ffn[:, None, :], w_r, b_r, l, seq)
        bucket = meta[0].astype(jnp.int32)
        rank = meta[1].astype(jnp.int32)
        starts, fill, ffn_plan = _bucket_layout(counts[0, :N_BUCKETS], n_tiles)
        in_bucket = bucket[:, None] == jnp.arange(N_BUCKETS, dtype=jnp.int32)[None, :]
        pos = rank + jnp.sum(jnp.where(in_bucket, starts[None, :], 0), axis=1)

        xs = _permute(fill, pos, xp, n_rows)
        y = _expert_ffn(l, ffn_plan, xs, w_gate, w_up, w_down)
        pending_moe = (pos, mod, y)
    return _combine(pos, x2d, mod, g_final[None, :], y, seq).reshape(bsz, seq, d)
```
